```python
import jax
import jax.numpy as jnp
from jax import lax
import numpy as np

D_MODEL = 1024
BATCH = 1
SEQ = 16384
DEPTH = 1
DEC_BATCH = 8
DEC_SEQ = 64
PAST_LEN = 1024

CHUNK = 64
N_PAST_CHUNKS = 8
BAND = (N_PAST_CHUNKS + 1) * CHUNK
ATT_REACH = N_PAST_CHUNKS * CHUNK
ATT_HEADS = 8
ATT_HEAD_DIM = 64
ATT_DIM = ATT_HEADS * ATT_HEAD_DIM
REL_CLIP = 128
N_REL = 2 * REL_CLIP + 1
RWKV_HEADS = 8
RWKV_HEAD = 64
RWKV_DIM = RWKV_HEADS * RWKV_HEAD
LORA_W = 64
LORA_A = 64
LORA_G = 128
ATT_COLS = 3 * ATT_DIM
RW_COLS = 3 * RWKV_DIM + LORA_W + LORA_A + LORA_G
GATE_COLS = 2 * D_MODEL
IN_COLS = ATT_COLS + RW_COLS + GATE_COLS
D_FF = 2816
CONV_W = 3
LN_EPS = 1e-5
GN_EPS = 64e-5
ALPHA = (2 * DEPTH) ** 0.25
BETA = (8 * DEPTH) ** -0.25

kernel_name = 'streaming_chunkattn_rwkv7_convffn_deepnorm_adaln'


def layer_norm(x, g, b):
    xf = x.astype(jnp.float32)
    mu = jnp.mean(xf, axis=-1, keepdims=True)
    var = jnp.mean(jnp.square(xf - mu), axis=-1, keepdims=True)
    y = (xf - mu) * lax.rsqrt(var + LN_EPS) * g.astype(jnp.float32) + b.astype(jnp.float32)
    return y.astype(x.dtype)


def rel_bias(table, dist):
    return table[:, jnp.clip(dist, -REL_CLIP, REL_CLIP) + REL_CLIP].astype(jnp.float32)


def chunk_band_attention(q, k, v, rel_table):
    b, s, h, dh = q.shape
    n_chunks = s // CHUNK
    pad = jnp.zeros((b, ATT_REACH, h, dh), k.dtype)
    k_pad = jnp.concatenate([pad, k], axis=1)
    v_pad = jnp.concatenate([pad, v], axis=1)
    q_pos = jnp.arange(CHUNK)
    k_pos = jnp.arange(BAND)
    bias = rel_bias(rel_table, ATT_REACH + q_pos[:, None] - k_pos[None, :])
    scale = dh ** -0.5

    def one_chunk(n):
        start = n * CHUNK
        q_c = lax.dynamic_slice_in_dim(q, start, CHUNK, axis=1)
        k_b = lax.dynamic_slice_in_dim(k_pad, start, BAND, axis=1)
        v_b = lax.dynamic_slice_in_dim(v_pad, start, BAND, axis=1)
        sc = jnp.einsum('bqhd,bkhd->bhqk', q_c, k_b).astype(jnp.float32) * scale + bias
        valid = (start + k_pos) >= ATT_REACH
        sc = jnp.where(valid, sc, -jnp.inf)
        p = jax.nn.softmax(sc, axis=-1).astype(v_b.dtype)
        return jnp.einsum('bhqk,bkhd->bqhd', p, v_b)

    out = lax.map(one_chunk, jnp.arange(n_chunks))
    return jnp.swapaxes(out, 0, 1).reshape(b, s, h, dh)


def cached_chunk_attention(q, k, v, k_cache, v_cache, rel_table):
    b, t, h, dh = q.shape
    past = k_cache.shape[1]
    k_all = jnp.concatenate([k_cache.astype(k.dtype), k], axis=1)
    v_all = jnp.concatenate([v_cache.astype(v.dtype), v], axis=1)
    bias = rel_bias(rel_table, past + jnp.arange(t)[:, None] - jnp.arange(past + t)[None, :])
    sc = jnp.einsum('bqhd,bkhd->bhqk', q, k_all).astype(jnp.float32) * (dh ** -0.5) + bias
    p = jax.nn.softmax(sc, axis=-1).astype(v_all.dtype)
    return jnp.einsum('bhqk,bkhd->bqhd', p, v_all)


def wkv7_scan(r, decay, k, v, a_vec, b_vec, state):
    seq = tuple(jnp.swapaxes(z, 0, 1) for z in (r, decay, k, v, a_vec, b_vec))

    def step(s, inp):
        r_t, w_t, k_t, v_t, a_t, b_t = inp
        sa = jnp.einsum('bhvk,bhk->bhv', s, a_t)
        s = s * w_t[:, :, None, :] + sa[..., None] * b_t[:, :, None, :] + v_t[..., None] * k_t[:, :, None, :]
        return s, jnp.einsum('bhvk,bhk->bhv', s, r_t)

    s_final, ys = lax.scan(step, state, seq)
    return jnp.swapaxes(ys, 0, 1), s_final


def rwkv7_time_mix(p, shift_prev, state, lp):
    b, t, _ = p.shape
    f32 = jnp.float32
    prev = jnp.concatenate([shift_prev.astype(p.dtype), p[:, :-1]], axis=1)
    xs = p + (prev - p) * lp['rwkv_mu']
    r = xs[..., 0:RWKV_DIM]
    k = xs[..., RWKV_DIM:2 * RWKV_DIM]
    v = xs[..., 2 * RWKV_DIM:3 * RWKV_DIM]
    o = 3 * RWKV_DIM
    lw = xs[..., o:o + LORA_W]
    la = xs[..., o + LORA_W:o + LORA_W + LORA_A]
    lg = xs[..., o + LORA_W + LORA_A:]
    w_log = -jax.nn.softplus(-(lp['rwkv_w0'] + jnp.tanh(lw) @ lp['rwkv_w_up']).astype(f32)) - 0.5
    decay = jnp.exp(-jnp.exp(w_log))
    a = jax.nn.sigmoid((lp['rwkv_a0'] + la @ lp['rwkv_a_up']).astype(f32))
    g = (jax.nn.sigmoid(lg) @ lp['rwkv_g_up']).astype(f32)

    def heads(z):
        return z.reshape(b, t, RWKV_HEADS, RWKV_HEAD)

    hn = (RWKV_HEADS, RWKV_HEAD)
    rf = heads(r.astype(f32))
    kf = heads(k.astype(f32))
    vf = heads(v.astype(f32))
    ah = heads(a)
    kk = kf * lp['rwkv_k_k'].astype(f32).reshape(hn)
    kk = kk / jnp.maximum(jnp.sqrt(jnp.sum(kk * kk, axis=-1, keepdims=True)), 1e-12)
    kf = kf * (1.0 + (ah - 1.0) * lp['rwkv_k_a'].astype(f32).reshape(hn))
    y, new_state = wkv7_scan(rf, heads(decay), kf, vf, -kk, kk * ah, state.astype(f32))
    mu = jnp.mean(y, axis=-1, keepdims=True)
    var = jnp.mean(jnp.square(y - mu), axis=-1, keepdims=True)
    yn = (y - mu) * lax.rsqrt(var + GN_EPS) * lp['rwkv_gn_g'].astype(f32).reshape(hn) \
        + lp['rwkv_gn_b'].astype(f32).reshape(hn)
    yn = yn + jnp.sum(rf * kf * lp['rwkv_r_k'].astype(f32), axis=-1, keepdims=True) * vf
    out = (yn.reshape(b, t, RWKV_DIM) * g).astype(p.dtype)
    return out, p[:, -1:], new_state


def conv_ffn(h, buf, lp):
    t = h.shape[1]
    u = h @ lp['w_ffn_up']
    u_c = u[..., :D_FF]
    u_v = u[..., D_FF:]
    padded = jnp.concatenate([buf.astype(u.dtype), u_c], axis=1)
    conv = lp['ffn_conv_b'] + padded[:, 0:t] * lp['ffn_conv_w'][0]
    for i in range(1, CONV_W):
        conv = conv + padded[:, i:i + t] * lp['ffn_conv_w'][i]
    y = jax.nn.silu(conv) * u_v
    return y @ lp['w_ffn_down'], padded[:, -(CONV_W - 1):]


def trunk_layer(x, c, kv_cache, shift_prev, rwkv_state, conv_buf, lp):
    b, t, _ = x.shape
    mod = jax.nn.silu(c) @ lp['w_ada'] + lp['b_ada']
    sh1, sc1, g1, sh2, sc2, g2 = jnp.split(mod[:, None, :], 6, axis=-1)
    h = x * (1 + sc1) + sh1
    proj = h @ lp['w_in']
    hs = (b, t, ATT_HEADS, ATT_HEAD_DIM)
    q = proj[..., 0:ATT_DIM].reshape(hs)
    k = proj[..., ATT_DIM:2 * ATT_DIM].reshape(hs)
    v = proj[..., 2 * ATT_DIM:3 * ATT_DIM].reshape(hs)
    if kv_cache is None:
        att = chunk_band_attention(q, k, v, lp['attn_rel_bias'])
    else:
        att = cached_chunk_attention(q, k, v, kv_cache[0], kv_cache[1], lp['attn_rel_bias'])
    rw_out, new_shift, new_state = rwkv7_time_mix(proj[..., ATT_COLS:ATT_COLS + RW_COLS],
                                                  shift_prev, rwkv_state, lp)
    gates = jax.nn.sigmoid(proj[..., ATT_COLS + RW_COLS:].astype(jnp.float32)).astype(x.dtype)
    merged = gates[..., :D_MODEL] * (att.reshape(b, t, ATT_DIM) @ lp['w_branch_attn']) \
        + gates[..., D_MODEL:] * (rw_out @ lp['w_branch_rwkv'])
    mix = merged @ lp['w_out']
    x = layer_norm(ALPHA * x + (1 + g1) * mix, lp['ln1_g'], lp['ln1_b'])
    h2 = x * (1 + sc2) + sh2
    ff, new_conv = conv_ffn(h2, conv_buf, lp)
    x = layer_norm(ALPHA * x + (1 + g2) * ff, lp['ln2_g'], lp['ln2_b'])
    return x, k, v, new_shift, new_state, new_conv


def setup_inputs(seed: int = 0) -> dict:
    key = jax.random.key(seed)
    ks = iter(jax.random.split(key, 64))
    f32 = jnp.float32
    L = DEPTH
    D = D_MODEL

    def nrm(shape, scale):
        return jax.random.normal(next(ks), shape, f32) * scale

    att_rows = min(ATT_REACH, PAST_LEN)
    col_scale = jnp.ones((IN_COLS,), f32).at[2 * ATT_DIM:3 * ATT_DIM].set(BETA) \
        .at[ATT_COLS + 2 * RWKV_DIM:ATT_COLS + 3 * RWKV_DIM].set(BETA)
    return {
        'x_prompt': nrm((BATCH, SEQ, D), 1.0),
        'x_sample': nrm((DEC_BATCH, DEC_SEQ, D), 1.0),
        'cache_attn_k': nrm((L, DEC_BATCH, att_rows, ATT_HEADS, ATT_HEAD_DIM), 1.0),
        'cache_attn_v': nrm((L, DEC_BATCH, att_rows, ATT_HEADS, ATT_HEAD_DIM), 1.0),
        'state_rwkv': nrm((L, DEC_BATCH, RWKV_HEADS, RWKV_HEAD, RWKV_HEAD), 0.3),
        'state_shift': nrm((L, DEC_BATCH, 1, RW_COLS), 1.0),
        'state_conv': nrm((L, DEC_BATCH, CONV_W - 1, D_FF), 1.0),
        'c_prompt': nrm((BATCH, D), 1.0),
        'c_sample': nrm((DEC_BATCH, D), 1.0),
        'ln_in_g': 1.0 + nrm((D,), 0.02),
        'ln_in_b': nrm((D,), 0.02),
        'w_ada': nrm((L, D, 6 * D), 0.1 * D ** -0.5),
        'b_ada': nrm((L, 6 * D), 0.02),
        'w_in': nrm((L, D, IN_COLS), D ** -0.5) * col_scale,
        'attn_rel_bias': nrm((L, ATT_HEADS, N_REL), 0.1),
        'rwkv_mu': jax.random.uniform(next(ks), (L, RW_COLS), f32),
        'rwkv_w0': jax.random.uniform(next(ks), (L, RWKV_DIM), f32, -6.0, 1.0),
        'rwkv_w_up': nrm((L, LORA_W, RWKV_DIM), 0.1 * LORA_W ** -0.5),
        'rwkv_a0': nrm((L, RWKV_DIM), 0.1),
        'rwkv_a_up': nrm((L, LORA_A, RWKV_DIM), 0.1 * LORA_A ** -0.5),
        'rwkv_g_up': nrm((L, LORA_G, RWKV_DIM), LORA_G ** -0.5),
        'rwkv_k_k': 0.85 + nrm((L, RWKV_DIM), 0.02),
        'rwkv_k_a': 1.0 + nrm((L, RWKV_DIM), 0.02),
        'rwkv_r_k': nrm((L, RWKV_HEADS, RWKV_HEAD), 0.1),
        'rwkv_gn_g': 1.0 + nrm((L, RWKV_DIM), 0.02),
        'rwkv_gn_b': nrm((L, RWKV_DIM), 0.02),
        'w_branch_attn': nrm((L, ATT_DIM, D), BETA * ATT_DIM ** -0.5),
        'w_branch_rwkv': nrm((L, RWKV_DIM, D), BETA * RWKV_DIM ** -0.5),
        'w_out': nrm((L, D, D), BETA * D ** -0.5),
        'ln1_g': 1.0 + nrm((L, D), 0.02),
        'ln1_b': nrm((L, D), 0.02),
        'ln2_g': 1.0 + nrm((L, D), 0.02),
        'ln2_b': nrm((L, D), 0.02),
        'w_ffn_up': nrm((L, D, 2 * D_FF), BETA * D ** -0.5),
        'ffn_conv_w': nrm((L, CONV_W, D_FF), CONV_W ** -0.5),
        'ffn_conv_b': nrm((L, D_FF), 0.02),
        'w_ffn_down': nrm((L, D_FF, D), BETA * D_FF ** -0.5),
    }


def reference(x_prompt, x_sample, cache_attn_k, cache_attn_v, state_rwkv, state_shift, state_conv,
              c_prompt, c_sample, ln_in_g, ln_in_b, w_ada, b_ada, w_in, attn_rel_bias,
              rwkv_mu, rwkv_w0, rwkv_w_up, rwkv_a0, rwkv_a_up, rwkv_g_up, rwkv_k_k, rwkv_k_a,
              rwkv_r_k, rwkv_gn_g, rwkv_gn_b, w_branch_attn, w_branch_rwkv, w_out,
              ln1_g, ln1_b, ln2_g, ln2_b, w_ffn_up, ffn_conv_w, ffn_conv_b, w_ffn_down):
    bp, sp, _ = x_prompt.shape
    keep = min(ATT_REACH, sp)
    xp = layer_norm(x_prompt, ln_in_g, ln_in_b)
    xs = layer_norm(x_sample, ln_in_g, ln_in_b)
    kp_l, vp_l, ks_l, vs_l = [], [], [], []
    rp_l, rs_l, shp_l, shs_l, cvp_l, cvs_l = [], [], [], [], [], []
    for l in range(DEPTH):
        lp = dict(w_ada=w_ada[l], b_ada=b_ada[l], w_in=w_in[l], attn_rel_bias=attn_rel_bias[l],
                  rwkv_mu=rwkv_mu[l], rwkv_w0=rwkv_w0[l], rwkv_w_up=rwkv_w_up[l],
                  rwkv_a0=rwkv_a0[l], rwkv_a_up=rwkv_a_up[l], rwkv_g_up=rwkv_g_up[l],
                  rwkv_k_k=rwkv_k_k[l], rwkv_k_a=rwkv_k_a[l], rwkv_r_k=rwkv_r_k[l],
                  rwkv_gn_g=rwkv_gn_g[l], rwkv_gn_b=rwkv_gn_b[l],
                  w_branch_attn=w_branch_attn[l], w_branch_rwkv=w_branch_rwkv[l], w_out=w_out[l],
                  ln1_g=ln1_g[l], ln1_b=ln1_b[l], ln2_g=ln2_g[l], ln2_b=ln2_b[l],
                  w_ffn_up=w_ffn_up[l], ffn_conv_w=ffn_conv_w[l], ffn_conv_b=ffn_conv_b[l],
                  w_ffn_down=w_ffn_down[l])
        z_shift = jnp.zeros((bp, 1, RW_COLS), xp.dtype)
        z_state = jnp.zeros((bp, RWKV_HEADS, RWKV_HEAD, RWKV_HEAD), jnp.float32)
        z_conv = jnp.zeros((bp, CONV_W - 1, D_FF), xp.dtype)
        xp, kp, vp, shp, stp, cvp = trunk_layer(xp, c_prompt, None, z_shift, z_state, z_conv, lp)
        xs, kn, vn, shs, sts, cvs = trunk_layer(xs, c_sample, (cache_attn_k[l], cache_attn_v[l]),
                                                state_shift[l], state_rwkv[l], state_conv[l], lp)
        kp_l.append(kp[:, sp - keep:])
        vp_l.append(vp[:, sp - keep:])
        ks_l.append(kn)
        vs_l.append(vn)
        rp_l.append(stp.astype(xp.dtype))
        rs_l.append(sts.astype(xs.dtype))
        shp_l.append(shp)
        shs_l.append(shs)
        cvp_l.append(cvp)
        cvs_l.append(cvs)
    new_k_prompt = jnp.stack(kp_l)
    new_v_prompt = jnp.stack(vp_l)
    new_k_sample = jnp.stack(ks_l)
    new_v_sample = jnp.stack(vs_l)
    rwkv_state_prompt = jnp.stack(rp_l)
    rwkv_state_sample = jnp.stack(rs_l)
    shift_prompt = jnp.stack(shp_l)
    shift_sample = jnp.stack(shs_l)
    conv_prompt = jnp.stack(cvp_l)
    conv_sample = jnp.stack(cvs_l)
    return (xp, xs, new_k_prompt, new_v_prompt, new_k_sample, new_v_sample,
            rwkv_state_prompt, rwkv_state_sample, shift_prompt, shift_sample,
            conv_prompt, conv_sample)
```

```python
import functools

import jax
import jax.numpy as jnp
from jax import lax
from jax.experimental import pallas as pl
from jax.experimental.pallas import tpu as pltpu

F32 = jnp.float32
BF16 = jnp.bfloat16

D_MODEL = 1024
CHUNK = 64
ATT_REACH = 512
BAND = ATT_REACH + CHUNK
N_HEADS = 8
HEAD = 64
HEADS_DIM = N_HEADS * HEAD
REL_CLIP = 128
LORA_W = 64
LORA_A = 64
LORA_G = 128
ATT_COLS = 3 * HEADS_DIM
RW_COLS = 3 * HEADS_DIM + LORA_W + LORA_A + LORA_G
GATE_COLS = 2 * D_MODEL
D_FF = 2816
CONV_W = 3
LN_EPS = 1e-5
GN_EPS = 64e-5
DEPTH = 1
ALPHA = (2 * DEPTH) ** 0.25

GROUP = 256
VMEM_LIMIT = 56 * 1024 * 1024


def _const_spec(shape):
    nd = len(shape)
    return pl.BlockSpec(shape, lambda *_: (0,) * nd, pipeline_mode=pl.Buffered(1))


def _layer_norm(x, g, b):
    mu = jnp.mean(x, axis=-1, keepdims=True)
    xc = x - mu
    var = jnp.mean(xc * xc, axis=-1, keepdims=True)
    return xc * lax.rsqrt(var + LN_EPS) * g + b


def _sigmoid(x):
    return 1.0 / (1.0 + jnp.exp(-x))


def _split3(x):
    hi = x.astype(BF16)
    r1 = x - hi.astype(F32)
    mid = r1.astype(BF16)
    lo = (r1 - mid.astype(F32)).astype(BF16)
    return hi, mid, lo


def _dot(a, b):
    return jnp.dot(a, b, preferred_element_type=F32)


def _dot_nt(a, b):
    return lax.dot_general(a, b, (((1,), (1,)), ((), ())), preferred_element_type=F32)


def _dot_tn(a, b):
    return lax.dot_general(a, b, (((0,), (0,)), ((), ())), preferred_element_type=F32)


def _mod_kernel(c_ref, w_ref, b_ref, o_ref):
    c = c_ref[...]
    s = (c * _sigmoid(c)).astype(BF16)
    o_ref[...] = _dot(s, w_ref[...].astype(BF16)) + b_ref[...]


def _mod_call(c_all, w_ada, b_ada):
    n = c_all.shape[0]
    nblk = 6
    return pl.pallas_call(
        _mod_kernel,
        grid=(nblk,),
        in_specs=[
            pl.BlockSpec((n, D_MODEL), lambda i: (0, 0)),
            pl.BlockSpec((D_MODEL, D_MODEL), lambda i: (0, i)),
            pl.BlockSpec((1, D_MODEL), lambda i: (0, i)),
        ],
        out_specs=pl.BlockSpec((n, D_MODEL), lambda i: (0, i)),
        out_shape=jax.ShapeDtypeStruct((n, 6 * D_MODEL), F32),
        compiler_params=pltpu.CompilerParams(dimension_semantics=("arbitrary",)),
        name="mod",
    )(c_all, w_ada, b_ada)


def _inproj_kernel(x_ref, mod_ref, lng_ref, lnb_ref, w_ref,
                   q_ref, k_ref, v_ref, kv_ref, p_ref, g_ref):
    xn = _layer_norm(x_ref[...], lng_ref[...], lnb_ref[...])
    hb = (xn * (1.0 + mod_ref[1:2, :]) + mod_ref[0:1, :]).astype(BF16)

    def seg(a, b):
        return _dot(hb, w_ref[:, a:b])

    q_ref[...] = (seg(0, HEADS_DIM) * (HEAD ** -0.5)).astype(BF16)
    k = seg(HEADS_DIM, 2 * HEADS_DIM)
    k_ref[...] = k.astype(BF16)
    kv_ref[:, 0:HEADS_DIM] = k
    v = seg(2 * HEADS_DIM, 3 * HEADS_DIM)
    v_ref[...] = v.astype(BF16)
    kv_ref[:, HEADS_DIM:2 * HEADS_DIM] = v
    for c in range(0, RW_COLS, 256):
        p_ref[:, c:c + 256] = seg(ATT_COLS + c, ATT_COLS + c + 256)
    g0 = ATT_COLS + RW_COLS
    for c in range(0, GATE_COLS, 512):
        g_ref[:, c:c + 512] = _sigmoid(seg(g0 + c, g0 + c + 512))


def _inproj_call(x2d, mod, ln_g, ln_b, w_in_b, n_seq, steps, tm, keep_last):
    rows = x2d.shape[0]
    in_cols = w_in_b.shape[1]
    row_map = lambda s, j: (s * steps + j, 0)
    if keep_last:
        keep_blocks = ATT_REACH // tm
        kv_rows = ATT_REACH
        kv_map = lambda s, j: (jnp.maximum(j - (steps - keep_blocks), 0), 0)
    else:
        kv_rows = rows
        kv_map = row_map
    return pl.pallas_call(
        _inproj_kernel,
        grid=(n_seq, steps),
        in_specs=[
            pl.BlockSpec((tm, D_MODEL), row_map),
            pl.BlockSpec((None, 6, D_MODEL), lambda s, j: (s, 0, 0)),
            _const_spec((1, D_MODEL)),
            _const_spec((1, D_MODEL)),
            _const_spec((D_MODEL, in_cols)),
        ],
        out_specs=[
            pl.BlockSpec((tm, HEADS_DIM), row_map),
            pl.BlockSpec((tm, HEADS_DIM), row_map),
            pl.BlockSpec((tm, HEADS_DIM), row_map),
            pl.BlockSpec((tm, 2 * HEADS_DIM), kv_map),
            pl.BlockSpec((tm, RW_COLS), row_map),
            pl.BlockSpec((tm, GATE_COLS), row_map),
        ],
        out_shape=[
            jax.ShapeDtypeStruct((rows, HEADS_DIM), BF16),
            jax.ShapeDtypeStruct((rows, HEADS_DIM), BF16),
            jax.ShapeDtypeStruct((rows, HEADS_DIM), BF16),
            jax.ShapeDtypeStruct((kv_rows, 2 * HEADS_DIM), F32),
            jax.ShapeDtypeStruct((rows, RW_COLS), F32),
            jax.ShapeDtypeStruct((rows, GATE_COLS), F32),
        ],
        compiler_params=pltpu.CompilerParams(
            dimension_semantics=("arbitrary", "arbitrary"), vmem_limit_bytes=VMEM_LIMIT),
        name="inproj",
    )(x2d, mod, ln_g, ln_b, w_in_b)


def _attn_chunk(qc, kb, vb, bias_ref, thr):
    lane = lax.broadcasted_iota(jnp.int32, (CHUNK, 128), 1)
    first = lane < HEAD
    outs = []
    for pr in range(N_HEADS // 2):
        sl = slice(pr * 128, (pr + 1) * 128)
        q2 = qc[:, sl].astype(F32)
        qs = jnp.concatenate([jnp.where(first, q2, 0.0), jnp.where(first, 0.0, q2)],
                             axis=0).astype(BF16)
        s = _dot_nt(qs, kb[:, sl]) + bias_ref[pr]
        if thr is not None:
            col = lax.broadcasted_iota(jnp.int32, s.shape, 1)
            s = jnp.where(col >= thr, s, -jnp.inf)
        m = jnp.max(s, axis=1, keepdims=True)
        e = jnp.exp(s - m)
        l = jnp.sum(e, axis=1, keepdims=True)
        o = _dot(e.astype(BF16), vb[:, sl]) / l
        outs.append(jnp.where(first, o[0:CHUNK], o[CHUNK:2 * CHUNK]))
    return jnp.concatenate(outs, axis=1).astype(BF16)


def _attn_prompt_kernel(q_ref, ka_ref, kb_ref, va_ref, vb_ref, bias_ref, o_ref, kbuf, vbuf,
                        *, chunks):
    tq = chunks * CHUNK
    kbuf[0:tq, :] = ka_ref[...]
    kbuf[tq:2 * tq, :] = kb_ref[...]
    vbuf[0:tq, :] = va_ref[...]
    vbuf[tq:2 * tq, :] = vb_ref[...]
    s = pl.program_id(0)

    def body(g, carry):
        r0 = pl.multiple_of(g * CHUNK, CHUNK)
        thr = ATT_REACH - (s * chunks + g) * CHUNK
        o_ref[pl.ds(r0, CHUNK), :] = _attn_chunk(
            q_ref[pl.ds(r0, CHUNK), :], kbuf[pl.ds(r0, BAND), :], vbuf[pl.ds(r0, BAND), :],
            bias_ref, thr)
        return carry

    lax.fori_loop(0, chunks, body, 0)


def _attn_prompt_call(q, kext, vext, bias):
    rows = q.shape[0]
    tq = ATT_REACH
    blk = lambda off: pl.BlockSpec((tq, HEADS_DIM), lambda s: (s + off, 0))
    return pl.pallas_call(
        functools.partial(_attn_prompt_kernel, chunks=tq // CHUNK),
        grid=(rows // tq,),
        in_specs=[blk(0), blk(0), blk(1), blk(0), blk(1), _const_spec(bias.shape)],
        out_specs=blk(0),
        out_shape=jax.ShapeDtypeStruct((rows, HEADS_DIM), BF16),
        scratch_shapes=[pltpu.VMEM((2 * tq, HEADS_DIM), BF16), pltpu.VMEM((2 * tq, HEADS_DIM), BF16)],
        compiler_params=pltpu.CompilerParams(dimension_semantics=("arbitrary",)),
        name="attn_prompt",
    )(q, kext, kext, vext, vext, bias)


def _attn_sample_kernel(q_ref, k_ref, v_ref, bias_ref, o_ref):
    o_ref[...] = _attn_chunk(q_ref[...], k_ref[...], v_ref[...], bias_ref, None)


def _attn_sample_call(q, kext, vext, bias):
    nb = kext.shape[0]
    return pl.pallas_call(
        _attn_sample_kernel,
        grid=(nb,),
        in_specs=[
            pl.BlockSpec((CHUNK, HEADS_DIM), lambda b: (b, 0)),
            pl.BlockSpec((None, BAND, HEADS_DIM), lambda b: (b, 0, 0)),
            pl.BlockSpec((None, BAND, HEADS_DIM), lambda b: (b, 0, 0)),
            _const_spec(bias.shape),
        ],
        out_specs=pl.BlockSpec((CHUNK, HEADS_DIM), lambda b: (b, 0)),
        out_shape=jax.ShapeDtypeStruct((nb * CHUNK, HEADS_DIM), BF16),
        compiler_params=pltpu.CompilerParams(dimension_semantics=("arbitrary",)),
        name="attn_sample",
    )(q, kext, vext, bias)


def _block_rows(x, same_head):
    return jnp.where(same_head, jnp.concatenate([x] * 4, axis=0), 0.0)


def _wkv_chunk_group(rt, at, kt, bt, v, fin, fout, st):
    r256 = lax.broadcasted_iota(jnp.int32, (GROUP, GROUP), 0)
    c256 = lax.broadcasted_iota(jnp.int32, (GROUP, GROUP), 1)
    same_head = (r256 // HEAD) == (c256 // HEAD)
    t64 = lax.broadcasted_iota(jnp.int32, (CHUNK, GROUP), 0)
    i64 = lax.broadcasted_iota(jnp.int32, (CHUNK, GROUP), 1) % HEAD
    strict = i64 < t64
    incl = i64 <= t64

    def blk(x):
        return _block_rows(x, same_head)

    lhs = jnp.concatenate([at, rt], axis=0).astype(BF16)
    rhs = jnp.concatenate([blk(bt), blk(kt)], axis=0).astype(BF16)
    a_all = _dot_nt(lhs, rhs)
    n = jnp.where(strict, a_all[0:CHUNK, 0:GROUP], 0.0)
    a_ak = jnp.where(strict, a_all[0:CHUNK, GROUP:], 0.0)
    a_rb = jnp.where(incl, a_all[CHUNK:, 0:GROUP], 0.0)
    a_rk = jnp.where(incl, a_all[CHUNK:, GROUP:], 0.0)

    vblk = blk(v).astype(BF16)
    z1 = at
    z2 = _dot(a_ak.astype(BF16), vblk)
    for step in range(6):
        if step < 5:
            rhs = jnp.concatenate([blk(z1), blk(z2), blk(n)], axis=1).astype(BF16)
        else:
            rhs = jnp.concatenate([blk(z1), blk(z2)], axis=1).astype(BF16)
        prod = _dot(n.astype(BF16), rhs)
        z1 = z1 + prod[:, 0:GROUP]
        z2 = z2 + prod[:, GROUP:2 * GROUP]
        if step < 5:
            n = prod[:, 2 * GROUP:]

    stt = st * fin
    sblk = blk(stt).astype(BF16)
    u = _dot(z1.astype(BF16), sblk) + z2
    ublk = blk(u).astype(BF16)
    y = _dot(jnp.concatenate([rt, a_rb, a_rk], axis=1).astype(BF16),
             jnp.concatenate([sblk, ublk, vblk], axis=0))
    g = _dot_tn(jnp.concatenate([bt, kt], axis=0).astype(BF16),
                jnp.concatenate([u, v], axis=0).astype(BF16))
    g = jnp.where(same_head, g, 0.0)
    delta = g[0:64] + g[64:128] + g[128:192] + g[192:256]
    return y, (stt + delta) * fout


def _wkv_kernel(p_ref, shift0_ref, state0_ref, mu_ref, w0_ref, wwa_ref, a0_ref, gup_ref,
                kkw_ref, kaw_ref, rkw_ref, gng_ref, gnb_ref, ones_ref, ltri_ref,
                out_ref, shift_out_ref, state_out_ref,
                st_scr, prev_scr, rt_scr, at_scr, kt_scr, bt_scr, v_scr, fin_scr, fout_scr,
                y_scr, g_scr, bon_scr, *, ts, steps):
    j = pl.program_id(1)
    nc = ts // CHUNK

    @pl.when(j == 0)
    def _():
        st_scr[...] = state0_ref[...]
        prev_scr[...] = shift0_ref[...]

    def bdsum(x):
        hi, mid, lo = _split3(x)
        ones = ones_ref[...]
        return _dot(hi, ones) + _dot(mid, ones) + _dot(lo, ones)

    row0 = lax.broadcasted_iota(jnp.int32, (ts, 1), 0) == 0

    def shifted(c0, w):
        pb = p_ref[:, c0:c0 + w]
        prev = jnp.where(row0, prev_scr[:, c0:c0 + w], pltpu.roll(pb, 1, 0))
        return pb + (prev - pb) * mu_ref[:, c0:c0 + w]

    r = shifted(0, HEADS_DIM)
    k = shifted(HEADS_DIM, HEADS_DIM)
    v = shifted(2 * HEADS_DIM, HEADS_DIM)
    lora = shifted(3 * HEADS_DIM, 256)
    lwla = lora[:, 0:128]
    lane = lax.broadcasted_iota(jnp.int32, (ts, 128), 1)
    raw = _dot(jnp.where(lane < LORA_W, jnp.tanh(lwla), lwla).astype(BF16), wwa_ref[...])
    w_pre = w0_ref[...] + raw[:, 0:HEADS_DIM]
    softplus = jnp.maximum(-w_pre, 0.0) + jnp.log(1.0 + jnp.exp(-jnp.abs(w_pre)))
    lw = -jnp.exp(-softplus - 0.5)
    a = _sigmoid(a0_ref[...] + raw[:, HEADS_DIM:])
    g_scr[...] = _dot(_sigmoid(lora[:, 128:256]).astype(BF16), gup_ref[...])

    kk = k * kkw_ref[...]
    kk = kk / jnp.maximum(jnp.sqrt(bdsum(kk * kk)), 1e-12)
    k2 = k * (1.0 + (a - 1.0) * kaw_ref[...])
    bon_scr[...] = bdsum(r * k2 * rkw_ref[...]) * v
    v_scr[...] = v

    hi, mid, lo = _split3(lw)
    ltri = ltri_ref[...]
    cum = _dot(ltri, hi) + _dot(ltri, mid) + _dot(ltri, lo)
    cum3 = cum.reshape(nc, CHUNK, HEADS_DIM)
    cmid = jnp.broadcast_to(cum3[:, CHUNK // 2 - 1:CHUNK // 2, :], cum3.shape).reshape(ts, HEADS_DIM)
    cend = jnp.broadcast_to(cum3[:, CHUNK - 1:CHUNK, :], cum3.shape).reshape(ts, HEADS_DIM)
    e_in = jnp.exp(cum - cmid)
    e_inv = jnp.exp(cmid - cum)
    rt_scr[...] = r * e_in
    at_scr[...] = -kk * jnp.exp(cum - lw - cmid)
    kt_scr[...] = k2 * e_inv
    bt_scr[...] = kk * a * e_inv
    diag = (lax.broadcasted_iota(jnp.int32, (ts, HEADS_DIM), 0) % CHUNK
            == lax.broadcasted_iota(jnp.int32, (ts, HEADS_DIM), 1) % HEAD)
    fin_scr[...] = bdsum(jnp.where(diag, jnp.exp(cmid), 0.0))
    fout_scr[...] = bdsum(jnp.where(diag, jnp.exp(cend - cmid), 0.0))

    def body(c, carry):
        rows = pl.ds(pl.multiple_of(c * CHUNK, CHUNK), CHUNK)
        for grp in range(HEADS_DIM // GROUP):
            cols = slice(grp * GROUP, (grp + 1) * GROUP)
            y, st_new = _wkv_chunk_group(
                rt_scr[rows, cols], at_scr[rows, cols], kt_scr[rows, cols], bt_scr[rows, cols],
                v_scr[rows, cols], fin_scr[rows, cols], fout_scr[rows, cols], st_scr[:, cols])
            y_scr[rows, cols] = y
            st_scr[:, cols] = st_new
        return carry

    lax.fori_loop(0, nc, body, 0)

    y = y_scr[...]
    d = y - bdsum(y) * (1.0 / HEAD)
    var = bdsum(d * d) * (1.0 / HEAD)
    yn = d * lax.rsqrt(var + GN_EPS) * gng_ref[...] + gnb_ref[...] + bon_scr[...]
    out_ref[...] = (yn * g_scr[...]).astype(BF16)

    prev_scr[...] = p_ref[ts - 1:ts, :]

    @pl.when(j == steps - 1)
    def _():
        shift_out_ref[...] = p_ref[ts - 1:ts, :]
        state_out_ref[...] = st_scr[...]


def _wkv_call(p2d, shift0, state0, prm, n_seq, steps, ts):
    rows = p2d.shape[0]
    row_map = lambda s, j: (s * steps + j, 0)
    seq3 = lambda s, j: (s, 0, 0)
    vec = _const_spec((1, HEADS_DIM))
    big = pltpu.VMEM((ts, HEADS_DIM), F32)
    return pl.pallas_call(
        functools.partial(_wkv_kernel, ts=ts, steps=steps),
        grid=(n_seq, steps),
        in_specs=[
            pl.BlockSpec((ts, RW_COLS), row_map),
            pl.BlockSpec((None, 1, RW_COLS), seq3),
            pl.BlockSpec((None, HEAD, HEADS_DIM), seq3),
            _const_spec((1, RW_COLS)),
            vec,
            _const_spec((128, 2 * HEADS_DIM)),
            vec,
            _const_spec((LORA_G, HEADS_DIM)),
            vec, vec, vec, vec, vec,
            _const_spec((HEADS_DIM, HEADS_DIM)),
            _const_spec((ts, ts)),
        ],
        out_specs=[
            pl.BlockSpec((ts, HEADS_DIM), row_map),
            pl.BlockSpec((None, 1, RW_COLS), seq3),
            pl.BlockSpec((None, HEAD, HEADS_DIM), seq3),
        ],
        out_shape=[
            jax.ShapeDtypeStruct((rows, HEADS_DIM), BF16),
            jax.ShapeDtypeStruct((n_seq, 1, RW_COLS), F32),
            jax.ShapeDtypeStruct((n_seq, HEAD, HEADS_DIM), F32),
        ],
        scratch_shapes=[
            pltpu.VMEM((HEAD, HEADS_DIM), F32),
            pltpu.VMEM((1, RW_COLS), F32),
            big, big, big, big, big, big, big, big, big, big,
        ],
        compiler_params=pltpu.CompilerParams(
            dimension_semantics=("arbitrary", "arbitrary"), vmem_limit_bytes=VMEM_LIMIT),
        name="wkv",
    )(p2d, shift0, state0, prm["mu"], prm["w0"], prm["wwa"], prm["a0"], prm["gup"],
      prm["kk"], prm["ka"], prm["rk"], prm["gng"], prm["gnb"], prm["ones"], prm["ltri"][ts])


def _mixffn_kernel(x_ref, att_ref, rw_ref, gate_ref, mod_ref, convp_ref,
                   lnig_ref, lnib_ref, ln1g_ref, ln1b_ref, ln2g_ref, ln2b_ref,
                   wa_ref, wr_ref, wo_ref, wup_ref, cw_ref, cb_ref, wdn_ref,
                   y_ref, convo_ref, carry_scr, yb_scr, *, tm, steps):
    j = pl.program_id(1)

    @pl.when(j == 0)
    def _():
        carry_scr[8 - (CONV_W - 1):8, :] = convp_ref[...]

    xn = _layer_norm(x_ref[...], lnig_ref[...], lnib_ref[...])
    merged = (gate_ref[:, 0:D_MODEL] * _dot(att_ref[...], wa_ref[...])
              + gate_ref[:, D_MODEL:] * _dot(rw_ref[...], wr_ref[...]))
    mix = _dot(merged.astype(BF16), wo_ref[...])
    x1 = _layer_norm(ALPHA * xn + (1.0 + mod_ref[2:3, :]) * mix, ln1g_ref[...], ln1b_ref[...])
    h2 = (x1 * (1.0 + mod_ref[4:5, :]) + mod_ref[3:4, :]).astype(BF16)

    cw_blk = 256
    row = lax.broadcasted_iota(jnp.int32, (tm, cw_blk), 0)
    for c in range(0, D_FF, cw_blk):
        cs = slice(c, c + cw_blk)
        uc = _dot(h2, wup_ref[:, cs])
        uv = _dot(h2, wup_ref[:, D_FF + c:D_FF + c + cw_blk])
        c6 = carry_scr[6:7, cs]
        c7 = carry_scr[7:8, cs]
        s1 = jnp.where(row == 0, c7, pltpu.roll(uc, 1, 0))
        s2 = jnp.where(row == 0, c6, jnp.where(row == 1, c7, pltpu.roll(uc, 2, 0)))
        conv = cb_ref[:, cs] + s2 * cw_ref[0:1, cs] + s1 * cw_ref[1:2, cs] + uc * cw_ref[2:3, cs]
        yb_scr[:, cs] = (conv * _sigmoid(conv) * uv).astype(BF16)
        carry_scr[:, cs] = uc[tm - 8:tm, :]

        @pl.when(j == steps - 1)
        def _():
            convo_ref[:, cs] = uc[tm - (CONV_W - 1):tm, :]

    ff = _dot(yb_scr[...], wdn_ref[...])
    y_ref[...] = _layer_norm(ALPHA * x1 + (1.0 + mod_ref[5:6, :]) * ff, ln2g_ref[...], ln2b_ref[...])


def _mixffn_call(x2d, att, rw, gates, mod, conv_prev, prm, n_seq, steps, tm):
    rows = x2d.shape[0]
    row_map = lambda s, j: (s * steps + j, 0)
    seq3 = lambda s, j: (s, 0, 0)
    vec = _const_spec((1, D_MODEL))
    return pl.pallas_call(
        functools.partial(_mixffn_kernel, tm=tm, steps=steps),
        grid=(n_seq, steps),
        in_specs=[
            pl.BlockSpec((tm, D_MODEL), row_map),
            pl.BlockSpec((tm, HEADS_DIM), row_map),
            pl.BlockSpec((tm, HEADS_DIM), row_map),
            pl.BlockSpec((tm, GATE_COLS), row_map),
            pl.BlockSpec((None, 6, D_MODEL), seq3),
            pl.BlockSpec((None, CONV_W - 1, D_FF), seq3),
            vec, vec, vec, vec, vec, vec,
            _const_spec((HEADS_DIM, D_MODEL)),
            _const_spec((HEADS_DIM, D_MODEL)),
            _const_spec((D_MODEL, D_MODEL)),
            _const_spec((D_MODEL, 2 * D_FF)),
            _const_spec((CONV_W, D_FF)),
            _const_spec((1, D_FF)),
            _const_spec((D_FF, D_MODEL)),
        ],
        out_specs=[
            pl.BlockSpec((tm, D_MODEL), row_map),
            pl.BlockSpec((None, CONV_W - 1, D_FF), seq3),
        ],
        out_shape=[
            jax.ShapeDtypeStruct((rows, D_MODEL), F32),
            jax.ShapeDtypeStruct((n_seq, CONV_W - 1, D_FF), F32),
        ],
        scratch_shapes=[pltpu.VMEM((8, D_FF), F32), pltpu.VMEM((tm, D_FF), BF16)],
        compiler_params=pltpu.CompilerParams(
            dimension_semantics=("arbitrary", "arbitrary"), vmem_limit_bytes=VMEM_LIMIT),
        name="mixffn",
    )(x2d, att, rw, gates, mod, conv_prev,
      prm["lnig"], prm["lnib"], prm["ln1g"], prm["ln1b"], prm["ln2g"], prm["ln2b"],
      prm["wa"], prm["wr"], prm["wo"], prm["wup"], prm["cw"], prm["cb"], prm["wdn"])


def _pair_bias(table):
    dist = ATT_REACH + jnp.arange(CHUNK)[:, None] - jnp.arange(BAND)[None, :]
    bias = table[:, jnp.clip(dist, -REL_CLIP, REL_CLIP) + REL_CLIP].astype(F32)
    return bias.reshape(N_HEADS // 2, 2 * CHUNK, BAND)


def _chunk_ltri(ts):
    t = jnp.arange(ts)
    return ((t[:, None] // CHUNK == t[None, :] // CHUNK) & (t[None, :] <= t[:, None])).astype(BF16)


def _trunk(x2d, mod, shift0, state0, conv_prev, kext_fn, prm, n_seq, steps, tm, prompt):
    q, k, v, kv32, p, gates = _inproj_call(
        x2d, mod, prm["lnig"], prm["lnib"], prm["win"], n_seq, steps, tm, keep_last=prompt)
    kext, vext = kext_fn(k, v)
    if prompt:
        att = _attn_prompt_call(q, kext, vext, prm["bias"])
    else:
        att = _attn_sample_call(q, kext, vext, prm["bias"])
    rw, shift, state = _wkv_call(p, shift0, state0, prm, n_seq, steps, tm)
    y, conv = _mixffn_call(x2d, att, rw, gates, mod, conv_prev, prm, n_seq, steps, tm)
    return y, kv32, state, shift, conv


def kernel(x_prompt, x_sample, cache_attn_k, cache_attn_v, state_rwkv, state_shift, state_conv,
           c_prompt, c_sample, ln_in_g, ln_in_b, w_ada, b_ada, w_in, attn_rel_bias,
           rwkv_mu, rwkv_w0, rwkv_w_up, rwkv_a0, rwkv_a_up, rwkv_g_up, rwkv_k_k, rwkv_k_a,
           rwkv_r_k, rwkv_gn_g, rwkv_gn_b, w_branch_attn, w_branch_rwkv, w_out,
           ln1_g, ln1_b, ln2_g, ln2_b, w_ffn_up, ffn_conv_w, ffn_conv_b, w_ffn_down):
    bp, sp, _ = x_prompt.shape
    bs, ss, _ = x_sample.shape
    assert bp == 1 and ss == CHUNK and w_ada.shape[0] == DEPTH
    tm_p = 512
    assert sp % tm_p == 0 and cache_attn_k.shape[2] == ATT_REACH

    row = lambda a: a.reshape(1, -1)
    wwa = jnp.zeros((LORA_W + LORA_A, 2 * HEADS_DIM), F32)
    wwa = wwa.at[:LORA_W, :HEADS_DIM].set(rwkv_w_up[0]).at[LORA_W:, HEADS_DIM:].set(rwkv_a_up[0])
    head_id = jnp.arange(HEADS_DIM) // HEAD
    prm = dict(
        lnig=row(ln_in_g), lnib=row(ln_in_b),
        ln1g=row(ln1_g[0]), ln1b=row(ln1_b[0]), ln2g=row(ln2_g[0]), ln2b=row(ln2_b[0]),
        win=w_in[0].astype(BF16), bias=_pair_bias(attn_rel_bias[0]),
        mu=row(rwkv_mu[0]), w0=row(rwkv_w0[0]), wwa=wwa.astype(BF16), a0=row(rwkv_a0[0]),
        gup=rwkv_g_up[0].astype(BF16), kk=row(rwkv_k_k[0]), ka=row(rwkv_k_a[0]),
        rk=row(rwkv_r_k[0]), gng=row(rwkv_gn_g[0]), gnb=row(rwkv_gn_b[0]),
        ones=(head_id[:, None] == head_id[None, :]).astype(BF16),
        ltri={tm_p: _chunk_ltri(tm_p), CHUNK: _chunk_ltri(CHUNK)},
        wa=w_branch_attn[0].astype(BF16), wr=w_branch_rwkv[0].astype(BF16),
        wo=w_out[0].astype(BF16), wup=w_ffn_up[0].astype(BF16),
        cw=ffn_conv_w[0], cb=row(ffn_conv_b[0]), wdn=w_ffn_down[0].astype(BF16),
    )

    n_c = bp + bs
    c_all = jnp.concatenate([c_prompt, c_sample, jnp.zeros((16 - n_c, D_MODEL), F32)], axis=0)
    mod = _mod_call(c_all, w_ada[0], row(b_ada[0])).reshape(16, 6, D_MODEL)

    def kext_prompt(k, v):
        pad = jnp.zeros((ATT_REACH, HEADS_DIM), BF16)
        return jnp.concatenate([pad, k], axis=0), jnp.concatenate([pad, v], axis=0)

    y_p, kv_p, st_p, sh_p, cv_p = _trunk(
        x_prompt.reshape(sp, D_MODEL), mod[0:bp],
        jnp.zeros((bp, 1, RW_COLS), F32), jnp.zeros((bp, HEAD, HEADS_DIM), F32),
        jnp.zeros((bp, CONV_W - 1, D_FF), F32), kext_prompt, prm,
        n_seq=bp, steps=sp // tm_p, tm=tm_p, prompt=True)

    def kext_sample(k, v):
        ck = cache_attn_k[0].reshape(bs, ATT_REACH, HEADS_DIM).astype(BF16)
        cv = cache_attn_v[0].reshape(bs, ATT_REACH, HEADS_DIM).astype(BF16)
        return (jnp.concatenate([ck, k.reshape(bs, ss, HEADS_DIM)], axis=1),
                jnp.concatenate([cv, v.reshape(bs, ss, HEADS_DIM)], axis=1))

    st0 = jnp.transpose(state_rwkv[0], (0, 3, 1, 2)).reshape(bs, HEAD, HEADS_DIM)
    y_s, kv_s, st_s, sh_s, cv_s = _trunk(
        x_sample.reshape(bs * ss, D_MODEL), mod[bp:n_c],
        state_shift[0], st0, state_conv[0], kext_sample, prm,
        n_seq=bs, steps=1, tm=ss, prompt=False)

    def state_out(st, b):
        return jnp.transpose(st.reshape(b, HEAD, N_HEADS, HEAD), (0, 2, 3, 1))[None]

    hs = (N_HEADS, HEAD)
    return (
        y_p.reshape(bp, sp, D_MODEL),
        y_s.reshape(bs, ss, D_MODEL),
        kv_p[:, :HEADS_DIM].reshape(1, bp, ATT_REACH, *hs),
        kv_p[:, HEADS_DIM:].reshape(1, bp, ATT_REACH, *hs),
        kv_s[:, :HEADS_DIM].reshape(1, bs, ss, *hs),
        kv_s[:, HEADS_DIM:].reshape(1, bs, ss, *hs),
        state_out(st_p, bp),
        state_out(st_s, bs),
        sh_p[None],
        sh_s[None],
        cv_p[None],
        cv_s[None],
    )
```

```python
import functools

import jax
import jax.numpy as jnp
from jax import lax
from jax.experimental import pallas as pl
from jax.experimental.pallas import tpu as pltpu

F32 = jnp.float32
BF16 = jnp.bfloat16

D_MODEL = 1024
CHUNK = 64
ATT_REACH = 512
BAND = ATT_REACH + CHUNK
N_HEADS = 8
HEAD = 64
HEADS_DIM = N_HEADS * HEAD
REL_CLIP = 128
LORA_W = 64
LORA_A = 64
LORA_G = 128
ATT_COLS = 3 * HEADS_DIM
RW_COLS = 3 * HEADS_DIM + LORA_W + LORA_A + LORA_G
GATE_COLS = 2 * D_MODEL
D_FF = 2816
CONV_W = 3
LN_EPS = 1e-5
GN_EPS = 64e-5
DEPTH = 1
ALPHA = (2 * DEPTH) ** 0.25

GROUP = 256
VMEM_LIMIT = 56 * 1024 * 1024


def _const_spec(shape):
    nd = len(shape)
    return pl.BlockSpec(shape, lambda *_: (0,) * nd, pipeline_mode=pl.Buffered(1))


def _layer_norm(x, g, b):
    mu = jnp.mean(x, axis=-1, keepdims=True)
    xc = x - mu
    var = jnp.mean(xc * xc, axis=-1, keepdims=True)
    return xc * lax.rsqrt(var + LN_EPS) * g + b


def _sigmoid(x):
    return 1.0 / (1.0 + jnp.exp(-x))


def _split3(x):
    hi = x.astype(BF16)
    r1 = x - hi.astype(F32)
    mid = r1.astype(BF16)
    lo = (r1 - mid.astype(F32)).astype(BF16)
    return hi, mid, lo


def _dot(a, b):
    return jnp.dot(a, b, preferred_element_type=F32)


def _dot_nt(a, b):
    return lax.dot_general(a, b, (((1,), (1,)), ((), ())), preferred_element_type=F32)


def _dot_tn(a, b):
    return lax.dot_general(a, b, (((0,), (0,)), ((), ())), preferred_element_type=F32)


def _mod_kernel(c_ref, w_ref, b_ref, o_ref):
    c = c_ref[...]
    s = (c * _sigmoid(c)).astype(BF16)
    o_ref[...] = _dot(s, w_ref[...].astype(BF16)) + b_ref[...]


def _mod_call(c_all, w_ada, b_ada):
    n = c_all.shape[0]
    nblk = 6
    return pl.pallas_call(
        _mod_kernel,
        grid=(nblk,),
        in_specs=[
            pl.BlockSpec((n, D_MODEL), lambda i: (0, 0)),
            pl.BlockSpec((D_MODEL, D_MODEL), lambda i: (0, i)),
            pl.BlockSpec((1, D_MODEL), lambda i: (0, i)),
        ],
        out_specs=pl.BlockSpec((n, D_MODEL), lambda i: (0, i)),
        out_shape=jax.ShapeDtypeStruct((n, 6 * D_MODEL), F32),
        compiler_params=pltpu.CompilerParams(dimension_semantics=("arbitrary",)),
        name="mod",
    )(c_all, w_ada, b_ada)


def _inproj_kernel(x_ref, mod_ref, lng_ref, lnb_ref, w_ref,
                   q_ref, k_ref, v_ref, kv_ref, p_ref, g_ref):
    xn = _layer_norm(x_ref[...], lng_ref[...], lnb_ref[...])
    hb = (xn * (1.0 + mod_ref[1:2, :]) + mod_ref[0:1, :]).astype(BF16)

    def seg(a, b):
        return _dot(hb, w_ref[:, a:b])

    q_ref[...] = (seg(0, HEADS_DIM) * (HEAD ** -0.5)).astype(BF16)
    k = seg(HEADS_DIM, 2 * HEADS_DIM)
    k_ref[...] = k.astype(BF16)
    kv_ref[:, 0:HEADS_DIM] = k
    v = seg(2 * HEADS_DIM, 3 * HEADS_DIM)
    v_ref[...] = v.astype(BF16)
    kv_ref[:, HEADS_DIM:2 * HEADS_DIM] = v
    for c in range(0, RW_COLS, 256):
        p_ref[:, c:c + 256] = seg(ATT_COLS + c, ATT_COLS + c + 256)
    g0 = ATT_COLS + RW_COLS
    for c in range(0, GATE_COLS, 512):
        g_ref[:, c:c + 512] = _sigmoid(seg(g0 + c, g0 + c + 512))


def _inproj_call(x2d, mod, ln_g, ln_b, w_in_b, n_seq, steps, tm, keep_last):
    rows = x2d.shape[0]
    in_cols = w_in_b.shape[1]
    row_map = lambda s, j: (s * steps + j, 0)
    if keep_last:
        keep_blocks = ATT_REACH // tm
        kv_rows = ATT_REACH
        kv_map = lambda s, j: (jnp.maximum(j - (steps - keep_blocks), 0), 0)
    else:
        kv_rows = rows
        kv_map = row_map
    return pl.pallas_call(
        _inproj_kernel,
        grid=(n_seq, steps),
        in_specs=[
            pl.BlockSpec((tm, D_MODEL), row_map),
            pl.BlockSpec((None, 6, D_MODEL), lambda s, j: (s, 0, 0)),
            _const_spec((1, D_MODEL)),
            _const_spec((1, D_MODEL)),
            _const_spec((D_MODEL, in_cols)),
        ],
        out_specs=[
            pl.BlockSpec((tm, HEADS_DIM), row_map),
            pl.BlockSpec((tm, HEADS_DIM), row_map),
            pl.BlockSpec((tm, HEADS_DIM), row_map),
            pl.BlockSpec((tm, 2 * HEADS_DIM), kv_map),
            pl.BlockSpec((tm, RW_COLS), row_map),
            pl.BlockSpec((tm, GATE_COLS), row_map),
        ],
        out_shape=[
            jax.ShapeDtypeStruct((rows, HEADS_DIM), BF16),
            jax.ShapeDtypeStruct((rows, HEADS_DIM), BF16),
            jax.ShapeDtypeStruct((rows, HEADS_DIM), BF16),
            jax.ShapeDtypeStruct((kv_rows, 2 * HEADS_DIM), F32),
            jax.ShapeDtypeStruct((rows, RW_COLS), F32),
            jax.ShapeDtypeStruct((rows, GATE_COLS), F32),
        ],
        compiler_params=pltpu.CompilerParams(
            dimension_semantics=("arbitrary", "arbitrary"), vmem_limit_bytes=VMEM_LIMIT),
        name="inproj",
    )(x2d, mod, ln_g, ln_b, w_in_b)


def _attn_chunk(qc, kb, vb, bias_ref, thr):
    lane = lax.broadcasted_iota(jnp.int32, (CHUNK, 128), 1)
    first = lane < HEAD
    outs = []
    for pr in range(N_HEADS // 2):
        sl = slice(pr * 128, (pr + 1) * 128)
        q2 = qc[:, sl].astype(F32)
        qs = jnp.concatenate([jnp.where(first, q2, 0.0), jnp.where(first, 0.0, q2)],
                             axis=0).astype(BF16)
        s = _dot_nt(qs, kb[:, sl]) + bias_ref[pr]
        if thr is not None:
            col = lax.broadcasted_iota(jnp.int32, s.shape, 1)
            s = jnp.where(col >= thr, s, -jnp.inf)
        m = jnp.max(s, axis=1, keepdims=True)
        e = jnp.exp(s - m)
        l = jnp.sum(e, axis=1, keepdims=True)
        o = _dot(e.astype(BF16), vb[:, sl]) / l
        outs.append(jnp.where(first, o[0:CHUNK], o[CHUNK:2 * CHUNK]))
    return jnp.concatenate(outs, axis=1).astype(BF16)


def _attn_prompt_kernel(q_ref, ka_ref, kb_ref, va_ref, vb_ref, bias_ref, o_ref, kbuf, vbuf,
                        *, chunks):
    tq = chunks * CHUNK
    kbuf[0:tq, :] = ka_ref[...]
    kbuf[tq:2 * tq, :] = kb_ref[...]
    vbuf[0:tq, :] = va_ref[...]
    vbuf[tq:2 * tq, :] = vb_ref[...]
    s = pl.program_id(0)

    def body(g, carry):
        r0 = pl.multiple_of(g * CHUNK, CHUNK)
        thr = ATT_REACH - (s * chunks + g) * CHUNK
        o_ref[pl.ds(r0, CHUNK), :] = _attn_chunk(
            q_ref[pl.ds(r0, CHUNK), :], kbuf[pl.ds(r0, BAND), :], vbuf[pl.ds(r0, BAND), :],
            bias_ref, thr)
        return carry

    lax.fori_loop(0, chunks, body, 0)


def _attn_prompt_call(q, kext, vext, bias):
    rows = q.shape[0]
    tq = ATT_REACH
    blk = lambda off: pl.BlockSpec((tq, HEADS_DIM), lambda s: (s + off, 0))
    return pl.pallas_call(
        functools.partial(_attn_prompt_kernel, chunks=tq // CHUNK),
        grid=(rows // tq,),
        in_specs=[blk(0), blk(0), blk(1), blk(0), blk(1), _const_spec(bias.shape)],
        out_specs=blk(0),
        out_shape=jax.ShapeDtypeStruct((rows, HEADS_DIM), BF16),
        scratch_shapes=[pltpu.VMEM((2 * tq, HEADS_DIM), BF16), pltpu.VMEM((2 * tq, HEADS_DIM), BF16)],
        compiler_params=pltpu.CompilerParams(dimension_semantics=("arbitrary",)),
        name="attn_prompt",
    )(q, kext, kext, vext, vext, bias)


def _attn_sample_kernel(q_ref, k_ref, v_ref, bias_ref, o_ref):
    o_ref[...] = _attn_chunk(q_ref[...], k_ref[...], v_ref[...], bias_ref, None)


def _attn_sample_call(q, kext, vext, bias):
    nb = kext.shape[0]
    return pl.pallas_call(
        _attn_sample_kernel,
        grid=(nb,),
        in_specs=[
            pl.BlockSpec((CHUNK, HEADS_DIM), lambda b: (b, 0)),
            pl.BlockSpec((None, BAND, HEADS_DIM), lambda b: (b, 0, 0)),
            pl.BlockSpec((None, BAND, HEADS_DIM), lambda b: (b, 0, 0)),
            _const_spec(bias.shape),
        ],
        out_specs=pl.BlockSpec((CHUNK, HEADS_DIM), lambda b: (b, 0)),
        out_shape=jax.ShapeDtypeStruct((nb * CHUNK, HEADS_DIM), BF16),
        compiler_params=pltpu.CompilerParams(dimension_semantics=("arbitrary",)),
        name="attn_sample",
    )(q, kext, vext, bias)


def _same_head():
    r = lax.broadcasted_iota(jnp.int32, (GROUP, GROUP), 0) // HEAD
    c = lax.broadcasted_iota(jnp.int32, (GROUP, GROUP), 1) // HEAD
    return r == c


def _blk(x, same_head):
    return jnp.where(same_head, jnp.concatenate([x] * 4, axis=0), 0.0).astype(BF16)


def _wkv_prep(units, hooks):
    same_head = _same_head()
    t64 = lax.broadcasted_iota(jnp.int32, (CHUNK, GROUP), 0)
    i64 = lax.broadcasted_iota(jnp.int32, (CHUNK, GROUP), 1) % HEAD
    strict = i64 < t64
    incl = i64 <= t64
    pending = list(hooks)

    def stage_done():
        if pending:
            pending.pop(0)()

    def blk(x):
        return _blk(x, same_head)

    n, a_ak, a_rb, a_rk = [], [], [], []
    for ld in units:
        lhs = jnp.concatenate([ld("at"), ld("rt")], axis=0).astype(BF16)
        rhs = jnp.concatenate([blk(ld("bt")), blk(ld("kt"))], axis=0)
        a_all = _dot_nt(lhs, rhs)
        n.append(jnp.where(strict, a_all[0:CHUNK, 0:GROUP], 0.0))
        a_ak.append(jnp.where(strict, a_all[0:CHUNK, GROUP:], 0.0))
        a_rb.append(jnp.where(incl, a_all[CHUNK:, 0:GROUP], 0.0))
        a_rk.append(jnp.where(incl, a_all[CHUNK:, GROUP:], 0.0))
    stage_done()

    x0 = [_dot(a.astype(BF16), blk(ld("v"))) for a, ld in zip(a_ak, units)]
    npow = [_dot(m.astype(BF16), blk(m)) for m in n]
    t = [jnp.where(i64 == t64, 1.0, 0.0) + m for m in n]
    stage_done()
    for _ in range(4):
        prod = [_dot(jnp.concatenate([a, b], axis=0).astype(BF16), blk(b))
                for a, b in zip(t, npow)]
        t = [a + p[0:CHUNK] for a, p in zip(t, prod)]
        npow = [p[CHUNK:] for p in prod]
        stage_done()
    t = [a + _dot(a.astype(BF16), blk(b)) for a, b in zip(t, npow)]
    stage_done()
    res = [_dot(a.astype(BF16), jnp.concatenate([blk(ld("at")), blk(x)], axis=1))
           for a, x, ld in zip(t, x0, units)]
    stage_done()
    while pending:
        stage_done()
    return [(r[:, 0:GROUP], r[:, GROUP:], b, k) for r, b, k in zip(res, a_rb, a_rk)]


def _wkv_step_stages(units, states, out):
    same_head = _same_head()
    held = {}

    def first():
        held["sblk"] = [_blk(st, same_head) for st in states]
        held["u"] = [_dot(ld("ah").astype(BF16), sb) + ld("vh")
                     for ld, sb in zip(units, held["sblk"])]

    def second():
        for i, ld in enumerate(units):
            u, v = held["u"][i], ld("v")
            y = _dot(jnp.concatenate([ld("rt"), ld("arb"), ld("ark")], axis=1).astype(BF16),
                     jnp.concatenate([held["sblk"][i], _blk(u, same_head), _blk(v, same_head)],
                                     axis=0))
            g = _dot_tn(jnp.concatenate([ld("btd"), ld("ktd")], axis=0).astype(BF16),
                        jnp.concatenate([u, v], axis=0).astype(BF16))
            g = jnp.where(same_head, g, 0.0)
            delta = g[0:64] + g[64:128] + g[128:192] + g[192:256]
            out.append(y)
            states[i] = states[i] * ld("fdec") + delta

    return [first, second]


def _wkv_kernel(p_ref, shift0_ref, state0_ref, mu_ref, w0_ref, wwa_ref, a0_ref, gup_ref,
                kkw_ref, kaw_ref, rkw_ref, gng_ref, gnb_ref, ones_ref, ltri_ref,
                out_ref, shift_out_ref, state_out_ref,
                st_scr, prev_scr, rt_scr, at_scr, kt_scr, bt_scr, v_scr, btd_scr, ktd_scr,
                fdec_scr, ah_scr, vh_scr, arb_scr, ark_scr, y_scr, g_scr, bon_scr, *, ts, steps):
    j = pl.program_id(1)
    nc = ts // CHUNK

    @pl.when(j == 0)
    def _():
        st_scr[...] = state0_ref[...]
        prev_scr[...] = shift0_ref[...]

    def bdsum(x, terms):
        ones = ones_ref[...]
        halves = []
        for c0 in range(0, HEADS_DIM, GROUP):
            parts = _split3(x[:, c0:c0 + GROUP])[:terms]
            acc = _dot(parts[0], ones)
            for part in parts[1:]:
                acc = acc + _dot(part, ones)
            halves.append(acc)
        return jnp.concatenate(halves, axis=1)

    row0 = lax.broadcasted_iota(jnp.int32, (ts, 1), 0) == 0

    def shifted(c0, w):
        pb = p_ref[:, c0:c0 + w]
        prev = jnp.where(row0, prev_scr[:, c0:c0 + w], pltpu.roll(pb, 1, 0))
        return pb + (prev - pb) * mu_ref[:, c0:c0 + w]

    r = shifted(0, HEADS_DIM)
    k = shifted(HEADS_DIM, HEADS_DIM)
    v = shifted(2 * HEADS_DIM, HEADS_DIM)
    lora = shifted(3 * HEADS_DIM, 256)
    lwla = lora[:, 0:128]
    lane = lax.broadcasted_iota(jnp.int32, (ts, 128), 1)
    raw = _dot(jnp.where(lane < LORA_W, jnp.tanh(lwla), lwla).astype(BF16), wwa_ref[...])
    w_pre = w0_ref[...] + raw[:, 0:HEADS_DIM]
    softplus = jnp.maximum(-w_pre, 0.0) + jnp.log(1.0 + jnp.exp(-jnp.abs(w_pre)))
    lw = -jnp.exp(-softplus - 0.5)
    a = _sigmoid(a0_ref[...] + raw[:, HEADS_DIM:])
    g_scr[...] = _dot(_sigmoid(lora[:, 128:256]).astype(BF16), gup_ref[...])

    kk = k * kkw_ref[...]
    kk = kk / jnp.maximum(jnp.sqrt(bdsum(kk * kk, 2)), 1e-12)
    k2 = k * (1.0 + (a - 1.0) * kaw_ref[...])
    bon_scr[...] = bdsum(r * k2 * rkw_ref[...], 2) * v
    v_scr[...] = v

    ltri = ltri_ref[...]
    lrows = ltri.shape[0]
    cums = []
    for r0 in range(0, ts, lrows):
        hi, mid, lo = _split3(lw[r0:r0 + lrows])
        cums.append(_dot(ltri, hi) + _dot(ltri, mid) + _dot(ltri, lo))
    cum = jnp.concatenate(cums, axis=0)
    cum3 = cum.reshape(nc, CHUNK, HEADS_DIM)
    cend = jnp.broadcast_to(cum3[:, CHUNK - 1:CHUNK, :], cum3.shape).reshape(ts, HEADS_DIM)
    e_inv = jnp.exp(-cum)
    e_rel = jnp.exp(cend - cum)
    b = kk * a
    rt_scr[...] = r * jnp.exp(cum)
    at_scr[...] = -kk * jnp.exp(cum - lw)
    kt_scr[...] = k2 * e_inv
    bt_scr[...] = b * e_inv
    ktd_scr[...] = k2 * e_rel
    btd_scr[...] = b * e_rel
    diag = (lax.broadcasted_iota(jnp.int32, (ts, HEADS_DIM), 0) % CHUNK
            == lax.broadcasted_iota(jnp.int32, (ts, HEADS_DIM), 1) % HEAD)
    fdec_scr[...] = bdsum(jnp.where(diag, jnp.exp(cend), 0.0), 3)

    def chunk_rows(c):
        start = c * CHUNK
        if not isinstance(start, int):
            start = pl.multiple_of(start, CHUNK)
        return pl.ds(start, CHUNK)

    groups = [slice(g0, g0 + GROUP) for g0 in range(0, HEADS_DIM, GROUP)]

    named = dict(at=at_scr, rt=rt_scr, kt=kt_scr, bt=bt_scr, v=v_scr, btd=btd_scr, ktd=ktd_scr,
                 fdec=fdec_scr, ah=ah_scr, vh=vh_scr, arb=arb_scr, ark=ark_scr)
    prep_outs = (ah_scr, vh_scr, arb_scr, ark_scr)

    def loader(c, cols):
        rows = chunk_rows(c)
        return lambda name: named[name][rows, cols]

    def run(step_chunks, prep_chunks):
        states = [st_scr[:, cols] for cols in groups] if step_chunks else []
        hooks, ys = [], []
        for c in step_chunks:
            out = []
            ys.append((c, out))
            hooks += _wkv_step_stages([loader(c, cols) for cols in groups], states, out)
        units = [(c, cols) for c in prep_chunks for cols in groups]
        prep_out = _wkv_prep([loader(c, cols) for c, cols in units], hooks) if units else []
        if not units:
            for hook in hooks:
                hook()
        for c, out in ys:
            rows = chunk_rows(c)
            for cols, y in zip(groups, out):
                y_scr[rows, cols] = y
        for cols, st in zip(groups, states):
            st_scr[:, cols] = st
        for (c, cols), outs in zip(units, prep_out):
            rows = chunk_rows(c)
            for ref, val in zip(prep_outs, outs):
                ref[rows, cols] = val

    per = 2 if nc % 2 == 0 else 1
    n_pass = nc // per
    run([], list(range(per)))

    def body(i, carry):
        run([i * per + k for k in range(per)], [(i + 1) * per + k for k in range(per)])
        return carry

    lax.fori_loop(0, n_pass - 1, body, 0)
    run([nc - per + k for k in range(per)], [])

    y = y_scr[...]
    d = y - bdsum(y, 2) * (1.0 / HEAD)
    var = bdsum(d * d, 2) * (1.0 / HEAD)
    yn = d * lax.rsqrt(var + GN_EPS) * gng_ref[...] + gnb_ref[...] + bon_scr[...]
    out_ref[...] = (yn * g_scr[...]).astype(BF16)

    prev_scr[...] = p_ref[ts - 1:ts, :]

    @pl.when(j == steps - 1)
    def _():
        shift_out_ref[...] = p_ref[ts - 1:ts, :]
        state_out_ref[...] = st_scr[...]


def _wkv_call(p2d, shift0, state0, prm, n_seq, steps, ts):
    rows = p2d.shape[0]
    row_map = lambda s, j: (s * steps + j, 0)
    seq3 = lambda s, j: (s, 0, 0)
    vec = _const_spec((1, HEADS_DIM))
    big = pltpu.VMEM((ts, HEADS_DIM), F32)
    ltri = prm["ltri"][min(ts, GROUP)]
    return pl.pallas_call(
        functools.partial(_wkv_kernel, ts=ts, steps=steps),
        grid=(n_seq, steps),
        in_specs=[
            pl.BlockSpec((ts, RW_COLS), row_map),
            pl.BlockSpec((None, 1, RW_COLS), seq3),
            pl.BlockSpec((None, HEAD, HEADS_DIM), seq3),
            _const_spec((1, RW_COLS)),
            vec,
            _const_spec((128, 2 * HEADS_DIM)),
            vec,
            _const_spec((LORA_G, HEADS_DIM)),
            vec, vec, vec, vec, vec,
            _const_spec((GROUP, GROUP)),
            _const_spec(ltri.shape),
        ],
        out_specs=[
            pl.BlockSpec((ts, HEADS_DIM), row_map),
            pl.BlockSpec((None, 1, RW_COLS), seq3),
            pl.BlockSpec((None, HEAD, HEADS_DIM), seq3),
        ],
        out_shape=[
            jax.ShapeDtypeStruct((rows, HEADS_DIM), BF16),
            jax.ShapeDtypeStruct((n_seq, 1, RW_COLS), F32),
            jax.ShapeDtypeStruct((n_seq, HEAD, HEADS_DIM), F32),
        ],
        scratch_shapes=[
            pltpu.VMEM((HEAD, HEADS_DIM), F32),
            pltpu.VMEM((1, RW_COLS), F32),
        ] + [big] * 15,
        compiler_params=pltpu.CompilerParams(
            dimension_semantics=("arbitrary", "arbitrary"), vmem_limit_bytes=VMEM_LIMIT),
        name="wkv",
    )(p2d, shift0, state0, prm["mu"], prm["w0"], prm["wwa"], prm["a0"], prm["gup"],
      prm["kk"], prm["ka"], prm["rk"], prm["gng"], prm["gnb"], prm["ones"], ltri)


def _mixffn_kernel(x_ref, att_ref, rw_ref, gate_ref, mod_ref, convp_ref,
                   lnig_ref, lnib_ref, ln1g_ref, ln1b_ref, ln2g_ref, ln2b_ref,
                   wa_ref, wr_ref, wo_ref, wup_ref, cw_ref, cb_ref, wdn_ref,
                   y_ref, convo_ref, carry_scr, yb_scr, *, tm, steps):
    j = pl.program_id(1)

    @pl.when(j == 0)
    def _():
        carry_scr[8 - (CONV_W - 1):8, :] = convp_ref[...]

    xn = _layer_norm(x_ref[...], lnig_ref[...], lnib_ref[...])
    merged = (gate_ref[:, 0:D_MODEL] * _dot(att_ref[...], wa_ref[...])
              + gate_ref[:, D_MODEL:] * _dot(rw_ref[...], wr_ref[...]))
    mix = _dot(merged.astype(BF16), wo_ref[...])
    x1 = _layer_norm(ALPHA * xn + (1.0 + mod_ref[2:3, :]) * mix, ln1g_ref[...], ln1b_ref[...])
    h2 = (x1 * (1.0 + mod_ref[4:5, :]) + mod_ref[3:4, :]).astype(BF16)

    cw_blk = 256
    row = lax.broadcasted_iota(jnp.int32, (tm, cw_blk), 0)
    for c in range(0, D_FF, cw_blk):
        cs = slice(c, c + cw_blk)
        uc = _dot(h2, wup_ref[:, cs])
        uv = _dot(h2, wup_ref[:, D_FF + c:D_FF + c + cw_blk])
        c6 = carry_scr[6:7, cs]
        c7 = carry_scr[7:8, cs]
        s1 = jnp.where(row == 0, c7, pltpu.roll(uc, 1, 0))
        s2 = jnp.where(row == 0, c6, jnp.where(row == 1, c7, pltpu.roll(uc, 2, 0)))
        conv = cb_ref[:, cs] + s2 * cw_ref[0:1, cs] + s1 * cw_ref[1:2, cs] + uc * cw_ref[2:3, cs]
        yb_scr[:, cs] = (conv * _sigmoid(conv) * uv).astype(BF16)
        carry_scr[:, cs] = uc[tm - 8:tm, :]

        @pl.when(j == steps - 1)
        def _():
            convo_ref[:, cs] = uc[tm - (CONV_W - 1):tm, :]

    ff = _dot(yb_scr[...], wdn_ref[...])
    y_ref[...] = _layer_norm(ALPHA * x1 + (1.0 + mod_ref[5:6, :]) * ff, ln2g_ref[...], ln2b_ref[...])


def _mixffn_call(x2d, att, rw, gates, mod, conv_prev, prm, n_seq, steps, tm):
    rows = x2d.shape[0]
    row_map = lambda s, j: (s * steps + j, 0)
    seq3 = lambda s, j: (s, 0, 0)
    vec = _const_spec((1, D_MODEL))
    return pl.pallas_call(
        functools.partial(_mixffn_kernel, tm=tm, steps=steps),
        grid=(n_seq, steps),
        in_specs=[
            pl.BlockSpec((tm, D_MODEL), row_map),
            pl.BlockSpec((tm, HEADS_DIM), row_map),
            pl.BlockSpec((tm, HEADS_DIM), row_map),
            pl.BlockSpec((tm, GATE_COLS), row_map),
            pl.BlockSpec((None, 6, D_MODEL), seq3),
            pl.BlockSpec((None, CONV_W - 1, D_FF), seq3),
            vec, vec, vec, vec, vec, vec,
            _const_spec((HEADS_DIM, D_MODEL)),
            _const_spec((HEADS_DIM, D_MODEL)),
            _const_spec((D_MODEL, D_MODEL)),
            _const_spec((D_MODEL, 2 * D_FF)),
            _const_spec((CONV_W, D_FF)),
            _const_spec((1, D_FF)),
            _const_spec((D_FF, D_MODEL)),
        ],
        out_specs=[
            pl.BlockSpec((tm, D_MODEL), row_map),
            pl.BlockSpec((None, CONV_W - 1, D_FF), seq3),
        ],
        out_shape=[
            jax.ShapeDtypeStruct((rows, D_MODEL), F32),
            jax.ShapeDtypeStruct((n_seq, CONV_W - 1, D_FF), F32),
        ],
        scratch_shapes=[pltpu.VMEM((8, D_FF), F32), pltpu.VMEM((tm, D_FF), BF16)],
        compiler_params=pltpu.CompilerParams(
            dimension_semantics=("arbitrary", "arbitrary"), vmem_limit_bytes=VMEM_LIMIT),
        name="mixffn",
    )(x2d, att, rw, gates, mod, conv_prev,
      prm["lnig"], prm["lnib"], prm["ln1g"], prm["ln1b"], prm["ln2g"], prm["ln2b"],
      prm["wa"], prm["wr"], prm["wo"], prm["wup"], prm["cw"], prm["cb"], prm["wdn"])


def _pair_bias(table):
    assert CHUNK - 1 <= REL_CLIP
    top = ATT_REACH + CHUNK - 1
    n_far = top - REL_CLIP + 1
    far = jnp.broadcast_to(table[:, 2 * REL_CLIP:], (N_HEADS, n_far))
    lo_idx = top - (BAND + CHUNK - 2) + REL_CLIP
    near = table[:, lo_idx:2 * REL_CLIP][:, ::-1]
    ext = jnp.concatenate([far, near], axis=1).astype(F32)
    bias = jnp.stack([ext[:, CHUNK - 1 - q:CHUNK - 1 - q + BAND] for q in range(CHUNK)], axis=1)
    return bias.reshape(N_HEADS // 2, 2 * CHUNK, BAND)


def _chunk_ltri(ts):
    t = jnp.arange(ts)
    return ((t[:, None] // CHUNK == t[None, :] // CHUNK) & (t[None, :] <= t[:, None])).astype(BF16)


def _trunk(x2d, mod, shift0, state0, conv_prev, kext_fn, prm, n_seq, steps, tm, prompt):
    q, k, v, kv32, p, gates = _inproj_call(
        x2d, mod, prm["lnig"], prm["lnib"], prm["win"], n_seq, steps, tm, keep_last=prompt)
    kext, vext = kext_fn(k, v)
    if prompt:
        att = _attn_prompt_call(q, kext, vext, prm["bias"])
    else:
        att = _attn_sample_call(q, kext, vext, prm["bias"])
    rw, shift, state = _wkv_call(p, shift0, state0, prm, n_seq, steps, tm)
    y, conv = _mixffn_call(x2d, att, rw, gates, mod, conv_prev, prm, n_seq, steps, tm)
    return y, kv32, state, shift, conv


def kernel(x_prompt, x_sample, cache_attn_k, cache_attn_v, state_rwkv, state_shift, state_conv,
           c_prompt, c_sample, ln_in_g, ln_in_b, w_ada, b_ada, w_in, attn_rel_bias,
           rwkv_mu, rwkv_w0, rwkv_w_up, rwkv_a0, rwkv_a_up, rwkv_g_up, rwkv_k_k, rwkv_k_a,
           rwkv_r_k, rwkv_gn_g, rwkv_gn_b, w_branch_attn, w_branch_rwkv, w_out,
           ln1_g, ln1_b, ln2_g, ln2_b, w_ffn_up, ffn_conv_w, ffn_conv_b, w_ffn_down):
    bp, sp, _ = x_prompt.shape
    bs, ss, _ = x_sample.shape
    assert bp == 1 and ss == CHUNK and w_ada.shape[0] == DEPTH
    tm_p = 512
    assert sp % tm_p == 0 and cache_attn_k.shape[2] == ATT_REACH

    row = lambda a: a.reshape(1, -1)
    wwa = jnp.zeros((LORA_W + LORA_A, 2 * HEADS_DIM), F32)
    wwa = wwa.at[:LORA_W, :HEADS_DIM].set(rwkv_w_up[0]).at[LORA_W:, HEADS_DIM:].set(rwkv_a_up[0])
    head_id = jnp.arange(GROUP) // HEAD
    prm = dict(
        lnig=row(ln_in_g), lnib=row(ln_in_b),
        ln1g=row(ln1_g[0]), ln1b=row(ln1_b[0]), ln2g=row(ln2_g[0]), ln2b=row(ln2_b[0]),
        win=w_in[0].astype(BF16), bias=_pair_bias(attn_rel_bias[0]),
        mu=row(rwkv_mu[0]), w0=row(rwkv_w0[0]), wwa=wwa.astype(BF16), a0=row(rwkv_a0[0]),
        gup=rwkv_g_up[0].astype(BF16), kk=row(rwkv_k_k[0]), ka=row(rwkv_k_a[0]),
        rk=row(rwkv_r_k[0]), gng=row(rwkv_gn_g[0]), gnb=row(rwkv_gn_b[0]),
        ones=(head_id[:, None] == head_id[None, :]).astype(BF16),
        ltri={GROUP: _chunk_ltri(GROUP), CHUNK: _chunk_ltri(CHUNK)},
        wa=w_branch_attn[0].astype(BF16), wr=w_branch_rwkv[0].astype(BF16),
        wo=w_out[0].astype(BF16), wup=w_ffn_up[0].astype(BF16),
        cw=ffn_conv_w[0], cb=row(ffn_conv_b[0]), wdn=w_ffn_down[0].astype(BF16),
    )

    n_c = bp + bs
    c_all = jnp.concatenate([c_prompt, c_sample, jnp.zeros((16 - n_c, D_MODEL), F32)], axis=0)
    mod = _mod_call(c_all, w_ada[0], row(b_ada[0])).reshape(16, 6, D_MODEL)

    def kext_prompt(k, v):
        pad = jnp.zeros((ATT_REACH, HEADS_DIM), BF16)
        return jnp.concatenate([pad, k], axis=0), jnp.concatenate([pad, v], axis=0)

    y_p, kv_p, st_p, sh_p, cv_p = _trunk(
        x_prompt.reshape(sp, D_MODEL), mod[0:bp],
        jnp.zeros((bp, 1, RW_COLS), F32), jnp.zeros((bp, HEAD, HEADS_DIM), F32),
        jnp.zeros((bp, CONV_W - 1, D_FF), F32), kext_prompt, prm,
        n_seq=bp, steps=sp // tm_p, tm=tm_p, prompt=True)

    def kext_sample(k, v):
        ck = cache_attn_k[0].reshape(bs, ATT_REACH, HEADS_DIM).astype(BF16)
        cv = cache_attn_v[0].reshape(bs, ATT_REACH, HEADS_DIM).astype(BF16)
        return (jnp.concatenate([ck, k.reshape(bs, ss, HEADS_DIM)], axis=1),
                jnp.concatenate([cv, v.reshape(bs, ss, HEADS_DIM)], axis=1))

    st0 = jnp.transpose(state_rwkv[0], (0, 3, 1, 2)).reshape(bs, HEAD, HEADS_DIM)
    y_s, kv_s, st_s, sh_s, cv_s = _trunk(
        x_sample.reshape(bs * ss, D_MODEL), mod[bp:n_c],
        state_shift[0], st0, state_conv[0], kext_sample, prm,
        n_seq=bs, steps=1, tm=ss, prompt=False)

    def state_out(st, b):
        return jnp.transpose(st.reshape(b, HEAD, N_HEADS, HEAD), (0, 2, 3, 1))[None]

    hs = (N_HEADS, HEAD)
    return (
        y_p.reshape(bp, sp, D_MODEL),
        y_s.reshape(bs, ss, D_MODEL),
        kv_p[:, :HEADS_DIM].reshape(1, bp, ATT_REACH, *hs),
        kv_p[:, HEADS_DIM:].reshape(1, bp, ATT_REACH, *hs),
        kv_s[:, :HEADS_DIM].reshape(1, bs, ss, *hs),
        kv_s[:, HEADS_DIM:].reshape(1, bs, ss, *hs),
        state_out(st_p, bp),
        state_out(st_s, bs),
        sh_p[None],
        sh_s[None],
        cv_p[None],
        cv_s[None],
    )
```

```python
import functools

import jax
import jax.numpy as jnp
from jax import lax
from jax.experimental import pallas as pl
from jax.experimental.pallas import tpu as pltpu

F32 = jnp.float32
BF16 = jnp.bfloat16

D_MODEL = 1024
CHUNK = 64
ATT_REACH = 512
BAND = ATT_REACH + CHUNK
N_HEADS = 8
HEAD = 64
HEADS_DIM = N_HEADS * HEAD
REL_CLIP = 128
LORA_W = 64
LORA_A = 64
LORA_G = 128
ATT_COLS = 3 * HEADS_DIM
RW_COLS = 3 * HEADS_DIM + LORA_W + LORA_A + LORA_G
GATE_COLS = 2 * D_MODEL
D_FF = 2816
CONV_W = 3
LN_EPS = 1e-5
GN_EPS = 64e-5
DEPTH = 1
ALPHA = (2 * DEPTH) ** 0.25
LOG2E = 1.4426950408889634

GROUP = 256
VMEM_LIMIT = 56 * 1024 * 1024


def _const_spec(shape):
    nd = len(shape)
    return pl.BlockSpec(shape, lambda *_: (0,) * nd, pipeline_mode=pl.Buffered(1))


def _layer_norm(x, g, b):
    mu = jnp.mean(x, axis=-1, keepdims=True)
    xc = x - mu
    var = jnp.mean(xc * xc, axis=-1, keepdims=True)
    return xc * lax.rsqrt(var + LN_EPS) * g + b


def _sigmoid(x):
    return 1.0 / (1.0 + jnp.exp(-x))


def _split3(x):
    hi = x.astype(BF16)
    r1 = x - hi.astype(F32)
    mid = r1.astype(BF16)
    lo = (r1 - mid.astype(F32)).astype(BF16)
    return hi, mid, lo


def _dot(a, b):
    return jnp.dot(a, b, preferred_element_type=F32)


def _dot_nt(a, b):
    return lax.dot_general(a, b, (((1,), (1,)), ((), ())), preferred_element_type=F32)


def _dot_tn(a, b):
    return lax.dot_general(a, b, (((0,), (0,)), ((), ())), preferred_element_type=F32)


def _mod_kernel(c_ref, w_ref, b_ref, o_ref):
    c = c_ref[...]
    s = (c * _sigmoid(c)).astype(BF16)
    o_ref[...] = _dot(s, w_ref[...].astype(BF16)) + b_ref[...]


def _mod_call(c_all, w_ada, b_ada):
    n = c_all.shape[0]
    nblk = 6
    return pl.pallas_call(
        _mod_kernel,
        grid=(nblk,),
        in_specs=[
            pl.BlockSpec((n, D_MODEL), lambda i: (0, 0)),
            pl.BlockSpec((D_MODEL, D_MODEL), lambda i: (0, i)),
            pl.BlockSpec((1, D_MODEL), lambda i: (0, i)),
        ],
        out_specs=pl.BlockSpec((n, D_MODEL), lambda i: (0, i)),
        out_shape=jax.ShapeDtypeStruct((n, 6 * D_MODEL), F32),
        compiler_params=pltpu.CompilerParams(dimension_semantics=("arbitrary",)),
        name="mod",
    )(c_all, w_ada, b_ada)


def _inproj_kernel(x_ref, mod_ref, lng_ref, lnb_ref, w_ref,
                   q_ref, k_ref, v_ref, kv_ref, p_ref, g_ref):
    xn = _layer_norm(x_ref[...], lng_ref[...], lnb_ref[...])
    hb = (xn * (1.0 + mod_ref[1:2, :]) + mod_ref[0:1, :]).astype(BF16)

    def seg(a, b):
        return _dot(hb, w_ref[:, a:b])

    q_ref[...] = (seg(0, HEADS_DIM) * (HEAD ** -0.5 * LOG2E)).astype(BF16)
    k = seg(HEADS_DIM, 2 * HEADS_DIM)
    k_ref[...] = k.astype(BF16)
    kv_ref[:, 0:HEADS_DIM] = k
    v = seg(2 * HEADS_DIM, 3 * HEADS_DIM)
    v_ref[...] = v.astype(BF16)
    kv_ref[:, HEADS_DIM:2 * HEADS_DIM] = v
    for c in range(0, RW_COLS, 256):
        p_ref[:, c:c + 256] = seg(ATT_COLS + c, ATT_COLS + c + 256)
    g0 = ATT_COLS + RW_COLS
    for c in range(0, GATE_COLS, 512):
        g_ref[:, c:c + 512] = _sigmoid(seg(g0 + c, g0 + c + 512))


def _inproj_call(x2d, mod, ln_g, ln_b, w_in_b, n_seq, steps, tm, keep_last):
    rows = x2d.shape[0]
    in_cols = w_in_b.shape[1]
    row_map = lambda s, j: (s * steps + j, 0)
    if keep_last:
        keep_blocks = ATT_REACH // tm
        kv_rows = ATT_REACH
        kv_map = lambda s, j: (jnp.maximum(j - (steps - keep_blocks), 0), 0)
    else:
        kv_rows = rows
        kv_map = row_map
    return pl.pallas_call(
        _inproj_kernel,
        grid=(n_seq, steps),
        in_specs=[
            pl.BlockSpec((tm, D_MODEL), row_map),
            pl.BlockSpec((None, 6, D_MODEL), lambda s, j: (s, 0, 0)),
            _const_spec((1, D_MODEL)),
            _const_spec((1, D_MODEL)),
            _const_spec((D_MODEL, in_cols)),
        ],
        out_specs=[
            pl.BlockSpec((tm, HEADS_DIM), row_map),
            pl.BlockSpec((tm, HEADS_DIM), row_map),
            pl.BlockSpec((tm, HEADS_DIM), row_map),
            pl.BlockSpec((tm, 2 * HEADS_DIM), kv_map),
            pl.BlockSpec((tm, RW_COLS), row_map),
            pl.BlockSpec((tm, GATE_COLS), row_map),
        ],
        out_shape=[
            jax.ShapeDtypeStruct((rows, HEADS_DIM), BF16),
            jax.ShapeDtypeStruct((rows, HEADS_DIM), BF16),
            jax.ShapeDtypeStruct((rows, HEADS_DIM), BF16),
            jax.ShapeDtypeStruct((kv_rows, 2 * HEADS_DIM), F32),
            jax.ShapeDtypeStruct((rows, RW_COLS), F32),
            jax.ShapeDtypeStruct((rows, GATE_COLS), F32),
        ],
        compiler_params=pltpu.CompilerParams(
            dimension_semantics=("arbitrary", "arbitrary"), vmem_limit_bytes=VMEM_LIMIT),
        name="inproj",
    )(x2d, mod, ln_g, ln_b, w_in_b)


def _attn_chunks(chunks, bias_ref):
    lane = lax.broadcasted_iota(jnp.int32, (CHUNK, 128), 1)
    first = lane < HEAD
    pairs = [slice(pr * 128, (pr + 1) * 128) for pr in range(N_HEADS // 2)]
    scores = []
    for qc, kb, _, _ in chunks:
        for sl in pairs:
            q2 = qc[:, sl].astype(F32)
            qs = jnp.concatenate([jnp.where(first, q2, 0.0), jnp.where(first, 0.0, q2)],
                                 axis=0).astype(BF16)
            scores.append(_dot_nt(qs, kb(sl)))
    probs, sums = [], []
    for i, s in enumerate(scores):
        thr = chunks[i // len(pairs)][3]
        s = s + bias_ref[i % len(pairs)]
        if thr is not None:
            col = lax.broadcasted_iota(jnp.int32, s.shape, 1)
            s = jnp.where(col >= thr, s, -jnp.inf)
        e = jnp.exp2(s - jnp.max(s, axis=1, keepdims=True))
        sums.append(jnp.sum(e, axis=1, keepdims=True))
        probs.append(e.astype(BF16))
    outs = []
    for ci, (_, _, vb, _) in enumerate(chunks):
        cols = []
        for pi, sl in enumerate(pairs):
            i = ci * len(pairs) + pi
            o = _dot(probs[i], vb(sl)) / sums[i]
            cols.append(jnp.where(first, o[0:CHUNK], o[CHUNK:2 * CHUNK]))
        outs.append(jnp.concatenate(cols, axis=1).astype(BF16))
    return outs


def _attn_prompt_kernel(q_ref, ka_ref, kb_ref, va_ref, vb_ref, bias_ref, o_ref, kbuf, vbuf,
                        *, chunks):
    tq = chunks * CHUNK
    kbuf[0:tq, :] = ka_ref[...]
    kbuf[tq:2 * tq, :] = kb_ref[...]
    vbuf[0:tq, :] = va_ref[...]
    vbuf[tq:2 * tq, :] = vb_ref[...]
    s = pl.program_id(0)

    per = 2

    def run(masked):
        def body(i, carry):
            units, starts = [], []
            for k in range(per):
                g = i * per + k
                r0 = pl.multiple_of(g * CHUNK, CHUNK)
                thr = ATT_REACH - (s * chunks + g) * CHUNK if masked else None
                band = pl.ds(r0, BAND)
                units.append((q_ref[pl.ds(r0, CHUNK), :],
                              lambda sl, band=band: kbuf[band, sl],
                              lambda sl, band=band: vbuf[band, sl], thr))
                starts.append(r0)
            for r0, o in zip(starts, _attn_chunks(units, bias_ref)):
                o_ref[pl.ds(r0, CHUNK), :] = o
            return carry

        lax.fori_loop(0, chunks // per, body, 0)

    assert chunks * CHUNK >= ATT_REACH
    pl.when(s == 0)(functools.partial(run, True))
    pl.when(s != 0)(functools.partial(run, False))


def _attn_prompt_call(q, kext, vext, bias):
    rows = q.shape[0]
    tq = ATT_REACH
    blk = lambda off: pl.BlockSpec((tq, HEADS_DIM), lambda s: (s + off, 0))
    return pl.pallas_call(
        functools.partial(_attn_prompt_kernel, chunks=tq // CHUNK),
        grid=(rows // tq,),
        in_specs=[blk(0), blk(0), blk(1), blk(0), blk(1), _const_spec(bias.shape)],
        out_specs=blk(0),
        out_shape=jax.ShapeDtypeStruct((rows, HEADS_DIM), BF16),
        scratch_shapes=[pltpu.VMEM((2 * tq, HEADS_DIM), BF16), pltpu.VMEM((2 * tq, HEADS_DIM), BF16)],
        compiler_params=pltpu.CompilerParams(dimension_semantics=("arbitrary",)),
        name="attn_prompt",
    )(q, kext, kext, vext, vext, bias)


def _attn_sample_kernel(q_ref, k_ref, v_ref, bias_ref, o_ref):
    unit = (q_ref[...], lambda sl: k_ref[:, sl], lambda sl: v_ref[:, sl], None)
    o_ref[...] = _attn_chunks([unit], bias_ref)[0]


def _attn_sample_call(q, kext, vext, bias):
    nb = kext.shape[0]
    return pl.pallas_call(
        _attn_sample_kernel,
        grid=(nb,),
        in_specs=[
            pl.BlockSpec((CHUNK, HEADS_DIM), lambda b: (b, 0)),
            pl.BlockSpec((None, BAND, HEADS_DIM), lambda b: (b, 0, 0)),
            pl.BlockSpec((None, BAND, HEADS_DIM), lambda b: (b, 0, 0)),
            _const_spec(bias.shape),
        ],
        out_specs=pl.BlockSpec((CHUNK, HEADS_DIM), lambda b: (b, 0)),
        out_shape=jax.ShapeDtypeStruct((nb * CHUNK, HEADS_DIM), BF16),
        compiler_params=pltpu.CompilerParams(dimension_semantics=("arbitrary",)),
        name="attn_sample",
    )(q, kext, vext, bias)


def _same_head():
    r = lax.broadcasted_iota(jnp.int32, (GROUP, GROUP), 0) // HEAD
    c = lax.broadcasted_iota(jnp.int32, (GROUP, GROUP), 1) // HEAD
    return r == c


def _blk(x, same_head):
    return jnp.where(same_head, jnp.concatenate([x] * 4, axis=0), 0.0).astype(BF16)


def _wkv_prep(units, hooks):
    same_head = _same_head()
    t64 = lax.broadcasted_iota(jnp.int32, (CHUNK, GROUP), 0)
    i64 = lax.broadcasted_iota(jnp.int32, (CHUNK, GROUP), 1) % HEAD
    strict = i64 < t64
    incl = i64 <= t64
    pending = list(hooks)

    def stage_done():
        if pending:
            pending.pop(0)()

    def blk(x):
        return _blk(x, same_head)

    n, a_ak, a_rb, a_rk = [], [], [], []
    for ld in units:
        lhs = jnp.concatenate([ld("at"), ld("rt")], axis=0).astype(BF16)
        rhs = jnp.concatenate([blk(ld("bt")), blk(ld("kt"))], axis=0)
        a_all = _dot_nt(lhs, rhs)
        n.append(jnp.where(strict, a_all[0:CHUNK, 0:GROUP], 0.0))
        a_ak.append(jnp.where(strict, a_all[0:CHUNK, GROUP:], 0.0))
        a_rb.append(jnp.where(incl, a_all[CHUNK:, 0:GROUP], 0.0))
        a_rk.append(jnp.where(incl, a_all[CHUNK:, GROUP:], 0.0))
    stage_done()

    x0 = [_dot(a.astype(BF16), blk(ld("v"))) for a, ld in zip(a_ak, units)]
    npow = [_dot(m.astype(BF16), blk(m)) for m in n]
    t = [jnp.where(i64 == t64, 1.0, 0.0) + m for m in n]
    stage_done()
    for _ in range(4):
        prod = [_dot(jnp.concatenate([a, b], axis=0).astype(BF16), blk(b))
                for a, b in zip(t, npow)]
        t = [a + p[0:CHUNK] for a, p in zip(t, prod)]
        npow = [p[CHUNK:] for p in prod]
        stage_done()
    t = [a + _dot(a.astype(BF16), blk(b)) for a, b in zip(t, npow)]
    stage_done()
    res = [_dot(a.astype(BF16), jnp.concatenate([blk(ld("at")), blk(x)], axis=1))
           for a, x, ld in zip(t, x0, units)]
    stage_done()
    while pending:
        stage_done()
    return [(r[:, 0:GROUP], r[:, GROUP:], b, k) for r, b, k in zip(res, a_rb, a_rk)]


def _wkv_step_stages(units, states, out):
    same_head = _same_head()
    held = {}

    def first():
        held["sblk"] = [_blk(st, same_head) for st in states]
        held["u"] = [_dot(ld("ah").astype(BF16), sb) + ld("vh")
                     for ld, sb in zip(units, held["sblk"])]

    def second():
        for i, ld in enumerate(units):
            u, v = held["u"][i], ld("v")
            y = _dot(jnp.concatenate([ld("rt"), ld("arb"), ld("ark")], axis=1).astype(BF16),
                     jnp.concatenate([held["sblk"][i], _blk(u, same_head), _blk(v, same_head)],
                                     axis=0))
            g = _dot_tn(jnp.concatenate([ld("btd"), ld("ktd")], axis=0).astype(BF16),
                        jnp.concatenate([u, v], axis=0).astype(BF16))
            g = jnp.where(same_head, g, 0.0)
            delta = g[0:64] + g[64:128] + g[128:192] + g[192:256]
            out.append(y)
            states[i] = states[i] * ld("fdec") + delta

    return [first, second]


def _wkv_kernel(p_ref, shift0_ref, state0_ref, mu_ref, w0_ref, wwa_ref, a0_ref, gup_ref,
                kkw_ref, kaw_ref, rkw_ref, gng_ref, gnb_ref, ones_ref, ltri_ref,
                out_ref, shift_out_ref, state_out_ref,
                st_scr, prev_scr, rt_scr, at_scr, kt_scr, bt_scr, v_scr, btd_scr, ktd_scr,
                fdec_scr, ah_scr, vh_scr, arb_scr, ark_scr, y_scr, g_scr, bon_scr, *, ts, steps):
    j = pl.program_id(1)
    nc = ts // CHUNK

    @pl.when(j == 0)
    def _():
        st_scr[...] = state0_ref[...]
        prev_scr[...] = shift0_ref[...]

    def bdsum(x, terms):
        ones = ones_ref[...]
        halves = []
        for c0 in range(0, HEADS_DIM, GROUP):
            parts = _split3(x[:, c0:c0 + GROUP])[:terms]
            acc = _dot(parts[0], ones)
            for part in parts[1:]:
                acc = acc + _dot(part, ones)
            halves.append(acc)
        return jnp.concatenate(halves, axis=1)

    row0 = lax.broadcasted_iota(jnp.int32, (ts, 1), 0) == 0

    def shifted(c0, w):
        pb = p_ref[:, c0:c0 + w]
        prev = jnp.where(row0, prev_scr[:, c0:c0 + w], pltpu.roll(pb, 1, 0))
        return pb + (prev - pb) * mu_ref[:, c0:c0 + w]

    r = shifted(0, HEADS_DIM)
    k = shifted(HEADS_DIM, HEADS_DIM)
    v = shifted(2 * HEADS_DIM, HEADS_DIM)
    lora = shifted(3 * HEADS_DIM, 256)
    lwla = lora[:, 0:128]
    lane = lax.broadcasted_iota(jnp.int32, (ts, 128), 1)
    raw = _dot(jnp.where(lane < LORA_W, jnp.tanh(lwla), lwla).astype(BF16), wwa_ref[...])
    w_pre = w0_ref[...] + raw[:, 0:HEADS_DIM]
    softplus = jnp.maximum(-w_pre, 0.0) + jnp.log(1.0 + jnp.exp(-jnp.abs(w_pre)))
    lw = -jnp.exp(-softplus - 0.5)
    a = _sigmoid(a0_ref[...] + raw[:, HEADS_DIM:])
    g_scr[...] = _dot(_sigmoid(lora[:, 128:256]).astype(BF16), gup_ref[...])

    kk = k * kkw_ref[...]
    kk = kk / jnp.maximum(jnp.sqrt(bdsum(kk * kk, 2)), 1e-12)
    k2 = k * (1.0 + (a - 1.0) * kaw_ref[...])
    bon_scr[...] = bdsum(r * k2 * rkw_ref[...], 2) * v
    v_scr[...] = v

    ltri = ltri_ref[...]
    lrows = ltri.shape[0]
    cums = []
    for r0 in range(0, ts, lrows):
        hi, mid, lo = _split3(lw[r0:r0 + lrows])
        cums.append(_dot(ltri, hi) + _dot(ltri, mid) + _dot(ltri, lo))
    cum = jnp.concatenate(cums, axis=0)
    cum3 = cum.reshape(nc, CHUNK, HEADS_DIM)
    cend = jnp.broadcast_to(cum3[:, CHUNK - 1:CHUNK, :], cum3.shape).reshape(ts, HEADS_DIM)
    e_inv = jnp.exp(-cum)
    e_rel = jnp.exp(cend - cum)
    b = kk * a
    rt_scr[...] = r * jnp.exp(cum)
    at_scr[...] = -kk * jnp.exp(cum - lw)
    kt_scr[...] = k2 * e_inv
    bt_scr[...] = b * e_inv
    ktd_scr[...] = k2 * e_rel
    btd_scr[...] = b * e_rel
    diag = (lax.broadcasted_iota(jnp.int32, (ts, HEADS_DIM), 0) % CHUNK
            == lax.broadcasted_iota(jnp.int32, (ts, HEADS_DIM), 1) % HEAD)
    fdec_scr[...] = bdsum(jnp.where(diag, jnp.exp(cend), 0.0), 3)

    def chunk_rows(c):
        start = c * CHUNK
        if not isinstance(start, int):
            start = pl.multiple_of(start, CHUNK)
        return pl.ds(start, CHUNK)

    groups = [slice(g0, g0 + GROUP) for g0 in range(0, HEADS_DIM, GROUP)]

    named = dict(at=at_scr, rt=rt_scr, kt=kt_scr, bt=bt_scr, v=v_scr, btd=btd_scr, ktd=ktd_scr,
                 fdec=fdec_scr, ah=ah_scr, vh=vh_scr, arb=arb_scr, ark=ark_scr)
    prep_outs = (ah_scr, vh_scr, arb_scr, ark_scr)

    def loader(c, cols):
        rows = chunk_rows(c)
        return lambda name: named[name][rows, cols]

    def run(step_chunks, prep_chunks):
        states = [st_scr[:, cols] for cols in groups] if step_chunks else []
        hooks, ys = [], []
        for c in step_chunks:
            out = []
            ys.append((c, out))
            hooks += _wkv_step_stages([loader(c, cols) for cols in groups], states, out)
        units = [(c, cols) for c in prep_chunks for cols in groups]
        prep_out = _wkv_prep([loader(c, cols) for c, cols in units], hooks) if units else []
        if not units:
            for hook in hooks:
                hook()
        for c, out in ys:
            rows = chunk_rows(c)
            for cols, y in zip(groups, out):
                y_scr[rows, cols] = y
        for cols, st in zip(groups, states):
            st_scr[:, cols] = st
        for (c, cols), outs in zip(units, prep_out):
            rows = chunk_rows(c)
            for ref, val in zip(prep_outs, outs):
                ref[rows, cols] = val

    per = 2 if nc % 2 == 0 else 1
    n_pass = nc // per
    run([], list(range(per)))

    def body(i, carry):
        run([i * per + k for k in range(per)], [(i + 1) * per + k for k in range(per)])
        return carry

    lax.fori_loop(0, n_pass - 1, body, 0)
    run([nc - per + k for k in range(per)], [])

    y = y_scr[...]
    d = y - bdsum(y, 2) * (1.0 / HEAD)
    var = bdsum(d * d, 2) * (1.0 / HEAD)
    yn = d * lax.rsqrt(var + GN_EPS) * gng_ref[...] + gnb_ref[...] + bon_scr[...]
    out_ref[...] = (yn * g_scr[...]).astype(BF16)

    prev_scr[...] = p_ref[ts - 1:ts, :]

    @pl.when(j == steps - 1)
    def _():
        shift_out_ref[...] = p_ref[ts - 1:ts, :]
        state_out_ref[...] = st_scr[...]


def _wkv_call(p2d, shift0, state0, prm, n_seq, steps, ts):
    rows = p2d.shape[0]
    row_map = lambda s, j: (s * steps + j, 0)
    seq3 = lambda s, j: (s, 0, 0)
    vec = _const_spec((1, HEADS_DIM))
    big = pltpu.VMEM((ts, HEADS_DIM), F32)
    ltri = prm["ltri"][min(ts, GROUP)]
    return pl.pallas_call(
        functools.partial(_wkv_kernel, ts=ts, steps=steps),
        grid=(n_seq, steps),
        in_specs=[
            pl.BlockSpec((ts, RW_COLS), row_map),
            pl.BlockSpec((None, 1, RW_COLS), seq3),
            pl.BlockSpec((None, HEAD, HEADS_DIM), seq3),
            _const_spec((1, RW_COLS)),
            vec,
            _const_spec((128, 2 * HEADS_DIM)),
            vec,
            _const_spec((LORA_G, HEADS_DIM)),
            vec, vec, vec, vec, vec,
            _const_spec((GROUP, GROUP)),
            _const_spec(ltri.shape),
        ],
        out_specs=[
            pl.BlockSpec((ts, HEADS_DIM), row_map),
            pl.BlockSpec((None, 1, RW_COLS), seq3),
            pl.BlockSpec((None, HEAD, HEADS_DIM), seq3),
        ],
        out_shape=[
            jax.ShapeDtypeStruct((rows, HEADS_DIM), BF16),
            jax.ShapeDtypeStruct((n_seq, 1, RW_COLS), F32),
            jax.ShapeDtypeStruct((n_seq, HEAD, HEADS_DIM), F32),
        ],
        scratch_shapes=[
            pltpu.VMEM((HEAD, HEADS_DIM), F32),
            pltpu.VMEM((1, RW_COLS), F32),
        ] + [big] * 15,
        compiler_params=pltpu.CompilerParams(
            dimension_semantics=("arbitrary", "arbitrary"), vmem_limit_bytes=VMEM_LIMIT),
        name="wkv",
    )(p2d, shift0, state0, prm["mu"], prm["w0"], prm["wwa"], prm["a0"], prm["gup"],
      prm["kk"], prm["ka"], prm["rk"], prm["gng"], prm["gnb"], prm["ones"], ltri)


def _mixffn_kernel(x_ref, att_ref, rw_ref, gate_ref, mod_ref, convp_ref,
                   lnig_ref, lnib_ref, ln1g_ref, ln1b_ref, ln2g_ref, ln2b_ref,
                   wa_ref, wr_ref, wo_ref, wup_ref, cw_ref, cb_ref, wdn_ref,
                   y_ref, convo_ref, carry_scr, yb_scr, h2_scr, *, tm, steps):
    j = pl.program_id(1)

    @pl.when(j == 0)
    def _():
        carry_scr[8 - (CONV_W - 1):8, :] = convp_ref[...]

    th = tm // 2 if tm >= 512 else tm
    subs = [slice(r0, r0 + th) for r0 in range(0, tm, th)]
    xn = [_layer_norm(x_ref[rs, :], lnig_ref[...], lnib_ref[...]) for rs in subs]
    ma = [_dot(att_ref[rs, :], wa_ref[...]) for rs in subs]
    mr = [_dot(rw_ref[rs, :], wr_ref[...]) for rs in subs]
    merged = [(gate_ref[rs, 0:D_MODEL] * a + gate_ref[rs, D_MODEL:] * r).astype(BF16)
              for rs, a, r in zip(subs, ma, mr)]
    mix = [_dot(m, wo_ref[...]) for m in merged]
    x1 = [_layer_norm(ALPHA * a + (1.0 + mod_ref[2:3, :]) * m, ln1g_ref[...], ln1b_ref[...])
          for a, m in zip(xn, mix)]
    h2 = [(a * (1.0 + mod_ref[4:5, :]) + mod_ref[3:4, :]).astype(BF16) for a in x1]

    cw_blk = 256
    rb = min(tm, 128)
    row = lax.broadcasted_iota(jnp.int32, (rb, cw_blk), 0)
    h2_scr[...] = jnp.concatenate(h2, axis=0)

    def up(blk):
        c, r0 = blk
        h = h2_scr[r0:r0 + rb, :]
        return (_dot(h, wup_ref[:, c:c + cw_blk]),
                _dot(h, wup_ref[:, D_FF + c:D_FF + c + cw_blk]))

    blocks = [(c, r0) for c in range(0, D_FF, cw_blk) for r0 in range(0, tm, rb)]
    cur = up(blocks[0])
    for bi, (c, r0) in enumerate(blocks):
        nxt = up(blocks[bi + 1]) if bi + 1 < len(blocks) else None
        uc, uv = cur
        cs = slice(c, c + cw_blk)
        if r0 == 0:
            c6 = carry_scr[6:7, cs]
            c7 = carry_scr[7:8, cs]
        s1 = jnp.where(row == 0, c7, pltpu.roll(uc, 1, 0))
        s2 = jnp.where(row == 0, c6, jnp.where(row == 1, c7, pltpu.roll(uc, 2, 0)))
        conv = cb_ref[:, cs] + s2 * cw_ref[0:1, cs] + s1 * cw_ref[1:2, cs] + uc * cw_ref[2:3, cs]
        yb_scr[r0:r0 + rb, cs] = (conv * _sigmoid(conv) * uv).astype(BF16)
        c6 = uc[rb - 2:rb - 1, :]
        c7 = uc[rb - 1:rb, :]
        if r0 + rb == tm:
            carry_scr[:, cs] = uc[rb - 8:rb, :]
            tail = uc[rb - (CONV_W - 1):rb, :]

            @pl.when(j == steps - 1)
            def _():
                convo_ref[:, cs] = tail

        cur = nxt

    ff = [_dot(yb_scr[rs, :], wdn_ref[...]) for rs in subs]
    for rs, a, f in zip(subs, x1, ff):
        y_ref[rs, :] = _layer_norm(ALPHA * a + (1.0 + mod_ref[5:6, :]) * f,
                                   ln2g_ref[...], ln2b_ref[...])


def _mixffn_call(x2d, att, rw, gates, mod, conv_prev, prm, n_seq, steps, tm):
    rows = x2d.shape[0]
    row_map = lambda s, j: (s * steps + j, 0)
    seq3 = lambda s, j: (s, 0, 0)
    vec = _const_spec((1, D_MODEL))
    return pl.pallas_call(
        functools.partial(_mixffn_kernel, tm=tm, steps=steps),
        grid=(n_seq, steps),
        in_specs=[
            pl.BlockSpec((tm, D_MODEL), row_map),
            pl.BlockSpec((tm, HEADS_DIM), row_map),
            pl.BlockSpec((tm, HEADS_DIM), row_map),
            pl.BlockSpec((tm, GATE_COLS), row_map),
            pl.BlockSpec((None, 6, D_MODEL), seq3),
            pl.BlockSpec((None, CONV_W - 1, D_FF), seq3),
            vec, vec, vec, vec, vec, vec,
            _const_spec((HEADS_DIM, D_MODEL)),
            _const_spec((HEADS_DIM, D_MODEL)),
            _const_spec((D_MODEL, D_MODEL)),
            _const_spec((D_MODEL, 2 * D_FF)),
            _const_spec((CONV_W, D_FF)),
            _const_spec((1, D_FF)),
            _const_spec((D_FF, D_MODEL)),
        ],
        out_specs=[
            pl.BlockSpec((tm, D_MODEL), row_map),
            pl.BlockSpec((None, CONV_W - 1, D_FF), seq3),
        ],
        out_shape=[
            jax.ShapeDtypeStruct((rows, D_MODEL), F32),
            jax.ShapeDtypeStruct((n_seq, CONV_W - 1, D_FF), F32),
        ],
        scratch_shapes=[pltpu.VMEM((8, D_FF), F32), pltpu.VMEM((tm, D_FF), BF16),
                        pltpu.VMEM((tm, D_MODEL), BF16)],
        compiler_params=pltpu.CompilerParams(
            dimension_semantics=("arbitrary", "arbitrary"), vmem_limit_bytes=VMEM_LIMIT),
        name="mixffn",
    )(x2d, att, rw, gates, mod, conv_prev,
      prm["lnig"], prm["lnib"], prm["ln1g"], prm["ln1b"], prm["ln2g"], prm["ln2b"],
      prm["wa"], prm["wr"], prm["wo"], prm["wup"], prm["cw"], prm["cb"], prm["wdn"])


def _pair_bias(table):
    assert CHUNK - 1 <= REL_CLIP
    top = ATT_REACH + CHUNK - 1
    n_far = top - REL_CLIP + 1
    far = jnp.broadcast_to(table[:, 2 * REL_CLIP:], (N_HEADS, n_far))
    lo_idx = top - (BAND + CHUNK - 2) + REL_CLIP
    near = table[:, lo_idx:2 * REL_CLIP][:, ::-1]
    ext = jnp.concatenate([far, near], axis=1).astype(F32) * LOG2E
    bias = jnp.stack([ext[:, CHUNK - 1 - q:CHUNK - 1 - q + BAND] for q in range(CHUNK)], axis=1)
    return bias.reshape(N_HEADS // 2, 2 * CHUNK, BAND)


def _chunk_ltri(ts):
    t = jnp.arange(ts)
    return ((t[:, None] // CHUNK == t[None, :] // CHUNK) & (t[None, :] <= t[:, None])).astype(BF16)


def _trunk(x2d, mod, shift0, state0, conv_prev, kext_fn, prm, n_seq, steps, tm, prompt):
    q, k, v, kv32, p, gates = _inproj_call(
        x2d, mod, prm["lnig"], prm["lnib"], prm["win"], n_seq, steps, tm, keep_last=prompt)
    kext, vext = kext_fn(k, v)
    if prompt:
        att = _attn_prompt_call(q, kext, vext, prm["bias"])
    else:
        att = _attn_sample_call(q, kext, vext, prm["bias"])
    rw, shift, state = _wkv_call(p, shift0, state0, prm, n_seq, steps, tm)
    y, conv = _mixffn_call(x2d, att, rw, gates, mod, conv_prev, prm, n_seq, steps, tm)
    return y, kv32, state, shift, conv


def kernel(x_prompt, x_sample, cache_attn_k, cache_attn_v, state_rwkv, state_shift, state_conv,
           c_prompt, c_sample, ln_in_g, ln_in_b, w_ada, b_ada, w_in, attn_rel_bias,
           rwkv_mu, rwkv_w0, rwkv_w_up, rwkv_a0, rwkv_a_up, rwkv_g_up, rwkv_k_k, rwkv_k_a,
           rwkv_r_k, rwkv_gn_g, rwkv_gn_b, w_branch_attn, w_branch_rwkv, w_out,
           ln1_g, ln1_b, ln2_g, ln2_b, w_ffn_up, ffn_conv_w, ffn_conv_b, w_ffn_down):
    bp, sp, _ = x_prompt.shape
    bs, ss, _ = x_sample.shape
    assert bp == 1 and ss == CHUNK and w_ada.shape[0] == DEPTH
    tm_p = 512
    assert sp % tm_p == 0 and cache_attn_k.shape[2] == ATT_REACH

    row = lambda a: a.reshape(1, -1)
    wwa = jnp.zeros((LORA_W + LORA_A, 2 * HEADS_DIM), F32)
    wwa = wwa.at[:LORA_W, :HEADS_DIM].set(rwkv_w_up[0]).at[LORA_W:, HEADS_DIM:].set(rwkv_a_up[0])
    head_id = jnp.arange(GROUP) // HEAD
    prm = dict(
        lnig=row(ln_in_g), lnib=row(ln_in_b),
        ln1g=row(ln1_g[0]), ln1b=row(ln1_b[0]), ln2g=row(ln2_g[0]), ln2b=row(ln2_b[0]),
        win=w_in[0].astype(BF16), bias=_pair_bias(attn_rel_bias[0]),
        mu=row(rwkv_mu[0]), w0=row(rwkv_w0[0]), wwa=wwa.astype(BF16), a0=row(rwkv_a0[0]),
        gup=rwkv_g_up[0].astype(BF16), kk=row(rwkv_k_k[0]), ka=row(rwkv_k_a[0]),
        rk=row(rwkv_r_k[0]), gng=row(rwkv_gn_g[0]), gnb=row(rwkv_gn_b[0]),
        ones=(head_id[:, None] == head_id[None, :]).astype(BF16),
        ltri={GROUP: _chunk_ltri(GROUP), CHUNK: _chunk_ltri(CHUNK)},
        wa=w_branch_attn[0].astype(BF16), wr=w_branch_rwkv[0].astype(BF16),
        wo=w_out[0].astype(BF16), wup=w_ffn_up[0].astype(BF16),
        cw=ffn_conv_w[0], cb=row(ffn_conv_b[0]), wdn=w_ffn_down[0].astype(BF16),
    )

    n_c = bp + bs
    c_all = jnp.concatenate([c_prompt, c_sample, jnp.zeros((16 - n_c, D_MODEL), F32)], axis=0)
    mod = _mod_call(c_all, w_ada[0], row(b_ada[0])).reshape(16, 6, D_MODEL)

    def kext_prompt(k, v):
        pad = jnp.zeros((ATT_REACH, HEADS_DIM), BF16)
        return jnp.concatenate([pad, k], axis=0), jnp.concatenate([pad, v], axis=0)

    y_p, kv_p, st_p, sh_p, cv_p = _trunk(
        x_prompt.reshape(sp, D_MODEL), mod[0:bp],
        jnp.zeros((bp, 1, RW_COLS), F32), jnp.zeros((bp, HEAD, HEADS_DIM), F32),
        jnp.zeros((bp, CONV_W - 1, D_FF), F32), kext_prompt, prm,
        n_seq=bp, steps=sp // tm_p, tm=tm_p, prompt=True)

    def kext_sample(k, v):
        ck = cache_attn_k[0].reshape(bs, ATT_REACH, HEADS_DIM).astype(BF16)
        cv = cache_attn_v[0].reshape(bs, ATT_REACH, HEADS_DIM).astype(BF16)
        return (jnp.concatenate([ck, k.reshape(bs, ss, HEADS_DIM)], axis=1),
                jnp.concatenate([cv, v.reshape(bs, ss, HEADS_DIM)], axis=1))

    st0 = jnp.transpose(state_rwkv[0], (0, 3, 1, 2)).reshape(bs, HEAD, HEADS_DIM)
    y_s, kv_s, st_s, sh_s, cv_s = _trunk(
        x_sample.reshape(bs * ss, D_MODEL), mod[bp:n_c],
        state_shift[0], st0, state_conv[0], kext_sample, prm,
        n_seq=bs, steps=1, tm=ss, prompt=False)

    def state_out(st, b):
        return jnp.transpose(st.reshape(b, HEAD, N_HEADS, HEAD), (0, 2, 3, 1))[None]

    hs = (N_HEADS, HEAD)
    return (
        y_p.reshape(bp, sp, D_MODEL),
        y_s.reshape(bs, ss, D_MODEL),
        kv_p[:, :HEADS_DIM].reshape(1, bp, ATT_REACH, *hs),
        kv_p[:, HEADS_DIM:].reshape(1, bp, ATT_REACH, *hs),
        kv_s[:, :HEADS_DIM].reshape(1, bs, ss, *hs),
        kv_s[:, HEADS_DIM:].reshape(1, bs, ss, *hs),
        state_out(st_p, bp),
        state_out(st_s, bs),
        sh_p[None],
        sh_s[None],
        cv_p[None],
        cv_s[None],
    )
```

```python
import functools

import jax
import jax.numpy as jnp
from jax import lax
from jax.experimental import pallas as pl
from jax.experimental.pallas import tpu as pltpu

F32 = jnp.float32
BF16 = jnp.bfloat16

D_MODEL = 1024
CHUNK = 64
ATT_REACH = 512
BAND = ATT_REACH + CHUNK
N_HEADS = 8
HEAD = 64
HEADS_DIM = N_HEADS * HEAD
REL_CLIP = 128
LORA_W = 64
LORA_A = 64
LORA_G = 128
ATT_COLS = 3 * HEADS_DIM
RW_COLS = 3 * HEADS_DIM + LORA_W + LORA_A + LORA_G
GATE_COLS = 2 * D_MODEL
D_FF = 2816
CONV_W = 3
LN_EPS = 1e-5
GN_EPS = 64e-5
DEPTH = 1
ALPHA = (2 * DEPTH) ** 0.25
LOG2E = 1.4426950408889634

GROUP = 256
ROW_TILE = 512
VMEM_LIMIT = 56 * 1024 * 1024


def _const_spec(shape):
    nd = len(shape)
    return pl.BlockSpec(shape, lambda *_: (0,) * nd, pipeline_mode=pl.Buffered(1))


def _layer_norm(x, g, b):
    mu = jnp.mean(x, axis=-1, keepdims=True)
    xc = x - mu
    var = jnp.mean(xc * xc, axis=-1, keepdims=True)
    return xc * lax.rsqrt(var + LN_EPS) * g + b


def _sigmoid(x):
    return 1.0 / (1.0 + jnp.exp(-x))


def _split3(x):
    hi = x.astype(BF16)
    r1 = x - hi.astype(F32)
    mid = r1.astype(BF16)
    lo = (r1 - mid.astype(F32)).astype(BF16)
    return hi, mid, lo


def _dot(a, b):
    return jnp.dot(a, b, preferred_element_type=F32)


def _dot_nt(a, b):
    return lax.dot_general(a, b, (((1,), (1,)), ((), ())), preferred_element_type=F32)


def _dot_tn(a, b):
    return lax.dot_general(a, b, (((0,), (0,)), ((), ())), preferred_element_type=F32)


def _mod_kernel(c_ref, w_ref, b_ref, o_ref):
    c = c_ref[...]
    s = (c * _sigmoid(c)).astype(BF16)
    o_ref[...] = _dot(s, w_ref[...].astype(BF16)) + b_ref[...]


def _mod_call(c_all, w_ada, b_ada):
    n = c_all.shape[0]
    nblk = 6
    return pl.pallas_call(
        _mod_kernel,
        grid=(nblk,),
        in_specs=[
            pl.BlockSpec((n, D_MODEL), lambda i: (0, 0)),
            pl.BlockSpec((D_MODEL, D_MODEL), lambda i: (0, i)),
            pl.BlockSpec((1, D_MODEL), lambda i: (0, i)),
        ],
        out_specs=pl.BlockSpec((n, D_MODEL), lambda i: (0, i)),
        out_shape=jax.ShapeDtypeStruct((n, 6 * D_MODEL), F32),
        compiler_params=pltpu.CompilerParams(dimension_semantics=("arbitrary",)),
        name="mod",
    )(c_all, w_ada, b_ada)


def _mod_row(mod_ref, idx, tm):
    groups, _, d = mod_ref.shape
    m = mod_ref[:, idx:idx + 1, :]
    if groups == 1:
        return m[0]
    return jnp.broadcast_to(m, (groups, tm // groups, d)).reshape(tm, d)


def _inproj_kernel(x_ref, mod_ref, lng_ref, lnb_ref, w_ref,
                   q_ref, k_ref, v_ref, kv_ref, p_ref, g_ref, *, lead):
    tm = x_ref.shape[0]
    j = pl.program_id(0)

    @pl.when(j < lead)
    def _():
        k_ref[...] = jnp.zeros_like(k_ref)
        v_ref[...] = jnp.zeros_like(v_ref)

    @pl.when(j >= lead)
    def _():
        xn = _layer_norm(x_ref[...], lng_ref[...], lnb_ref[...])
        hb = (xn * (1.0 + _mod_row(mod_ref, 1, tm)) + _mod_row(mod_ref, 0, tm)).astype(BF16)

        def seg(a, b):
            return _dot(hb, w_ref[:, a:b])

        q_ref[...] = (seg(0, HEADS_DIM) * (HEAD ** -0.5 * LOG2E)).astype(BF16)
        k = seg(HEADS_DIM, 2 * HEADS_DIM)
        k_ref[...] = k.astype(BF16)
        kv_ref[:, 0:HEADS_DIM] = k
        v = seg(2 * HEADS_DIM, 3 * HEADS_DIM)
        v_ref[...] = v.astype(BF16)
        kv_ref[:, HEADS_DIM:2 * HEADS_DIM] = v
        for c in range(0, RW_COLS, 256):
            p_ref[:, c:c + 256] = seg(ATT_COLS + c, ATT_COLS + c + 256)
        g0 = ATT_COLS + RW_COLS
        for c in range(0, GATE_COLS, 512):
            g_ref[:, c:c + 512] = _sigmoid(seg(g0 + c, g0 + c + 512))


def _inproj_call(x2d, mod, ln_g, ln_b, w_in_b, steps, tm, prompt):
    rows = x2d.shape[0]
    in_cols = w_in_b.shape[1]
    groups = mod.shape[0]
    if prompt:
        lead = ATT_REACH // tm
        kv_rows = ATT_REACH
        row_map = lambda j: (jnp.maximum(j - lead, 0), 0)
        kv_map = lambda j: (jnp.maximum(j - steps, 0), 0)
        ext_map = lambda j: (j, 0)
    else:
        lead = 0
        kv_rows = rows
        row_map = kv_map = ext_map = lambda j: (j, 0)
    return pl.pallas_call(
        functools.partial(_inproj_kernel, lead=lead),
        grid=(steps + lead,),
        in_specs=[
            pl.BlockSpec((tm, D_MODEL), row_map),
            _const_spec((groups, 6, D_MODEL)),
            _const_spec((1, D_MODEL)),
            _const_spec((1, D_MODEL)),
            _const_spec((D_MODEL, in_cols)),
        ],
        out_specs=[
            pl.BlockSpec((tm, HEADS_DIM), row_map),
            pl.BlockSpec((tm, HEADS_DIM), ext_map),
            pl.BlockSpec((tm, HEADS_DIM), ext_map),
            pl.BlockSpec((tm, 2 * HEADS_DIM), kv_map),
            pl.BlockSpec((tm, RW_COLS), row_map),
            pl.BlockSpec((tm, GATE_COLS), row_map),
        ],
        out_shape=[
            jax.ShapeDtypeStruct((rows, HEADS_DIM), BF16),
            jax.ShapeDtypeStruct((rows + lead * tm, HEADS_DIM), BF16),
            jax.ShapeDtypeStruct((rows + lead * tm, HEADS_DIM), BF16),
            jax.ShapeDtypeStruct((kv_rows, 2 * HEADS_DIM), F32),
            jax.ShapeDtypeStruct((rows, RW_COLS), F32),
            jax.ShapeDtypeStruct((rows, GATE_COLS), F32),
        ],
        compiler_params=pltpu.CompilerParams(
            dimension_semantics=("arbitrary",), vmem_limit_bytes=VMEM_LIMIT),
        name="inproj",
    )(x2d, mod, ln_g, ln_b, w_in_b)


def _attn_chunks(chunks, bias_ref):
    lane = lax.broadcasted_iota(jnp.int32, (CHUNK, 128), 1)
    first = lane < HEAD
    pairs = [slice(pr * 128, (pr + 1) * 128) for pr in range(N_HEADS // 2)]
    scores = []
    for qc, kb, _, _ in chunks:
        for sl in pairs:
            q2 = qc[:, sl].astype(F32)
            qs = jnp.concatenate([jnp.where(first, q2, 0.0), jnp.where(first, 0.0, q2)],
                                 axis=0).astype(BF16)
            scores.append(_dot_nt(qs, kb(sl)))
    probs, sums = [], []
    for i, s in enumerate(scores):
        thr = chunks[i // len(pairs)][3]
        s = s + bias_ref[i % len(pairs)]
        if thr is not None:
            col = lax.broadcasted_iota(jnp.int32, s.shape, 1)
            s = jnp.where(col >= thr, s, -jnp.inf)
        e = jnp.exp2(s - jnp.max(s, axis=1, keepdims=True))
        sums.append(jnp.sum(e, axis=1, keepdims=True))
        probs.append(e.astype(BF16))
    outs = []
    for ci, (_, _, vb, _) in enumerate(chunks):
        cols = []
        for pi, sl in enumerate(pairs):
            i = ci * len(pairs) + pi
            o = _dot(probs[i], vb(sl)) / sums[i]
            cols.append(jnp.where(first, o[0:CHUNK], o[CHUNK:2 * CHUNK]))
        outs.append(jnp.concatenate(cols, axis=1).astype(BF16))
    return outs


def _attn_prompt_kernel(q_ref, ka_ref, kb_ref, va_ref, vb_ref, bias_ref, o_ref, kbuf, vbuf,
                        *, chunks):
    tq = chunks * CHUNK
    kbuf[0:tq, :] = ka_ref[...]
    kbuf[tq:2 * tq, :] = kb_ref[...]
    vbuf[0:tq, :] = va_ref[...]
    vbuf[tq:2 * tq, :] = vb_ref[...]
    s = pl.program_id(0)

    per = 2

    def run(masked):
        def body(i, carry):
            units, starts = [], []
            for k in range(per):
                g = i * per + k
                r0 = pl.multiple_of(g * CHUNK, CHUNK)
                thr = ATT_REACH - (s * chunks + g) * CHUNK if masked else None
                band = pl.ds(r0, BAND)
                units.append((q_ref[pl.ds(r0, CHUNK), :],
                              lambda sl, band=band: kbuf[band, sl],
                              lambda sl, band=band: vbuf[band, sl], thr))
                starts.append(r0)
            for r0, o in zip(starts, _attn_chunks(units, bias_ref)):
                o_ref[pl.ds(r0, CHUNK), :] = o
            return carry

        lax.fori_loop(0, chunks // per, body, 0)

    assert chunks * CHUNK >= ATT_REACH
    pl.when(s == 0)(functools.partial(run, True))
    pl.when(s != 0)(functools.partial(run, False))


def _attn_prompt_call(q, kext, vext, bias):
    rows = q.shape[0]
    tq = ATT_REACH
    blk = lambda off: pl.BlockSpec((tq, HEADS_DIM), lambda s: (s + off, 0))
    return pl.pallas_call(
        functools.partial(_attn_prompt_kernel, chunks=tq // CHUNK),
        grid=(rows // tq,),
        in_specs=[blk(0), blk(0), blk(1), blk(0), blk(1), _const_spec(bias.shape)],
        out_specs=blk(0),
        out_shape=jax.ShapeDtypeStruct((rows, HEADS_DIM), BF16),
        scratch_shapes=[pltpu.VMEM((2 * tq, HEADS_DIM), BF16), pltpu.VMEM((2 * tq, HEADS_DIM), BF16)],
        compiler_params=pltpu.CompilerParams(dimension_semantics=("arbitrary",)),
        name="attn_prompt",
    )(q, kext, kext, vext, vext, bias)


def _attn_sample_kernel(q_ref, k_ref, v_ref, bias_ref, o_ref):
    unit = (q_ref[...], lambda sl: k_ref[:, sl], lambda sl: v_ref[:, sl], None)
    o_ref[...] = _attn_chunks([unit], bias_ref)[0]


def _attn_sample_call(q, kext, vext, bias):
    nb = kext.shape[0]
    return pl.pallas_call(
        _attn_sample_kernel,
        grid=(nb,),
        in_specs=[
            pl.BlockSpec((CHUNK, HEADS_DIM), lambda b: (b, 0)),
            pl.BlockSpec((None, BAND, HEADS_DIM), lambda b: (b, 0, 0)),
            pl.BlockSpec((None, BAND, HEADS_DIM), lambda b: (b, 0, 0)),
            _const_spec(bias.shape),
        ],
        out_specs=pl.BlockSpec((CHUNK, HEADS_DIM), lambda b: (b, 0)),
        out_shape=jax.ShapeDtypeStruct((nb * CHUNK, HEADS_DIM), BF16),
        compiler_params=pltpu.CompilerParams(dimension_semantics=("arbitrary",)),
        name="attn_sample",
    )(q, kext, vext, bias)


def _same_head():
    r = lax.broadcasted_iota(jnp.int32, (GROUP, GROUP), 0) // HEAD
    c = lax.broadcasted_iota(jnp.int32, (GROUP, GROUP), 1) // HEAD
    return r == c


def _blk(x, same_head):
    return jnp.where(same_head, jnp.concatenate([x] * 4, axis=0), 0.0).astype(BF16)


def _wkv_prep(units, hooks):
    same_head = _same_head()
    t64 = lax.broadcasted_iota(jnp.int32, (CHUNK, GROUP), 0)
    i64 = lax.broadcasted_iota(jnp.int32, (CHUNK, GROUP), 1) % HEAD
    strict = i64 < t64
    incl = i64 <= t64
    pending = list(hooks)

    def stage_done():
        if pending:
            pending.pop(0)()

    def blk(x):
        return _blk(x, same_head)

    n, a_ak, a_rb, a_rk = [], [], [], []
    for ld in units:
        lhs = jnp.concatenate([ld("at"), ld("rt")], axis=0).astype(BF16)
        rhs = jnp.concatenate([blk(ld("bt")), blk(ld("kt"))], axis=0)
        a_all = _dot_nt(lhs, rhs)
        n.append(jnp.where(strict, a_all[0:CHUNK, 0:GROUP], 0.0))
        a_ak.append(jnp.where(strict, a_all[0:CHUNK, GROUP:], 0.0))
        a_rb.append(jnp.where(incl, a_all[CHUNK:, 0:GROUP], 0.0))
        a_rk.append(jnp.where(incl, a_all[CHUNK:, GROUP:], 0.0))
    stage_done()

    x0 = [_dot(a.astype(BF16), blk(ld("v"))) for a, ld in zip(a_ak, units)]
    npow = [_dot(m.astype(BF16), blk(m)) for m in n]
    t = [jnp.where(i64 == t64, 1.0, 0.0) + m for m in n]
    stage_done()
    for _ in range(4):
        prod = [_dot(jnp.concatenate([a, b], axis=0).astype(BF16), blk(b))
                for a, b in zip(t, npow)]
        t = [a + p[0:CHUNK] for a, p in zip(t, prod)]
        npow = [p[CHUNK:] for p in prod]
        stage_done()
    t = [a + _dot(a.astype(BF16), blk(b)) for a, b in zip(t, npow)]
    stage_done()
    res = [_dot(a.astype(BF16), jnp.concatenate([blk(ld("at")), blk(x)], axis=1))
           for a, x, ld in zip(t, x0, units)]
    stage_done()
    while pending:
        stage_done()
    return [(r[:, 0:GROUP], r[:, GROUP:], b, k) for r, b, k in zip(res, a_rb, a_rk)]


def _wkv_step_stages(units, states, out):
    same_head = _same_head()
    held = {}

    def first():
        held["sblk"] = [_blk(st, same_head) for st in states]
        held["u"] = [_dot(ld("ah").astype(BF16), sb) + ld("vh")
                     for ld, sb in zip(units, held["sblk"])]

    def second():
        for i, ld in enumerate(units):
            u, v = held["u"][i], ld("v")
            y = _dot(jnp.concatenate([ld("rt"), ld("arb"), ld("ark")], axis=1).astype(BF16),
                     jnp.concatenate([held["sblk"][i], _blk(u, same_head), _blk(v, same_head)],
                                     axis=0))
            g = _dot_tn(jnp.concatenate([ld("btd"), ld("ktd")], axis=0).astype(BF16),
                        jnp.concatenate([u, v], axis=0).astype(BF16))
            g = jnp.where(same_head, g, 0.0)
            delta = g[0:64] + g[64:128] + g[128:192] + g[192:256]
            out.append(y)
            states[i] = states[i] * ld("fdec") + delta

    return [first, second]


def _wkv_kernel(p_ref, shift0_ref, state0_ref, mu_ref, w0_ref, wwa_ref, a0_ref, gup_ref,
                kkw_ref, kaw_ref, rkw_ref, gng_ref, gnb_ref, ones_ref, ltri_ref,
                out_ref, shift_out_ref, state_out_ref,
                st_scr, prev_scr, rt_scr, at_scr, kt_scr, bt_scr, v_scr, btd_scr, ktd_scr,
                fdec_scr, ah_scr, vh_scr, arb_scr, ark_scr, y_scr, g_scr, bon_scr, *, ts, steps):
    j = pl.program_id(1)
    nc = ts // CHUNK

    @pl.when(j == 0)
    def _():
        st_scr[...] = state0_ref[...]
        prev_scr[...] = shift0_ref[...]

    def bdsum(x, terms):
        ones = ones_ref[...]
        halves = []
        for c0 in range(0, HEADS_DIM, GROUP):
            parts = _split3(x[:, c0:c0 + GROUP])[:terms]
            acc = _dot(parts[0], ones)
            for part in parts[1:]:
                acc = acc + _dot(part, ones)
            halves.append(acc)
        return jnp.concatenate(halves, axis=1)

    row0 = lax.broadcasted_iota(jnp.int32, (ts, 1), 0) == 0

    def shifted(c0, w):
        pb = p_ref[:, c0:c0 + w]
        prev = jnp.where(row0, prev_scr[:, c0:c0 + w], pltpu.roll(pb, 1, 0))
        return pb + (prev - pb) * mu_ref[:, c0:c0 + w]

    r = shifted(0, HEADS_DIM)
    k = shifted(HEADS_DIM, HEADS_DIM)
    v = shifted(2 * HEADS_DIM, HEADS_DIM)
    lora = shifted(3 * HEADS_DIM, 256)
    lwla = lora[:, 0:128]
    lane = lax.broadcasted_iota(jnp.int32, (ts, 128), 1)
    raw = _dot(jnp.where(lane < LORA_W, jnp.tanh(lwla), lwla).astype(BF16), wwa_ref[...])
    w_pre = w0_ref[...] + raw[:, 0:HEADS_DIM]
    softplus = jnp.maximum(-w_pre, 0.0) + jnp.log(1.0 + jnp.exp(-jnp.abs(w_pre)))
    lw = -jnp.exp(-softplus - 0.5)
    a = _sigmoid(a0_ref[...] + raw[:, HEADS_DIM:])
    g_scr[...] = _dot(_sigmoid(lora[:, 128:256]).astype(BF16), gup_ref[...])

    kk = k * kkw_ref[...]
    kk = kk / jnp.maximum(jnp.sqrt(bdsum(kk * kk, 2)), 1e-12)
    k2 = k * (1.0 + (a - 1.0) * kaw_ref[...])
    bon_scr[...] = bdsum(r * k2 * rkw_ref[...], 2) * v
    v_scr[...] = v

    ltri = ltri_ref[...]
    lrows = ltri.shape[0]
    cums = []
    for r0 in range(0, ts, lrows):
        hi, mid, lo = _split3(lw[r0:r0 + lrows])
        cums.append(_dot(ltri, hi) + _dot(ltri, mid) + _dot(ltri, lo))
    cum = jnp.concatenate(cums, axis=0)
    cum3 = cum.reshape(nc, CHUNK, HEADS_DIM)
    cend = jnp.broadcast_to(cum3[:, CHUNK - 1:CHUNK, :], cum3.shape).reshape(ts, HEADS_DIM)
    e_inv = jnp.exp(-cum)
    e_rel = jnp.exp(cend - cum)
    b = kk * a
    rt_scr[...] = r * jnp.exp(cum)
    at_scr[...] = -kk * jnp.exp(cum - lw)
    kt_scr[...] = k2 * e_inv
    bt_scr[...] = b * e_inv
    ktd_scr[...] = k2 * e_rel
    btd_scr[...] = b * e_rel
    diag = (lax.broadcasted_iota(jnp.int32, (ts, HEADS_DIM), 0) % CHUNK
            == lax.broadcasted_iota(jnp.int32, (ts, HEADS_DIM), 1) % HEAD)
    fdec_scr[...] = bdsum(jnp.where(diag, jnp.exp(cend), 0.0), 3)

    def chunk_rows(c):
        start = c * CHUNK
        if not isinstance(start, int):
            start = pl.multiple_of(start, CHUNK)
        return pl.ds(start, CHUNK)

    groups = [slice(g0, g0 + GROUP) for g0 in range(0, HEADS_DIM, GROUP)]

    named = dict(at=at_scr, rt=rt_scr, kt=kt_scr, bt=bt_scr, v=v_scr, btd=btd_scr, ktd=ktd_scr,
                 fdec=fdec_scr, ah=ah_scr, vh=vh_scr, arb=arb_scr, ark=ark_scr)
    prep_outs = (ah_scr, vh_scr, arb_scr, ark_scr)

    def loader(c, cols):
        rows = chunk_rows(c)
        return lambda name: named[name][rows, cols]

    def run(step_chunks, prep_chunks):
        states = [st_scr[:, cols] for cols in groups] if step_chunks else []
        hooks, ys = [], []
        for c in step_chunks:
            out = []
            ys.append((c, out))
            hooks += _wkv_step_stages([loader(c, cols) for cols in groups], states, out)
        units = [(c, cols) for c in prep_chunks for cols in groups]
        prep_out = _wkv_prep([loader(c, cols) for c, cols in units], hooks) if units else []
        if not units:
            for hook in hooks:
                hook()
        for c, out in ys:
            rows = chunk_rows(c)
            for cols, y in zip(groups, out):
                y_scr[rows, cols] = y
        for cols, st in zip(groups, states):
            st_scr[:, cols] = st
        for (c, cols), outs in zip(units, prep_out):
            rows = chunk_rows(c)
            for ref, val in zip(prep_outs, outs):
                ref[rows, cols] = val

    per = 2 if nc % 2 == 0 else 1
    n_pass = nc // per
    run([], list(range(per)))

    def body(i, carry):
        run([i * per + k for k in range(per)], [(i + 1) * per + k for k in range(per)])
        return carry

    lax.fori_loop(0, n_pass - 1, body, 0)
    run([nc - per + k for k in range(per)], [])

    y = y_scr[...]
    d = y - bdsum(y, 2) * (1.0 / HEAD)
    var = bdsum(d * d, 2) * (1.0 / HEAD)
    yn = d * lax.rsqrt(var + GN_EPS) * gng_ref[...] + gnb_ref[...] + bon_scr[...]
    out_ref[...] = (yn * g_scr[...]).astype(BF16)

    prev_scr[...] = p_ref[ts - 1:ts, :]

    @pl.when(j == steps - 1)
    def _():
        shift_out_ref[...] = p_ref[ts - 1:ts, :]
        state_out_ref[...] = st_scr[...]


def _wkv_call(p2d, shift0, state0, prm, n_seq, steps, ts):
    rows = p2d.shape[0]
    row_map = lambda s, j: (s * steps + j, 0)
    seq3 = lambda s, j: (s, 0, 0)
    vec = _const_spec((1, HEADS_DIM))
    big = pltpu.VMEM((ts, HEADS_DIM), F32)
    ltri = prm["ltri"][min(ts, GROUP)]
    return pl.pallas_call(
        functools.partial(_wkv_kernel, ts=ts, steps=steps),
        grid=(n_seq, steps),
        in_specs=[
            pl.BlockSpec((ts, RW_COLS), row_map),
            pl.BlockSpec((None, 1, RW_COLS), seq3),
            pl.BlockSpec((None, HEAD, HEADS_DIM), seq3),
            _const_spec((1, RW_COLS)),
            vec,
            _const_spec((128, 2 * HEADS_DIM)),
            vec,
            _const_spec((LORA_G, HEADS_DIM)),
            vec, vec, vec, vec, vec,
            _const_spec((GROUP, GROUP)),
            _const_spec(ltri.shape),
        ],
        out_specs=[
            pl.BlockSpec((ts, HEADS_DIM), row_map),
            pl.BlockSpec((None, 1, RW_COLS), seq3),
            pl.BlockSpec((None, HEAD, HEADS_DIM), seq3),
        ],
        out_shape=[
            jax.ShapeDtypeStruct((rows, HEADS_DIM), BF16),
            jax.ShapeDtypeStruct((n_seq, 1, RW_COLS), F32),
            jax.ShapeDtypeStruct((n_seq, HEAD, HEADS_DIM), F32),
        ],
        scratch_shapes=[
            pltpu.VMEM((HEAD, HEADS_DIM), F32),
            pltpu.VMEM((1, RW_COLS), F32),
        ] + [big] * 15,
        compiler_params=pltpu.CompilerParams(
            dimension_semantics=("arbitrary", "arbitrary"), vmem_limit_bytes=VMEM_LIMIT),
        name="wkv",
    )(p2d, shift0, state0, prm["mu"], prm["w0"], prm["wwa"], prm["a0"], prm["gup"],
      prm["kk"], prm["ka"], prm["rk"], prm["gng"], prm["gnb"], prm["ones"], ltri)


def _mixffn_kernel(x_ref, att_ref, rw_ref, gate_ref, mod_ref, convp_ref,
                   lnig_ref, lnib_ref, ln1g_ref, ln1b_ref, ln2g_ref, ln2b_ref,
                   wa_ref, wr_ref, wo_ref, wup_ref, cw_ref, cb_ref, wdn_ref,
                   y_ref, convo_ref, carry_scr, yb_scr, *, tm, steps):
    j = pl.program_id(0)
    groups = mod_ref.shape[0]
    assert groups == 1 or steps == 1

    if groups == 1:
        @pl.when(j == 0)
        def _():
            carry_scr[8 - (CONV_W - 1):8, :] = convp_ref[0]

    def mod(idx):
        return _mod_row(mod_ref, idx, tm)

    xn = _layer_norm(x_ref[...], lnig_ref[...], lnib_ref[...])
    merged = (gate_ref[:, 0:D_MODEL] * _dot(att_ref[...], wa_ref[...])
              + gate_ref[:, D_MODEL:] * _dot(rw_ref[...], wr_ref[...]))
    mix = _dot(merged.astype(BF16), wo_ref[...])
    x1 = _layer_norm(ALPHA * xn + (1.0 + mod(2)) * mix, ln1g_ref[...], ln1b_ref[...])
    h2 = (x1 * (1.0 + mod(4)) + mod(3)).astype(BF16)

    cw_blk = 256
    glen = tm // groups
    row8 = lax.broadcasted_iota(jnp.int32, (8, cw_blk), 0)
    grow = lax.broadcasted_iota(jnp.int32, (tm, cw_blk), 0) % glen
    for c in range(0, D_FF, cw_blk):
        cs = slice(c, c + cw_blk)
        uc = _dot(h2, wup_ref[:, cs])
        uv = _dot(h2, wup_ref[:, D_FF + c:D_FF + c + cw_blk])
        r1 = pltpu.roll(uc, 1, 0)
        r2 = pltpu.roll(uc, 2, 0)
        if groups == 1:
            c6 = carry_scr[6:7, cs]
            c7 = carry_scr[7:8, cs]
            s1 = jnp.concatenate([jnp.where(row8 == 0, c7, r1[0:8]), r1[8:]], axis=0)
            s2 = jnp.concatenate(
                [jnp.where(row8 == 0, c6, jnp.where(row8 == 1, c7, r2[0:8])), r2[8:]], axis=0)
            carry_scr[:, cs] = uc[tm - 8:tm, :]
            tail = uc[tm - (CONV_W - 1):tm, :][None]
        else:
            hist = jnp.broadcast_to(convp_ref[:, :, cs][:, :, None, :],
                                    (groups, CONV_W - 1, glen, cw_blk))
            c6 = hist[:, 0].reshape(tm, cw_blk)
            c7 = hist[:, 1].reshape(tm, cw_blk)
            s1 = jnp.where(grow == 0, c7, r1)
            s2 = jnp.where(grow == 0, c6, jnp.where(grow == 1, c7, r2))
            tail = uc.reshape(groups, glen, cw_blk)[:, glen - (CONV_W - 1):, :]
        conv = cb_ref[:, cs] + s2 * cw_ref[0:1, cs] + s1 * cw_ref[1:2, cs] + uc * cw_ref[2:3, cs]
        yb_scr[:, cs] = (conv * _sigmoid(conv) * uv).astype(BF16)

        @pl.when(j == steps - 1)
        def _():
            convo_ref[:, :, cs] = tail

    ff = _dot(yb_scr[...], wdn_ref[...])
    y_ref[...] = _layer_norm(ALPHA * x1 + (1.0 + mod(5)) * ff, ln2g_ref[...], ln2b_ref[...])


def _mixffn_call(x2d, att, rw, gates, mod, conv_prev, prm, steps, tm):
    rows = x2d.shape[0]
    groups = mod.shape[0]
    row_map = lambda j: (j, 0)
    vec = _const_spec((1, D_MODEL))
    return pl.pallas_call(
        functools.partial(_mixffn_kernel, tm=tm, steps=steps),
        grid=(steps,),
        in_specs=[
            pl.BlockSpec((tm, D_MODEL), row_map),
            pl.BlockSpec((tm, HEADS_DIM), row_map),
            pl.BlockSpec((tm, HEADS_DIM), row_map),
            pl.BlockSpec((tm, GATE_COLS), row_map),
            _const_spec((groups, 6, D_MODEL)),
            _const_spec((groups, CONV_W - 1, D_FF)),
            vec, vec, vec, vec, vec, vec,
            _const_spec((HEADS_DIM, D_MODEL)),
            _const_spec((HEADS_DIM, D_MODEL)),
            _const_spec((D_MODEL, D_MODEL)),
            _const_spec((D_MODEL, 2 * D_FF)),
            _const_spec((CONV_W, D_FF)),
            _const_spec((1, D_FF)),
            _const_spec((D_FF, D_MODEL)),
        ],
        out_specs=[
            pl.BlockSpec((tm, D_MODEL), row_map),
            pl.BlockSpec((groups, CONV_W - 1, D_FF), lambda j: (0, 0, 0)),
        ],
        out_shape=[
            jax.ShapeDtypeStruct((rows, D_MODEL), F32),
            jax.ShapeDtypeStruct((groups, CONV_W - 1, D_FF), F32),
        ],
        scratch_shapes=[pltpu.VMEM((8, D_FF), F32), pltpu.VMEM((tm, D_FF), BF16)],
        compiler_params=pltpu.CompilerParams(
            dimension_semantics=("arbitrary",), vmem_limit_bytes=VMEM_LIMIT),
        name="mixffn",
    )(x2d, att, rw, gates, mod, conv_prev,
      prm["lnig"], prm["lnib"], prm["ln1g"], prm["ln1b"], prm["ln2g"], prm["ln2b"],
      prm["wa"], prm["wr"], prm["wo"], prm["wup"], prm["cw"], prm["cb"], prm["wdn"])


def _pair_bias(table):
    assert CHUNK - 1 <= REL_CLIP
    top = ATT_REACH + CHUNK - 1
    n_far = top - REL_CLIP + 1
    far = jnp.broadcast_to(table[:, 2 * REL_CLIP:], (N_HEADS, n_far))
    lo_idx = top - (BAND + CHUNK - 2) + REL_CLIP
    near = table[:, lo_idx:2 * REL_CLIP][:, ::-1]
    ext = jnp.concatenate([far, near], axis=1).astype(F32) * LOG2E
    bias = jnp.stack([ext[:, CHUNK - 1 - q:CHUNK - 1 - q + BAND] for q in range(CHUNK)], axis=1)
    return bias.reshape(N_HEADS // 2, 2 * CHUNK, BAND)


def _chunk_ltri(ts):
    t = jnp.arange(ts)
    return ((t[:, None] // CHUNK == t[None, :] // CHUNK) & (t[None, :] <= t[:, None])).astype(BF16)


def _trunk(x2d, mod, shift0, state0, conv_prev, kext_fn, prm, n_seq, prompt):
    rows = x2d.shape[0]
    tm = min(rows, ROW_TILE)
    steps = rows // tm
    q, k, v, kv32, p, gates = _inproj_call(
        x2d, mod, prm["lnig"], prm["lnib"], prm["win"], steps, tm, prompt)
    kext, vext = kext_fn(k, v)
    if prompt:
        att = _attn_prompt_call(q, kext, vext, prm["bias"])
    else:
        att = _attn_sample_call(q, kext, vext, prm["bias"])
    ts = min(rows // n_seq, ROW_TILE)
    rw, shift, state = _wkv_call(p, shift0, state0, prm, n_seq, rows // n_seq // ts, ts)
    y, conv = _mixffn_call(x2d, att, rw, gates, mod, conv_prev, prm, steps, tm)
    return y, kv32, state, shift, conv


def kernel(x_prompt, x_sample, cache_attn_k, cache_attn_v, state_rwkv, state_shift, state_conv,
           c_prompt, c_sample, ln_in_g, ln_in_b, w_ada, b_ada, w_in, attn_rel_bias,
           rwkv_mu, rwkv_w0, rwkv_w_up, rwkv_a0, rwkv_a_up, rwkv_g_up, rwkv_k_k, rwkv_k_a,
           rwkv_r_k, rwkv_gn_g, rwkv_gn_b, w_branch_attn, w_branch_rwkv, w_out,
           ln1_g, ln1_b, ln2_g, ln2_b, w_ffn_up, ffn_conv_w, ffn_conv_b, w_ffn_down):
    bp, sp, _ = x_prompt.shape
    bs, ss, _ = x_sample.shape
    assert bp == 1 and ss == CHUNK and w_ada.shape[0] == DEPTH
    assert sp % ROW_TILE == 0 and bs * ss <= ROW_TILE and cache_attn_k.shape[2] == ATT_REACH

    row = lambda a: a.reshape(1, -1)
    wwa = jnp.zeros((LORA_W + LORA_A, 2 * HEADS_DIM), F32)
    wwa = wwa.at[:LORA_W, :HEADS_DIM].set(rwkv_w_up[0]).at[LORA_W:, HEADS_DIM:].set(rwkv_a_up[0])
    head_id = jnp.arange(GROUP) // HEAD
    prm = dict(
        lnig=row(ln_in_g), lnib=row(ln_in_b),
        ln1g=row(ln1_g[0]), ln1b=row(ln1_b[0]), ln2g=row(ln2_g[0]), ln2b=row(ln2_b[0]),
        win=w_in[0].astype(BF16), bias=_pair_bias(attn_rel_bias[0]),
        mu=row(rwkv_mu[0]), w0=row(rwkv_w0[0]), wwa=wwa.astype(BF16), a0=row(rwkv_a0[0]),
        gup=rwkv_g_up[0].astype(BF16), kk=row(rwkv_k_k[0]), ka=row(rwkv_k_a[0]),
        rk=row(rwkv_r_k[0]), gng=row(rwkv_gn_g[0]), gnb=row(rwkv_gn_b[0]),
        ones=(head_id[:, None] == head_id[None, :]).astype(BF16),
        ltri={GROUP: _chunk_ltri(GROUP), CHUNK: _chunk_ltri(CHUNK)},
        wa=w_branch_attn[0].astype(BF16), wr=w_branch_rwkv[0].astype(BF16),
        wo=w_out[0].astype(BF16), wup=w_ffn_up[0].astype(BF16),
        cw=ffn_conv_w[0], cb=row(ffn_conv_b[0]), wdn=w_ffn_down[0].astype(BF16),
    )

    n_c = bp + bs
    c_all = jnp.concatenate([c_prompt, c_sample, jnp.zeros((16 - n_c, D_MODEL), F32)], axis=0)
    mod = _mod_call(c_all, w_ada[0], row(b_ada[0])).reshape(16, 6, D_MODEL)

    y_p, kv_p, st_p, sh_p, cv_p = _trunk(
        x_prompt.reshape(sp, D_MODEL), mod[0:bp],
        jnp.zeros((bp, 1, RW_COLS), F32), jnp.zeros((bp, HEAD, HEADS_DIM), F32),
        jnp.zeros((bp, CONV_W - 1, D_FF), F32), lambda k, v: (k, v), prm,
        n_seq=bp, prompt=True)

    def kext_sample(k, v):
        ck = cache_attn_k[0].reshape(bs, ATT_REACH, HEADS_DIM).astype(BF16)
        cv = cache_attn_v[0].reshape(bs, ATT_REACH, HEADS_DIM).astype(BF16)
        return (jnp.concatenate([ck, k.reshape(bs, ss, HEADS_DIM)], axis=1),
                jnp.concatenate([cv, v.reshape(bs, ss, HEADS_DIM)], axis=1))

    st0 = jnp.transpose(state_rwkv[0], (0, 3, 1, 2)).reshape(bs, HEAD, HEADS_DIM)
    y_s, kv_s, st_s, sh_s, cv_s = _trunk(
        x_sample.reshape(bs * ss, D_MODEL), mod[bp:n_c],
        state_shift[0], st0, state_conv[0], kext_sample, prm,
        n_seq=bs, prompt=False)

    def state_out(st, b):
        return jnp.transpose(st.reshape(b, HEAD, N_HEADS, HEAD), (0, 2, 3, 1))[None]

    hs = (N_HEADS, HEAD)
    return (
        y_p.reshape(bp, sp, D_MODEL),
        y_s.reshape(bs, ss, D_MODEL),
        kv_p[:, :HEADS_DIM].reshape(1, bp, ATT_REACH, *hs),
        kv_p[:, HEADS_DIM:].reshape(1, bp, ATT_REACH, *hs),
        kv_s[:, :HEADS_DIM].reshape(1, bs, ss, *hs),
        kv_s[:, HEADS_DIM:].reshape(1, bs, ss, *hs),
        state_out(st_p, bp),
        state_out(st_s, bs),
        sh_p[None],
        sh_s[None],
        cv_p[None],
        cv_s[None],
    )
```

```python
import functools

import jax
import jax.numpy as jnp
from jax import lax
from jax.experimental import pallas as pl
from jax.experimental.pallas import tpu as pltpu

F32 = jnp.float32
BF16 = jnp.bfloat16

D_MODEL = 1024
CHUNK = 64
ATT_REACH = 512
BAND = ATT_REACH + CHUNK
N_HEADS = 8
HEAD = 64
HEADS_DIM = N_HEADS * HEAD
REL_CLIP = 128
LORA_W = 64
LORA_A = 64
LORA_G = 128
ATT_COLS = 3 * HEADS_DIM
RW_COLS = 3 * HEADS_DIM + LORA_W + LORA_A + LORA_G
GATE_COLS = 2 * D_MODEL
D_FF = 2816
CONV_W = 3
LN_EPS = 1e-5
GN_EPS = 64e-5
DEPTH = 1
ALPHA = (2 * DEPTH) ** 0.25
LOG2E = 1.4426950408889634

GROUP = 256
ROW_TILE = 512
WKV_PIECE = 128
VMEM_LIMIT = 56 * 1024 * 1024


def _const_spec(shape):
    nd = len(shape)
    return pl.BlockSpec(shape, lambda *_: (0,) * nd, pipeline_mode=pl.Buffered(1))


def _layer_norm(x, g, b):
    mu = jnp.mean(x, axis=-1, keepdims=True)
    xc = x - mu
    var = jnp.mean(xc * xc, axis=-1, keepdims=True)
    return xc * lax.rsqrt(var + LN_EPS) * g + b


def _sigmoid(x):
    return 1.0 / (1.0 + jnp.exp(-x))


def _split3(x):
    hi = x.astype(BF16)
    r1 = x - hi.astype(F32)
    mid = r1.astype(BF16)
    lo = (r1 - mid.astype(F32)).astype(BF16)
    return hi, mid, lo


def _dot(a, b):
    return jnp.dot(a, b, preferred_element_type=F32)


def _dot_nt(a, b):
    return lax.dot_general(a, b, (((1,), (1,)), ((), ())), preferred_element_type=F32)


def _dot_tn(a, b):
    return lax.dot_general(a, b, (((0,), (0,)), ((), ())), preferred_element_type=F32)


def _mod_kernel(c_ref, w_ref, b_ref, o_ref):
    c = c_ref[...]
    s = (c * _sigmoid(c)).astype(BF16)
    o_ref[...] = _dot(s, w_ref[...].astype(BF16)) + b_ref[...]


def _mod_call(c_all, w_ada, b_ada):
    n = c_all.shape[0]
    nblk = 6
    return pl.pallas_call(
        _mod_kernel,
        grid=(nblk,),
        in_specs=[
            pl.BlockSpec((n, D_MODEL), lambda i: (0, 0)),
            pl.BlockSpec((D_MODEL, D_MODEL), lambda i: (0, i)),
            pl.BlockSpec((1, D_MODEL), lambda i: (0, i)),
        ],
        out_specs=pl.BlockSpec((n, D_MODEL), lambda i: (0, i)),
        out_shape=jax.ShapeDtypeStruct((n, 6 * D_MODEL), F32),
        compiler_params=pltpu.CompilerParams(dimension_semantics=("arbitrary",)),
        name="mod",
    )(c_all, w_ada, b_ada)


def _mod_row(mod_ref, idx, tm):
    groups, _, d = mod_ref.shape
    m = mod_ref[:, idx:idx + 1, :]
    if groups == 1:
        return m[0]
    return jnp.broadcast_to(m, (groups, tm // groups, d)).reshape(tm, d)


def _inproj_kernel(x_ref, mod_ref, lng_ref, lnb_ref, w_ref,
                   q_ref, k_ref, v_ref, kv_ref, p_ref, g_ref, *, lead):
    tm = x_ref.shape[0]
    j = pl.program_id(0)

    @pl.when(j < lead)
    def _():
        k_ref[...] = jnp.zeros_like(k_ref)
        v_ref[...] = jnp.zeros_like(v_ref)

    @pl.when(j >= lead)
    def _():
        xn = _layer_norm(x_ref[...], lng_ref[...], lnb_ref[...])
        hb = (xn * (1.0 + _mod_row(mod_ref, 1, tm)) + _mod_row(mod_ref, 0, tm)).astype(BF16)

        def seg(a, b):
            return _dot(hb, w_ref[:, a:b])

        q_ref[...] = (seg(0, HEADS_DIM) * (HEAD ** -0.5 * LOG2E)).astype(BF16)
        k = seg(HEADS_DIM, 2 * HEADS_DIM)
        k_ref[...] = k.astype(BF16)
        kv_ref[:, 0:HEADS_DIM] = k
        v = seg(2 * HEADS_DIM, 3 * HEADS_DIM)
        v_ref[...] = v.astype(BF16)
        kv_ref[:, HEADS_DIM:2 * HEADS_DIM] = v
        for c in range(0, RW_COLS, 256):
            p_ref[:, c:c + 256] = seg(ATT_COLS + c, ATT_COLS + c + 256)
        g0 = ATT_COLS + RW_COLS
        for c in range(0, GATE_COLS, 512):
            g_ref[:, c:c + 512] = _sigmoid(seg(g0 + c, g0 + c + 512))


def _inproj_call(x2d, mod, ln_g, ln_b, w_in_b, steps, tm, prompt):
    rows = x2d.shape[0]
    in_cols = w_in_b.shape[1]
    groups = mod.shape[0]
    if prompt:
        lead = ATT_REACH // tm
        kv_rows = ATT_REACH
        row_map = lambda j: (jnp.maximum(j - lead, 0), 0)
        kv_map = lambda j: (jnp.maximum(j - steps, 0), 0)
        ext_map = lambda j: (j, 0)
    else:
        lead = 0
        kv_rows = rows
        row_map = kv_map = ext_map = lambda j: (j, 0)
    return pl.pallas_call(
        functools.partial(_inproj_kernel, lead=lead),
        grid=(steps + lead,),
        in_specs=[
            pl.BlockSpec((tm, D_MODEL), row_map),
            _const_spec((groups, 6, D_MODEL)),
            _const_spec((1, D_MODEL)),
            _const_spec((1, D_MODEL)),
            _const_spec((D_MODEL, in_cols)),
        ],
        out_specs=[
            pl.BlockSpec((tm, HEADS_DIM), row_map),
            pl.BlockSpec((tm, HEADS_DIM), ext_map),
            pl.BlockSpec((tm, HEADS_DIM), ext_map),
            pl.BlockSpec((tm, 2 * HEADS_DIM), kv_map),
            pl.BlockSpec((tm, RW_COLS), row_map),
            pl.BlockSpec((tm, GATE_COLS), row_map),
        ],
        out_shape=[
            jax.ShapeDtypeStruct((rows, HEADS_DIM), BF16),
            jax.ShapeDtypeStruct((rows + lead * tm, HEADS_DIM), BF16),
            jax.ShapeDtypeStruct((rows + lead * tm, HEADS_DIM), BF16),
            jax.ShapeDtypeStruct((kv_rows, 2 * HEADS_DIM), F32),
            jax.ShapeDtypeStruct((rows, RW_COLS), F32),
            jax.ShapeDtypeStruct((rows, GATE_COLS), F32),
        ],
        compiler_params=pltpu.CompilerParams(
            dimension_semantics=("arbitrary",), vmem_limit_bytes=VMEM_LIMIT),
        name="inproj",
    )(x2d, mod, ln_g, ln_b, w_in_b)


def _attn_chunks(chunks, bias_ref):
    lane = lax.broadcasted_iota(jnp.int32, (CHUNK, 128), 1)
    first = lane < HEAD
    pairs = [slice(pr * 128, (pr + 1) * 128) for pr in range(N_HEADS // 2)]
    scores = []
    for qc, kb, _, _ in chunks:
        for sl in pairs:
            q2 = qc[:, sl].astype(F32)
            qs = jnp.concatenate([jnp.where(first, q2, 0.0), jnp.where(first, 0.0, q2)],
                                 axis=0).astype(BF16)
            scores.append(_dot_nt(qs, kb(sl)))
    probs, sums = [], []
    for i, s in enumerate(scores):
        thr = chunks[i // len(pairs)][3]
        s = s + bias_ref[i % len(pairs)]
        if thr is not None:
            col = lax.broadcasted_iota(jnp.int32, s.shape, 1)
            s = jnp.where(col >= thr, s, -jnp.inf)
        e = jnp.exp2(s - jnp.max(s, axis=1, keepdims=True))
        sums.append(jnp.sum(e, axis=1, keepdims=True))
        probs.append(e.astype(BF16))
    outs = []
    for ci, (_, _, vb, _) in enumerate(chunks):
        cols = []
        for pi, sl in enumerate(pairs):
            i = ci * len(pairs) + pi
            o = _dot(probs[i], vb(sl)) / sums[i]
            cols.append(jnp.where(first, o[0:CHUNK], o[CHUNK:2 * CHUNK]))
        outs.append(jnp.concatenate(cols, axis=1).astype(BF16))
    return outs


def _attn_prompt_kernel(q_ref, ka_ref, kb_ref, va_ref, vb_ref, bias_ref, o_ref, kbuf, vbuf,
                        *, chunks):
    tq = chunks * CHUNK
    kbuf[0:tq, :] = ka_ref[...]
    kbuf[tq:2 * tq, :] = kb_ref[...]
    vbuf[0:tq, :] = va_ref[...]
    vbuf[tq:2 * tq, :] = vb_ref[...]
    s = pl.program_id(0)

    per = 2

    def run(masked):
        def body(i, carry):
            units, starts = [], []
            for k in range(per):
                g = i * per + k
                r0 = pl.multiple_of(g * CHUNK, CHUNK)
                thr = ATT_REACH - (s * chunks + g) * CHUNK if masked else None
                band = pl.ds(r0, BAND)
                units.append((q_ref[pl.ds(r0, CHUNK), :],
                              lambda sl, band=band: kbuf[band, sl],
                              lambda sl, band=band: vbuf[band, sl], thr))
                starts.append(r0)
            for r0, o in zip(starts, _attn_chunks(units, bias_ref)):
                o_ref[pl.ds(r0, CHUNK), :] = o
            return carry

        lax.fori_loop(0, chunks // per, body, 0)

    assert chunks * CHUNK >= ATT_REACH
    pl.when(s == 0)(functools.partial(run, True))
    pl.when(s != 0)(functools.partial(run, False))


def _attn_prompt_call(q, kext, vext, bias):
    rows = q.shape[0]
    tq = ATT_REACH
    blk = lambda off: pl.BlockSpec((tq, HEADS_DIM), lambda s: (s + off, 0))
    return pl.pallas_call(
        functools.partial(_attn_prompt_kernel, chunks=tq // CHUNK),
        grid=(rows // tq,),
        in_specs=[blk(0), blk(0), blk(1), blk(0), blk(1), _const_spec(bias.shape)],
        out_specs=blk(0),
        out_shape=jax.ShapeDtypeStruct((rows, HEADS_DIM), BF16),
        scratch_shapes=[pltpu.VMEM((2 * tq, HEADS_DIM), BF16), pltpu.VMEM((2 * tq, HEADS_DIM), BF16)],
        compiler_params=pltpu.CompilerParams(dimension_semantics=("arbitrary",)),
        name="attn_prompt",
    )(q, kext, kext, vext, vext, bias)


def _attn_sample_kernel(q_ref, k_ref, v_ref, bias_ref, o_ref):
    unit = (q_ref[...], lambda sl: k_ref[:, sl], lambda sl: v_ref[:, sl], None)
    o_ref[...] = _attn_chunks([unit], bias_ref)[0]


def _attn_sample_call(q, kext, vext, bias):
    nb = kext.shape[0]
    return pl.pallas_call(
        _attn_sample_kernel,
        grid=(nb,),
        in_specs=[
            pl.BlockSpec((CHUNK, HEADS_DIM), lambda b: (b, 0)),
            pl.BlockSpec((None, BAND, HEADS_DIM), lambda b: (b, 0, 0)),
            pl.BlockSpec((None, BAND, HEADS_DIM), lambda b: (b, 0, 0)),
            _const_spec(bias.shape),
        ],
        out_specs=pl.BlockSpec((CHUNK, HEADS_DIM), lambda b: (b, 0)),
        out_shape=jax.ShapeDtypeStruct((nb * CHUNK, HEADS_DIM), BF16),
        compiler_params=pltpu.CompilerParams(dimension_semantics=("arbitrary",)),
        name="attn_sample",
    )(q, kext, vext, bias)


def _same_head():
    r = lax.broadcasted_iota(jnp.int32, (GROUP, GROUP), 0) // HEAD
    c = lax.broadcasted_iota(jnp.int32, (GROUP, GROUP), 1) // HEAD
    return r == c


def _blk(x, same_head):
    return jnp.where(same_head, jnp.concatenate([x] * 4, axis=0), 0.0).astype(BF16)


def _wkv_prep(units, hooks):
    same_head = _same_head()
    t64 = lax.broadcasted_iota(jnp.int32, (CHUNK, GROUP), 0)
    i64 = lax.broadcasted_iota(jnp.int32, (CHUNK, GROUP), 1) % HEAD
    strict = i64 < t64
    incl = i64 <= t64
    pending = list(hooks)

    def stage_done():
        if pending:
            pending.pop(0)()

    def blk(x):
        return _blk(x, same_head)

    n, a_ak, a_rb, a_rk = [], [], [], []
    for ld in units:
        lhs = jnp.concatenate([ld("at"), ld("rt")], axis=0).astype(BF16)
        rhs = jnp.concatenate([blk(ld("bt")), blk(ld("kt"))], axis=0)
        a_all = _dot_nt(lhs, rhs)
        n.append(jnp.where(strict, a_all[0:CHUNK, 0:GROUP], 0.0))
        a_ak.append(jnp.where(strict, a_all[0:CHUNK, GROUP:], 0.0))
        a_rb.append(jnp.where(incl, a_all[CHUNK:, 0:GROUP], 0.0))
        a_rk.append(jnp.where(incl, a_all[CHUNK:, GROUP:], 0.0))
    stage_done()

    x0 = [_dot(a.astype(BF16), blk(ld("v"))) for a, ld in zip(a_ak, units)]
    npow = [_dot(m.astype(BF16), blk(m)) for m in n]
    t = [jnp.where(i64 == t64, 1.0, 0.0) + m for m in n]
    stage_done()
    for _ in range(4):
        prod = [_dot(jnp.concatenate([a, b], axis=0).astype(BF16), blk(b))
                for a, b in zip(t, npow)]
        t = [a + p[0:CHUNK] for a, p in zip(t, prod)]
        npow = [p[CHUNK:] for p in prod]
        stage_done()
    t = [a + _dot(a.astype(BF16), blk(b)) for a, b in zip(t, npow)]
    stage_done()
    res = [_dot(a.astype(BF16), jnp.concatenate([blk(ld("at")), blk(x)], axis=1))
           for a, x, ld in zip(t, x0, units)]
    stage_done()
    while pending:
        stage_done()
    return [(r[:, 0:GROUP], r[:, GROUP:], b, k) for r, b, k in zip(res, a_rb, a_rk)]


def _wkv_step_stages(units, states, out):
    same_head = _same_head()
    held = {}

    def blk(x):
        return _blk(x, same_head)

    def first():
        held["sblk"] = [blk(st) for st in states]
        held["u"] = [_dot(ld("ah").astype(BF16), sb) + ld("vh")
                     for ld, sb in zip(units, held["sblk"])]

    def second():
        for i, ld in enumerate(units):
            u, v = held["u"][i], ld("v")
            y = _dot(jnp.concatenate([ld("rt"), ld("arb"), ld("ark")], axis=1).astype(BF16),
                     jnp.concatenate([held["sblk"][i], blk(u), blk(v)], axis=0))
            g = _dot_tn(jnp.concatenate([ld("btd"), ld("ktd")], axis=0).astype(BF16),
                        jnp.concatenate([u, v], axis=0).astype(BF16))
            g = jnp.where(same_head, g, 0.0)
            delta = g[0:64] + g[64:128] + g[128:192] + g[192:256]
            out.append(y)
            states[i] = states[i] * ld("fdec") + delta

    return [first, second]


def _wkv_kernel(p_ref, pn_ref, shift0_ref, state0_ref, mu_ref, w0_ref, wwa_ref, a0_ref, gup_ref,
                kkw_ref, kaw_ref, rkw_ref, gng_ref, gnb_ref, ones_ref, ltri_ref,
                out_ref, shift_out_ref, state_out_ref,
                st_scr, rt_scr, at_scr, kt_scr, bt_scr, v_scr, btd_scr, ktd_scr,
                fdec_scr, ah_scr, vh_scr, arb_scr, ark_scr, g_scr, bon_scr, *, ts, steps):
    j = pl.program_id(1)

    @pl.when(j == 0)
    def _():
        st_scr[...] = state0_ref[...]

    def bdsum(x, terms):
        ones = ones_ref[...]
        halves = []
        for c0 in range(0, HEADS_DIM, GROUP):
            parts = _split3(x[:, c0:c0 + GROUP])[:terms]
            acc = _dot(parts[0], ones)
            for part in parts[1:]:
                acc = acc + _dot(part, ones)
            halves.append(acc)
        return jnp.concatenate(halves, axis=1)

    pq = ltri_ref.shape[0]
    per = pq // CHUNK
    n_piece = ts // pq

    def pre_stages(q, upcoming=False):
        r0 = q * pq
        rows = slice(r0, r0 + pq)
        src_ref = pn_ref if upcoming else p_ref
        held = {}

        def shifted(c0, w):
            cols = slice(c0, c0 + w)
            if upcoming:
                prev_row = p_ref[ts - 8:ts, cols][7:8]
            elif q == 0:
                prev_row = shift0_ref[:, cols]
            else:
                prev_row = p_ref[r0 - 8:r0, cols][7:8]
            top = lax.broadcasted_iota(jnp.int32, (pq, 1), 0) == 0
            pb = src_ref[rows, cols]
            prev = jnp.where(top, prev_row, pltpu.roll(pb, 1, 0))
            return pb + (prev - pb) * mu_ref[:, cols]

        def s_lora():
            lora = shifted(3 * HEADS_DIM, 256)
            lwla = lora[:, 0:128]
            lane = lax.broadcasted_iota(jnp.int32, (pq, 128), 1)
            held["raw"] = _dot(jnp.where(lane < LORA_W, jnp.tanh(lwla), lwla).astype(BF16),
                               wwa_ref[...])
            g_scr[rows, :] = _dot(_sigmoid(lora[:, 128:256]).astype(BF16), gup_ref[...])
            held["k"] = shifted(HEADS_DIM, HEADS_DIM)

        def s_decay():
            w_pre = w0_ref[...] + held["raw"][:, 0:HEADS_DIM]
            softplus = jnp.maximum(-w_pre, 0.0) + jnp.log(1.0 + jnp.exp(-jnp.abs(w_pre)))
            lw = -jnp.exp(-softplus - 0.5)
            ltri = ltri_ref[...]
            hi, mid, _ = _split3(lw)
            held["cum"] = _dot(ltri, hi) + _dot(ltri, mid)
            held["lw"] = lw

        def s_keys():
            k = held["k"]
            a = _sigmoid(a0_ref[...] + held["raw"][:, HEADS_DIM:])
            kk = k * kkw_ref[...]
            held["nrm"] = bdsum(kk * kk, 1)
            held.update(a=a, kk=kk, k2=k * (1.0 + (a - 1.0) * kaw_ref[...]))

        def s_bonus():
            r = shifted(0, HEADS_DIM)
            held["bon"] = bdsum(r * held["k2"] * rkw_ref[...], 1)
            rt_scr[rows, :] = r * jnp.exp(held["cum"])

        def s_norm():
            cum = held["cum"]
            kk = held["kk"] / jnp.maximum(jnp.sqrt(held["nrm"]), 1e-12)
            cum3 = cum.reshape(per, CHUNK, HEADS_DIM)
            cend = jnp.broadcast_to(cum3[:, CHUNK - 1:CHUNK, :], cum3.shape).reshape(pq, HEADS_DIM)
            diag = (lax.broadcasted_iota(jnp.int32, (pq, HEADS_DIM), 0) % CHUNK
                    == lax.broadcasted_iota(jnp.int32, (pq, HEADS_DIM), 1) % HEAD)
            held["fdec"] = bdsum(jnp.where(diag, jnp.exp(cend), 0.0), 2)
            at_scr[rows, :] = -kk * jnp.exp(cum - held["lw"])
            held.update(kk=kk, cend=cend)

        def s_inv():
            e_inv = jnp.exp(-held["cum"])
            kt_scr[rows, :] = held["k2"] * e_inv
            bt_scr[rows, :] = held["kk"] * held["a"] * e_inv

        def s_rel():
            e_rel = jnp.exp(held["cend"] - held["cum"])
            ktd_scr[rows, :] = held["k2"] * e_rel
            btd_scr[rows, :] = held["kk"] * held["a"] * e_rel
            fdec_scr[rows, :] = held["fdec"]

        def s_value():
            v = shifted(2 * HEADS_DIM, HEADS_DIM)
            v_scr[rows, :] = v
            bon_scr[rows, :] = held["bon"] * v

        return [s_lora, s_decay, s_keys, s_bonus, s_norm, s_inv, s_rel, s_value]

    def chunk_rows(c):
        return slice(c * CHUNK, (c + 1) * CHUNK)

    groups = [slice(g0, g0 + GROUP) for g0 in range(0, HEADS_DIM, GROUP)]

    named = dict(at=at_scr, rt=rt_scr, kt=kt_scr, bt=bt_scr, v=v_scr, btd=btd_scr, ktd=ktd_scr,
                 fdec=fdec_scr, ah=ah_scr, vh=vh_scr, arb=arb_scr, ark=ark_scr)
    prep_outs = (ah_scr, vh_scr, arb_scr, ark_scr)

    def loader(c, cols):
        rows = chunk_rows(c)
        return lambda name: named[name][rows, cols]

    def run(step_piece, prep_piece, extra):
        step_chunks = [] if step_piece is None else [step_piece * per + i for i in range(per)]
        prep_chunks = [] if prep_piece is None else [prep_piece * per + i for i in range(per)]
        states = [st_scr[:, cols] for cols in groups] if step_chunks else []
        stages, ys = [], []
        for c in step_chunks:
            out = []
            ys.append(out)
            stages += _wkv_step_stages([loader(c, cols) for cols in groups], states, out)
        if step_chunks:
            rows = slice(step_piece * pq, (step_piece + 1) * pq)
            gn = {}

            def gn_mean():
                gn["y"] = jnp.concatenate([jnp.concatenate(out, axis=1) for out in ys], axis=0)
                gn["sum"] = bdsum(gn["y"], 1)

            def gn_var():
                gn["d"] = gn["y"] - gn["sum"] * (1.0 / HEAD)
                gn["sq"] = bdsum(gn["d"] * gn["d"], 1)

            def gn_out():
                var = gn["sq"] * (1.0 / HEAD)
                yn = (gn["d"] * lax.rsqrt(var + GN_EPS) * gng_ref[...] + gnb_ref[...]
                      + bon_scr[rows, :])
                out_ref[rows, :] = (yn * g_scr[rows, :]).astype(BF16)

            stages += [gn_mean, gn_var, gn_out]
        hooks = []
        for i in range(max(len(stages), len(extra))):
            both = stages[i:i + 1] + extra[i:i + 1]
            hooks.append(lambda both=both: [f() for f in both])
        units = [(c, cols) for c in prep_chunks for cols in groups]
        prep_out = _wkv_prep([loader(c, cols) for c, cols in units], hooks)
        for cols, st in zip(groups, states):
            st_scr[:, cols] = st
        for (c, cols), outs in zip(units, prep_out):
            for ref, val in zip(prep_outs, outs):
                ref[chunk_rows(c), cols] = val

    look_ahead = steps > 1
    assert n_piece >= 3 or not look_ahead

    def first_piece():
        for stage in pre_stages(0):
            stage()

    if look_ahead:
        pl.when(j == 0)(first_piece)
    else:
        first_piece()
    for piece in range(n_piece + 1):
        if piece + 1 < n_piece:
            extra = pre_stages(piece + 1)
        elif piece + 1 == n_piece and look_ahead:
            extra = pre_stages(0, upcoming=True)
        else:
            extra = []
        run(piece - 1 if piece >= 1 else None, piece if piece < n_piece else None, extra)

    @pl.when(j == steps - 1)
    def _():
        shift_out_ref[...] = p_ref[ts - 1:ts, :]
        state_out_ref[...] = st_scr[...]


def _wkv_call(p2d, shift0, state0, prm, n_seq, steps, ts):
    rows = p2d.shape[0]
    row_map = lambda s, j: (s * steps + j, 0)
    next_map = lambda s, j: (s * steps + jnp.minimum(j + 1, steps - 1), 0)
    seq3 = lambda s, j: (s, 0, 0)
    vec = _const_spec((1, HEADS_DIM))
    big = pltpu.VMEM((ts, HEADS_DIM), F32)
    ltri = prm["ltri"][min(ts, WKV_PIECE)]
    return pl.pallas_call(
        functools.partial(_wkv_kernel, ts=ts, steps=steps),
        grid=(n_seq, steps),
        in_specs=[
            pl.BlockSpec((ts, RW_COLS), row_map),
            pl.BlockSpec((ts, RW_COLS), next_map),
            pl.BlockSpec((None, 1, RW_COLS), seq3),
            pl.BlockSpec((None, HEAD, HEADS_DIM), seq3),
            _const_spec((1, RW_COLS)),
            vec,
            _const_spec((128, 2 * HEADS_DIM)),
            vec,
            _const_spec((LORA_G, HEADS_DIM)),
            vec, vec, vec, vec, vec,
            _const_spec((GROUP, GROUP)),
            _const_spec(ltri.shape),
        ],
        out_specs=[
            pl.BlockSpec((ts, HEADS_DIM), row_map),
            pl.BlockSpec((None, 1, RW_COLS), seq3),
            pl.BlockSpec((None, HEAD, HEADS_DIM), seq3),
        ],
        out_shape=[
            jax.ShapeDtypeStruct((rows, HEADS_DIM), BF16),
            jax.ShapeDtypeStruct((n_seq, 1, RW_COLS), F32),
            jax.ShapeDtypeStruct((n_seq, HEAD, HEADS_DIM), F32),
        ],
        scratch_shapes=[
            pltpu.VMEM((HEAD, HEADS_DIM), F32),
        ] + [big] * 14,
        compiler_params=pltpu.CompilerParams(
            dimension_semantics=("arbitrary", "arbitrary"), vmem_limit_bytes=VMEM_LIMIT),
        name="wkv",
    )(p2d, p2d, shift0, state0, prm["mu"], prm["w0"], prm["wwa"], prm["a0"], prm["gup"],
      prm["kk"], prm["ka"], prm["rk"], prm["gng"], prm["gnb"], prm["ones"], ltri)


def _mixffn_kernel(x_ref, att_ref, rw_ref, gate_ref, mod_ref, convp_ref,
                   lnig_ref, lnib_ref, ln1g_ref, ln1b_ref, ln2g_ref, ln2b_ref,
                   wa_ref, wr_ref, wo_ref, wup_ref, cw_ref, cb_ref, wdn_ref,
                   y_ref, convo_ref, carry_scr, yb_scr, *, tm, steps):
    j = pl.program_id(0)
    groups = mod_ref.shape[0]
    assert groups == 1 or steps == 1

    if groups == 1:
        @pl.when(j == 0)
        def _():
            carry_scr[8 - (CONV_W - 1):8, :] = convp_ref[0]

    def mod(idx):
        return _mod_row(mod_ref, idx, tm)

    xn = _layer_norm(x_ref[...], lnig_ref[...], lnib_ref[...])
    merged = (gate_ref[:, 0:D_MODEL] * _dot(att_ref[...], wa_ref[...])
              + gate_ref[:, D_MODEL:] * _dot(rw_ref[...], wr_ref[...]))
    mix = _dot(merged.astype(BF16), wo_ref[...])
    x1 = _layer_norm(ALPHA * xn + (1.0 + mod(2)) * mix, ln1g_ref[...], ln1b_ref[...])
    h2 = (x1 * (1.0 + mod(4)) + mod(3)).astype(BF16)

    cw_blk = 256
    glen = tm // groups
    row8 = lax.broadcasted_iota(jnp.int32, (8, cw_blk), 0)
    grow = lax.broadcasted_iota(jnp.int32, (tm, cw_blk), 0) % glen
    for c in range(0, D_FF, cw_blk):
        cs = slice(c, c + cw_blk)
        uc = _dot(h2, wup_ref[:, cs])
        uv = _dot(h2, wup_ref[:, D_FF + c:D_FF + c + cw_blk])
        r1 = pltpu.roll(uc, 1, 0)
        r2 = pltpu.roll(uc, 2, 0)
        if groups == 1:
            c6 = carry_scr[6:7, cs]
            c7 = carry_scr[7:8, cs]
            s1 = jnp.concatenate([jnp.where(row8 == 0, c7, r1[0:8]), r1[8:]], axis=0)
            s2 = jnp.concatenate(
                [jnp.where(row8 == 0, c6, jnp.where(row8 == 1, c7, r2[0:8])), r2[8:]], axis=0)
            carry_scr[:, cs] = uc[tm - 8:tm, :]
            tail = uc[tm - (CONV_W - 1):tm, :][None]
        else:
            hist = jnp.broadcast_to(convp_ref[:, :, cs][:, :, None, :],
                                    (groups, CONV_W - 1, glen, cw_blk))
            c6 = hist[:, 0].reshape(tm, cw_blk)
            c7 = hist[:, 1].reshape(tm, cw_blk)
            s1 = jnp.where(grow == 0, c7, r1)
            s2 = jnp.where(grow == 0, c6, jnp.where(grow == 1, c7, r2))
            tail = uc.reshape(groups, glen, cw_blk)[:, glen - (CONV_W - 1):, :]
        conv = cb_ref[:, cs] + s2 * cw_ref[0:1, cs] + s1 * cw_ref[1:2, cs] + uc * cw_ref[2:3, cs]
        yb_scr[:, cs] = (conv * _sigmoid(conv) * uv).astype(BF16)

        @pl.when(j == steps - 1)
        def _():
            convo_ref[:, :, cs] = tail

    ff = _dot(yb_scr[...], wdn_ref[...])
    y_ref[...] = _layer_norm(ALPHA * x1 + (1.0 + mod(5)) * ff, ln2g_ref[...], ln2b_ref[...])


def _mixffn_call(x2d, att, rw, gates, mod, conv_prev, prm, steps, tm):
    rows = x2d.shape[0]
    groups = mod.shape[0]
    row_map = lambda j: (j, 0)
    vec = _const_spec((1, D_MODEL))
    return pl.pallas_call(
        functools.partial(_mixffn_kernel, tm=tm, steps=steps),
        grid=(steps,),
        in_specs=[
            pl.BlockSpec((tm, D_MODEL), row_map),
            pl.BlockSpec((tm, HEADS_DIM), row_map),
            pl.BlockSpec((tm, HEADS_DIM), row_map),
            pl.BlockSpec((tm, GATE_COLS), row_map),
            _const_spec((groups, 6, D_MODEL)),
            _const_spec((groups, CONV_W - 1, D_FF)),
            vec, vec, vec, vec, vec, vec,
            _const_spec((HEADS_DIM, D_MODEL)),
            _const_spec((HEADS_DIM, D_MODEL)),
            _const_spec((D_MODEL, D_MODEL)),
            _const_spec((D_MODEL, 2 * D_FF)),
            _const_spec((CONV_W, D_FF)),
            _const_spec((1, D_FF)),
            _const_spec((D_FF, D_MODEL)),
        ],
        out_specs=[
            pl.BlockSpec((tm, D_MODEL), row_map),
            pl.BlockSpec((groups, CONV_W - 1, D_FF), lambda j: (0, 0, 0)),
        ],
        out_shape=[
            jax.ShapeDtypeStruct((rows, D_MODEL), F32),
            jax.ShapeDtypeStruct((groups, CONV_W - 1, D_FF), F32),
        ],
        scratch_shapes=[pltpu.VMEM((8, D_FF), F32), pltpu.VMEM((tm, D_FF), BF16)],
        compiler_params=pltpu.CompilerParams(
            dimension_semantics=("arbitrary",), vmem_limit_bytes=VMEM_LIMIT),
        name="mixffn",
    )(x2d, att, rw, gates, mod, conv_prev,
      prm["lnig"], prm["lnib"], prm["ln1g"], prm["ln1b"], prm["ln2g"], prm["ln2b"],
      prm["wa"], prm["wr"], prm["wo"], prm["wup"], prm["cw"], prm["cb"], prm["wdn"])


def _pair_bias(table):
    assert CHUNK - 1 <= REL_CLIP
    top = ATT_REACH + CHUNK - 1
    n_far = top - REL_CLIP + 1
    far = jnp.broadcast_to(table[:, 2 * REL_CLIP:], (N_HEADS, n_far))
    lo_idx = top - (BAND + CHUNK - 2) + REL_CLIP
    near = table[:, lo_idx:2 * REL_CLIP][:, ::-1]
    ext = jnp.concatenate([far, near], axis=1).astype(F32) * LOG2E
    bias = jnp.stack([ext[:, CHUNK - 1 - q:CHUNK - 1 - q + BAND] for q in range(CHUNK)], axis=1)
    return bias.reshape(N_HEADS // 2, 2 * CHUNK, BAND)


def _chunk_ltri(ts):
    t = jnp.arange(ts)
    return ((t[:, None] // CHUNK == t[None, :] // CHUNK) & (t[None, :] <= t[:, None])).astype(BF16)


def _trunk(x2d, mod, shift0, state0, conv_prev, kext_fn, prm, n_seq, prompt):
    rows = x2d.shape[0]
    tm = min(rows, ROW_TILE)
    steps = rows // tm
    q, k, v, kv32, p, gates = _inproj_call(
        x2d, mod, prm["lnig"], prm["lnib"], prm["win"], steps, tm, prompt)
    kext, vext = kext_fn(k, v)
    if prompt:
        att = _attn_prompt_call(q, kext, vext, prm["bias"])
    else:
        att = _attn_sample_call(q, kext, vext, prm["bias"])
    ts = min(rows // n_seq, ROW_TILE)
    rw, shift, state = _wkv_call(p, shift0, state0, prm, n_seq, rows // n_seq // ts, ts)
    y, conv = _mixffn_call(x2d, att, rw, gates, mod, conv_prev, prm, steps, tm)
    return y, kv32, state, shift, conv


def kernel(x_prompt, x_sample, cache_attn_k, cache_attn_v, state_rwkv, state_shift, state_conv,
           c_prompt, c_sample, ln_in_g, ln_in_b, w_ada, b_ada, w_in, attn_rel_bias,
           rwkv_mu, rwkv_w0, rwkv_w_up, rwkv_a0, rwkv_a_up, rwkv_g_up, rwkv_k_k, rwkv_k_a,
           rwkv_r_k, rwkv_gn_g, rwkv_gn_b, w_branch_attn, w_branch_rwkv, w_out,
           ln1_g, ln1_b, ln2_g, ln2_b, w_ffn_up, ffn_conv_w, ffn_conv_b, w_ffn_down):
    bp, sp, _ = x_prompt.shape
    bs, ss, _ = x_sample.shape
    assert bp == 1 and ss == CHUNK and w_ada.shape[0] == DEPTH
    assert sp % ROW_TILE == 0 and bs * ss <= ROW_TILE and cache_attn_k.shape[2] == ATT_REACH

    row = lambda a: a.reshape(1, -1)
    wwa = jnp.zeros((LORA_W + LORA_A, 2 * HEADS_DIM), F32)
    wwa = wwa.at[:LORA_W, :HEADS_DIM].set(rwkv_w_up[0]).at[LORA_W:, HEADS_DIM:].set(rwkv_a_up[0])
    head_id = jnp.arange(GROUP) // HEAD
    prm = dict(
        lnig=row(ln_in_g), lnib=row(ln_in_b),
        ln1g=row(ln1_g[0]), ln1b=row(ln1_b[0]), ln2g=row(ln2_g[0]), ln2b=row(ln2_b[0]),
        win=w_in[0].astype(BF16), bias=_pair_bias(attn_rel_bias[0]),
        mu=row(rwkv_mu[0]), w0=row(rwkv_w0[0]), wwa=wwa.astype(BF16), a0=row(rwkv_a0[0]),
        gup=rwkv_g_up[0].astype(BF16), kk=row(rwkv_k_k[0]), ka=row(rwkv_k_a[0]),
        rk=row(rwkv_r_k[0]), gng=row(rwkv_gn_g[0]), gnb=row(rwkv_gn_b[0]),
        ones=(head_id[:, None] == head_id[None, :]).astype(BF16),
        ltri={WKV_PIECE: _chunk_ltri(WKV_PIECE), CHUNK: _chunk_ltri(CHUNK)},
        wa=w_branch_attn[0].astype(BF16), wr=w_branch_rwkv[0].astype(BF16),
        wo=w_out[0].astype(BF16), wup=w_ffn_up[0].astype(BF16),
        cw=ffn_conv_w[0], cb=row(ffn_conv_b[0]), wdn=w_ffn_down[0].astype(BF16),
    )

    n_c = bp + bs
    c_all = jnp.concatenate([c_prompt, c_sample, jnp.zeros((16 - n_c, D_MODEL), F32)], axis=0)
    mod = _mod_call(c_all, w_ada[0], row(b_ada[0])).reshape(16, 6, D_MODEL)

    y_p, kv_p, st_p, sh_p, cv_p = _trunk(
        x_prompt.reshape(sp, D_MODEL), mod[0:bp],
        jnp.zeros((bp, 1, RW_COLS), F32), jnp.zeros((bp, HEAD, HEADS_DIM), F32),
        jnp.zeros((bp, CONV_W - 1, D_FF), F32), lambda k, v: (k, v), prm,
        n_seq=bp, prompt=True)

    def kext_sample(k, v):
        ck = cache_attn_k[0].reshape(bs, ATT_REACH, HEADS_DIM).astype(BF16)
        cv = cache_attn_v[0].reshape(bs, ATT_REACH, HEADS_DIM).astype(BF16)
        return (jnp.concatenate([ck, k.reshape(bs, ss, HEADS_DIM)], axis=1),
                jnp.concatenate([cv, v.reshape(bs, ss, HEADS_DIM)], axis=1))

    st0 = jnp.transpose(state_rwkv[0], (0, 3, 1, 2)).reshape(bs, HEAD, HEADS_DIM)
    y_s, kv_s, st_s, sh_s, cv_s = _trunk(
        x_sample.reshape(bs * ss, D_MODEL), mod[bp:n_c],
        state_shift[0], st0, state_conv[0], kext_sample, prm,
        n_seq=bs, prompt=False)

    def state_out(st, b):
        return jnp.transpose(st.reshape(b, HEAD, N_HEADS, HEAD), (0, 2, 3, 1))[None]

    hs = (N_HEADS, HEAD)
    return (
        y_p.reshape(bp, sp, D_MODEL),
        y_s.reshape(bs, ss, D_MODEL),
        kv_p[:, :HEADS_DIM].reshape(1, bp, ATT_REACH, *hs),
        kv_p[:, HEADS_DIM:].reshape(1, bp, ATT_REACH, *hs),
        kv_s[:, :HEADS_DIM].reshape(1, bs, ss, *hs),
        kv_s[:, HEADS_DIM:].reshape(1, bs, ss, *hs),
        state_out(st_p, bp),
        state_out(st_s, bs),
        sh_p[None],
        sh_s[None],
        cv_p[None],
        cv_s[None],
    )
```

```python
import functools

import jax
import jax.numpy as jnp
from jax import lax
from jax.experimental import pallas as pl
from jax.experimental.pallas import tpu as pltpu

F32 = jnp.float32
BF16 = jnp.bfloat16

D_MODEL = 1024
CHUNK = 64
ATT_REACH = 512
BAND = ATT_REACH + CHUNK
N_HEADS = 8
HEAD = 64
HEADS_DIM = N_HEADS * HEAD
REL_CLIP = 128
LORA_W = 64
LORA_A = 64
LORA_G = 128
ATT_COLS = 3 * HEADS_DIM
RW_COLS = 3 * HEADS_DIM + LORA_W + LORA_A + LORA_G
GATE_COLS = 2 * D_MODEL
D_FF = 2816
CONV_W = 3
LN_EPS = 1e-5
GN_EPS = 64e-5
DEPTH = 1
ALPHA = (2 * DEPTH) ** 0.25
LOG2E = 1.4426950408889634

GROUP = 256
ROW_TILE = 512
WKV_PIECE = 128
VMEM_LIMIT = 56 * 1024 * 1024


def _const_spec(shape):
    nd = len(shape)
    return pl.BlockSpec(shape, lambda *_: (0,) * nd, pipeline_mode=pl.Buffered(1))


def _layer_norm(x, g, b):
    mu = jnp.mean(x, axis=-1, keepdims=True)
    xc = x - mu
    var = jnp.mean(xc * xc, axis=-1, keepdims=True)
    return xc * lax.rsqrt(var + LN_EPS) * g + b


def _sigmoid(x):
    return 1.0 / (1.0 + jnp.exp(-x))


def _split3(x):
    hi = x.astype(BF16)
    r1 = x - hi.astype(F32)
    mid = r1.astype(BF16)
    lo = (r1 - mid.astype(F32)).astype(BF16)
    return hi, mid, lo


def _dot(a, b):
    return jnp.dot(a, b, preferred_element_type=F32)


def _dot_nt(a, b):
    return lax.dot_general(a, b, (((1,), (1,)), ((), ())), preferred_element_type=F32)


def _dot_tn(a, b):
    return lax.dot_general(a, b, (((0,), (0,)), ((), ())), preferred_element_type=F32)


def _mod_kernel(c_ref, w_ref, b_ref, o_ref):
    c = c_ref[...]
    s = (c * _sigmoid(c)).astype(BF16)
    o_ref[...] = _dot(s, w_ref[...].astype(BF16)) + b_ref[...]


def _mod_call(c_all, w_ada, b_ada):
    n = c_all.shape[0]
    nblk = 6
    return pl.pallas_call(
        _mod_kernel,
        grid=(nblk,),
        in_specs=[
            pl.BlockSpec((n, D_MODEL), lambda i: (0, 0)),
            pl.BlockSpec((D_MODEL, D_MODEL), lambda i: (0, i)),
            pl.BlockSpec((1, D_MODEL), lambda i: (0, i)),
        ],
        out_specs=pl.BlockSpec((n, D_MODEL), lambda i: (0, i)),
        out_shape=jax.ShapeDtypeStruct((n, 6 * D_MODEL), F32),
        compiler_params=pltpu.CompilerParams(dimension_semantics=("arbitrary",)),
        name="mod",
    )(c_all, w_ada, b_ada)


def _mod_row(mod_ref, idx, tm):
    groups, _, d = mod_ref.shape
    m = mod_ref[:, idx:idx + 1, :]
    if groups == 1:
        return m[0]
    return jnp.broadcast_to(m, (groups, tm // groups, d)).reshape(tm, d)


def _inproj_kernel(x_ref, mod_ref, lng_ref, lnb_ref, w_ref,
                   q_ref, k_ref, v_ref, kv_ref, p_ref, g_ref, xn_ref, *, lead):
    tm = x_ref.shape[0]
    j = pl.program_id(0)

    @pl.when(j < lead)
    def _():
        k_ref[...] = jnp.zeros_like(k_ref)
        v_ref[...] = jnp.zeros_like(v_ref)

    @pl.when(j >= lead)
    def _():
        xn = _layer_norm(x_ref[...], lng_ref[...], lnb_ref[...])
        xn_ref[...] = xn
        hb = (xn * (1.0 + _mod_row(mod_ref, 1, tm)) + _mod_row(mod_ref, 0, tm)).astype(BF16)

        def seg(a, b):
            return _dot(hb, w_ref[:, a:b])

        q_ref[...] = (seg(0, HEADS_DIM) * (HEAD ** -0.5 * LOG2E)).astype(BF16)
        k = seg(HEADS_DIM, 2 * HEADS_DIM)
        k_ref[...] = k.astype(BF16)
        kv_ref[:, 0:HEADS_DIM] = k
        v = seg(2 * HEADS_DIM, 3 * HEADS_DIM)
        v_ref[...] = v.astype(BF16)
        kv_ref[:, HEADS_DIM:2 * HEADS_DIM] = v
        for c in range(0, RW_COLS, 256):
            p_ref[:, c:c + 256] = seg(ATT_COLS + c, ATT_COLS + c + 256)
        g0 = ATT_COLS + RW_COLS
        for c in range(0, GATE_COLS, 512):
            g_ref[:, c:c + 512] = _sigmoid(seg(g0 + c, g0 + c + 512))


def _inproj_call(x2d, mod, ln_g, ln_b, w_in_b, steps, tm, prompt):
    rows = x2d.shape[0]
    in_cols = w_in_b.shape[1]
    groups = mod.shape[0]
    if prompt:
        lead = ATT_REACH // tm
        kv_rows = ATT_REACH
        row_map = lambda j: (jnp.maximum(j - lead, 0), 0)
        kv_map = lambda j: (jnp.maximum(j - steps, 0), 0)
        ext_map = lambda j: (j, 0)
    else:
        lead = 0
        kv_rows = rows
        row_map = kv_map = ext_map = lambda j: (j, 0)
    return pl.pallas_call(
        functools.partial(_inproj_kernel, lead=lead),
        grid=(steps + lead,),
        in_specs=[
            pl.BlockSpec((tm, D_MODEL), row_map),
            _const_spec((groups, 6, D_MODEL)),
            _const_spec((1, D_MODEL)),
            _const_spec((1, D_MODEL)),
            _const_spec((D_MODEL, in_cols)),
        ],
        out_specs=[
            pl.BlockSpec((tm, HEADS_DIM), row_map),
            pl.BlockSpec((tm, HEADS_DIM), ext_map),
            pl.BlockSpec((tm, HEADS_DIM), ext_map),
            pl.BlockSpec((tm, 2 * HEADS_DIM), kv_map),
            pl.BlockSpec((tm, RW_COLS), row_map),
            pl.BlockSpec((tm, GATE_COLS), row_map),
            pl.BlockSpec((tm, D_MODEL), row_map),
        ],
        out_shape=[
            jax.ShapeDtypeStruct((rows, HEADS_DIM), BF16),
            jax.ShapeDtypeStruct((rows + lead * tm, HEADS_DIM), BF16),
            jax.ShapeDtypeStruct((rows + lead * tm, HEADS_DIM), BF16),
            jax.ShapeDtypeStruct((kv_rows, 2 * HEADS_DIM), F32),
            jax.ShapeDtypeStruct((rows, RW_COLS), F32),
            jax.ShapeDtypeStruct((rows, GATE_COLS), F32),
            jax.ShapeDtypeStruct((rows, D_MODEL), F32),
        ],
        compiler_params=pltpu.CompilerParams(
            dimension_semantics=("arbitrary",), vmem_limit_bytes=VMEM_LIMIT),
        name="inproj",
    )(x2d, mod, ln_g, ln_b, w_in_b)


def _attn_chunks(chunks, bias_ref):
    lane = lax.broadcasted_iota(jnp.int32, (CHUNK, 128), 1)
    first = lane < HEAD
    pairs = [slice(pr * 128, (pr + 1) * 128) for pr in range(N_HEADS // 2)]
    scores = []
    for qc, kb, _, _ in chunks:
        for sl in pairs:
            q2 = qc[:, sl].astype(F32)
            qs = jnp.concatenate([jnp.where(first, q2, 0.0), jnp.where(first, 0.0, q2)],
                                 axis=0).astype(BF16)
            scores.append(_dot_nt(qs, kb(sl)))
    probs, sums = [], []
    for i, s in enumerate(scores):
        thr = chunks[i // len(pairs)][3]
        s = s + bias_ref[i % len(pairs)]
        if thr is not None:
            col = lax.broadcasted_iota(jnp.int32, s.shape, 1)
            s = jnp.where(col >= thr, s, -jnp.inf)
        e = jnp.exp2(s - jnp.max(s, axis=1, keepdims=True))
        sums.append(jnp.sum(e, axis=1, keepdims=True))
        probs.append(e.astype(BF16))
    outs = []
    for ci, (_, _, vb, _) in enumerate(chunks):
        cols = []
        for pi, sl in enumerate(pairs):
            i = ci * len(pairs) + pi
            o = _dot(probs[i], vb(sl)) / sums[i]
            cols.append(jnp.where(first, o[0:CHUNK], o[CHUNK:2 * CHUNK]))
        outs.append(jnp.concatenate(cols, axis=1).astype(BF16))
    return outs


def _attn_prompt_kernel(q_ref, ka_ref, kb_ref, va_ref, vb_ref, bias_ref, o_ref, kbuf, vbuf,
                        *, chunks):
    tq = chunks * CHUNK
    kbuf[0:tq, :] = ka_ref[...]
    kbuf[tq:2 * tq, :] = kb_ref[...]
    vbuf[0:tq, :] = va_ref[...]
    vbuf[tq:2 * tq, :] = vb_ref[...]
    s = pl.program_id(0)

    per = 2

    def run(masked):
        def body(i, carry):
            units, starts = [], []
            for k in range(per):
                g = i * per + k
                r0 = pl.multiple_of(g * CHUNK, CHUNK)
                thr = ATT_REACH - (s * chunks + g) * CHUNK if masked else None
                band = pl.ds(r0, BAND)
                units.append((q_ref[pl.ds(r0, CHUNK), :],
                              lambda sl, band=band: kbuf[band, sl],
                              lambda sl, band=band: vbuf[band, sl], thr))
                starts.append(r0)
            for r0, o in zip(starts, _attn_chunks(units, bias_ref)):
                o_ref[pl.ds(r0, CHUNK), :] = o
            return carry

        lax.fori_loop(0, chunks // per, body, 0)

    assert chunks * CHUNK >= ATT_REACH
    pl.when(s == 0)(functools.partial(run, True))
    pl.when(s != 0)(functools.partial(run, False))


def _attn_prompt_call(q, kext, vext, bias):
    rows = q.shape[0]
    tq = ATT_REACH
    blk = lambda off: pl.BlockSpec((tq, HEADS_DIM), lambda s: (s + off, 0))
    return pl.pallas_call(
        functools.partial(_attn_prompt_kernel, chunks=tq // CHUNK),
        grid=(rows // tq,),
        in_specs=[blk(0), blk(0), blk(1), blk(0), blk(1), _const_spec(bias.shape)],
        out_specs=blk(0),
        out_shape=jax.ShapeDtypeStruct((rows, HEADS_DIM), BF16),
        scratch_shapes=[pltpu.VMEM((2 * tq, HEADS_DIM), BF16), pltpu.VMEM((2 * tq, HEADS_DIM), BF16)],
        compiler_params=pltpu.CompilerParams(dimension_semantics=("arbitrary",)),
        name="attn_prompt",
    )(q, kext, kext, vext, vext, bias)


def _attn_sample_kernel(q_ref, k_ref, v_ref, ck_ref, cv_ref, bias_ref, o_ref):
    def band(cache_ref, new_ref):
        return lambda sl: jnp.concatenate([cache_ref[:, sl].astype(BF16), new_ref[:, sl]], axis=0)

    unit = (q_ref[...], band(ck_ref, k_ref), band(cv_ref, v_ref), None)
    o_ref[...] = _attn_chunks([unit], bias_ref)[0]


def _attn_sample_call(q, k, v, cache_k, cache_v, bias):
    nb = cache_k.shape[0]
    new = pl.BlockSpec((CHUNK, HEADS_DIM), lambda b: (b, 0))
    cache = pl.BlockSpec((None, ATT_REACH, HEADS_DIM), lambda b: (b, 0, 0))
    return pl.pallas_call(
        _attn_sample_kernel,
        grid=(nb,),
        in_specs=[new, new, new, cache, cache, _const_spec(bias.shape)],
        out_specs=new,
        out_shape=jax.ShapeDtypeStruct((nb * CHUNK, HEADS_DIM), BF16),
        compiler_params=pltpu.CompilerParams(dimension_semantics=("arbitrary",)),
        name="attn_sample",
    )(q, k, v, cache_k, cache_v, bias)


def _same_head():
    r = lax.broadcasted_iota(jnp.int32, (GROUP, GROUP), 0) // HEAD
    c = lax.broadcasted_iota(jnp.int32, (GROUP, GROUP), 1) // HEAD
    return r == c


def _blk(x, same_head):
    return jnp.where(same_head, jnp.concatenate([x] * 4, axis=0), 0.0).astype(BF16)


def _wkv_prep(units, hooks):
    same_head = _same_head()
    t64 = lax.broadcasted_iota(jnp.int32, (CHUNK, GROUP), 0)
    i64 = lax.broadcasted_iota(jnp.int32, (CHUNK, GROUP), 1) % HEAD
    strict = i64 < t64
    incl = i64 <= t64
    pending = list(hooks)

    def stage_done():
        if pending:
            pending.pop(0)()

    def blk(x):
        return _blk(x, same_head)

    n, a_ak, a_rb, a_rk = [], [], [], []
    for ld in units:
        lhs = jnp.concatenate([ld("at"), ld("rt")], axis=0).astype(BF16)
        rhs = jnp.concatenate([blk(ld("bt")), blk(ld("kt"))], axis=0)
        a_all = _dot_nt(lhs, rhs)
        n.append(jnp.where(strict, a_all[0:CHUNK, 0:GROUP], 0.0))
        a_ak.append(jnp.where(strict, a_all[0:CHUNK, GROUP:], 0.0))
        a_rb.append(jnp.where(incl, a_all[CHUNK:, 0:GROUP], 0.0))
        a_rk.append(jnp.where(incl, a_all[CHUNK:, GROUP:], 0.0))
    stage_done()

    x0 = [_dot(a.astype(BF16), blk(ld("v"))) for a, ld in zip(a_ak, units)]
    npow = [_dot(m.astype(BF16), blk(m)) for m in n]
    t = [jnp.where(i64 == t64, 1.0, 0.0) + m for m in n]
    stage_done()
    for _ in range(4):
        prod = [_dot(jnp.concatenate([a, b], axis=0).astype(BF16), blk(b))
                for a, b in zip(t, npow)]
        t = [a + p[0:CHUNK] for a, p in zip(t, prod)]
        npow = [p[CHUNK:] for p in prod]
        stage_done()
    t = [a + _dot(a.astype(BF16), blk(b)) for a, b in zip(t, npow)]
    stage_done()
    res = [_dot(a.astype(BF16), jnp.concatenate([blk(ld("at")), blk(x)], axis=1))
           for a, x, ld in zip(t, x0, units)]
    stage_done()
    while pending:
        stage_done()
    return [(r[:, 0:GROUP], r[:, GROUP:], b, k) for r, b, k in zip(res, a_rb, a_rk)]


def _wkv_step_stages(units, states, out):
    same_head = _same_head()
    held = {}

    def blk(x):
        return _blk(x, same_head)

    def first():
        held["sblk"] = [blk(st) for st in states]
        held["u"] = [_dot(ld("ah").astype(BF16), sb) + ld("vh")
                     for ld, sb in zip(units, held["sblk"])]

    def second():
        for i, ld in enumerate(units):
            u, v = held["u"][i], ld("v")
            y = _dot(jnp.concatenate([ld("rt"), ld("arb"), ld("ark")], axis=1).astype(BF16),
                     jnp.concatenate([held["sblk"][i], blk(u), blk(v)], axis=0))
            g = _dot_tn(jnp.concatenate([ld("btd"), ld("ktd")], axis=0).astype(BF16),
                        jnp.concatenate([u, v], axis=0).astype(BF16))
            g = jnp.where(same_head, g, 0.0)
            delta = g[0:64] + g[64:128] + g[128:192] + g[192:256]
            out.append(y)
            states[i] = states[i] * ld("fdec") + delta

    return [first, second]


def _wkv_kernel(p_ref, pn_ref, shift0_ref, state0_ref, mu_ref, w0_ref, wwa_ref, a0_ref, gup_ref,
                kkw_ref, kaw_ref, rkw_ref, gng_ref, gnb_ref, ones_ref, ltri_ref,
                out_ref, shift_out_ref, state_out_ref,
                st_scr, rt_scr, at_scr, kt_scr, bt_scr, v_scr, btd_scr, ktd_scr,
                fdec_scr, ah_scr, vh_scr, arb_scr, ark_scr, g_scr, bon_scr, *, ts, steps):
    j = pl.program_id(1)

    @pl.when(j == 0)
    def _():
        st_scr[...] = state0_ref[...]

    def bdsum(x, terms):
        ones = ones_ref[...]
        halves = []
        for c0 in range(0, HEADS_DIM, GROUP):
            parts = _split3(x[:, c0:c0 + GROUP])[:terms]
            acc = _dot(parts[0], ones)
            for part in parts[1:]:
                acc = acc + _dot(part, ones)
            halves.append(acc)
        return jnp.concatenate(halves, axis=1)

    pq = ltri_ref.shape[0]
    per = pq // CHUNK
    n_piece = ts // pq

    def pre_stages(q, upcoming=False):
        r0 = q * pq
        rows = slice(r0, r0 + pq)
        src_ref = pn_ref if upcoming else p_ref
        held = {}

        def shifted(c0, w):
            cols = slice(c0, c0 + w)
            if upcoming:
                prev_row = p_ref[ts - 8:ts, cols][7:8]
            elif q == 0:
                prev_row = shift0_ref[:, cols]
            else:
                prev_row = p_ref[r0 - 8:r0, cols][7:8]
            top = lax.broadcasted_iota(jnp.int32, (pq, 1), 0) == 0
            pb = src_ref[rows, cols]
            prev = jnp.where(top, prev_row, pltpu.roll(pb, 1, 0))
            return pb + (prev - pb) * mu_ref[:, cols]

        def s_lora():
            lora = shifted(3 * HEADS_DIM, 256)
            lwla = lora[:, 0:128]
            lane = lax.broadcasted_iota(jnp.int32, (pq, 128), 1)
            held["raw"] = _dot(jnp.where(lane < LORA_W, jnp.tanh(lwla), lwla).astype(BF16),
                               wwa_ref[...])
            g_scr[rows, :] = _dot(_sigmoid(lora[:, 128:256]).astype(BF16), gup_ref[...])
            held["k"] = shifted(HEADS_DIM, HEADS_DIM)

        def s_decay():
            w_pre = w0_ref[...] + held["raw"][:, 0:HEADS_DIM]
            softplus = jnp.maximum(-w_pre, 0.0) + jnp.log(1.0 + jnp.exp(-jnp.abs(w_pre)))
            lw = -jnp.exp(-softplus - 0.5)
            ltri = ltri_ref[...]
            hi, mid, _ = _split3(lw)
            held["cum"] = _dot(ltri, hi) + _dot(ltri, mid)
            held["lw"] = lw

        def s_keys():
            k = held["k"]
            a = _sigmoid(a0_ref[...] + held["raw"][:, HEADS_DIM:])
            kk = k * kkw_ref[...]
            held["nrm"] = bdsum(kk * kk, 1)
            held.update(a=a, kk=kk, k2=k * (1.0 + (a - 1.0) * kaw_ref[...]))

        def s_bonus():
            r = shifted(0, HEADS_DIM)
            held["bon"] = bdsum(r * held["k2"] * rkw_ref[...], 1)
            rt_scr[rows, :] = r * jnp.exp(held["cum"])

        def s_norm():
            cum = held["cum"]
            kk = held["kk"] / jnp.maximum(jnp.sqrt(held["nrm"]), 1e-12)
            cum3 = cum.reshape(per, CHUNK, HEADS_DIM)
            cend = jnp.broadcast_to(cum3[:, CHUNK - 1:CHUNK, :], cum3.shape).reshape(pq, HEADS_DIM)
            diag = (lax.broadcasted_iota(jnp.int32, (pq, HEADS_DIM), 0) % CHUNK
                    == lax.broadcasted_iota(jnp.int32, (pq, HEADS_DIM), 1) % HEAD)
            held["fdec"] = bdsum(jnp.where(diag, jnp.exp(cend), 0.0), 2)
            at_scr[rows, :] = -kk * jnp.exp(cum - held["lw"])
            held.update(kk=kk, cend=cend)

        def s_inv():
            e_inv = jnp.exp(-held["cum"])
            kt_scr[rows, :] = held["k2"] * e_inv
            bt_scr[rows, :] = held["kk"] * held["a"] * e_inv

        def s_rel():
            e_rel = jnp.exp(held["cend"] - held["cum"])
            ktd_scr[rows, :] = held["k2"] * e_rel
            btd_scr[rows, :] = held["kk"] * held["a"] * e_rel
            fdec_scr[rows, :] = held["fdec"]

        def s_value():
            v = shifted(2 * HEADS_DIM, HEADS_DIM)
            v_scr[rows, :] = v
            bon_scr[rows, :] = held["bon"] * v

        return [s_lora, s_decay, s_keys, s_bonus, s_norm, s_inv, s_rel, s_value]

    def chunk_rows(c):
        return slice(c * CHUNK, (c + 1) * CHUNK)

    groups = [slice(g0, g0 + GROUP) for g0 in range(0, HEADS_DIM, GROUP)]

    named = dict(at=at_scr, rt=rt_scr, kt=kt_scr, bt=bt_scr, v=v_scr, btd=btd_scr, ktd=ktd_scr,
                 fdec=fdec_scr, ah=ah_scr, vh=vh_scr, arb=arb_scr, ark=ark_scr)
    prep_outs = (ah_scr, vh_scr, arb_scr, ark_scr)

    def loader(c, cols):
        rows = chunk_rows(c)
        return lambda name: named[name][rows, cols]

    def run(step_piece, prep_piece, extra):
        step_chunks = [] if step_piece is None else [step_piece * per + i for i in range(per)]
        prep_chunks = [] if prep_piece is None else [prep_piece * per + i for i in range(per)]
        states = [st_scr[:, cols] for cols in groups] if step_chunks else []
        stages, ys = [], []
        for c in step_chunks:
            out = []
            ys.append(out)
            stages += _wkv_step_stages([loader(c, cols) for cols in groups], states, out)
        if step_chunks:
            rows = slice(step_piece * pq, (step_piece + 1) * pq)
            gn = {}

            def gn_mean():
                gn["y"] = jnp.concatenate([jnp.concatenate(out, axis=1) for out in ys], axis=0)
                gn["sum"] = bdsum(gn["y"], 1)

            def gn_var():
                gn["d"] = gn["y"] - gn["sum"] * (1.0 / HEAD)
                gn["sq"] = bdsum(gn["d"] * gn["d"], 1)

            def gn_out():
                var = gn["sq"] * (1.0 / HEAD)
                yn = (gn["d"] * lax.rsqrt(var + GN_EPS) * gng_ref[...] + gnb_ref[...]
                      + bon_scr[rows, :])
                out_ref[rows, :] = (yn * g_scr[rows, :]).astype(BF16)

            stages += [gn_mean, gn_var, gn_out]
        hooks = []
        for i in range(max(len(stages), len(extra))):
            both = stages[i:i + 1] + extra[i:i + 1]
            hooks.append(lambda both=both: [f() for f in both])
        units = [(c, cols) for c in prep_chunks for cols in groups]
        prep_out = _wkv_prep([loader(c, cols) for c, cols in units], hooks)
        for cols, st in zip(groups, states):
            st_scr[:, cols] = st
        for (c, cols), outs in zip(units, prep_out):
            for ref, val in zip(prep_outs, outs):
                ref[chunk_rows(c), cols] = val

    look_ahead = steps > 1
    assert n_piece >= 3 or not look_ahead

    def first_piece():
        for stage in pre_stages(0):
            stage()

    if look_ahead:
        pl.when(j == 0)(first_piece)
    else:
        first_piece()
    for piece in range(n_piece + 1):
        if piece + 1 < n_piece:
            extra = pre_stages(piece + 1)
        elif piece + 1 == n_piece and look_ahead:
            extra = pre_stages(0, upcoming=True)
        else:
            extra = []
        run(piece - 1 if piece >= 1 else None, piece if piece < n_piece else None, extra)

    @pl.when(j == steps - 1)
    def _():
        shift_out_ref[...] = p_ref[ts - 1:ts, :]
        state_out_ref[...] = st_scr[...]


def _wkv_call(p2d, shift0, state0, prm, n_seq, steps, ts):
    rows = p2d.shape[0]
    row_map = lambda s, j: (s * steps + j, 0)
    next_map = lambda s, j: (s * steps + jnp.minimum(j + 1, steps - 1), 0)
    seq3 = lambda s, j: (s, 0, 0)
    vec = _const_spec((1, HEADS_DIM))
    big = pltpu.VMEM((ts, HEADS_DIM), F32)
    ltri = prm["ltri"][min(ts, WKV_PIECE)]
    return pl.pallas_call(
        functools.partial(_wkv_kernel, ts=ts, steps=steps),
        grid=(n_seq, steps),
        in_specs=[
            pl.BlockSpec((ts, RW_COLS), row_map),
            pl.BlockSpec((ts, RW_COLS), next_map),
            pl.BlockSpec((None, 1, RW_COLS), seq3),
            pl.BlockSpec((None, HEAD, HEADS_DIM), seq3),
            _const_spec((1, RW_COLS)),
            vec,
            _const_spec((128, 2 * HEADS_DIM)),
            vec,
            _const_spec((LORA_G, HEADS_DIM)),
            vec, vec, vec, vec, vec,
            _const_spec((GROUP, GROUP)),
            _const_spec(ltri.shape),
        ],
        out_specs=[
            pl.BlockSpec((ts, HEADS_DIM), row_map),
            pl.BlockSpec((None, 1, RW_COLS), seq3),
            pl.BlockSpec((None, HEAD, HEADS_DIM), seq3),
        ],
        out_shape=[
            jax.ShapeDtypeStruct((rows, HEADS_DIM), BF16),
            jax.ShapeDtypeStruct((n_seq, 1, RW_COLS), F32),
            jax.ShapeDtypeStruct((n_seq, HEAD, HEADS_DIM), F32),
        ],
        scratch_shapes=[
            pltpu.VMEM((HEAD, HEADS_DIM), F32),
        ] + [big] * 14,
        compiler_params=pltpu.CompilerParams(
            dimension_semantics=("arbitrary", "arbitrary"), vmem_limit_bytes=VMEM_LIMIT),
        name="wkv",
    )(p2d, p2d, shift0, state0, prm["mu"], prm["w0"], prm["wwa"], prm["a0"], prm["gup"],
      prm["kk"], prm["ka"], prm["rk"], prm["gng"], prm["gnb"], prm["ones"], ltri)


def _mixffn_kernel(xn_ref, att_ref, rw_ref, gate_ref, mod_ref, convp_ref,
                   ln1g_ref, ln1b_ref, ln2g_ref, ln2b_ref,
                   wa_ref, wr_ref, wo_ref, wup_ref, cw_ref, cb_ref, wdn_ref,
                   y_ref, convo_ref, carry_scr, yb_scr, *, tm, steps):
    j = pl.program_id(0)
    groups = mod_ref.shape[0]
    assert groups == 1 or steps == 1

    if groups == 1:
        @pl.when(j == 0)
        def _():
            carry_scr[8 - (CONV_W - 1):8, :] = convp_ref[0]

    def mod(idx):
        return _mod_row(mod_ref, idx, tm)

    half = tm // 2
    subs = [slice(0, half), slice(half, tm)]

    def modr(idx, rs):
        m = mod(idx)
        return m if m.shape[0] == 1 else m[rs]

    ma = [_dot(att_ref[rs, :], wa_ref[...]) for rs in subs]
    mr = [_dot(rw_ref[rs, :], wr_ref[...]) for rs in subs]
    merged = [(gate_ref[rs, 0:D_MODEL] * a + gate_ref[rs, D_MODEL:] * r).astype(BF16)
              for rs, a, r in zip(subs, ma, mr)]
    mix = [_dot(m, wo_ref[...]) for m in merged]
    x1 = [_layer_norm(ALPHA * xn_ref[rs, :] + (1.0 + modr(2, rs)) * m, ln1g_ref[...], ln1b_ref[...])
          for rs, m in zip(subs, mix)]
    h2 = jnp.concatenate([(a * (1.0 + modr(4, rs)) + modr(3, rs)).astype(BF16)
                          for rs, a in zip(subs, x1)], axis=0)

    cw_blk = 256
    glen = tm // groups
    row8 = lax.broadcasted_iota(jnp.int32, (8, cw_blk), 0)
    grow = lax.broadcasted_iota(jnp.int32, (tm, cw_blk), 0) % glen
    for c in range(0, D_FF, cw_blk):
        cs = slice(c, c + cw_blk)
        uc = _dot(h2, wup_ref[:, cs])
        uv = _dot(h2, wup_ref[:, D_FF + c:D_FF + c + cw_blk])
        r1 = pltpu.roll(uc, 1, 0)
        r2 = pltpu.roll(uc, 2, 0)
        if groups == 1:
            c6 = carry_scr[6:7, cs]
            c7 = carry_scr[7:8, cs]
            s1 = jnp.concatenate([jnp.where(row8 == 0, c7, r1[0:8]), r1[8:]], axis=0)
            s2 = jnp.concatenate(
                [jnp.where(row8 == 0, c6, jnp.where(row8 == 1, c7, r2[0:8])), r2[8:]], axis=0)
            carry_scr[:, cs] = uc[tm - 8:tm, :]
            tail = uc[tm - (CONV_W - 1):tm, :][None]
        else:
            hist = jnp.broadcast_to(convp_ref[:, :, cs][:, :, None, :],
                                    (groups, CONV_W - 1, glen, cw_blk))
            c6 = hist[:, 0].reshape(tm, cw_blk)
            c7 = hist[:, 1].reshape(tm, cw_blk)
            s1 = jnp.where(grow == 0, c7, r1)
            s2 = jnp.where(grow == 0, c6, jnp.where(grow == 1, c7, r2))
            tail = uc.reshape(groups, glen, cw_blk)[:, glen - (CONV_W - 1):, :]
        conv = cb_ref[:, cs] + s2 * cw_ref[0:1, cs] + s1 * cw_ref[1:2, cs] + uc * cw_ref[2:3, cs]
        yb_scr[:, cs] = (conv * _sigmoid(conv) * uv).astype(BF16)

        @pl.when(j == steps - 1)
        def _():
            convo_ref[:, :, cs] = tail

    ff = [_dot(yb_scr[rs, :], wdn_ref[...]) for rs in subs]
    for rs, a, f in zip(subs, x1, ff):
        y_ref[rs, :] = _layer_norm(ALPHA * a + (1.0 + modr(5, rs)) * f, ln2g_ref[...], ln2b_ref[...])


def _mixffn_call(xn2d, att, rw, gates, mod, conv_prev, prm, steps, tm):
    rows = xn2d.shape[0]
    groups = mod.shape[0]
    row_map = lambda j: (j, 0)
    vec = _const_spec((1, D_MODEL))
    return pl.pallas_call(
        functools.partial(_mixffn_kernel, tm=tm, steps=steps),
        grid=(steps,),
        in_specs=[
            pl.BlockSpec((tm, D_MODEL), row_map),
            pl.BlockSpec((tm, HEADS_DIM), row_map),
            pl.BlockSpec((tm, HEADS_DIM), row_map),
            pl.BlockSpec((tm, GATE_COLS), row_map),
            _const_spec((groups, 6, D_MODEL)),
            _const_spec((groups, CONV_W - 1, D_FF)),
            vec, vec, vec, vec,
            _const_spec((HEADS_DIM, D_MODEL)),
            _const_spec((HEADS_DIM, D_MODEL)),
            _const_spec((D_MODEL, D_MODEL)),
            _const_spec((D_MODEL, 2 * D_FF)),
            _const_spec((CONV_W, D_FF)),
            _const_spec((1, D_FF)),
            _const_spec((D_FF, D_MODEL)),
        ],
        out_specs=[
            pl.BlockSpec((tm, D_MODEL), row_map),
            pl.BlockSpec((groups, CONV_W - 1, D_FF), lambda j: (0, 0, 0)),
        ],
        out_shape=[
            jax.ShapeDtypeStruct((rows, D_MODEL), F32),
            jax.ShapeDtypeStruct((groups, CONV_W - 1, D_FF), F32),
        ],
        scratch_shapes=[pltpu.VMEM((8, D_FF), F32), pltpu.VMEM((tm, D_FF), BF16)],
        compiler_params=pltpu.CompilerParams(
            dimension_semantics=("arbitrary",), vmem_limit_bytes=VMEM_LIMIT),
        name="mixffn",
    )(xn2d, att, rw, gates, mod, conv_prev,
      prm["ln1g"], prm["ln1b"], prm["ln2g"], prm["ln2b"],
      prm["wa"], prm["wr"], prm["wo"], prm["wup"], prm["cw"], prm["cb"], prm["wdn"])


def _pair_bias(table):
    assert CHUNK - 1 <= REL_CLIP
    top = ATT_REACH + CHUNK - 1
    n_far = top - REL_CLIP + 1
    far = jnp.broadcast_to(table[:, 2 * REL_CLIP:], (N_HEADS, n_far))
    lo_idx = top - (BAND + CHUNK - 2) + REL_CLIP
    near = table[:, lo_idx:2 * REL_CLIP][:, ::-1]
    ext = jnp.concatenate([far, near], axis=1).astype(F32) * LOG2E
    n_ext = BAND + CHUNK - 1
    period = jnp.concatenate([ext, jnp.zeros((N_HEADS, 1), F32)], axis=1)
    skew = jnp.tile(period, (1, CHUNK))[:, :CHUNK * n_ext].reshape(N_HEADS, CHUNK, n_ext)
    bias = skew[:, :, CHUNK - 1:CHUNK - 1 + BAND]
    return bias.reshape(N_HEADS // 2, 2 * CHUNK, BAND)


def _chunk_ltri(ts):
    t = jnp.arange(ts)
    return ((t[:, None] // CHUNK == t[None, :] // CHUNK) & (t[None, :] <= t[:, None])).astype(BF16)


def _trunk(x2d, mod, shift0, state0, conv_prev, caches, prm, n_seq):
    prompt = caches is None
    rows = x2d.shape[0]
    tm = min(rows, ROW_TILE)
    steps = rows // tm
    q, k, v, kv32, p, gates, xn = _inproj_call(
        x2d, mod, prm["lnig"], prm["lnib"], prm["win"], steps, tm, prompt)
    if prompt:
        att = _attn_prompt_call(q, k, v, prm["bias"])
    else:
        att = _attn_sample_call(q, k, v, *caches, prm["bias"])
    ts = min(rows // n_seq, ROW_TILE)
    rw, shift, state = _wkv_call(p, shift0, state0, prm, n_seq, rows // n_seq // ts, ts)
    y, conv = _mixffn_call(xn, att, rw, gates, mod, conv_prev, prm, steps, tm)
    return y, kv32, state, shift, conv


def kernel(x_prompt, x_sample, cache_attn_k, cache_attn_v, state_rwkv, state_shift, state_conv,
           c_prompt, c_sample, ln_in_g, ln_in_b, w_ada, b_ada, w_in, attn_rel_bias,
           rwkv_mu, rwkv_w0, rwkv_w_up, rwkv_a0, rwkv_a_up, rwkv_g_up, rwkv_k_k, rwkv_k_a,
           rwkv_r_k, rwkv_gn_g, rwkv_gn_b, w_branch_attn, w_branch_rwkv, w_out,
           ln1_g, ln1_b, ln2_g, ln2_b, w_ffn_up, ffn_conv_w, ffn_conv_b, w_ffn_down):
    bp, sp, _ = x_prompt.shape
    bs, ss, _ = x_sample.shape
    assert bp == 1 and ss == CHUNK and w_ada.shape[0] == DEPTH
    assert sp % ROW_TILE == 0 and bs * ss <= ROW_TILE and cache_attn_k.shape[2] == ATT_REACH

    row = lambda a: a.reshape(1, -1)
    wwa = jnp.zeros((LORA_W + LORA_A, 2 * HEADS_DIM), F32)
    wwa = wwa.at[:LORA_W, :HEADS_DIM].set(rwkv_w_up[0]).at[LORA_W:, HEADS_DIM:].set(rwkv_a_up[0])
    head_id = jnp.arange(GROUP) // HEAD
    prm = dict(
        lnig=row(ln_in_g), lnib=row(ln_in_b),
        ln1g=row(ln1_g[0]), ln1b=row(ln1_b[0]), ln2g=row(ln2_g[0]), ln2b=row(ln2_b[0]),
        win=w_in[0].astype(BF16), bias=_pair_bias(attn_rel_bias[0]),
        mu=row(rwkv_mu[0]), w0=row(rwkv_w0[0]), wwa=wwa.astype(BF16), a0=row(rwkv_a0[0]),
        gup=rwkv_g_up[0].astype(BF16), kk=row(rwkv_k_k[0]), ka=row(rwkv_k_a[0]),
        rk=row(rwkv_r_k[0]), gng=row(rwkv_gn_g[0]), gnb=row(rwkv_gn_b[0]),
        ones=(head_id[:, None] == head_id[None, :]).astype(BF16),
        ltri={WKV_PIECE: _chunk_ltri(WKV_PIECE), CHUNK: _chunk_ltri(CHUNK)},
        wa=w_branch_attn[0].astype(BF16), wr=w_branch_rwkv[0].astype(BF16),
        wo=w_out[0].astype(BF16), wup=w_ffn_up[0].astype(BF16),
        cw=ffn_conv_w[0], cb=row(ffn_conv_b[0]), wdn=w_ffn_down[0].astype(BF16),
    )

    n_c = bp + bs
    c_all = jnp.concatenate([c_prompt, c_sample, jnp.zeros((16 - n_c, D_MODEL), F32)], axis=0)
    mod = _mod_call(c_all, w_ada[0], row(b_ada[0])).reshape(16, 6, D_MODEL)

    y_p, kv_p, st_p, sh_p, cv_p = _trunk(
        x_prompt.reshape(sp, D_MODEL), mod[0:bp],
        jnp.zeros((bp, 1, RW_COLS), F32), jnp.zeros((bp, HEAD, HEADS_DIM), F32),
        jnp.zeros((bp, CONV_W - 1, D_FF), F32), None, prm, n_seq=bp)

    caches = (cache_attn_k[0].reshape(bs, ATT_REACH, HEADS_DIM),
              cache_attn_v[0].reshape(bs, ATT_REACH, HEADS_DIM))
    st0 = jnp.transpose(state_rwkv[0], (0, 3, 1, 2)).reshape(bs, HEAD, HEADS_DIM)
    y_s, kv_s, st_s, sh_s, cv_s = _trunk(
        x_sample.reshape(bs * ss, D_MODEL), mod[bp:n_c],
        state_shift[0], st0, state_conv[0], caches, prm, n_seq=bs)

    def state_out(st, b):
        return jnp.transpose(st.reshape(b, HEAD, N_HEADS, HEAD), (0, 2, 3, 1))[None]

    hs = (N_HEADS, HEAD)
    return (
        y_p.reshape(bp, sp, D_MODEL),
        y_s.reshape(bs, ss, D_MODEL),
        kv_p[:, :HEADS_DIM].reshape(1, bp, ATT_REACH, *hs),
        kv_p[:, HEADS_DIM:].reshape(1, bp, ATT_REACH, *hs),
        kv_s[:, :HEADS_DIM].reshape(1, bs, ss, *hs),
        kv_s[:, HEADS_DIM:].reshape(1, bs, ss, *hs),
        state_out(st_p, bp),
        state_out(st_s, bs),
        sh_p[None],
        sh_s[None],
        cv_p[None],
        cv_s[None],
    )
```

```python
import functools

import jax
import jax.numpy as jnp
from jax import lax
from jax.experimental import pallas as pl
from jax.experimental.pallas import tpu as pltpu

F32 = jnp.float32
BF16 = jnp.bfloat16

D_MODEL = 1024
CHUNK = 64
ATT_REACH = 512
BAND = ATT_REACH + CHUNK
N_HEADS = 8
HEAD = 64
HEADS_DIM = N_HEADS * HEAD
REL_CLIP = 128
LORA_W = 64
LORA_A = 64
LORA_G = 128
ATT_COLS = 3 * HEADS_DIM
RW_COLS = 3 * HEADS_DIM + LORA_W + LORA_A + LORA_G
GATE_COLS = 2 * D_MODEL
D_FF = 2816
CONV_W = 3
LN_EPS = 1e-5
GN_EPS = 64e-5
DEPTH = 1
ALPHA = (2 * DEPTH) ** 0.25
LOG2E = 1.4426950408889634

GROUP = 256
ROW_TILE = 512
WKV_PIECE = 128
FFN_ROWS = 512
VMEM_LIMIT = 56 * 1024 * 1024


def _const_spec(shape):
    nd = len(shape)
    return pl.BlockSpec(shape, lambda *_: (0,) * nd, pipeline_mode=pl.Buffered(1))


def _layer_norm(x, g, b):
    mu = jnp.mean(x, axis=-1, keepdims=True)
    xc = x - mu
    var = jnp.mean(xc * xc, axis=-1, keepdims=True)
    return xc * lax.rsqrt(var + LN_EPS) * g + b


def _sigmoid(x):
    return 1.0 / (1.0 + jnp.exp(-x))


def _split3(x):
    hi = x.astype(BF16)
    r1 = x - hi.astype(F32)
    mid = r1.astype(BF16)
    lo = (r1 - mid.astype(F32)).astype(BF16)
    return hi, mid, lo


def _dot(a, b):
    return jnp.dot(a, b, preferred_element_type=F32)


def _dot_nt(a, b):
    return lax.dot_general(a, b, (((1,), (1,)), ((), ())), preferred_element_type=F32)


def _dot_tn(a, b):
    return lax.dot_general(a, b, (((0,), (0,)), ((), ())), preferred_element_type=F32)


def _mod_kernel(c_ref, w_ref, b_ref, o_ref):
    c = c_ref[...]
    s = (c * _sigmoid(c)).astype(BF16)
    o_ref[...] = _dot(s, w_ref[...].astype(BF16)) + b_ref[...]


def _mod_call(c_all, w_ada, b_ada):
    n = c_all.shape[0]
    nblk = 6
    return pl.pallas_call(
        _mod_kernel,
        grid=(nblk,),
        in_specs=[
            pl.BlockSpec((n, D_MODEL), lambda i: (0, 0)),
            pl.BlockSpec((D_MODEL, D_MODEL), lambda i: (0, i)),
            pl.BlockSpec((1, D_MODEL), lambda i: (0, i)),
        ],
        out_specs=pl.BlockSpec((n, D_MODEL), lambda i: (0, i)),
        out_shape=jax.ShapeDtypeStruct((n, 6 * D_MODEL), F32),
        compiler_params=pltpu.CompilerParams(dimension_semantics=("arbitrary",)),
        name="mod",
    )(c_all, w_ada, b_ada)


def _mod_row(mod_ref, idx, tm):
    groups, _, d = mod_ref.shape
    m = mod_ref[:, idx:idx + 1, :]
    if groups == 1:
        return m[0]
    return jnp.broadcast_to(m, (groups, tm // groups, d)).reshape(tm, d)


def _inproj_kernel(x_ref, mod_ref, lng_ref, lnb_ref, w_ref,
                   q_ref, k_ref, v_ref, kv_ref, p_ref, g_ref, xn_ref, *, lead):
    tm = x_ref.shape[0]
    j = pl.program_id(0)

    @pl.when(j < lead)
    def _():
        k_ref[...] = jnp.zeros_like(k_ref)
        v_ref[...] = jnp.zeros_like(v_ref)

    def half_tile(rs):
        def mrow(idx):
            m = _mod_row(mod_ref, idx, tm)
            return m if m.shape[0] == 1 else m[rs]

        xn = _layer_norm(x_ref[rs, :], lng_ref[...], lnb_ref[...])
        xn_ref[rs, :] = xn
        hb = (xn * (1.0 + mrow(1)) + mrow(0)).astype(BF16)

        def seg(a, b):
            return _dot(hb, w_ref[:, a:b])

        q_ref[rs, :] = (seg(0, HEADS_DIM) * (HEAD ** -0.5 * LOG2E)).astype(BF16)
        k = seg(HEADS_DIM, 2 * HEADS_DIM)
        k_ref[rs, :] = k.astype(BF16)
        kv_ref[rs, 0:HEADS_DIM] = k
        v = seg(2 * HEADS_DIM, 3 * HEADS_DIM)
        v_ref[rs, :] = v.astype(BF16)
        kv_ref[rs, HEADS_DIM:2 * HEADS_DIM] = v
        for c in range(0, RW_COLS, 256):
            p_ref[rs, c:c + 256] = seg(ATT_COLS + c, ATT_COLS + c + 256)
        g0 = ATT_COLS + RW_COLS
        for c in range(0, GATE_COLS, 512):
            g_ref[rs, c:c + 512] = _sigmoid(seg(g0 + c, g0 + c + 512))

    @pl.when(j >= lead)
    def _():
        half_tile(slice(0, tm // 2))
        half_tile(slice(tm // 2, tm))


def _inproj_call(x2d, mod, ln_g, ln_b, w_in_b, steps, tm, prompt):
    rows = x2d.shape[0]
    in_cols = w_in_b.shape[1]
    groups = mod.shape[0]
    if prompt:
        lead = ATT_REACH // tm
        kv_rows = ATT_REACH
        row_map = lambda j: (jnp.maximum(j - lead, 0), 0)
        kv_map = lambda j: (jnp.maximum(j - steps, 0), 0)
        ext_map = lambda j: (j, 0)
    else:
        lead = 0
        kv_rows = rows
        row_map = kv_map = ext_map = lambda j: (j, 0)
    return pl.pallas_call(
        functools.partial(_inproj_kernel, lead=lead),
        grid=(steps + lead,),
        in_specs=[
            pl.BlockSpec((tm, D_MODEL), row_map),
            _const_spec((groups, 6, D_MODEL)),
            _const_spec((1, D_MODEL)),
            _const_spec((1, D_MODEL)),
            _const_spec((D_MODEL, in_cols)),
        ],
        out_specs=[
            pl.BlockSpec((tm, HEADS_DIM), row_map),
            pl.BlockSpec((tm, HEADS_DIM), ext_map),
            pl.BlockSpec((tm, HEADS_DIM), ext_map),
            pl.BlockSpec((tm, 2 * HEADS_DIM), kv_map),
            pl.BlockSpec((tm, RW_COLS), row_map),
            pl.BlockSpec((tm, GATE_COLS), row_map),
            pl.BlockSpec((tm, D_MODEL), row_map),
        ],
        out_shape=[
            jax.ShapeDtypeStruct((rows, HEADS_DIM), BF16),
            jax.ShapeDtypeStruct((rows + lead * tm, HEADS_DIM), BF16),
            jax.ShapeDtypeStruct((rows + lead * tm, HEADS_DIM), BF16),
            jax.ShapeDtypeStruct((kv_rows, 2 * HEADS_DIM), F32),
            jax.ShapeDtypeStruct((rows, RW_COLS), F32),
            jax.ShapeDtypeStruct((rows, GATE_COLS), F32),
            jax.ShapeDtypeStruct((rows, D_MODEL), F32),
        ],
        compiler_params=pltpu.CompilerParams(
            dimension_semantics=("arbitrary",), vmem_limit_bytes=VMEM_LIMIT),
        name="inproj",
    )(x2d, mod, ln_g, ln_b, w_in_b)


def _attn_chunks(chunks, bias_ref):
    lane = lax.broadcasted_iota(jnp.int32, (CHUNK, 128), 1)
    first = lane < HEAD
    pairs = [slice(pr * 128, (pr + 1) * 128) for pr in range(N_HEADS // 2)]
    scores = []
    for qc, kb, _, _ in chunks:
        for sl in pairs:
            q2 = qc[:, sl].astype(F32)
            qs = jnp.concatenate([jnp.where(first, q2, 0.0), jnp.where(first, 0.0, q2)],
                                 axis=0).astype(BF16)
            scores.append(_dot_nt(qs, kb(sl)))
    probs, sums = [], []
    for i, s in enumerate(scores):
        thr = chunks[i // len(pairs)][3]
        s = s + bias_ref[i % len(pairs)]
        if thr is not None:
            col = lax.broadcasted_iota(jnp.int32, s.shape, 1)
            s = jnp.where(col >= thr, s, -jnp.inf)
        e = jnp.exp2(s - jnp.max(s, axis=1, keepdims=True))
        sums.append(jnp.sum(e, axis=1, keepdims=True))
        probs.append(e.astype(BF16))
    outs = []
    for ci, (_, _, vb, _) in enumerate(chunks):
        cols = []
        for pi, sl in enumerate(pairs):
            i = ci * len(pairs) + pi
            o = _dot(probs[i], vb(sl)) / sums[i]
            cols.append(jnp.where(first, o[0:CHUNK], o[CHUNK:2 * CHUNK]))
        outs.append(jnp.concatenate(cols, axis=1).astype(BF16))
    return outs


def _attn_prompt_kernel(q_ref, ka_ref, kb_ref, va_ref, vb_ref, bias_ref, o_ref, kbuf, vbuf,
                        *, chunks):
    tq = chunks * CHUNK
    kbuf[0:tq, :] = ka_ref[...]
    kbuf[tq:2 * tq, :] = kb_ref[...]
    vbuf[0:tq, :] = va_ref[...]
    vbuf[tq:2 * tq, :] = vb_ref[...]
    s = pl.program_id(0)

    per = 4

    def run(masked):
        def body(i, carry):
            units, starts = [], []
            for k in range(per):
                g = i * per + k
                r0 = pl.multiple_of(g * CHUNK, CHUNK)
                thr = ATT_REACH - (s * chunks + g) * CHUNK if masked else None
                band = pl.ds(r0, BAND)
                units.append((q_ref[pl.ds(r0, CHUNK), :],
                              lambda sl, band=band: kbuf[band, sl],
                              lambda sl, band=band: vbuf[band, sl], thr))
                starts.append(r0)
            for r0, o in zip(starts, _attn_chunks(units, bias_ref)):
                o_ref[pl.ds(r0, CHUNK), :] = o
            return carry

        lax.fori_loop(0, chunks // per, body, 0)

    assert chunks * CHUNK >= ATT_REACH
    pl.when(s == 0)(functools.partial(run, True))
    pl.when(s != 0)(functools.partial(run, False))


def _attn_prompt_call(q, kext, vext, bias):
    rows = q.shape[0]
    tq = ATT_REACH
    blk = lambda off: pl.BlockSpec((tq, HEADS_DIM), lambda s: (s + off, 0))
    return pl.pallas_call(
        functools.partial(_attn_prompt_kernel, chunks=tq // CHUNK),
        grid=(rows // tq,),
        in_specs=[blk(0), blk(0), blk(1), blk(0), blk(1), _const_spec(bias.shape)],
        out_specs=blk(0),
        out_shape=jax.ShapeDtypeStruct((rows, HEADS_DIM), BF16),
        scratch_shapes=[pltpu.VMEM((2 * tq, HEADS_DIM), BF16), pltpu.VMEM((2 * tq, HEADS_DIM), BF16)],
        compiler_params=pltpu.CompilerParams(dimension_semantics=("arbitrary",)),
        name="attn_prompt",
    )(q, kext, kext, vext, vext, bias)


def _attn_sample_kernel(q_ref, k_ref, v_ref, ck_ref, cv_ref, bias_ref, o_ref):
    def band(cache_ref, new_ref):
        return lambda sl: jnp.concatenate([cache_ref[:, sl].astype(BF16), new_ref[:, sl]], axis=0)

    unit = (q_ref[...], band(ck_ref, k_ref), band(cv_ref, v_ref), None)
    o_ref[...] = _attn_chunks([unit], bias_ref)[0]


def _attn_sample_call(q, k, v, cache_k, cache_v, bias):
    nb = cache_k.shape[0]
    new = pl.BlockSpec((CHUNK, HEADS_DIM), lambda b: (b, 0))
    cache = pl.BlockSpec((None, ATT_REACH, HEADS_DIM), lambda b: (b, 0, 0))
    return pl.pallas_call(
        _attn_sample_kernel,
        grid=(nb,),
        in_specs=[new, new, new, cache, cache, _const_spec(bias.shape)],
        out_specs=new,
        out_shape=jax.ShapeDtypeStruct((nb * CHUNK, HEADS_DIM), BF16),
        compiler_params=pltpu.CompilerParams(dimension_semantics=("arbitrary",)),
        name="attn_sample",
    )(q, k, v, cache_k, cache_v, bias)


def _same_head():
    r = lax.broadcasted_iota(jnp.int32, (GROUP, GROUP), 0) // HEAD
    c = lax.broadcasted_iota(jnp.int32, (GROUP, GROUP), 1) // HEAD
    return r == c


def _blk(x, same_head):
    return jnp.where(same_head, jnp.concatenate([x] * 4, axis=0), 0.0).astype(BF16)


def _wkv_prep(units, hooks):
    same_head = _same_head()
    t64 = lax.broadcasted_iota(jnp.int32, (CHUNK, GROUP), 0)
    i64 = lax.broadcasted_iota(jnp.int32, (CHUNK, GROUP), 1) % HEAD
    strict = i64 < t64
    incl = i64 <= t64
    pending = list(hooks)

    def stage_done():
        if pending:
            pending.pop(0)()

    def blk(x):
        return _blk(x, same_head)

    n, a_ak, a_rb, a_rk = [], [], [], []
    for ld in units:
        lhs = jnp.concatenate([ld("at"), ld("rt")], axis=0).astype(BF16)
        rhs = jnp.concatenate([blk(ld("bt")), blk(ld("kt"))], axis=0)
        a_all = _dot_nt(lhs, rhs)
        n.append(jnp.where(strict, a_all[0:CHUNK, 0:GROUP], 0.0))
        a_ak.append(jnp.where(strict, a_all[0:CHUNK, GROUP:], 0.0))
        a_rb.append(jnp.where(incl, a_all[CHUNK:, 0:GROUP], 0.0))
        a_rk.append(jnp.where(incl, a_all[CHUNK:, GROUP:], 0.0))
    stage_done()

    x0 = [_dot(a.astype(BF16), blk(ld("v"))) for a, ld in zip(a_ak, units)]
    npow = [_dot(m.astype(BF16), blk(m)) for m in n]
    t = [jnp.where(i64 == t64, 1.0, 0.0) + m for m in n]
    stage_done()
    for _ in range(4):
        prod = [_dot(jnp.concatenate([a, b], axis=0).astype(BF16), blk(b))
                for a, b in zip(t, npow)]
        t = [a + p[0:CHUNK] for a, p in zip(t, prod)]
        npow = [p[CHUNK:] for p in prod]
        stage_done()
    t = [a + _dot(a.astype(BF16), blk(b)) for a, b in zip(t, npow)]
    stage_done()
    res = [_dot(a.astype(BF16), jnp.concatenate([blk(ld("at")), blk(x)], axis=1))
           for a, x, ld in zip(t, x0, units)]
    stage_done()
    while pending:
        stage_done()
    return [(r[:, 0:GROUP], r[:, GROUP:], b, k) for r, b, k in zip(res, a_rb, a_rk)]


def _wkv_step_stages(units, states, out):
    same_head = _same_head()
    held = {}

    def blk(x):
        return _blk(x, same_head)

    def first():
        held["sblk"] = [blk(st) for st in states]
        held["u"] = [_dot(ld("ah").astype(BF16), sb) + ld("vh")
                     for ld, sb in zip(units, held["sblk"])]

    def second():
        for i, ld in enumerate(units):
            u, v = held["u"][i], ld("v")
            y = _dot(jnp.concatenate([ld("rt"), ld("arb"), ld("ark")], axis=1).astype(BF16),
                     jnp.concatenate([held["sblk"][i], blk(u), blk(v)], axis=0))
            g = _dot_tn(jnp.concatenate([ld("btd"), ld("ktd")], axis=0).astype(BF16),
                        jnp.concatenate([u, v], axis=0).astype(BF16))
            g = jnp.where(same_head, g, 0.0)
            delta = g[0:64] + g[64:128] + g[128:192] + g[192:256]
            out.append(y)
            states[i] = states[i] * ld("fdec") + delta

    return [first, second]


def _wkv_kernel(p_ref, pn_ref, shift0_ref, state0_ref, mu_ref, w0_ref, wwa_ref, a0_ref, gup_ref,
                kkw_ref, kaw_ref, rkw_ref, gng_ref, gnb_ref, ones_ref, ltri_ref,
                out_ref, shift_out_ref, state_out_ref,
                st_scr, rt_scr, at_scr, kt_scr, bt_scr, v_scr, btd_scr, ktd_scr,
                fdec_scr, ah_scr, vh_scr, arb_scr, ark_scr, g_scr, bon_scr, *, ts, steps):
    j = pl.program_id(1)

    @pl.when(j == 0)
    def _():
        st_scr[...] = state0_ref[...]

    def bdsum(x, terms):
        ones = ones_ref[...]
        halves = []
        for c0 in range(0, HEADS_DIM, GROUP):
            parts = _split3(x[:, c0:c0 + GROUP])[:terms]
            acc = _dot(parts[0], ones)
            for part in parts[1:]:
                acc = acc + _dot(part, ones)
            halves.append(acc)
        return jnp.concatenate(halves, axis=1)

    pq = ltri_ref.shape[0]
    per = pq // CHUNK
    n_piece = ts // pq

    def pre_stages(q, upcoming=False):
        r0 = q * pq
        rows = slice(r0, r0 + pq)
        src_ref = pn_ref if upcoming else p_ref
        held = {}

        def shifted(c0, w):
            cols = slice(c0, c0 + w)
            if upcoming:
                prev_row = p_ref[ts - 8:ts, cols][7:8]
            elif q == 0:
                prev_row = shift0_ref[:, cols]
            else:
                prev_row = p_ref[r0 - 8:r0, cols][7:8]
            top = lax.broadcasted_iota(jnp.int32, (pq, 1), 0) == 0
            pb = src_ref[rows, cols]
            prev = jnp.where(top, prev_row, pltpu.roll(pb, 1, 0))
            return pb + (prev - pb) * mu_ref[:, cols]

        def s_lora():
            lora = shifted(3 * HEADS_DIM, 256)
            lwla = lora[:, 0:128]
            lane = lax.broadcasted_iota(jnp.int32, (pq, 128), 1)
            held["raw"] = _dot(jnp.where(lane < LORA_W, jnp.tanh(lwla), lwla).astype(BF16),
                               wwa_ref[...])
            g_scr[rows, :] = _dot(_sigmoid(lora[:, 128:256]).astype(BF16), gup_ref[...])
            held["k"] = shifted(HEADS_DIM, HEADS_DIM)

        def s_decay():
            w_pre = w0_ref[...] + held["raw"][:, 0:HEADS_DIM]
            softplus = jnp.maximum(-w_pre, 0.0) + jnp.log(1.0 + jnp.exp(-jnp.abs(w_pre)))
            lw = -jnp.exp(-softplus - 0.5)
            ltri = ltri_ref[...]
            hi, mid, _ = _split3(lw)
            held["cum"] = _dot(ltri, hi) + _dot(ltri, mid)
            held["lw"] = lw

        def s_keys():
            k = held["k"]
            a = _sigmoid(a0_ref[...] + held["raw"][:, HEADS_DIM:])
            kk = k * kkw_ref[...]
            held["nrm"] = bdsum(kk * kk, 1)
            held.update(a=a, kk=kk, k2=k * (1.0 + (a - 1.0) * kaw_ref[...]))

        def s_bonus():
            r = shifted(0, HEADS_DIM)
            held["bon"] = bdsum(r * held["k2"] * rkw_ref[...], 1)
            rt_scr[rows, :] = r * jnp.exp(held["cum"])

        def s_norm():
            cum = held["cum"]
            kk = held["kk"] / jnp.maximum(jnp.sqrt(held["nrm"]), 1e-12)
            cum3 = cum.reshape(per, CHUNK, HEADS_DIM)
            cend = jnp.broadcast_to(cum3[:, CHUNK - 1:CHUNK, :], cum3.shape).reshape(pq, HEADS_DIM)
            diag = (lax.broadcasted_iota(jnp.int32, (pq, HEADS_DIM), 0) % CHUNK
                    == lax.broadcasted_iota(jnp.int32, (pq, HEADS_DIM), 1) % HEAD)
            held["fdec"] = bdsum(jnp.where(diag, jnp.exp(cend), 0.0), 2)
            at_scr[rows, :] = -kk * jnp.exp(cum - held["lw"])
            held.update(kk=kk, cend=cend)

        def s_inv():
            e_inv = jnp.exp(-held["cum"])
            kt_scr[rows, :] = held["k2"] * e_inv
            bt_scr[rows, :] = held["kk"] * held["a"] * e_inv

        def s_rel():
            e_rel = jnp.exp(held["cend"] - held["cum"])
            ktd_scr[rows, :] = held["k2"] * e_rel
            btd_scr[rows, :] = held["kk"] * held["a"] * e_rel
            fdec_scr[rows, :] = held["fdec"]

        def s_value():
            v = shifted(2 * HEADS_DIM, HEADS_DIM)
            v_scr[rows, :] = v
            bon_scr[rows, :] = held["bon"] * v

        return [s_lora, s_decay, s_keys, s_bonus, s_norm, s_inv, s_rel, s_value]

    def chunk_rows(c):
        return slice(c * CHUNK, (c + 1) * CHUNK)

    groups = [slice(g0, g0 + GROUP) for g0 in range(0, HEADS_DIM, GROUP)]

    named = dict(at=at_scr, rt=rt_scr, kt=kt_scr, bt=bt_scr, v=v_scr, btd=btd_scr, ktd=ktd_scr,
                 fdec=fdec_scr, ah=ah_scr, vh=vh_scr, arb=arb_scr, ark=ark_scr)
    prep_outs = (ah_scr, vh_scr, arb_scr, ark_scr)

    def loader(c, cols):
        rows = chunk_rows(c)
        return lambda name: named[name][rows, cols]

    def run(step_piece, prep_piece, extra):
        step_chunks = [] if step_piece is None else [step_piece * per + i for i in range(per)]
        prep_chunks = [] if prep_piece is None else [prep_piece * per + i for i in range(per)]
        states = [st_scr[:, cols] for cols in groups] if step_chunks else []
        stages, ys = [], []
        for c in step_chunks:
            out = []
            ys.append(out)
            stages += _wkv_step_stages([loader(c, cols) for cols in groups], states, out)
        if step_chunks:
            rows = slice(step_piece * pq, (step_piece + 1) * pq)
            gn = {}

            def gn_mean():
                gn["y"] = jnp.concatenate([jnp.concatenate(out, axis=1) for out in ys], axis=0)
                gn["sum"] = bdsum(gn["y"], 1)

            def gn_var():
                gn["d"] = gn["y"] - gn["sum"] * (1.0 / HEAD)
                gn["sq"] = bdsum(gn["d"] * gn["d"], 1)

            def gn_out():
                var = gn["sq"] * (1.0 / HEAD)
                yn = (gn["d"] * lax.rsqrt(var + GN_EPS) * gng_ref[...] + gnb_ref[...]
                      + bon_scr[rows, :])
                out_ref[rows, :] = (yn * g_scr[rows, :]).astype(BF16)

            stages += [gn_mean, gn_var, gn_out]
        hooks = []
        for i in range(max(len(stages), len(extra))):
            both = stages[i:i + 1] + extra[i:i + 1]
            hooks.append(lambda both=both: [f() for f in both])
        units = [(c, cols) for c in prep_chunks for cols in groups]
        prep_out = _wkv_prep([loader(c, cols) for c, cols in units], hooks)
        for cols, st in zip(groups, states):
            st_scr[:, cols] = st
        for (c, cols), outs in zip(units, prep_out):
            for ref, val in zip(prep_outs, outs):
                ref[chunk_rows(c), cols] = val

    look_ahead = steps > 1
    assert n_piece >= 3 or not look_ahead

    def first_piece():
        for stage in pre_stages(0):
            stage()

    if look_ahead:
        pl.when(j == 0)(first_piece)
    else:
        first_piece()
    for piece in range(n_piece + 1):
        if piece + 1 < n_piece:
            extra = pre_stages(piece + 1)
        elif piece + 1 == n_piece and look_ahead:
            extra = pre_stages(0, upcoming=True)
        else:
            extra = []
        run(piece - 1 if piece >= 1 else None, piece if piece < n_piece else None, extra)

    @pl.when(j == steps - 1)
    def _():
        shift_out_ref[...] = p_ref[ts - 1:ts, :]
        state_out_ref[...] = st_scr[...]


def _wkv_call(p2d, shift0, state0, prm, n_seq, steps, ts):
    rows = p2d.shape[0]
    row_map = lambda s, j: (s * steps + j, 0)
    next_map = lambda s, j: (s * steps + jnp.minimum(j + 1, steps - 1), 0)
    seq3 = lambda s, j: (s, 0, 0)
    vec = _const_spec((1, HEADS_DIM))
    big = pltpu.VMEM((ts, HEADS_DIM), F32)
    ltri = prm["ltri"][min(ts, WKV_PIECE)]
    return pl.pallas_call(
        functools.partial(_wkv_kernel, ts=ts, steps=steps),
        grid=(n_seq, steps),
        in_specs=[
            pl.BlockSpec((ts, RW_COLS), row_map),
            pl.BlockSpec((ts, RW_COLS), next_map),
            pl.BlockSpec((None, 1, RW_COLS), seq3),
            pl.BlockSpec((None, HEAD, HEADS_DIM), seq3),
            _const_spec((1, RW_COLS)),
            vec,
            _const_spec((128, 2 * HEADS_DIM)),
            vec,
            _const_spec((LORA_G, HEADS_DIM)),
            vec, vec, vec, vec, vec,
            _const_spec((GROUP, GROUP)),
            _const_spec(ltri.shape),
        ],
        out_specs=[
            pl.BlockSpec((ts, HEADS_DIM), row_map),
            pl.BlockSpec((None, 1, RW_COLS), seq3),
            pl.BlockSpec((None, HEAD, HEADS_DIM), seq3),
        ],
        out_shape=[
            jax.ShapeDtypeStruct((rows, HEADS_DIM), BF16),
            jax.ShapeDtypeStruct((n_seq, 1, RW_COLS), F32),
            jax.ShapeDtypeStruct((n_seq, HEAD, HEADS_DIM), F32),
        ],
        scratch_shapes=[
            pltpu.VMEM((HEAD, HEADS_DIM), F32),
        ] + [big] * 14,
        compiler_params=pltpu.CompilerParams(
            dimension_semantics=("arbitrary", "arbitrary"), vmem_limit_bytes=VMEM_LIMIT),
        name="wkv",
    )(p2d, p2d, shift0, state0, prm["mu"], prm["w0"], prm["wwa"], prm["a0"], prm["gup"],
      prm["kk"], prm["ka"], prm["rk"], prm["gng"], prm["gnb"], prm["ones"], ltri)


def _mixffn_kernel(xn_ref, att_ref, rw_ref, gate_ref, mod_ref, convp_ref,
                   ln1g_ref, ln1b_ref, ln2g_ref, ln2b_ref,
                   wa_ref, wr_ref, wo_ref, wup_ref, cw_ref, cb_ref, wdn_ref,
                   y_ref, convo_ref, carry_scr, yb_scr, *, tm, steps):
    j = pl.program_id(0)
    groups = mod_ref.shape[0]
    assert groups == 1 or steps == 1

    if groups == 1:
        @pl.when(j == 0)
        def _():
            carry_scr[8 - (CONV_W - 1):8, :] = convp_ref[0]

    def mod(idx):
        return _mod_row(mod_ref, idx, tm)

    half = tm // 2
    subs = [slice(0, half), slice(half, tm)]

    def modr(idx, rs):
        m = mod(idx)
        return m if m.shape[0] == 1 else m[rs]

    ma = [_dot(att_ref[rs, :], wa_ref[...]) for rs in subs]
    mr = [_dot(rw_ref[rs, :], wr_ref[...]) for rs in subs]
    merged = [(gate_ref[rs, 0:D_MODEL] * a + gate_ref[rs, D_MODEL:] * r).astype(BF16)
              for rs, a, r in zip(subs, ma, mr)]
    mix = [_dot(m, wo_ref[...]) for m in merged]
    x1 = [_layer_norm(ALPHA * xn_ref[rs, :] + (1.0 + modr(2, rs)) * m, ln1g_ref[...], ln1b_ref[...])
          for rs, m in zip(subs, mix)]
    h2 = jnp.concatenate([(a * (1.0 + modr(4, rs)) + modr(3, rs)).astype(BF16)
                          for rs, a in zip(subs, x1)], axis=0)

    cw_blk = 256
    glen = tm // groups
    rb = FFN_ROWS if groups == 1 else tm
    row8 = lax.broadcasted_iota(jnp.int32, (8, cw_blk), 0)
    grow = lax.broadcasted_iota(jnp.int32, (tm, cw_blk), 0) % glen

    def up(blk):
        c, r0 = blk
        h = h2[r0:r0 + rb]
        return (_dot(h, wup_ref[:, c:c + cw_blk]),
                _dot(h, wup_ref[:, D_FF + c:D_FF + c + cw_blk]))

    blocks = [(c, r0) for c in range(0, D_FF, cw_blk) for r0 in range(0, tm, rb)]
    for c, r0 in blocks:
        uc, uv = up((c, r0))
        cs = slice(c, c + cw_blk)
        r1 = pltpu.roll(uc, 1, 0)
        r2 = pltpu.roll(uc, 2, 0)
        if groups == 1:
            if r0 == 0:
                c6 = carry_scr[6:7, cs]
                c7 = carry_scr[7:8, cs]
            s1 = jnp.concatenate([jnp.where(row8 == 0, c7, r1[0:8]), r1[8:]], axis=0)
            s2 = jnp.concatenate(
                [jnp.where(row8 == 0, c6, jnp.where(row8 == 1, c7, r2[0:8])), r2[8:]], axis=0)
            c6 = uc[rb - 2:rb - 1, :]
            c7 = uc[rb - 1:rb, :]
            if r0 + rb == tm:
                carry_scr[:, cs] = uc[rb - 8:rb, :]
            tail = uc[rb - (CONV_W - 1):rb, :][None]
        else:
            hist = jnp.broadcast_to(convp_ref[:, :, cs][:, :, None, :],
                                    (groups, CONV_W - 1, glen, cw_blk))
            c6 = hist[:, 0].reshape(tm, cw_blk)
            c7 = hist[:, 1].reshape(tm, cw_blk)
            s1 = jnp.where(grow == 0, c7, r1)
            s2 = jnp.where(grow == 0, c6, jnp.where(grow == 1, c7, r2))
            tail = uc.reshape(groups, glen, cw_blk)[:, glen - (CONV_W - 1):, :]
        conv = cb_ref[:, cs] + s2 * cw_ref[0:1, cs] + s1 * cw_ref[1:2, cs] + uc * cw_ref[2:3, cs]
        yb_scr[r0:r0 + rb, cs] = (conv * _sigmoid(conv) * uv).astype(BF16)

        if r0 + rb == tm:
            @pl.when(j == steps - 1)
            def _():
                convo_ref[:, :, cs] = tail

    ff = [_dot(yb_scr[rs, :], wdn_ref[...]) for rs in subs]
    for rs, a, f in zip(subs, x1, ff):
        y_ref[rs, :] = _layer_norm(ALPHA * a + (1.0 + modr(5, rs)) * f, ln2g_ref[...], ln2b_ref[...])


def _mixffn_call(xn2d, att, rw, gates, mod, conv_prev, prm, steps, tm):
    rows = xn2d.shape[0]
    groups = mod.shape[0]
    row_map = lambda j: (j, 0)
    vec = _const_spec((1, D_MODEL))
    return pl.pallas_call(
        functools.partial(_mixffn_kernel, tm=tm, steps=steps),
        grid=(steps,),
        in_specs=[
            pl.BlockSpec((tm, D_MODEL), row_map),
            pl.BlockSpec((tm, HEADS_DIM), row_map),
            pl.BlockSpec((tm, HEADS_DIM), row_map),
            pl.BlockSpec((tm, GATE_COLS), row_map),
            _const_spec((groups, 6, D_MODEL)),
            _const_spec((groups, CONV_W - 1, D_FF)),
            vec, vec, vec, vec,
            _const_spec((HEADS_DIM, D_MODEL)),
            _const_spec((HEADS_DIM, D_MODEL)),
            _const_spec((D_MODEL, D_MODEL)),
            _const_spec((D_MODEL, 2 * D_FF)),
            _const_spec((CONV_W, D_FF)),
            _const_spec((1, D_FF)),
            _const_spec((D_FF, D_MODEL)),
        ],
        out_specs=[
            pl.BlockSpec((tm, D_MODEL), row_map),
            pl.BlockSpec((groups, CONV_W - 1, D_FF), lambda j: (0, 0, 0)),
        ],
        out_shape=[
            jax.ShapeDtypeStruct((rows, D_MODEL), F32),
            jax.ShapeDtypeStruct((groups, CONV_W - 1, D_FF), F32),
        ],
        scratch_shapes=[pltpu.VMEM((8, D_FF), F32), pltpu.VMEM((tm, D_FF), BF16)],
        compiler_params=pltpu.CompilerParams(
            dimension_semantics=("arbitrary",), vmem_limit_bytes=VMEM_LIMIT),
        name="mixffn",
    )(xn2d, att, rw, gates, mod, conv_prev,
      prm["ln1g"], prm["ln1b"], prm["ln2g"], prm["ln2b"],
      prm["wa"], prm["wr"], prm["wo"], prm["wup"], prm["cw"], prm["cb"], prm["wdn"])


def _pair_bias(table):
    assert CHUNK - 1 <= REL_CLIP
    top = ATT_REACH + CHUNK - 1
    n_far = top - REL_CLIP + 1
    far = jnp.broadcast_to(table[:, 2 * REL_CLIP:], (N_HEADS, n_far))
    lo_idx = top - (BAND + CHUNK - 2) + REL_CLIP
    near = table[:, lo_idx:2 * REL_CLIP][:, ::-1]
    ext = jnp.concatenate([far, near], axis=1).astype(F32) * LOG2E
    n_ext = BAND + CHUNK - 1
    period = jnp.concatenate([ext, jnp.zeros((N_HEADS, 1), F32)], axis=1)
    skew = jnp.tile(period, (1, CHUNK))[:, :CHUNK * n_ext].reshape(N_HEADS, CHUNK, n_ext)
    bias = skew[:, :, CHUNK - 1:CHUNK - 1 + BAND]
    return bias.reshape(N_HEADS // 2, 2 * CHUNK, BAND)


def _chunk_ltri(ts):
    t = jnp.arange(ts)
    return ((t[:, None] // CHUNK == t[None, :] // CHUNK) & (t[None, :] <= t[:, None])).astype(BF16)


def _trunk(x2d, mod, shift0, state0, conv_prev, caches, prm, n_seq):
    prompt = caches is None
    rows = x2d.shape[0]
    tm = min(rows, ROW_TILE)
    steps = rows // tm
    q, k, v, kv32, p, gates, xn = _inproj_call(
        x2d, mod, prm["lnig"], prm["lnib"], prm["win"], steps, tm, prompt)
    if prompt:
        att = _attn_prompt_call(q, k, v, prm["bias"])
    else:
        att = _attn_sample_call(q, k, v, *caches, prm["bias"])
    ts = min(rows // n_seq, ROW_TILE)
    rw, shift, state = _wkv_call(p, shift0, state0, prm, n_seq, rows // n_seq // ts, ts)
    y, conv = _mixffn_call(xn, att, rw, gates, mod, conv_prev, prm, steps, tm)
    return y, kv32, state, shift, conv


def kernel(x_prompt, x_sample, cache_attn_k, cache_attn_v, state_rwkv, state_shift, state_conv,
           c_prompt, c_sample, ln_in_g, ln_in_b, w_ada, b_ada, w_in, attn_rel_bias,
           rwkv_mu, rwkv_w0, rwkv_w_up, rwkv_a0, rwkv_a_up, rwkv_g_up, rwkv_k_k, rwkv_k_a,
           rwkv_r_k, rwkv_gn_g, rwkv_gn_b, w_branch_attn, w_branch_rwkv, w_out,
           ln1_g, ln1_b, ln2_g, ln2_b, w_ffn_up, ffn_conv_w, ffn_conv_b, w_ffn_down):
    bp, sp, _ = x_prompt.shape
    bs, ss, _ = x_sample.shape
    assert bp == 1 and ss == CHUNK and w_ada.shape[0] == DEPTH
    assert sp % ROW_TILE == 0 and bs * ss <= ROW_TILE and cache_attn_k.shape[2] == ATT_REACH

    row = lambda a: a.reshape(1, -1)
    wwa = jnp.zeros((LORA_W + LORA_A, 2 * HEADS_DIM), F32)
    wwa = wwa.at[:LORA_W, :HEADS_DIM].set(rwkv_w_up[0]).at[LORA_W:, HEADS_DIM:].set(rwkv_a_up[0])
    head_id = jnp.arange(GROUP) // HEAD
    prm = dict(
        lnig=row(ln_in_g), lnib=row(ln_in_b),
        ln1g=row(ln1_g[0]), ln1b=row(ln1_b[0]), ln2g=row(ln2_g[0]), ln2b=row(ln2_b[0]),
        win=w_in[0].astype(BF16), bias=_pair_bias(attn_rel_bias[0]),
        mu=row(rwkv_mu[0]), w0=row(rwkv_w0[0]), wwa=wwa.astype(BF16), a0=row(rwkv_a0[0]),
        gup=rwkv_g_up[0].astype(BF16), kk=row(rwkv_k_k[0]), ka=row(rwkv_k_a[0]),
        rk=row(rwkv_r_k[0]), gng=row(rwkv_gn_g[0]), gnb=row(rwkv_gn_b[0]),
        ones=(head_id[:, None] == head_id[None, :]).astype(BF16),
        ltri={WKV_PIECE: _chunk_ltri(WKV_PIECE), CHUNK: _chunk_ltri(CHUNK)},
        wa=w_branch_attn[0].astype(BF16), wr=w_branch_rwkv[0].astype(BF16),
        wo=w_out[0].astype(BF16), wup=w_ffn_up[0].astype(BF16),
        cw=ffn_conv_w[0], cb=row(ffn_conv_b[0]), wdn=w_ffn_down[0].astype(BF16),
    )

    n_c = bp + bs
    c_all = jnp.concatenate([c_prompt, c_sample, jnp.zeros((16 - n_c, D_MODEL), F32)], axis=0)
    mod = _mod_call(c_all, w_ada[0], row(b_ada[0])).reshape(16, 6, D_MODEL)

    y_p, kv_p, st_p, sh_p, cv_p = _trunk(
        x_prompt.reshape(sp, D_MODEL), mod[0:bp],
        jnp.zeros((bp, 1, RW_COLS), F32), jnp.zeros((bp, HEAD, HEADS_DIM), F32),
        jnp.zeros((bp, CONV_W - 1, D_FF), F32), None, prm, n_seq=bp)

    caches = (cache_attn_k[0].reshape(bs, ATT_REACH, HEADS_DIM),
              cache_attn_v[0].reshape(bs, ATT_REACH, HEADS_DIM))
    st0 = jnp.transpose(state_rwkv[0], (0, 3, 1, 2)).reshape(bs, HEAD, HEADS_DIM)
    y_s, kv_s, st_s, sh_s, cv_s = _trunk(
        x_sample.reshape(bs * ss, D_MODEL), mod[bp:n_c],
        state_shift[0], st0, state_conv[0], caches, prm, n_seq=bs)

    def state_out(st, b):
        return jnp.transpose(st.reshape(b, HEAD, N_HEADS, HEAD), (0, 2, 3, 1))[None]

    hs = (N_HEADS, HEAD)
    return (
        y_p.reshape(bp, sp, D_MODEL),
        y_s.reshape(bs, ss, D_MODEL),
        kv_p[:, :HEADS_DIM].reshape(1, bp, ATT_REACH, *hs),
        kv_p[:, HEADS_DIM:].reshape(1, bp, ATT_REACH, *hs),
        kv_s[:, :HEADS_DIM].reshape(1, bs, ss, *hs),
        kv_s[:, HEADS_DIM:].reshape(1, bs, ss, *hs),
        state_out(st_p, bp),
        state_out(st_s, bs),
        sh_p[None],
        sh_s[None],
        cv_p[None],
        cv_s[None],
    )
```

```python
import functools

import jax
import jax.numpy as jnp
from jax import lax
from jax.experimental import pallas as pl
from jax.experimental.pallas import tpu as pltpu

F32 = jnp.float32
BF16 = jnp.bfloat16

D_MODEL = 1024
CHUNK = 64
ATT_REACH = 512
BAND = ATT_REACH + CHUNK
N_HEADS = 8
HEAD = 64
HEADS_DIM = N_HEADS * HEAD
REL_CLIP = 128
LORA_W = 64
LORA_A = 64
LORA_G = 128
ATT_COLS = 3 * HEADS_DIM
RW_COLS = 3 * HEADS_DIM + LORA_W + LORA_A + LORA_G
GATE_COLS = 2 * D_MODEL
D_FF = 2816
CONV_W = 3
LN_EPS = 1e-5
GN_EPS = 64e-5
DEPTH = 1
ALPHA = (2 * DEPTH) ** 0.25
LOG2E = 1.4426950408889634

GROUP = 256
ROW_TILE = 512
WKV_PIECE = 128
FFN_ROWS = 512
VMEM_LIMIT = 56 * 1024 * 1024


def _const_spec(shape):
    nd = len(shape)
    return pl.BlockSpec(shape, lambda *_: (0,) * nd, pipeline_mode=pl.Buffered(1))


def _layer_norm(x, g, b):
    mu = jnp.mean(x, axis=-1, keepdims=True)
    xc = x - mu
    var = jnp.mean(xc * xc, axis=-1, keepdims=True)
    return xc * lax.rsqrt(var + LN_EPS) * g + b


def _sigmoid(x):
    return 1.0 / (1.0 + jnp.exp(-x))


def _split3(x):
    hi = x.astype(BF16)
    r1 = x - hi.astype(F32)
    mid = r1.astype(BF16)
    lo = (r1 - mid.astype(F32)).astype(BF16)
    return hi, mid, lo


def _dot(a, b):
    return jnp.dot(a, b, preferred_element_type=F32)


def _dot_nt(a, b):
    return lax.dot_general(a, b, (((1,), (1,)), ((), ())), preferred_element_type=F32)


def _dot_tn(a, b):
    return lax.dot_general(a, b, (((0,), (0,)), ((), ())), preferred_element_type=F32)


def _mod_kernel(c_ref, w_ref, b_ref, o_ref):
    c = c_ref[...]
    s = (c * _sigmoid(c)).astype(BF16)
    o_ref[...] = _dot(s, w_ref[...].astype(BF16)) + b_ref[...]


def _mod_call(c_all, w_ada, b_ada):
    n = c_all.shape[0]
    nblk = 6
    return pl.pallas_call(
        _mod_kernel,
        grid=(nblk,),
        in_specs=[
            pl.BlockSpec((n, D_MODEL), lambda i: (0, 0)),
            pl.BlockSpec((D_MODEL, D_MODEL), lambda i: (0, i)),
            pl.BlockSpec((1, D_MODEL), lambda i: (0, i)),
        ],
        out_specs=pl.BlockSpec((n, D_MODEL), lambda i: (0, i)),
        out_shape=jax.ShapeDtypeStruct((n, 6 * D_MODEL), F32),
        compiler_params=pltpu.CompilerParams(dimension_semantics=("arbitrary",)),
        name="mod",
    )(c_all, w_ada, b_ada)


def _mod_row(mod_ref, idx, tm):
    groups, _, d = mod_ref.shape
    m = mod_ref[:, idx:idx + 1, :]
    if groups == 1:
        return m[0]
    return jnp.broadcast_to(m, (groups, tm // groups, d)).reshape(tm, d)


def _inproj_kernel(x_ref, mod_ref, lng_ref, lnb_ref, w_ref,
                   q_ref, k_ref, v_ref, kv_ref, p_ref, g_ref, xn_ref, *, lead):
    tm = x_ref.shape[0]
    j = pl.program_id(0)

    @pl.when(j < lead)
    def _():
        k_ref[...] = jnp.zeros_like(k_ref)
        v_ref[...] = jnp.zeros_like(v_ref)

    def half_tile(rs):
        def mrow(idx):
            m = _mod_row(mod_ref, idx, tm)
            return m if m.shape[0] == 1 else m[rs]

        xn = _layer_norm(x_ref[rs, :], lng_ref[...], lnb_ref[...])
        xn_ref[rs, :] = xn
        hb = (xn * (1.0 + mrow(1)) + mrow(0)).astype(BF16)

        def seg(a, b):
            return _dot(hb, w_ref[:, a:b])

        q_ref[rs, :] = (seg(0, HEADS_DIM) * (HEAD ** -0.5 * LOG2E)).astype(BF16)
        k = seg(HEADS_DIM, 2 * HEADS_DIM)
        k_ref[rs, :] = k.astype(BF16)
        kv_ref[rs, 0:HEADS_DIM] = k
        v = seg(2 * HEADS_DIM, 3 * HEADS_DIM)
        v_ref[rs, :] = v.astype(BF16)
        kv_ref[rs, HEADS_DIM:2 * HEADS_DIM] = v
        for c in range(0, RW_COLS, 256):
            p_ref[rs, c:c + 256] = seg(ATT_COLS + c, ATT_COLS + c + 256)
        g0 = ATT_COLS + RW_COLS
        for c in range(0, GATE_COLS, 512):
            g_ref[rs, c:c + 512] = _sigmoid(seg(g0 + c, g0 + c + 512))

    @pl.when(j >= lead)
    def _():
        half_tile(slice(0, tm // 2))
        half_tile(slice(tm // 2, tm))


def _inproj_call(x2d, mod, ln_g, ln_b, w_in_b, steps, tm, prompt):
    rows = x2d.shape[0]
    in_cols = w_in_b.shape[1]
    groups = mod.shape[0]
    if prompt:
        lead = ATT_REACH // tm
        kv_rows = ATT_REACH
        row_map = lambda j: (jnp.maximum(j - lead, 0), 0)
        kv_map = lambda j: (jnp.maximum(j - steps, 0), 0)
        ext_map = lambda j: (j, 0)
    else:
        lead = 0
        kv_rows = rows
        row_map = kv_map = ext_map = lambda j: (j, 0)
    return pl.pallas_call(
        functools.partial(_inproj_kernel, lead=lead),
        grid=(steps + lead,),
        in_specs=[
            pl.BlockSpec((tm, D_MODEL), row_map),
            _const_spec((groups, 6, D_MODEL)),
            _const_spec((1, D_MODEL)),
            _const_spec((1, D_MODEL)),
            _const_spec((D_MODEL, in_cols)),
        ],
        out_specs=[
            pl.BlockSpec((tm, HEADS_DIM), row_map),
            pl.BlockSpec((tm, HEADS_DIM), ext_map),
            pl.BlockSpec((tm, HEADS_DIM), ext_map),
            pl.BlockSpec((tm, 2 * HEADS_DIM), kv_map),
            pl.BlockSpec((tm, RW_COLS), row_map),
            pl.BlockSpec((tm, GATE_COLS), row_map),
            pl.BlockSpec((tm, D_MODEL), row_map),
        ],
        out_shape=[
            jax.ShapeDtypeStruct((rows, HEADS_DIM), BF16),
            jax.ShapeDtypeStruct((rows + lead * tm, HEADS_DIM), BF16),
            jax.ShapeDtypeStruct((rows + lead * tm, HEADS_DIM), BF16),
            jax.ShapeDtypeStruct((kv_rows, 2 * HEADS_DIM), F32),
            jax.ShapeDtypeStruct((rows, RW_COLS), F32),
            jax.ShapeDtypeStruct((rows, GATE_COLS), F32),
            jax.ShapeDtypeStruct((rows, D_MODEL), F32),
        ],
        compiler_params=pltpu.CompilerParams(
            dimension_semantics=("arbitrary",), vmem_limit_bytes=VMEM_LIMIT),
        name="inproj",
    )(x2d, mod, ln_g, ln_b, w_in_b)


def _attn_chunks(chunks, bias_ref):
    lane = lax.broadcasted_iota(jnp.int32, (CHUNK, 128), 1)
    first = lane < HEAD
    pairs = [slice(pr * 128, (pr + 1) * 128) for pr in range(N_HEADS // 2)]
    scores = []
    for qc, kb, _, _ in chunks:
        for sl in pairs:
            q2 = qc[:, sl].astype(F32)
            qs = jnp.concatenate([jnp.where(first, q2, 0.0), jnp.where(first, 0.0, q2)],
                                 axis=0).astype(BF16)
            scores.append(_dot_nt(qs, kb(sl)))
    probs, sums = [], []
    for i, s in enumerate(scores):
        thr = chunks[i // len(pairs)][3]
        s = s + bias_ref[i % len(pairs)]
        if thr is not None:
            col = lax.broadcasted_iota(jnp.int32, s.shape, 1)
            s = jnp.where(col >= thr, s, -jnp.inf)
        e = jnp.exp2(s - jnp.max(s, axis=1, keepdims=True))
        sums.append(jnp.sum(e, axis=1, keepdims=True))
        probs.append(e.astype(BF16))
    outs = []
    for ci, (_, _, vb, _) in enumerate(chunks):
        cols = []
        for pi, sl in enumerate(pairs):
            i = ci * len(pairs) + pi
            o = _dot(probs[i], vb(sl)) / sums[i]
            cols.append(jnp.where(first, o[0:CHUNK], o[CHUNK:2 * CHUNK]))
        outs.append(jnp.concatenate(cols, axis=1).astype(BF16))
    return outs


def _attn_prompt_kernel(q_ref, ka_ref, kb_ref, va_ref, vb_ref, bias_ref, o_ref, kbuf, vbuf,
                        *, chunks):
    tq = chunks * CHUNK
    kbuf[0:tq, :] = ka_ref[...]
    kbuf[tq:2 * tq, :] = kb_ref[...]
    vbuf[0:tq, :] = va_ref[...]
    vbuf[tq:2 * tq, :] = vb_ref[...]
    s = pl.program_id(0)

    per = 4

    def run(masked):
        def body(i, carry):
            units, starts = [], []
            for k in range(per):
                g = i * per + k
                r0 = pl.multiple_of(g * CHUNK, CHUNK)
                thr = ATT_REACH - (s * chunks + g) * CHUNK if masked else None
                band = pl.ds(r0, BAND)
                units.append((q_ref[pl.ds(r0, CHUNK), :],
                              lambda sl, band=band: kbuf[band, sl],
                              lambda sl, band=band: vbuf[band, sl], thr))
                starts.append(r0)
            for r0, o in zip(starts, _attn_chunks(units, bias_ref)):
                o_ref[pl.ds(r0, CHUNK), :] = o
            return carry

        lax.fori_loop(0, chunks // per, body, 0)

    assert chunks * CHUNK >= ATT_REACH
    pl.when(s == 0)(functools.partial(run, True))
    pl.when(s != 0)(functools.partial(run, False))


def _attn_prompt_call(q, kext, vext, bias):
    rows = q.shape[0]
    tq = ATT_REACH
    blk = lambda off: pl.BlockSpec((tq, HEADS_DIM), lambda s: (s + off, 0))
    return pl.pallas_call(
        functools.partial(_attn_prompt_kernel, chunks=tq // CHUNK),
        grid=(rows // tq,),
        in_specs=[blk(0), blk(0), blk(1), blk(0), blk(1), _const_spec(bias.shape)],
        out_specs=blk(0),
        out_shape=jax.ShapeDtypeStruct((rows, HEADS_DIM), BF16),
        scratch_shapes=[pltpu.VMEM((2 * tq, HEADS_DIM), BF16), pltpu.VMEM((2 * tq, HEADS_DIM), BF16)],
        compiler_params=pltpu.CompilerParams(dimension_semantics=("arbitrary",)),
        name="attn_prompt",
    )(q, kext, kext, vext, vext, bias)


def _attn_sample_kernel(q_ref, k_ref, v_ref, ck_ref, cv_ref, bias_ref, o_ref):
    def band(cache_ref, new_ref):
        return lambda sl: jnp.concatenate([cache_ref[:, sl].astype(BF16), new_ref[:, sl]], axis=0)

    unit = (q_ref[...], band(ck_ref, k_ref), band(cv_ref, v_ref), None)
    o_ref[...] = _attn_chunks([unit], bias_ref)[0]


def _attn_sample_call(q, k, v, cache_k, cache_v, bias):
    nb = cache_k.shape[0]
    new = pl.BlockSpec((CHUNK, HEADS_DIM), lambda b: (b, 0))
    cache = pl.BlockSpec((None, ATT_REACH, HEADS_DIM), lambda b: (b, 0, 0))
    return pl.pallas_call(
        _attn_sample_kernel,
        grid=(nb,),
        in_specs=[new, new, new, cache, cache, _const_spec(bias.shape)],
        out_specs=new,
        out_shape=jax.ShapeDtypeStruct((nb * CHUNK, HEADS_DIM), BF16),
        compiler_params=pltpu.CompilerParams(dimension_semantics=("arbitrary",)),
        name="attn_sample",
    )(q, k, v, cache_k, cache_v, bias)


def _same_head():
    r = lax.broadcasted_iota(jnp.int32, (GROUP, GROUP), 0) // HEAD
    c = lax.broadcasted_iota(jnp.int32, (GROUP, GROUP), 1) // HEAD
    return r == c


def _blk(x, same_head):
    return jnp.where(same_head, jnp.concatenate([x] * 4, axis=0), 0.0).astype(BF16)


def _wkv_prep(units, hooks):
    same_head = _same_head()
    t64 = lax.broadcasted_iota(jnp.int32, (CHUNK, GROUP), 0)
    i64 = lax.broadcasted_iota(jnp.int32, (CHUNK, GROUP), 1) % HEAD
    strict = i64 < t64
    incl = i64 <= t64
    pending = list(hooks)

    def stage_done():
        if pending:
            pending.pop(0)()

    def blk(x):
        return _blk(x, same_head)

    n, a_ak, a_rb, a_rk = [], [], [], []
    for ld in units:
        lhs = jnp.concatenate([ld("at"), ld("rt")], axis=0).astype(BF16)
        rhs = jnp.concatenate([blk(ld("bt")), blk(ld("kt"))], axis=0)
        a_all = _dot_nt(lhs, rhs)
        n.append(jnp.where(strict, a_all[0:CHUNK, 0:GROUP], 0.0))
        a_ak.append(jnp.where(strict, a_all[0:CHUNK, GROUP:], 0.0))
        a_rb.append(jnp.where(incl, a_all[CHUNK:, 0:GROUP], 0.0))
        a_rk.append(jnp.where(incl, a_all[CHUNK:, GROUP:], 0.0))
    stage_done()

    x0 = [_dot(a.astype(BF16), blk(ld("v"))) for a, ld in zip(a_ak, units)]
    npow = [_dot(m.astype(BF16), blk(m)) for m in n]
    t = [jnp.where(i64 == t64, 1.0, 0.0) + m for m in n]
    stage_done()
    for _ in range(4):
        prod = [_dot(jnp.concatenate([a, b], axis=0).astype(BF16), blk(b))
                for a, b in zip(t, npow)]
        t = [a + p[0:CHUNK] for a, p in zip(t, prod)]
        npow = [p[CHUNK:] for p in prod]
        stage_done()
    t = [a + _dot(a.astype(BF16), blk(b)) for a, b in zip(t, npow)]
    stage_done()
    res = [_dot(a.astype(BF16), jnp.concatenate([blk(ld("at")), blk(x)], axis=1))
           for a, x, ld in zip(t, x0, units)]
    stage_done()
    while pending:
        stage_done()
    return [(r[:, 0:GROUP], r[:, GROUP:], b, k) for r, b, k in zip(res, a_rb, a_rk)]


def _wkv_step_stages(units, states, out):
    same_head = _same_head()
    held = {}

    def blk(x):
        return _blk(x, same_head)

    def first():
        held["sblk"] = [blk(st) for st in states]
        held["u"] = [_dot(ld("ah").astype(BF16), sb) + ld("vh")
                     for ld, sb in zip(units, held["sblk"])]

    def second():
        for i, ld in enumerate(units):
            u, v = held["u"][i], ld("v")
            y = _dot(jnp.concatenate([ld("rt"), ld("arb"), ld("ark")], axis=1).astype(BF16),
                     jnp.concatenate([held["sblk"][i], blk(u), blk(v)], axis=0))
            g = _dot_tn(jnp.concatenate([ld("btd"), ld("ktd")], axis=0).astype(BF16),
                        jnp.concatenate([u, v], axis=0).astype(BF16))
            g = jnp.where(same_head, g, 0.0)
            delta = g[0:64] + g[64:128] + g[128:192] + g[192:256]
            out.append(y)
            states[i] = states[i] * ld("fdec") + delta

    return [first, second]


def _wkv_kernel(p_ref, pn_ref, shift0_ref, state0_ref, mu_ref, w0_ref, wwa_ref, a0_ref, gup_ref,
                kkw_ref, kaw_ref, rkw_ref, gng_ref, gnb_ref, ones_ref, ltri_ref,
                out_ref, shift_out_ref, state_out_ref,
                st_scr, rt_scr, at_scr, kt_scr, bt_scr, v_scr, btd_scr, ktd_scr,
                fdec_scr, ah_scr, vh_scr, arb_scr, ark_scr, g_scr, bon_scr, *, ts, steps):
    j = pl.program_id(1)

    @pl.when(j == 0)
    def _():
        st_scr[...] = state0_ref[...]

    def bdsum(x, terms):
        ones = ones_ref[...]
        halves = []
        for c0 in range(0, HEADS_DIM, GROUP):
            parts = _split3(x[:, c0:c0 + GROUP])[:terms]
            acc = _dot(parts[0], ones)
            for part in parts[1:]:
                acc = acc + _dot(part, ones)
            halves.append(acc)
        return jnp.concatenate(halves, axis=1)

    pq = ltri_ref.shape[0]
    per = pq // CHUNK
    n_piece = ts // pq

    def pre_stages(q, upcoming=False):
        r0 = q * pq
        rows = slice(r0, r0 + pq)
        src_ref = pn_ref if upcoming else p_ref
        held = {}

        def shifted(c0, w):
            cols = slice(c0, c0 + w)
            if upcoming:
                prev_row = p_ref[ts - 8:ts, cols][7:8]
            elif q == 0:
                prev_row = shift0_ref[:, cols]
            else:
                prev_row = p_ref[r0 - 8:r0, cols][7:8]
            top = lax.broadcasted_iota(jnp.int32, (pq, 1), 0) == 0
            pb = src_ref[rows, cols]
            prev = jnp.where(top, prev_row, pltpu.roll(pb, 1, 0))
            return pb + (prev - pb) * mu_ref[:, cols]

        def s_lora():
            lora = shifted(3 * HEADS_DIM, 256)
            lwla = lora[:, 0:128]
            lane = lax.broadcasted_iota(jnp.int32, (pq, 128), 1)
            held["raw"] = _dot(jnp.where(lane < LORA_W, jnp.tanh(lwla), lwla).astype(BF16),
                               wwa_ref[...])
            g_scr[rows, :] = _dot(_sigmoid(lora[:, 128:256]).astype(BF16), gup_ref[...])
            held["k"] = shifted(HEADS_DIM, HEADS_DIM)

        def s_decay():
            w_pre = w0_ref[...] + held["raw"][:, 0:HEADS_DIM]
            softplus = jnp.maximum(-w_pre, 0.0) + jnp.log(1.0 + jnp.exp(-jnp.abs(w_pre)))
            lw = -jnp.exp(-softplus - 0.5)
            ltri = ltri_ref[...]
            hi, mid, _ = _split3(lw)
            held["cum"] = _dot(ltri, hi) + _dot(ltri, mid)
            held["lw"] = lw

        def s_keys():
            k = held["k"]
            a = _sigmoid(a0_ref[...] + held["raw"][:, HEADS_DIM:])
            kk = k * kkw_ref[...]
            held["nrm"] = bdsum(kk * kk, 1)
            held.update(a=a, kk=kk, k2=k * (1.0 + (a - 1.0) * kaw_ref[...]))

        def s_bonus():
            r = shifted(0, HEADS_DIM)
            held["bon"] = bdsum(r * held["k2"] * rkw_ref[...], 1)
            rt_scr[rows, :] = r * jnp.exp(held["cum"])

        def s_norm():
            cum = held["cum"]
            kk = held["kk"] / jnp.maximum(jnp.sqrt(held["nrm"]), 1e-12)
            cum3 = cum.reshape(per, CHUNK, HEADS_DIM)
            cend = jnp.broadcast_to(cum3[:, CHUNK - 1:CHUNK, :], cum3.shape).reshape(pq, HEADS_DIM)
            diag = (lax.broadcasted_iota(jnp.int32, (pq, HEADS_DIM), 0) % CHUNK
                    == lax.broadcasted_iota(jnp.int32, (pq, HEADS_DIM), 1) % HEAD)
            held["fdec"] = bdsum(jnp.where(diag, jnp.exp(cend), 0.0), 2)
            at_scr[rows, :] = -kk * jnp.exp(cum - held["lw"])
            held.update(kk=kk, cend=cend)

        def s_inv():
            e_inv = jnp.exp(-held["cum"])
            kt_scr[rows, :] = held["k2"] * e_inv
            bt_scr[rows, :] = held["kk"] * held["a"] * e_inv

        def s_rel():
            e_rel = jnp.exp(held["cend"] - held["cum"])
            ktd_scr[rows, :] = held["k2"] * e_rel
            btd_scr[rows, :] = held["kk"] * held["a"] * e_rel
            fdec_scr[rows, :] = held["fdec"]

        def s_value():
            v = shifted(2 * HEADS_DIM, HEADS_DIM)
            v_scr[rows, :] = v
            bon_scr[rows, :] = held["bon"] * v

        return [s_lora, s_decay, s_keys, s_bonus, s_norm, s_inv, s_rel, s_value]

    def chunk_rows(c):
        return slice(c * CHUNK, (c + 1) * CHUNK)

    groups = [slice(g0, g0 + GROUP) for g0 in range(0, HEADS_DIM, GROUP)]

    named = dict(at=at_scr, rt=rt_scr, kt=kt_scr, bt=bt_scr, v=v_scr, btd=btd_scr, ktd=ktd_scr,
                 fdec=fdec_scr, ah=ah_scr, vh=vh_scr, arb=arb_scr, ark=ark_scr)
    prep_outs = (ah_scr, vh_scr, arb_scr, ark_scr)

    def loader(c, cols):
        rows = chunk_rows(c)
        return lambda name: named[name][rows, cols]

    def run(step_piece, prep_piece, extra):
        step_chunks = [] if step_piece is None else [step_piece * per + i for i in range(per)]
        prep_chunks = [] if prep_piece is None else [prep_piece * per + i for i in range(per)]
        states = [st_scr[:, cols] for cols in groups] if step_chunks else []
        stages, ys = [], []
        for c in step_chunks:
            out = []
            ys.append(out)
            stages += _wkv_step_stages([loader(c, cols) for cols in groups], states, out)
        if step_chunks:
            rows = slice(step_piece * pq, (step_piece + 1) * pq)
            gn = {}

            def gn_mean():
                gn["y"] = jnp.concatenate([jnp.concatenate(out, axis=1) for out in ys], axis=0)
                gn["sum"] = bdsum(gn["y"], 1)

            def gn_var():
                gn["d"] = gn["y"] - gn["sum"] * (1.0 / HEAD)
                gn["sq"] = bdsum(gn["d"] * gn["d"], 1)

            def gn_out():
                var = gn["sq"] * (1.0 / HEAD)
                yn = (gn["d"] * lax.rsqrt(var + GN_EPS) * gng_ref[...] + gnb_ref[...]
                      + bon_scr[rows, :])
                out_ref[rows, :] = (yn * g_scr[rows, :]).astype(BF16)

            stages += [gn_mean, gn_var, gn_out]
        hooks = []
        for i in range(max(len(stages), len(extra))):
            both = stages[i:i + 1] + extra[i:i + 1]
            hooks.append(lambda both=both: [f() for f in both])
        units = [(c, cols) for c in prep_chunks for cols in groups]
        prep_out = _wkv_prep([loader(c, cols) for c, cols in units], hooks)
        for cols, st in zip(groups, states):
            st_scr[:, cols] = st
        for (c, cols), outs in zip(units, prep_out):
            for ref, val in zip(prep_outs, outs):
                ref[chunk_rows(c), cols] = val

    look_ahead = steps > 1
    assert n_piece >= 3 or not look_ahead

    def first_piece():
        for stage in pre_stages(0):
            stage()

    if look_ahead:
        pl.when(j == 0)(first_piece)
    else:
        first_piece()
    for piece in range(n_piece + 1):
        if piece + 1 < n_piece:
            extra = pre_stages(piece + 1)
        elif piece + 1 == n_piece and look_ahead:
            extra = pre_stages(0, upcoming=True)
        else:
            extra = []
        run(piece - 1 if piece >= 1 else None, piece if piece < n_piece else None, extra)

    @pl.when(j == steps - 1)
    def _():
        shift_out_ref[...] = p_ref[ts - 1:ts, :]
        state_out_ref[...] = st_scr[...]


def _wkv_call(p2d, shift0, state0, prm, n_seq, steps, ts):
    rows = p2d.shape[0]
    row_map = lambda s, j: (s * steps + j, 0)
    next_map = lambda s, j: (s * steps + jnp.minimum(j + 1, steps - 1), 0)
    seq3 = lambda s, j: (s, 0, 0)
    vec = _const_spec((1, HEADS_DIM))
    big = pltpu.VMEM((ts, HEADS_DIM), F32)
    ltri = prm["ltri"][min(ts, WKV_PIECE)]
    return pl.pallas_call(
        functools.partial(_wkv_kernel, ts=ts, steps=steps),
        grid=(n_seq, steps),
        in_specs=[
            pl.BlockSpec((ts, RW_COLS), row_map),
            pl.BlockSpec((ts, RW_COLS), next_map),
            pl.BlockSpec((None, 1, RW_COLS), seq3),
            pl.BlockSpec((None, HEAD, HEADS_DIM), seq3),
            _const_spec((1, RW_COLS)),
            vec,
            _const_spec((128, 2 * HEADS_DIM)),
            vec,
            _const_spec((LORA_G, HEADS_DIM)),
            vec, vec, vec, vec, vec,
            _const_spec((GROUP, GROUP)),
            _const_spec(ltri.shape),
        ],
        out_specs=[
            pl.BlockSpec((ts, HEADS_DIM), row_map),
            pl.BlockSpec((None, 1, RW_COLS), seq3),
            pl.BlockSpec((None, HEAD, HEADS_DIM), seq3),
        ],
        out_shape=[
            jax.ShapeDtypeStruct((rows, HEADS_DIM), BF16),
            jax.ShapeDtypeStruct((n_seq, 1, RW_COLS), F32),
            jax.ShapeDtypeStruct((n_seq, HEAD, HEADS_DIM), F32),
        ],
        scratch_shapes=[
            pltpu.VMEM((HEAD, HEADS_DIM), F32),
        ] + [big] * 14,
        compiler_params=pltpu.CompilerParams(
            dimension_semantics=("arbitrary", "arbitrary"), vmem_limit_bytes=VMEM_LIMIT),
        name="wkv",
    )(p2d, p2d, shift0, state0, prm["mu"], prm["w0"], prm["wwa"], prm["a0"], prm["gup"],
      prm["kk"], prm["ka"], prm["rk"], prm["gng"], prm["gnb"], prm["ones"], ltri)


def _mixffn_kernel(xn_ref, att_ref, rw_ref, gate_ref, mod_ref, convp_ref,
                   ln1g_ref, ln1b_ref, ln2g_ref, ln2b_ref,
                   wa_ref, wr_ref, wo_ref, wup_ref, cw_ref, cb_ref, wdn_ref,
                   y_ref, convo_ref, carry_scr, yb_scr, *, tm, steps):
    j = pl.program_id(0)
    groups = mod_ref.shape[0]
    assert groups == 1 or steps == 1

    if groups == 1:
        @pl.when(j == 0)
        def _():
            carry_scr[8 - (CONV_W - 1):8, :] = convp_ref[0]

    def mod(idx):
        return _mod_row(mod_ref, idx, tm)

    half = tm // 2
    subs = [slice(0, half), slice(half, tm)]

    def modr(idx, rs):
        m = mod(idx)
        return m if m.shape[0] == 1 else m[rs]

    ma = [_dot(att_ref[rs, :], wa_ref[...]) for rs in subs]
    mr = [_dot(rw_ref[rs, :], wr_ref[...]) for rs in subs]
    merged = [(gate_ref[rs, 0:D_MODEL] * a + gate_ref[rs, D_MODEL:] * r).astype(BF16)
              for rs, a, r in zip(subs, ma, mr)]
    mix = [_dot(m, wo_ref[...]) for m in merged]
    x1 = [_layer_norm(ALPHA * xn_ref[rs, :] + (1.0 + modr(2, rs)) * m, ln1g_ref[...], ln1b_ref[...])
          for rs, m in zip(subs, mix)]
    h2 = jnp.concatenate([(a * (1.0 + modr(4, rs)) + modr(3, rs)).astype(BF16)
                          for rs, a in zip(subs, x1)], axis=0)

    cw_blk = 256
    glen = tm // groups
    rb = FFN_ROWS if groups == 1 else tm
    row8 = lax.broadcasted_iota(jnp.int32, (8, cw_blk), 0)
    grow = lax.broadcasted_iota(jnp.int32, (tm, cw_blk), 0) % glen

    def up(blk):
        c, r0 = blk
        h = h2[r0:r0 + rb]
        return (_dot(h, wup_ref[:, c:c + cw_blk]),
                _dot(h, wup_ref[:, D_FF + c:D_FF + c + cw_blk]))

    blocks = [(c, r0) for c in range(0, D_FF, cw_blk) for r0 in range(0, tm, rb)]
    for c, r0 in blocks:
        uc, uv = up((c, r0))
        cs = slice(c, c + cw_blk)
        r1 = pltpu.roll(uc, 1, 0)
        r2 = pltpu.roll(uc, 2, 0)
        if groups == 1:
            if r0 == 0:
                c6 = carry_scr[6:7, cs]
                c7 = carry_scr[7:8, cs]
            s1 = jnp.concatenate([jnp.where(row8 == 0, c7, r1[0:8]), r1[8:]], axis=0)
            s2 = jnp.concatenate(
                [jnp.where(row8 == 0, c6, jnp.where(row8 == 1, c7, r2[0:8])), r2[8:]], axis=0)
            c6 = uc[rb - 2:rb - 1, :]
            c7 = uc[rb - 1:rb, :]
            if r0 + rb == tm:
                carry_scr[:, cs] = uc[rb - 8:rb, :]
            tail = uc[rb - (CONV_W - 1):rb, :][None]
        else:
            hist = jnp.broadcast_to(convp_ref[:, :, cs][:, :, None, :],
                                    (groups, CONV_W - 1, glen, cw_blk))
            c6 = hist[:, 0].reshape(tm, cw_blk)
            c7 = hist[:, 1].reshape(tm, cw_blk)
            s1 = jnp.where(grow == 0, c7, r1)
            s2 = jnp.where(grow == 0, c6, jnp.where(grow == 1, c7, r2))
            tail = uc.reshape(groups, glen, cw_blk)[:, glen - (CONV_W - 1):, :]
        conv = cb_ref[:, cs] + s2 * cw_ref[0:1, cs] + s1 * cw_ref[1:2, cs] + uc * cw_ref[2:3, cs]
        yb_scr[r0:r0 + rb, cs] = (conv * _sigmoid(conv) * uv).astype(BF16)

        if r0 + rb == tm:
            convo_ref[:, :, cs] = tail

    ff = [_dot(yb_scr[rs, :], wdn_ref[...]) for rs in subs]
    for rs, a, f in zip(subs, x1, ff):
        y_ref[rs, :] = _layer_norm(ALPHA * a + (1.0 + modr(5, rs)) * f, ln2g_ref[...], ln2b_ref[...])


def _mixffn_call(xn2d, att, rw, gates, mod, conv_prev, prm, steps, tm):
    rows = xn2d.shape[0]
    groups = mod.shape[0]
    row_map = lambda j: (j, 0)
    vec = _const_spec((1, D_MODEL))
    return pl.pallas_call(
        functools.partial(_mixffn_kernel, tm=tm, steps=steps),
        grid=(steps,),
        in_specs=[
            pl.BlockSpec((tm, D_MODEL), row_map),
            pl.BlockSpec((tm, HEADS_DIM), row_map),
            pl.BlockSpec((tm, HEADS_DIM), row_map),
            pl.BlockSpec((tm, GATE_COLS), row_map),
            _const_spec((groups, 6, D_MODEL)),
            _const_spec((groups, CONV_W - 1, D_FF)),
            vec, vec, vec, vec,
            _const_spec((HEADS_DIM, D_MODEL)),
            _const_spec((HEADS_DIM, D_MODEL)),
            _const_spec((D_MODEL, D_MODEL)),
            _const_spec((D_MODEL, 2 * D_FF)),
            _const_spec((CONV_W, D_FF)),
            _const_spec((1, D_FF)),
            _const_spec((D_FF, D_MODEL)),
        ],
        out_specs=[
            pl.BlockSpec((tm, D_MODEL), row_map),
            pl.BlockSpec((groups, CONV_W - 1, D_FF), lambda j: (0, 0, 0)),
        ],
        out_shape=[
            jax.ShapeDtypeStruct((rows, D_MODEL), F32),
            jax.ShapeDtypeStruct((groups, CONV_W - 1, D_FF), F32),
        ],
        scratch_shapes=[pltpu.VMEM((8, D_FF), F32), pltpu.VMEM((tm, D_FF), BF16)],
        compiler_params=pltpu.CompilerParams(
            dimension_semantics=("arbitrary",), vmem_limit_bytes=VMEM_LIMIT),
        name="mixffn",
    )(xn2d, att, rw, gates, mod, conv_prev,
      prm["ln1g"], prm["ln1b"], prm["ln2g"], prm["ln2b"],
      prm["wa"], prm["wr"], prm["wo"], prm["wup"], prm["cw"], prm["cb"], prm["wdn"])


def _pair_bias(table):
    assert CHUNK - 1 <= REL_CLIP
    top = ATT_REACH + CHUNK - 1
    n_far = top - REL_CLIP + 1
    far = jnp.broadcast_to(table[:, 2 * REL_CLIP:], (N_HEADS, n_far))
    lo_idx = top - (BAND + CHUNK - 2) + REL_CLIP
    near = table[:, lo_idx:2 * REL_CLIP][:, ::-1]
    ext = jnp.concatenate([far, near], axis=1).astype(F32) * LOG2E
    n_ext = BAND + CHUNK - 1
    period = jnp.concatenate([ext, jnp.zeros((N_HEADS, 1), F32)], axis=1)
    skew = jnp.tile(period, (1, CHUNK))[:, :CHUNK * n_ext].reshape(N_HEADS, CHUNK, n_ext)
    bias = skew[:, :, CHUNK - 1:CHUNK - 1 + BAND]
    return bias.reshape(N_HEADS // 2, 2 * CHUNK, BAND)


def _chunk_ltri(ts):
    t = jnp.arange(ts)
    return ((t[:, None] // CHUNK == t[None, :] // CHUNK) & (t[None, :] <= t[:, None])).astype(BF16)


def _trunk(x2d, mod, shift0, state0, conv_prev, caches, prm, n_seq):
    prompt = caches is None
    rows = x2d.shape[0]
    tm = min(rows, ROW_TILE)
    steps = rows // tm
    q, k, v, kv32, p, gates, xn = _inproj_call(
        x2d, mod, prm["lnig"], prm["lnib"], prm["win"], steps, tm, prompt)
    if prompt:
        att = _attn_prompt_call(q, k, v, prm["bias"])
    else:
        att = _attn_sample_call(q, k, v, *caches, prm["bias"])
    ts = min(rows // n_seq, ROW_TILE)
    rw, shift, state = _wkv_call(p, shift0, state0, prm, n_seq, rows // n_seq // ts, ts)
    y, conv = _mixffn_call(xn, att, rw, gates, mod, conv_prev, prm, steps, tm)
    return y, kv32, state, shift, conv


def kernel(x_prompt, x_sample, cache_attn_k, cache_attn_v, state_rwkv, state_shift, state_conv,
           c_prompt, c_sample, ln_in_g, ln_in_b, w_ada, b_ada, w_in, attn_rel_bias,
           rwkv_mu, rwkv_w0, rwkv_w_up, rwkv_a0, rwkv_a_up, rwkv_g_up, rwkv_k_k, rwkv_k_a,
           rwkv_r_k, rwkv_gn_g, rwkv_gn_b, w_branch_attn, w_branch_rwkv, w_out,
           ln1_g, ln1_b, ln2_g, ln2_b, w_ffn_up, ffn_conv_w, ffn_conv_b, w_ffn_down):
    bp, sp, _ = x_prompt.shape
    bs, ss, _ = x_sample.shape
    assert bp == 1 and ss == CHUNK and w_ada.shape[0] == DEPTH
    assert sp % ROW_TILE == 0 and bs * ss <= ROW_TILE and cache_attn_k.shape[2] == ATT_REACH

    row = lambda a: a.reshape(1, -1)
    wwa = jnp.zeros((LORA_W + LORA_A, 2 * HEADS_DIM), F32)
    wwa = wwa.at[:LORA_W, :HEADS_DIM].set(rwkv_w_up[0]).at[LORA_W:, HEADS_DIM:].set(rwkv_a_up[0])
    head_id = jnp.arange(GROUP) // HEAD
    prm = dict(
        lnig=row(ln_in_g), lnib=row(ln_in_b),
        ln1g=row(ln1_g[0]), ln1b=row(ln1_b[0]), ln2g=row(ln2_g[0]), ln2b=row(ln2_b[0]),
        win=w_in[0].astype(BF16), bias=_pair_bias(attn_rel_bias[0]),
        mu=row(rwkv_mu[0]), w0=row(rwkv_w0[0]), wwa=wwa.astype(BF16), a0=row(rwkv_a0[0]),
        gup=rwkv_g_up[0].astype(BF16), kk=row(rwkv_k_k[0]), ka=row(rwkv_k_a[0]),
        rk=row(rwkv_r_k[0]), gng=row(rwkv_gn_g[0]), gnb=row(rwkv_gn_b[0]),
        ones=(head_id[:, None] == head_id[None, :]).astype(BF16),
        ltri={WKV_PIECE: _chunk_ltri(WKV_PIECE), CHUNK: _chunk_ltri(CHUNK)},
        wa=w_branch_attn[0].astype(BF16), wr=w_branch_rwkv[0].astype(BF16),
        wo=w_out[0].astype(BF16), wup=w_ffn_up[0].astype(BF16),
        cw=ffn_conv_w[0], cb=row(ffn_conv_b[0]), wdn=w_ffn_down[0].astype(BF16),
    )

    n_c = bp + bs
    c_all = jnp.concatenate([c_prompt, c_sample, jnp.zeros((16 - n_c, D_MODEL), F32)], axis=0)
    mod = _mod_call(c_all, w_ada[0], row(b_ada[0])).reshape(16, 6, D_MODEL)

    y_p, kv_p, st_p, sh_p, cv_p = _trunk(
        x_prompt.reshape(sp, D_MODEL), mod[0:bp],
        jnp.zeros((bp, 1, RW_COLS), F32), jnp.zeros((bp, HEAD, HEADS_DIM), F32),
        jnp.zeros((bp, CONV_W - 1, D_FF), F32), None, prm, n_seq=bp)

    caches = (cache_attn_k[0].reshape(bs, ATT_REACH, HEADS_DIM),
              cache_attn_v[0].reshape(bs, ATT_REACH, HEADS_DIM))
    st0 = jnp.transpose(state_rwkv[0], (0, 3, 1, 2)).reshape(bs, HEAD, HEADS_DIM)
    y_s, kv_s, st_s, sh_s, cv_s = _trunk(
        x_sample.reshape(bs * ss, D_MODEL), mod[bp:n_c],
        state_shift[0], st0, state_conv[0], caches, prm, n_seq=bs)

    def state_out(st, b):
        return jnp.transpose(st.reshape(b, HEAD, N_HEADS, HEAD), (0, 2, 3, 1))[None]

    hs = (N_HEADS, HEAD)
    return (
        y_p.reshape(bp, sp, D_MODEL),
        y_s.reshape(bs, ss, D_MODEL),
        kv_p[:, :HEADS_DIM].reshape(1, bp, ATT_REACH, *hs),
        kv_p[:, HEADS_DIM:].reshape(1, bp, ATT_REACH, *hs),
        kv_s[:, :HEADS_DIM].reshape(1, bs, ss, *hs),
        kv_s[:, HEADS_DIM:].reshape(1, bs, ss, *hs),
        state_out(st_p, bp),
        state_out(st_s, bs),
        sh_p[None],
        sh_s[None],
        cv_p[None],
        cv_s[None],
    )
```

```python
import functools

import jax
import jax.numpy as jnp
from jax import lax
from jax.experimental import pallas as pl
from jax.experimental.pallas import tpu as pltpu

F32 = jnp.float32
BF16 = jnp.bfloat16

D_MODEL = 1024
CHUNK = 64
ATT_REACH = 512
BAND = ATT_REACH + CHUNK
N_HEADS = 8
HEAD = 64
HEADS_DIM = N_HEADS * HEAD
REL_CLIP = 128
LORA_W = 64
LORA_A = 64
LORA_G = 128
ATT_COLS = 3 * HEADS_DIM
RW_COLS = 3 * HEADS_DIM + LORA_W + LORA_A + LORA_G
GATE_COLS = 2 * D_MODEL
D_FF = 2816
CONV_W = 3
LN_EPS = 1e-5
GN_EPS = 64e-5
DEPTH = 1
ALPHA = (2 * DEPTH) ** 0.25
LOG2E = 1.4426950408889634

GROUP = 256
ROW_TILE = 512
WKV_PIECE = 256
FFN_ROWS = 512
VMEM_LIMIT = 56 * 1024 * 1024


def _const_spec(shape):
    nd = len(shape)
    return pl.BlockSpec(shape, lambda *_: (0,) * nd, pipeline_mode=pl.Buffered(1))


def _layer_norm(x, g, b):
    mu = jnp.mean(x, axis=-1, keepdims=True)
    xc = x - mu
    var = jnp.mean(xc * xc, axis=-1, keepdims=True)
    return xc * lax.rsqrt(var + LN_EPS) * g + b


def _sigmoid(x):
    return 1.0 / (1.0 + jnp.exp(-x))


def _split3(x):
    hi = x.astype(BF16)
    r1 = x - hi.astype(F32)
    mid = r1.astype(BF16)
    lo = (r1 - mid.astype(F32)).astype(BF16)
    return hi, mid, lo


def _dot(a, b):
    return jnp.dot(a, b, preferred_element_type=F32)


def _dot_nt(a, b):
    return lax.dot_general(a, b, (((1,), (1,)), ((), ())), preferred_element_type=F32)


def _dot_tn(a, b):
    return lax.dot_general(a, b, (((0,), (0,)), ((), ())), preferred_element_type=F32)


def _mod_kernel(c_ref, w_ref, b_ref, o_ref):
    c = c_ref[...]
    s = (c * _sigmoid(c)).astype(BF16)
    o_ref[...] = _dot(s, w_ref[...].astype(BF16)) + b_ref[...]


def _mod_call(c_all, w_ada, b_ada):
    n = c_all.shape[0]
    nblk = 6
    return pl.pallas_call(
        _mod_kernel,
        grid=(nblk,),
        in_specs=[
            pl.BlockSpec((n, D_MODEL), lambda i: (0, 0)),
            pl.BlockSpec((D_MODEL, D_MODEL), lambda i: (0, i)),
            pl.BlockSpec((1, D_MODEL), lambda i: (0, i)),
        ],
        out_specs=pl.BlockSpec((n, D_MODEL), lambda i: (0, i)),
        out_shape=jax.ShapeDtypeStruct((n, 6 * D_MODEL), F32),
        compiler_params=pltpu.CompilerParams(dimension_semantics=("arbitrary",)),
        name="mod",
    )(c_all, w_ada, b_ada)


def _mod_row(mod_ref, idx, tm):
    groups, _, d = mod_ref.shape
    m = mod_ref[:, idx:idx + 1, :]
    if groups == 1:
        return m[0]
    return jnp.broadcast_to(m, (groups, tm // groups, d)).reshape(tm, d)


def _inproj_kernel(x_ref, mod_ref, lng_ref, lnb_ref, w_ref,
                   q_ref, k_ref, v_ref, kv_ref, p_ref, g_ref, xn_ref, *, lead):
    tm = x_ref.shape[0]
    j = pl.program_id(0)

    @pl.when(j < lead)
    def _():
        k_ref[...] = jnp.zeros_like(k_ref)
        v_ref[...] = jnp.zeros_like(v_ref)

    def half_tile(rs):
        def mrow(idx):
            m = _mod_row(mod_ref, idx, tm)
            return m if m.shape[0] == 1 else m[rs]

        xn = _layer_norm(x_ref[rs, :], lng_ref[...], lnb_ref[...])
        xn_ref[rs, :] = xn
        hb = (xn * (1.0 + mrow(1)) + mrow(0)).astype(BF16)

        def seg(a, b):
            return _dot(hb, w_ref[:, a:b])

        q_ref[rs, :] = (seg(0, HEADS_DIM) * (HEAD ** -0.5 * LOG2E)).astype(BF16)
        k = seg(HEADS_DIM, 2 * HEADS_DIM)
        k_ref[rs, :] = k.astype(BF16)
        kv_ref[rs, 0:HEADS_DIM] = k
        v = seg(2 * HEADS_DIM, 3 * HEADS_DIM)
        v_ref[rs, :] = v.astype(BF16)
        kv_ref[rs, HEADS_DIM:2 * HEADS_DIM] = v
        for c in range(0, RW_COLS, 256):
            p_ref[rs, c:c + 256] = seg(ATT_COLS + c, ATT_COLS + c + 256)
        g0 = ATT_COLS + RW_COLS
        for c in range(0, GATE_COLS, 512):
            g_ref[rs, c:c + 512] = _sigmoid(seg(g0 + c, g0 + c + 512))

    @pl.when(j >= lead)
    def _():
        half_tile(slice(0, tm // 2))
        half_tile(slice(tm // 2, tm))


def _inproj_call(x2d, mod, ln_g, ln_b, w_in_b, steps, tm, prompt):
    rows = x2d.shape[0]
    in_cols = w_in_b.shape[1]
    groups = mod.shape[0]
    if prompt:
        lead = ATT_REACH // tm
        kv_rows = ATT_REACH
        row_map = lambda j: (jnp.maximum(j - lead, 0), 0)
        kv_map = lambda j: (jnp.maximum(j - steps, 0), 0)
        ext_map = lambda j: (j, 0)
    else:
        lead = 0
        kv_rows = rows
        row_map = kv_map = ext_map = lambda j: (j, 0)
    return pl.pallas_call(
        functools.partial(_inproj_kernel, lead=lead),
        grid=(steps + lead,),
        in_specs=[
            pl.BlockSpec((tm, D_MODEL), row_map),
            _const_spec((groups, 6, D_MODEL)),
            _const_spec((1, D_MODEL)),
            _const_spec((1, D_MODEL)),
            _const_spec((D_MODEL, in_cols)),
        ],
        out_specs=[
            pl.BlockSpec((tm, HEADS_DIM), row_map),
            pl.BlockSpec((tm, HEADS_DIM), ext_map),
            pl.BlockSpec((tm, HEADS_DIM), ext_map),
            pl.BlockSpec((tm, 2 * HEADS_DIM), kv_map),
            pl.BlockSpec((tm, RW_COLS), row_map),
            pl.BlockSpec((tm, GATE_COLS), row_map),
            pl.BlockSpec((tm, D_MODEL), row_map),
        ],
        out_shape=[
            jax.ShapeDtypeStruct((rows, HEADS_DIM), BF16),
            jax.ShapeDtypeStruct((rows + lead * tm, HEADS_DIM), BF16),
            jax.ShapeDtypeStruct((rows + lead * tm, HEADS_DIM), BF16),
            jax.ShapeDtypeStruct((kv_rows, 2 * HEADS_DIM), F32),
            jax.ShapeDtypeStruct((rows, RW_COLS), F32),
            jax.ShapeDtypeStruct((rows, GATE_COLS), F32),
            jax.ShapeDtypeStruct((rows, D_MODEL), F32),
        ],
        compiler_params=pltpu.CompilerParams(
            dimension_semantics=("arbitrary",), vmem_limit_bytes=VMEM_LIMIT),
        name="inproj",
    )(x2d, mod, ln_g, ln_b, w_in_b)


def _attn_chunks(chunks, bias_ref):
    lane = lax.broadcasted_iota(jnp.int32, (CHUNK, 128), 1)
    first = lane < HEAD
    pairs = [slice(pr * 128, (pr + 1) * 128) for pr in range(N_HEADS // 2)]
    scores = []
    for qc, kb, _, _ in chunks:
        for sl in pairs:
            q2 = qc[:, sl].astype(F32)
            qs = jnp.concatenate([jnp.where(first, q2, 0.0), jnp.where(first, 0.0, q2)],
                                 axis=0).astype(BF16)
            scores.append(_dot_nt(qs, kb(sl)))
    probs, sums = [], []
    for i, s in enumerate(scores):
        thr = chunks[i // len(pairs)][3]
        s = s + bias_ref[i % len(pairs)]
        if thr is not None:
            col = lax.broadcasted_iota(jnp.int32, s.shape, 1)
            s = jnp.where(col >= thr, s, -jnp.inf)
        e = jnp.exp2(s - jnp.max(s, axis=1, keepdims=True))
        sums.append(jnp.sum(e, axis=1, keepdims=True))
        probs.append(e.astype(BF16))
    outs = []
    for ci, (_, _, vb, _) in enumerate(chunks):
        cols = []
        for pi, sl in enumerate(pairs):
            i = ci * len(pairs) + pi
            o = _dot(probs[i], vb(sl)) / sums[i]
            cols.append(jnp.where(first, o[0:CHUNK], o[CHUNK:2 * CHUNK]))
        outs.append(jnp.concatenate(cols, axis=1).astype(BF16))
    return outs


def _attn_prompt_kernel(q_ref, ka_ref, kb_ref, va_ref, vb_ref, bias_ref, o_ref, kbuf, vbuf,
                        *, chunks):
    tq = chunks * CHUNK
    kbuf[0:tq, :] = ka_ref[...]
    kbuf[tq:2 * tq, :] = kb_ref[...]
    vbuf[0:tq, :] = va_ref[...]
    vbuf[tq:2 * tq, :] = vb_ref[...]
    s = pl.program_id(0)

    per = 2

    def run(masked):
        for i in range(chunks // per):
            units, starts = [], []
            for k in range(per):
                g = i * per + k
                r0 = g * CHUNK
                thr = ATT_REACH - (s * chunks + g) * CHUNK if masked else None
                band = slice(r0, r0 + BAND)
                units.append((q_ref[r0:r0 + CHUNK, :],
                              lambda sl, band=band: kbuf[band, sl],
                              lambda sl, band=band: vbuf[band, sl], thr))
                starts.append(r0)
            for r0, o in zip(starts, _attn_chunks(units, bias_ref)):
                o_ref[r0:r0 + CHUNK, :] = o

    assert chunks * CHUNK >= ATT_REACH
    pl.when(s == 0)(functools.partial(run, True))
    pl.when(s != 0)(functools.partial(run, False))


def _attn_prompt_call(q, kext, vext, bias):
    rows = q.shape[0]
    tq = ATT_REACH
    blk = lambda off: pl.BlockSpec((tq, HEADS_DIM), lambda s: (s + off, 0))
    return pl.pallas_call(
        functools.partial(_attn_prompt_kernel, chunks=tq // CHUNK),
        grid=(rows // tq,),
        in_specs=[blk(0), blk(0), blk(1), blk(0), blk(1), _const_spec(bias.shape)],
        out_specs=blk(0),
        out_shape=jax.ShapeDtypeStruct((rows, HEADS_DIM), BF16),
        scratch_shapes=[pltpu.VMEM((2 * tq, HEADS_DIM), BF16), pltpu.VMEM((2 * tq, HEADS_DIM), BF16)],
        compiler_params=pltpu.CompilerParams(dimension_semantics=("arbitrary",)),
        name="attn_prompt",
    )(q, kext, kext, vext, vext, bias)


def _attn_sample_kernel(q_ref, k_ref, v_ref, ck_ref, cv_ref, bias_ref, o_ref):
    def band(cache_ref, new_ref):
        return lambda sl: jnp.concatenate([cache_ref[:, sl].astype(BF16), new_ref[:, sl]], axis=0)

    unit = (q_ref[...], band(ck_ref, k_ref), band(cv_ref, v_ref), None)
    o_ref[...] = _attn_chunks([unit], bias_ref)[0]


def _attn_sample_call(q, k, v, cache_k, cache_v, bias):
    nb = cache_k.shape[0]
    new = pl.BlockSpec((CHUNK, HEADS_DIM), lambda b: (b, 0))
    cache = pl.BlockSpec((None, ATT_REACH, HEADS_DIM), lambda b: (b, 0, 0))
    return pl.pallas_call(
        _attn_sample_kernel,
        grid=(nb,),
        in_specs=[new, new, new, cache, cache, _const_spec(bias.shape)],
        out_specs=new,
        out_shape=jax.ShapeDtypeStruct((nb * CHUNK, HEADS_DIM), BF16),
        compiler_params=pltpu.CompilerParams(dimension_semantics=("arbitrary",)),
        name="attn_sample",
    )(q, k, v, cache_k, cache_v, bias)


def _same_head():
    r = lax.broadcasted_iota(jnp.int32, (GROUP, GROUP), 0) // HEAD
    c = lax.broadcasted_iota(jnp.int32, (GROUP, GROUP), 1) // HEAD
    return r == c


def _blk(x, same_head):
    return jnp.where(same_head, jnp.concatenate([x] * 4, axis=0), 0.0).astype(BF16)


def _wkv_prep(units, hooks):
    same_head = _same_head()
    t64 = lax.broadcasted_iota(jnp.int32, (CHUNK, GROUP), 0)
    i64 = lax.broadcasted_iota(jnp.int32, (CHUNK, GROUP), 1) % HEAD
    strict = i64 < t64
    incl = i64 <= t64
    pending = list(hooks)

    def stage_done():
        if pending:
            pending.pop(0)()

    def blk(x):
        return _blk(x, same_head)

    n, a_ak, a_rb, a_rk = [], [], [], []
    for ld in units:
        lhs = jnp.concatenate([ld("at"), ld("rt")], axis=0).astype(BF16)
        rhs = jnp.concatenate([blk(ld("bt")), blk(ld("kt"))], axis=0)
        a_all = _dot_nt(lhs, rhs)
        n.append(jnp.where(strict, a_all[0:CHUNK, 0:GROUP], 0.0))
        a_ak.append(jnp.where(strict, a_all[0:CHUNK, GROUP:], 0.0))
        a_rb.append(jnp.where(incl, a_all[CHUNK:, 0:GROUP], 0.0))
        a_rk.append(jnp.where(incl, a_all[CHUNK:, GROUP:], 0.0))
    stage_done()

    x0 = [_dot(a.astype(BF16), blk(ld("v"))) for a, ld in zip(a_ak, units)]
    npow = [_dot(m.astype(BF16), blk(m)) for m in n]
    t = [jnp.where(i64 == t64, 1.0, 0.0) + m for m in n]
    stage_done()
    for _ in range(4):
        prod = [_dot(jnp.concatenate([a, b], axis=0).astype(BF16), blk(b))
                for a, b in zip(t, npow)]
        t = [a + p[0:CHUNK] for a, p in zip(t, prod)]
        npow = [p[CHUNK:] for p in prod]
        stage_done()
    t = [a + _dot(a.astype(BF16), blk(b)) for a, b in zip(t, npow)]
    stage_done()
    res = [_dot(a.astype(BF16), jnp.concatenate([blk(ld("at")), blk(x)], axis=1))
           for a, x, ld in zip(t, x0, units)]
    stage_done()
    while pending:
        stage_done()
    return [(r[:, 0:GROUP], r[:, GROUP:], b, k) for r, b, k in zip(res, a_rb, a_rk)]


def _wkv_step_stages(units, states, out):
    same_head = _same_head()
    held = {}

    def blk(x):
        return _blk(x, same_head)

    def first():
        held["sblk"] = [blk(st) for st in states]
        held["u"] = [_dot(ld("ah").astype(BF16), sb) + ld("vh")
                     for ld, sb in zip(units, held["sblk"])]

    def second():
        for i, ld in enumerate(units):
            u, v = held["u"][i], ld("v")
            y = _dot(jnp.concatenate([ld("rt"), ld("arb"), ld("ark")], axis=1).astype(BF16),
                     jnp.concatenate([held["sblk"][i], blk(u), blk(v)], axis=0))
            g = _dot_tn(jnp.concatenate([ld("btd"), ld("ktd")], axis=0).astype(BF16),
                        jnp.concatenate([u, v], axis=0).astype(BF16))
            g = jnp.where(same_head, g, 0.0)
            delta = g[0:64] + g[64:128] + g[128:192] + g[192:256]
            out.append(y)
            states[i] = states[i] * ld("fdec") + delta

    return [first, second]


def _wkv_kernel(p_ref, pn_ref, shift0_ref, state0_ref, mu_ref, w0_ref, wwa_ref, a0_ref, gup_ref,
                kkw_ref, kaw_ref, rkw_ref, gng_ref, gnb_ref, ones_ref, ltri_ref,
                out_ref, shift_out_ref, state_out_ref,
                st_scr, rt_scr, at_scr, kt_scr, bt_scr, v_scr, btd_scr, ktd_scr,
                fdec_scr, ah_scr, vh_scr, arb_scr, ark_scr, g_scr, bon_scr, *, ts, steps):
    j = pl.program_id(1)

    @pl.when(j == 0)
    def _():
        st_scr[...] = state0_ref[...]

    def bdsum(x, terms):
        ones = ones_ref[...]
        halves = []
        for c0 in range(0, HEADS_DIM, GROUP):
            parts = _split3(x[:, c0:c0 + GROUP])[:terms]
            acc = _dot(parts[0], ones)
            for part in parts[1:]:
                acc = acc + _dot(part, ones)
            halves.append(acc)
        return jnp.concatenate(halves, axis=1)

    pq = ltri_ref.shape[0]
    per = pq // CHUNK
    n_piece = ts // pq

    def pre_stages(q, upcoming=False):
        r0 = q * pq
        rows = slice(r0, r0 + pq)
        src_ref = pn_ref if upcoming else p_ref
        held = {}

        def shifted(c0, w):
            cols = slice(c0, c0 + w)
            if upcoming:
                prev_row = p_ref[ts - 8:ts, cols][7:8]
            elif q == 0:
                prev_row = shift0_ref[:, cols]
            else:
                prev_row = p_ref[r0 - 8:r0, cols][7:8]
            top = lax.broadcasted_iota(jnp.int32, (pq, 1), 0) == 0
            pb = src_ref[rows, cols]
            prev = jnp.where(top, prev_row, pltpu.roll(pb, 1, 0))
            return pb + (prev - pb) * mu_ref[:, cols]

        def s_lora():
            lora = shifted(3 * HEADS_DIM, 256)
            lwla = lora[:, 0:128]
            lane = lax.broadcasted_iota(jnp.int32, (pq, 128), 1)
            held["raw"] = _dot(jnp.where(lane < LORA_W, jnp.tanh(lwla), lwla).astype(BF16),
                               wwa_ref[...])
            g_scr[rows, :] = _dot(_sigmoid(lora[:, 128:256]).astype(BF16), gup_ref[...])
            held["k"] = shifted(HEADS_DIM, HEADS_DIM)

        def s_decay():
            w_pre = w0_ref[...] + held["raw"][:, 0:HEADS_DIM]
            softplus = jnp.maximum(-w_pre, 0.0) + jnp.log(1.0 + jnp.exp(-jnp.abs(w_pre)))
            lw = -jnp.exp(-softplus - 0.5)
            ltri = ltri_ref[...]
            hi, mid, _ = _split3(lw)
            held["cum"] = _dot(ltri, hi) + _dot(ltri, mid)
            held["lw"] = lw

        def s_keys():
            k = held["k"]
            a = _sigmoid(a0_ref[...] + held["raw"][:, HEADS_DIM:])
            kk = k * kkw_ref[...]
            held["nrm"] = bdsum(kk * kk, 1)
            held.update(a=a, kk=kk, k2=k * (1.0 + (a - 1.0) * kaw_ref[...]))

        def s_bonus():
            r = shifted(0, HEADS_DIM)
            held["bon"] = bdsum(r * held["k2"] * rkw_ref[...], 1)
            rt_scr[rows, :] = r * jnp.exp(held["cum"])

        def s_norm():
            cum = held["cum"]
            kk = held["kk"] / jnp.maximum(jnp.sqrt(held["nrm"]), 1e-12)
            cum3 = cum.reshape(per, CHUNK, HEADS_DIM)
            cend = jnp.broadcast_to(cum3[:, CHUNK - 1:CHUNK, :], cum3.shape).reshape(pq, HEADS_DIM)
            diag = (lax.broadcasted_iota(jnp.int32, (pq, HEADS_DIM), 0) % CHUNK
                    == lax.broadcasted_iota(jnp.int32, (pq, HEADS_DIM), 1) % HEAD)
            held["fdec"] = bdsum(jnp.where(diag, jnp.exp(cend), 0.0), 2)
            at_scr[rows, :] = -kk * jnp.exp(cum - held["lw"])
            held.update(kk=kk, cend=cend)

        def s_inv():
            e_inv = jnp.exp(-held["cum"])
            kt_scr[rows, :] = held["k2"] * e_inv
            bt_scr[rows, :] = held["kk"] * held["a"] * e_inv

        def s_rel():
            e_rel = jnp.exp(held["cend"] - held["cum"])
            ktd_scr[rows, :] = held["k2"] * e_rel
            btd_scr[rows, :] = held["kk"] * held["a"] * e_rel
            fdec_scr[rows, :] = held["fdec"]

        def s_value():
            v = shifted(2 * HEADS_DIM, HEADS_DIM)
            v_scr[rows, :] = v
            bon_scr[rows, :] = held["bon"] * v

        return [s_lora, s_decay, s_keys, s_bonus, s_norm, s_inv, s_rel, s_value]

    def chunk_rows(c):
        return slice(c * CHUNK, (c + 1) * CHUNK)

    groups = [slice(g0, g0 + GROUP) for g0 in range(0, HEADS_DIM, GROUP)]

    named = dict(at=at_scr, rt=rt_scr, kt=kt_scr, bt=bt_scr, v=v_scr, btd=btd_scr, ktd=ktd_scr,
                 fdec=fdec_scr, ah=ah_scr, vh=vh_scr, arb=arb_scr, ark=ark_scr)
    prep_outs = (ah_scr, vh_scr, arb_scr, ark_scr)

    def loader(c, cols):
        rows = chunk_rows(c)
        return lambda name: named[name][rows, cols]

    def run(step_piece, prep_piece, extra):
        step_chunks = [] if step_piece is None else [step_piece * per + i for i in range(per)]
        prep_chunks = [] if prep_piece is None else [prep_piece * per + i for i in range(per)]
        states = [st_scr[:, cols] for cols in groups] if step_chunks else []
        stages, ys = [], []
        for c in step_chunks:
            out = []
            ys.append(out)
            stages += _wkv_step_stages([loader(c, cols) for cols in groups], states, out)
        if step_chunks:
            rows = slice(step_piece * pq, (step_piece + 1) * pq)
            gn = {}

            def gn_mean():
                gn["y"] = jnp.concatenate([jnp.concatenate(out, axis=1) for out in ys], axis=0)
                gn["sum"] = bdsum(gn["y"], 1)

            def gn_var():
                gn["d"] = gn["y"] - gn["sum"] * (1.0 / HEAD)
                gn["sq"] = bdsum(gn["d"] * gn["d"], 1)

            def gn_out():
                var = gn["sq"] * (1.0 / HEAD)
                yn = (gn["d"] * lax.rsqrt(var + GN_EPS) * gng_ref[...] + gnb_ref[...]
                      + bon_scr[rows, :])
                out_ref[rows, :] = (yn * g_scr[rows, :]).astype(BF16)

            stages += [gn_mean, gn_var, gn_out]
        hooks = []
        for i in range(max(len(stages), len(extra))):
            both = stages[i:i + 1] + extra[i:i + 1]
            hooks.append(lambda both=both: [f() for f in both])
        units = [(c, cols) for c in prep_chunks for cols in groups]
        prep_out = _wkv_prep([loader(c, cols) for c, cols in units], hooks)
        for cols, st in zip(groups, states):
            st_scr[:, cols] = st
        for (c, cols), outs in zip(units, prep_out):
            for ref, val in zip(prep_outs, outs):
                ref[chunk_rows(c), cols] = val

    look_ahead = steps > 1
    ahead_pass = max(2, n_piece - 1)
    assert ahead_pass <= n_piece or not look_ahead

    def first_piece():
        for stage in pre_stages(0):
            stage()

    if look_ahead:
        pl.when(j == 0)(first_piece)
    else:
        first_piece()
    for piece in range(n_piece + 1):
        if piece + 1 < n_piece:
            extra = pre_stages(piece + 1)
        elif piece == ahead_pass and look_ahead:
            extra = pre_stages(0, upcoming=True)
        else:
            extra = []
        run(piece - 1 if piece >= 1 else None, piece if piece < n_piece else None, extra)

    @pl.when(j == steps - 1)
    def _():
        shift_out_ref[...] = p_ref[ts - 1:ts, :]
        state_out_ref[...] = st_scr[...]


def _wkv_call(p2d, shift0, state0, prm, n_seq, steps, ts):
    rows = p2d.shape[0]
    row_map = lambda s, j: (s * steps + j, 0)
    next_map = lambda s, j: (s * steps + jnp.minimum(j + 1, steps - 1), 0)
    seq3 = lambda s, j: (s, 0, 0)
    vec = _const_spec((1, HEADS_DIM))
    big = pltpu.VMEM((ts, HEADS_DIM), F32)
    ltri = prm["ltri"][min(ts, WKV_PIECE)]
    return pl.pallas_call(
        functools.partial(_wkv_kernel, ts=ts, steps=steps),
        grid=(n_seq, steps),
        in_specs=[
            pl.BlockSpec((ts, RW_COLS), row_map),
            pl.BlockSpec((ts, RW_COLS), next_map),
            pl.BlockSpec((None, 1, RW_COLS), seq3),
            pl.BlockSpec((None, HEAD, HEADS_DIM), seq3),
            _const_spec((1, RW_COLS)),
            vec,
            _const_spec((128, 2 * HEADS_DIM)),
            vec,
            _const_spec((LORA_G, HEADS_DIM)),
            vec, vec, vec, vec, vec,
            _const_spec((GROUP, GROUP)),
            _const_spec(ltri.shape),
        ],
        out_specs=[
            pl.BlockSpec((ts, HEADS_DIM), row_map),
            pl.BlockSpec((None, 1, RW_COLS), seq3),
            pl.BlockSpec((None, HEAD, HEADS_DIM), seq3),
        ],
        out_shape=[
            jax.ShapeDtypeStruct((rows, HEADS_DIM), BF16),
            jax.ShapeDtypeStruct((n_seq, 1, RW_COLS), F32),
            jax.ShapeDtypeStruct((n_seq, HEAD, HEADS_DIM), F32),
        ],
        scratch_shapes=[
            pltpu.VMEM((HEAD, HEADS_DIM), F32),
        ] + [big] * 14,
        compiler_params=pltpu.CompilerParams(
            dimension_semantics=("arbitrary", "arbitrary"), vmem_limit_bytes=VMEM_LIMIT),
        name="wkv",
    )(p2d, p2d, shift0, state0, prm["mu"], prm["w0"], prm["wwa"], prm["a0"], prm["gup"],
      prm["kk"], prm["ka"], prm["rk"], prm["gng"], prm["gnb"], prm["ones"], ltri)


def _mixffn_kernel(xn_ref, att_ref, rw_ref, gate_ref, mod_ref, convp_ref,
                   ln1g_ref, ln1b_ref, ln2g_ref, ln2b_ref,
                   wa_ref, wr_ref, wo_ref, wup_ref, cw_ref, cb_ref, wdn_ref,
                   y_ref, convo_ref, carry_scr, yb_scr, *, tm, steps):
    j = pl.program_id(0)
    groups = mod_ref.shape[0]
    assert groups == 1 or steps == 1

    if groups == 1:
        @pl.when(j == 0)
        def _():
            carry_scr[8 - (CONV_W - 1):8, :] = convp_ref[0]

    def mod(idx):
        return _mod_row(mod_ref, idx, tm)

    half = tm // 2
    subs = [slice(0, half), slice(half, tm)]

    def modr(idx, rs):
        m = mod(idx)
        return m if m.shape[0] == 1 else m[rs]

    ma = [_dot(att_ref[rs, :], wa_ref[...]) for rs in subs]
    mr = [_dot(rw_ref[rs, :], wr_ref[...]) for rs in subs]
    merged = [(gate_ref[rs, 0:D_MODEL] * a + gate_ref[rs, D_MODEL:] * r).astype(BF16)
              for rs, a, r in zip(subs, ma, mr)]
    mix = [_dot(m, wo_ref[...]) for m in merged]
    x1 = [_layer_norm(ALPHA * xn_ref[rs, :] + (1.0 + modr(2, rs)) * m, ln1g_ref[...], ln1b_ref[...])
          for rs, m in zip(subs, mix)]
    h2 = jnp.concatenate([(a * (1.0 + modr(4, rs)) + modr(3, rs)).astype(BF16)
                          for rs, a in zip(subs, x1)], axis=0)

    cw_blk = 256
    glen = tm // groups
    rb = FFN_ROWS if groups == 1 else tm
    row8 = lax.broadcasted_iota(jnp.int32, (8, cw_blk), 0)
    grow = lax.broadcasted_iota(jnp.int32, (tm, cw_blk), 0) % glen

    def up(blk):
        c, r0 = blk
        h = h2[r0:r0 + rb]
        return (_dot(h, wup_ref[:, c:c + cw_blk]),
                _dot(h, wup_ref[:, D_FF + c:D_FF + c + cw_blk]))

    blocks = [(c, r0) for c in range(0, D_FF, cw_blk) for r0 in range(0, tm, rb)]
    for c, r0 in blocks:
        uc, uv = up((c, r0))
        cs = slice(c, c + cw_blk)
        r1 = pltpu.roll(uc, 1, 0)
        r2 = pltpu.roll(uc, 2, 0)
        if groups == 1:
            if r0 == 0:
                c6 = carry_scr[6:7, cs]
                c7 = carry_scr[7:8, cs]
            s1 = jnp.concatenate([jnp.where(row8 == 0, c7, r1[0:8]), r1[8:]], axis=0)
            s2 = jnp.concatenate(
                [jnp.where(row8 == 0, c6, jnp.where(row8 == 1, c7, r2[0:8])), r2[8:]], axis=0)
            c6 = uc[rb - 2:rb - 1, :]
            c7 = uc[rb - 1:rb, :]
            if r0 + rb == tm:
                carry_scr[:, cs] = uc[rb - 8:rb, :]
            tail = uc[rb - (CONV_W - 1):rb, :][None]
        else:
            hist = jnp.broadcast_to(convp_ref[:, :, cs][:, :, None, :],
                                    (groups, CONV_W - 1, glen, cw_blk))
            c6 = hist[:, 0].reshape(tm, cw_blk)
            c7 = hist[:, 1].reshape(tm, cw_blk)
            s1 = jnp.where(grow == 0, c7, r1)
            s2 = jnp.where(grow == 0, c6, jnp.where(grow == 1, c7, r2))
            tail = uc.reshape(groups, glen, cw_blk)[:, glen - (CONV_W - 1):, :]
        conv = cb_ref[:, cs] + s2 * cw_ref[0:1, cs] + s1 * cw_ref[1:2, cs] + uc * cw_ref[2:3, cs]
        yb_scr[r0:r0 + rb, cs] = (conv * _sigmoid(conv) * uv).astype(BF16)

        if r0 + rb == tm:
            convo_ref[:, :, cs] = tail

    ff = [_dot(yb_scr[rs, :], wdn_ref[...]) for rs in subs]
    for rs, a, f in zip(subs, x1, ff):
        y_ref[rs, :] = _layer_norm(ALPHA * a + (1.0 + modr(5, rs)) * f, ln2g_ref[...], ln2b_ref[...])


def _mixffn_call(xn2d, att, rw, gates, mod, conv_prev, prm, steps, tm):
    rows = xn2d.shape[0]
    groups = mod.shape[0]
    row_map = lambda j: (j, 0)
    vec = _const_spec((1, D_MODEL))
    return pl.pallas_call(
        functools.partial(_mixffn_kernel, tm=tm, steps=steps),
        grid=(steps,),
        in_specs=[
            pl.BlockSpec((tm, D_MODEL), row_map),
            pl.BlockSpec((tm, HEADS_DIM), row_map),
            pl.BlockSpec((tm, HEADS_DIM), row_map),
            pl.BlockSpec((tm, GATE_COLS), row_map),
            _const_spec((groups, 6, D_MODEL)),
            _const_spec((groups, CONV_W - 1, D_FF)),
            vec, vec, vec, vec,
            _const_spec((HEADS_DIM, D_MODEL)),
            _const_spec((HEADS_DIM, D_MODEL)),
            _const_spec((D_MODEL, D_MODEL)),
            _const_spec((D_MODEL, 2 * D_FF)),
            _const_spec((CONV_W, D_FF)),
            _const_spec((1, D_FF)),
            _const_spec((D_FF, D_MODEL)),
        ],
        out_specs=[
            pl.BlockSpec((tm, D_MODEL), row_map),
            pl.BlockSpec((groups, CONV_W - 1, D_FF), lambda j: (0, 0, 0)),
        ],
        out_shape=[
            jax.ShapeDtypeStruct((rows, D_MODEL), F32),
            jax.ShapeDtypeStruct((groups, CONV_W - 1, D_FF), F32),
        ],
        scratch_shapes=[pltpu.VMEM((8, D_FF), F32), pltpu.VMEM((tm, D_FF), BF16)],
        compiler_params=pltpu.CompilerParams(
            dimension_semantics=("arbitrary",), vmem_limit_bytes=VMEM_LIMIT),
        name="mixffn",
    )(xn2d, att, rw, gates, mod, conv_prev,
      prm["ln1g"], prm["ln1b"], prm["ln2g"], prm["ln2b"],
      prm["wa"], prm["wr"], prm["wo"], prm["wup"], prm["cw"], prm["cb"], prm["wdn"])


def _pair_bias(table):
    assert CHUNK - 1 <= REL_CLIP
    top = ATT_REACH + CHUNK - 1
    n_far = top - REL_CLIP + 1
    far = jnp.broadcast_to(table[:, 2 * REL_CLIP:], (N_HEADS, n_far))
    lo_idx = top - (BAND + CHUNK - 2) + REL_CLIP
    near = table[:, lo_idx:2 * REL_CLIP][:, ::-1]
    ext = jnp.concatenate([far, near], axis=1).astype(F32) * LOG2E
    n_ext = BAND + CHUNK - 1
    period = jnp.concatenate([ext, jnp.zeros((N_HEADS, 1), F32)], axis=1)
    skew = jnp.tile(period, (1, CHUNK))[:, :CHUNK * n_ext].reshape(N_HEADS, CHUNK, n_ext)
    bias = skew[:, :, CHUNK - 1:CHUNK - 1 + BAND]
    return bias.reshape(N_HEADS // 2, 2 * CHUNK, BAND)


def _chunk_ltri(ts):
    t = jnp.arange(ts)
    return ((t[:, None] // CHUNK == t[None, :] // CHUNK) & (t[None, :] <= t[:, None])).astype(BF16)


def _trunk(x2d, mod, shift0, state0, conv_prev, caches, prm, n_seq):
    prompt = caches is None
    rows = x2d.shape[0]
    tm = min(rows, ROW_TILE)
    steps = rows // tm
    q, k, v, kv32, p, gates, xn = _inproj_call(
        x2d, mod, prm["lnig"], prm["lnib"], prm["win"], steps, tm, prompt)
    if prompt:
        att = _attn_prompt_call(q, k, v, prm["bias"])
    else:
        att = _attn_sample_call(q, k, v, *caches, prm["bias"])
    ts = min(rows // n_seq, ROW_TILE)
    rw, shift, state = _wkv_call(p, shift0, state0, prm, n_seq, rows // n_seq // ts, ts)
    y, conv = _mixffn_call(xn, att, rw, gates, mod, conv_prev, prm, steps, tm)
    return y, kv32, state, shift, conv


def kernel(x_prompt, x_sample, cache_attn_k, cache_attn_v, state_rwkv, state_shift, state_conv,
           c_prompt, c_sample, ln_in_g, ln_in_b, w_ada, b_ada, w_in, attn_rel_bias,
           rwkv_mu, rwkv_w0, rwkv_w_up, rwkv_a0, rwkv_a_up, rwkv_g_up, rwkv_k_k, rwkv_k_a,
           rwkv_r_k, rwkv_gn_g, rwkv_gn_b, w_branch_attn, w_branch_rwkv, w_out,
           ln1_g, ln1_b, ln2_g, ln2_b, w_ffn_up, ffn_conv_w, ffn_conv_b, w_ffn_down):
    bp, sp, _ = x_prompt.shape
    bs, ss, _ = x_sample.shape
    assert bp == 1 and ss == CHUNK and w_ada.shape[0] == DEPTH
    assert sp % ROW_TILE == 0 and bs * ss <= ROW_TILE and cache_attn_k.shape[2] == ATT_REACH

    row = lambda a: a.reshape(1, -1)
    wwa = jnp.zeros((LORA_W + LORA_A, 2 * HEADS_DIM), F32)
    wwa = wwa.at[:LORA_W, :HEADS_DIM].set(rwkv_w_up[0]).at[LORA_W:, HEADS_DIM:].set(rwkv_a_up[0])
    head_id = jnp.arange(GROUP) // HEAD
    prm = dict(
        lnig=row(ln_in_g), lnib=row(ln_in_b),
        ln1g=row(ln1_g[0]), ln1b=row(ln1_b[0]), ln2g=row(ln2_g[0]), ln2b=row(ln2_b[0]),
        win=w_in[0].astype(BF16), bias=_pair_bias(attn_rel_bias[0]),
        mu=row(rwkv_mu[0]), w0=row(rwkv_w0[0]), wwa=wwa.astype(BF16), a0=row(rwkv_a0[0]),
        gup=rwkv_g_up[0].astype(BF16), kk=row(rwkv_k_k[0]), ka=row(rwkv_k_a[0]),
        rk=row(rwkv_r_k[0]), gng=row(rwkv_gn_g[0]), gnb=row(rwkv_gn_b[0]),
        ones=(head_id[:, None] == head_id[None, :]).astype(BF16),
        ltri={WKV_PIECE: _chunk_ltri(WKV_PIECE), CHUNK: _chunk_ltri(CHUNK)},
        wa=w_branch_attn[0].astype(BF16), wr=w_branch_rwkv[0].astype(BF16),
        wo=w_out[0].astype(BF16), wup=w_ffn_up[0].astype(BF16),
        cw=ffn_conv_w[0], cb=row(ffn_conv_b[0]), wdn=w_ffn_down[0].astype(BF16),
    )

    n_c = bp + bs
    c_all = jnp.concatenate([c_prompt, c_sample, jnp.zeros((16 - n_c, D_MODEL), F32)], axis=0)
    mod = _mod_call(c_all, w_ada[0], row(b_ada[0])).reshape(16, 6, D_MODEL)

    y_p, kv_p, st_p, sh_p, cv_p = _trunk(
        x_prompt.reshape(sp, D_MODEL), mod[0:bp],
        jnp.zeros((bp, 1, RW_COLS), F32), jnp.zeros((bp, HEAD, HEADS_DIM), F32),
        jnp.zeros((bp, CONV_W - 1, D_FF), F32), None, prm, n_seq=bp)

    caches = (cache_attn_k[0].reshape(bs, ATT_REACH, HEADS_DIM),
              cache_attn_v[0].reshape(bs, ATT_REACH, HEADS_DIM))
    st0 = jnp.transpose(state_rwkv[0], (0, 3, 1, 2)).reshape(bs, HEAD, HEADS_DIM)
    y_s, kv_s, st_s, sh_s, cv_s = _trunk(
        x_sample.reshape(bs * ss, D_MODEL), mod[bp:n_c],
        state_shift[0], st0, state_conv[0], caches, prm, n_seq=bs)

    def state_out(st, b):
        return jnp.transpose(st.reshape(b, HEAD, N_HEADS, HEAD), (0, 2, 3, 1))[None]

    hs = (N_HEADS, HEAD)
    return (
        y_p.reshape(bp, sp, D_MODEL),
        y_s.reshape(bs, ss, D_MODEL),
        kv_p[:, :HEADS_DIM].reshape(1, bp, ATT_REACH, *hs),
        kv_p[:, HEADS_DIM:].reshape(1, bp, ATT_REACH, *hs),
        kv_s[:, :HEADS_DIM].reshape(1, bs, ss, *hs),
        kv_s[:, HEADS_DIM:].reshape(1, bs, ss, *hs),
        state_out(st_p, bp),
        state_out(st_s, bs),
        sh_p[None],
        sh_s[None],
        cv_p[None],
        cv_s[None],
    )
```

```python
import functools

import jax
import jax.numpy as jnp
from jax import lax
from jax.experimental import pallas as pl
from jax.experimental.pallas import tpu as pltpu

F32 = jnp.float32
BF16 = jnp.bfloat16

D_MODEL = 1024
CHUNK = 64
ATT_REACH = 512
BAND = ATT_REACH + CHUNK
N_HEADS = 8
HEAD = 64
HEADS_DIM = N_HEADS * HEAD
REL_CLIP = 128
LORA_W = 64
LORA_A = 64
LORA_G = 128
ATT_COLS = 3 * HEADS_DIM
RW_COLS = 3 * HEADS_DIM + LORA_W + LORA_A + LORA_G
GATE_COLS = 2 * D_MODEL
D_FF = 2816
CONV_W = 3
LN_EPS = 1e-5
GN_EPS = 64e-5
DEPTH = 1
ALPHA = (2 * DEPTH) ** 0.25
LOG2E = 1.4426950408889634

GROUP = 256
ROW_TILE = 512
WKV_PIECE = 256
FFN_ROWS = 512
VMEM_LIMIT = 56 * 1024 * 1024


def _const_spec(shape):
    nd = len(shape)
    return pl.BlockSpec(shape, lambda *_: (0,) * nd, pipeline_mode=pl.Buffered(1))


def _layer_norm(x, g, b):
    mu = jnp.mean(x, axis=-1, keepdims=True)
    xc = x - mu
    var = jnp.mean(xc * xc, axis=-1, keepdims=True)
    return xc * lax.rsqrt(var + LN_EPS) * g + b


def _sigmoid(x):
    return 1.0 / (1.0 + jnp.exp(-x))


def _split3(x):
    hi = x.astype(BF16)
    r1 = x - hi.astype(F32)
    mid = r1.astype(BF16)
    lo = (r1 - mid.astype(F32)).astype(BF16)
    return hi, mid, lo


def _dot(a, b):
    return jnp.dot(a, b, preferred_element_type=F32)


def _dot_nt(a, b):
    return lax.dot_general(a, b, (((1,), (1,)), ((), ())), preferred_element_type=F32)


def _dot_tn(a, b):
    return lax.dot_general(a, b, (((0,), (0,)), ((), ())), preferred_element_type=F32)


def _mod_kernel(c_ref, w_ref, b_ref, o_ref):
    c = c_ref[...]
    s = (c * _sigmoid(c)).astype(BF16)
    o_ref[...] = _dot(s, w_ref[...].astype(BF16)) + b_ref[...]


def _mod_call(c_all, w_ada, b_ada):
    n = c_all.shape[0]
    nblk = 6
    return pl.pallas_call(
        _mod_kernel,
        grid=(nblk,),
        in_specs=[
            pl.BlockSpec((n, D_MODEL), lambda i: (0, 0)),
            pl.BlockSpec((D_MODEL, D_MODEL), lambda i: (0, i)),
            pl.BlockSpec((1, D_MODEL), lambda i: (0, i)),
        ],
        out_specs=pl.BlockSpec((n, D_MODEL), lambda i: (0, i)),
        out_shape=jax.ShapeDtypeStruct((n, 6 * D_MODEL), F32),
        compiler_params=pltpu.CompilerParams(dimension_semantics=("arbitrary",)),
        name="mod",
    )(c_all, w_ada, b_ada)


def _mod_row(mod_ref, idx, tm):
    groups, _, d = mod_ref.shape
    m = mod_ref[:, idx:idx + 1, :]
    if groups == 1:
        return m[0]
    return jnp.broadcast_to(m, (groups, tm // groups, d)).reshape(tm, d)


def _inproj_kernel(x_ref, mod_ref, lng_ref, lnb_ref, w_ref,
                   q_ref, k_ref, v_ref, kv_ref, p_ref, g_ref, xn_ref, *, lead):
    tm = x_ref.shape[0]
    j = pl.program_id(0)

    @pl.when(j < lead)
    def _():
        k_ref[...] = jnp.zeros_like(k_ref)
        v_ref[...] = jnp.zeros_like(v_ref)

    def half_tile(rs):
        def mrow(idx):
            m = _mod_row(mod_ref, idx, tm)
            return m if m.shape[0] == 1 else m[rs]

        xn = _layer_norm(x_ref[rs, :], lng_ref[...], lnb_ref[...])
        xn_ref[rs, :] = xn
        hb = (xn * (1.0 + mrow(1)) + mrow(0)).astype(BF16)

        def seg(a, b):
            return _dot(hb, w_ref[:, a:b])

        q_ref[rs, :] = (seg(0, HEADS_DIM) * (HEAD ** -0.5 * LOG2E)).astype(BF16)
        k = seg(HEADS_DIM, 2 * HEADS_DIM)
        k_ref[rs, :] = k.astype(BF16)
        kv_ref[rs, 0:HEADS_DIM] = k
        v = seg(2 * HEADS_DIM, 3 * HEADS_DIM)
        v_ref[rs, :] = v.astype(BF16)
        kv_ref[rs, HEADS_DIM:2 * HEADS_DIM] = v
        for c in range(0, RW_COLS, 256):
            p_ref[rs, c:c + 256] = seg(ATT_COLS + c, ATT_COLS + c + 256)
        g0 = ATT_COLS + RW_COLS
        for c in range(0, GATE_COLS, 512):
            g_ref[rs, c:c + 512] = _sigmoid(seg(g0 + c, g0 + c + 512))

    @pl.when(j >= lead)
    def _():
        half_tile(slice(0, tm // 2))
        half_tile(slice(tm // 2, tm))


def _inproj_call(x2d, mod, ln_g, ln_b, w_in_b, steps, tm, prompt):
    rows = x2d.shape[0]
    in_cols = w_in_b.shape[1]
    groups = mod.shape[0]
    if prompt:
        lead = ATT_REACH // tm
        kv_rows = ATT_REACH
        row_map = lambda j: (jnp.maximum(j - lead, 0), 0)
        kv_map = lambda j: (jnp.maximum(j - steps, 0), 0)
        ext_map = lambda j: (j, 0)
    else:
        lead = 0
        kv_rows = rows
        row_map = kv_map = ext_map = lambda j: (j, 0)
    return pl.pallas_call(
        functools.partial(_inproj_kernel, lead=lead),
        grid=(steps + lead,),
        in_specs=[
            pl.BlockSpec((tm, D_MODEL), row_map),
            _const_spec((groups, 6, D_MODEL)),
            _const_spec((1, D_MODEL)),
            _const_spec((1, D_MODEL)),
            _const_spec((D_MODEL, in_cols)),
        ],
        out_specs=[
            pl.BlockSpec((tm, HEADS_DIM), row_map),
            pl.BlockSpec((tm, HEADS_DIM), ext_map),
            pl.BlockSpec((tm, HEADS_DIM), ext_map),
            pl.BlockSpec((tm, 2 * HEADS_DIM), kv_map),
            pl.BlockSpec((tm, RW_COLS), row_map),
            pl.BlockSpec((tm, GATE_COLS), row_map),
            pl.BlockSpec((tm, D_MODEL), row_map),
        ],
        out_shape=[
            jax.ShapeDtypeStruct((rows, HEADS_DIM), BF16),
            jax.ShapeDtypeStruct((rows + lead * tm, HEADS_DIM), BF16),
            jax.ShapeDtypeStruct((rows + lead * tm, HEADS_DIM), BF16),
            jax.ShapeDtypeStruct((kv_rows, 2 * HEADS_DIM), F32),
            jax.ShapeDtypeStruct((rows, RW_COLS), F32),
            jax.ShapeDtypeStruct((rows, GATE_COLS), F32),
            jax.ShapeDtypeStruct((rows, D_MODEL), F32),
        ],
        compiler_params=pltpu.CompilerParams(
            dimension_semantics=("arbitrary",), vmem_limit_bytes=VMEM_LIMIT),
        name="inproj",
    )(x2d, mod, ln_g, ln_b, w_in_b)


def _attn_chunks(chunks, bias_ref):
    lane = lax.broadcasted_iota(jnp.int32, (CHUNK, 128), 1)
    first = lane < HEAD
    pairs = [slice(pr * 128, (pr + 1) * 128) for pr in range(N_HEADS // 2)]
    scores = []
    for qc, kb, _, _ in chunks:
        for sl in pairs:
            q2 = qc[:, sl].astype(F32)
            qs = jnp.concatenate([jnp.where(first, q2, 0.0), jnp.where(first, 0.0, q2)],
                                 axis=0).astype(BF16)
            scores.append(_dot_nt(qs, kb(sl)))
    probs, sums = [], []
    for i, s in enumerate(scores):
        thr = chunks[i // len(pairs)][3]
        s = s + bias_ref[i % len(pairs)]
        if thr is not None:
            col = lax.broadcasted_iota(jnp.int32, s.shape, 1)
            s = jnp.where(col >= thr, s, -jnp.inf)
        e = jnp.exp2(s - jnp.max(s, axis=1, keepdims=True))
        sums.append(jnp.sum(e, axis=1, keepdims=True))
        probs.append(e.astype(BF16))
    outs = []
    for ci, (_, _, vb, _) in enumerate(chunks):
        cols = []
        for pi, sl in enumerate(pairs):
            i = ci * len(pairs) + pi
            o = _dot(probs[i], vb(sl)) / sums[i]
            cols.append(jnp.where(first, o[0:CHUNK], o[CHUNK:2 * CHUNK]))
        outs.append(jnp.concatenate(cols, axis=1).astype(BF16))
    return outs


def _attn_prompt_kernel(q_ref, ka_ref, kb_ref, va_ref, vb_ref, bias_ref, o_ref, kbuf, vbuf,
                        *, chunks):
    tq = chunks * CHUNK
    kbuf[0:tq, :] = ka_ref[...]
    kbuf[tq:2 * tq, :] = kb_ref[...]
    vbuf[0:tq, :] = va_ref[...]
    vbuf[tq:2 * tq, :] = vb_ref[...]
    s = pl.program_id(0)

    per = 2

    def run(masked):
        for i in range(chunks // per):
            units, starts = [], []
            for k in range(per):
                g = i * per + k
                r0 = g * CHUNK
                thr = ATT_REACH - (s * chunks + g) * CHUNK if masked else None
                band = slice(r0, r0 + BAND)
                units.append((q_ref[r0:r0 + CHUNK, :],
                              lambda sl, band=band: kbuf[band, sl],
                              lambda sl, band=band: vbuf[band, sl], thr))
                starts.append(r0)
            for r0, o in zip(starts, _attn_chunks(units, bias_ref)):
                o_ref[r0:r0 + CHUNK, :] = o

    assert chunks * CHUNK >= ATT_REACH
    pl.when(s == 0)(functools.partial(run, True))
    pl.when(s != 0)(functools.partial(run, False))


def _attn_prompt_call(q, kext, vext, bias):
    rows = q.shape[0]
    tq = ATT_REACH
    blk = lambda off: pl.BlockSpec((tq, HEADS_DIM), lambda s: (s + off, 0))
    return pl.pallas_call(
        functools.partial(_attn_prompt_kernel, chunks=tq // CHUNK),
        grid=(rows // tq,),
        in_specs=[blk(0), blk(0), blk(1), blk(0), blk(1), _const_spec(bias.shape)],
        out_specs=blk(0),
        out_shape=jax.ShapeDtypeStruct((rows, HEADS_DIM), BF16),
        scratch_shapes=[pltpu.VMEM((2 * tq, HEADS_DIM), BF16), pltpu.VMEM((2 * tq, HEADS_DIM), BF16)],
        compiler_params=pltpu.CompilerParams(dimension_semantics=("arbitrary",)),
        name="attn_prompt",
    )(q, kext, kext, vext, vext, bias)


def _attn_sample_kernel(q_ref, k_ref, v_ref, ck_ref, cv_ref, bias_ref, o_ref):
    def band(cache_ref, new_ref):
        return lambda sl: jnp.concatenate([cache_ref[:, sl].astype(BF16), new_ref[:, sl]], axis=0)

    unit = (q_ref[...], band(ck_ref, k_ref), band(cv_ref, v_ref), None)
    o_ref[...] = _attn_chunks([unit], bias_ref)[0]


def _attn_sample_call(q, k, v, cache_k, cache_v, bias):
    nb = cache_k.shape[0]
    new = pl.BlockSpec((CHUNK, HEADS_DIM), lambda b: (b, 0))
    cache = pl.BlockSpec((None, ATT_REACH, HEADS_DIM), lambda b: (b, 0, 0))
    return pl.pallas_call(
        _attn_sample_kernel,
        grid=(nb,),
        in_specs=[new, new, new, cache, cache, _const_spec(bias.shape)],
        out_specs=new,
        out_shape=jax.ShapeDtypeStruct((nb * CHUNK, HEADS_DIM), BF16),
        compiler_params=pltpu.CompilerParams(dimension_semantics=("arbitrary",)),
        name="attn_sample",
    )(q, k, v, cache_k, cache_v, bias)


def _same_head():
    r = lax.broadcasted_iota(jnp.int32, (GROUP, GROUP), 0) // HEAD
    c = lax.broadcasted_iota(jnp.int32, (GROUP, GROUP), 1) // HEAD
    return r == c


def _blk(x, same_head):
    return jnp.where(same_head, jnp.concatenate([x] * 4, axis=0), 0.0).astype(BF16)


def _wkv_prep(units, hooks):
    same_head = _same_head()
    t64 = lax.broadcasted_iota(jnp.int32, (CHUNK, GROUP), 0)
    i64 = lax.broadcasted_iota(jnp.int32, (CHUNK, GROUP), 1) % HEAD
    strict = i64 < t64
    incl = i64 <= t64
    pending = list(hooks)

    def stage_done():
        if pending:
            pending.pop(0)()

    def blk(x):
        return _blk(x, same_head)

    n, a_ak, a_rb, a_rk = [], [], [], []
    for ld in units:
        lhs = jnp.concatenate([ld("at"), ld("rt")], axis=0).astype(BF16)
        rhs = jnp.concatenate([blk(ld("bt")), blk(ld("kt"))], axis=0)
        a_all = _dot_nt(lhs, rhs)
        n.append(jnp.where(strict, a_all[0:CHUNK, 0:GROUP], 0.0))
        a_ak.append(jnp.where(strict, a_all[0:CHUNK, GROUP:], 0.0))
        a_rb.append(jnp.where(incl, a_all[CHUNK:, 0:GROUP], 0.0))
        a_rk.append(jnp.where(incl, a_all[CHUNK:, GROUP:], 0.0))
    stage_done()

    x0 = [_dot(a.astype(BF16), blk(ld("v"))) for a, ld in zip(a_ak, units)]
    npow = [_dot(m.astype(BF16), blk(m)) for m in n]
    t = [jnp.where(i64 == t64, 1.0, 0.0) + m for m in n]
    stage_done()
    for _ in range(4):
        prod = [_dot(jnp.concatenate([a, b], axis=0).astype(BF16), blk(b))
                for a, b in zip(t, npow)]
        t = [a + p[0:CHUNK] for a, p in zip(t, prod)]
        npow = [p[CHUNK:] for p in prod]
        stage_done()
    t = [a + _dot(a.astype(BF16), blk(b)) for a, b in zip(t, npow)]
    stage_done()
    res = [_dot(a.astype(BF16), jnp.concatenate([blk(ld("at")), blk(x)], axis=1))
           for a, x, ld in zip(t, x0, units)]
    stage_done()
    while pending:
        stage_done()
    return [(r[:, 0:GROUP], r[:, GROUP:], b, k) for r, b, k in zip(res, a_rb, a_rk)]


def _wkv_step_stages(units, states, out):
    same_head = _same_head()
    held = {}

    def blk(x):
        return _blk(x, same_head)

    def first():
        held["sblk"] = [blk(st) for st in states]
        held["u"] = [_dot(ld("ah").astype(BF16), sb) + ld("vh")
                     for ld, sb in zip(units, held["sblk"])]

    def second():
        for i, ld in enumerate(units):
            u, v = held["u"][i], ld("v")
            y = _dot(jnp.concatenate([ld("rt"), ld("arb"), ld("ark")], axis=1).astype(BF16),
                     jnp.concatenate([held["sblk"][i], blk(u), blk(v)], axis=0))
            g = _dot_tn(jnp.concatenate([ld("btd"), ld("ktd")], axis=0).astype(BF16),
                        jnp.concatenate([u, v], axis=0).astype(BF16))
            g = jnp.where(same_head, g, 0.0)
            delta = g[0:64] + g[64:128] + g[128:192] + g[192:256]
            out.append(y)
            states[i] = states[i] * ld("fdec") + delta

    return [first, second]


def _wkv_kernel(p_ref, pn_ref, shift0_ref, state0_ref, mu_ref, w0_ref, wwa_ref, a0_ref, gup_ref,
                kkw_ref, kaw_ref, rkw_ref, gng_ref, gnb_ref, ones_ref, ltri_ref,
                out_ref, shift_out_ref, state_out_ref,
                st_scr, rt_scr, at_scr, kt_scr, bt_scr, v_scr, btd_scr, ktd_scr,
                fdec_scr, ah_scr, vh_scr, arb_scr, ark_scr, g_scr, bon_scr, *, ts, steps):
    j = pl.program_id(0)
    streams = state0_ref.shape[0]
    assert streams == 1 or (steps == 1 and ts == streams * CHUNK)

    if streams == 1:
        @pl.when(j == 0)
        def _():
            st_scr[...] = state0_ref[0]

    def bdsum(x, terms):
        ones = ones_ref[...]
        halves = []
        for c0 in range(0, HEADS_DIM, GROUP):
            parts = _split3(x[:, c0:c0 + GROUP])[:terms]
            acc = _dot(parts[0], ones)
            for part in parts[1:]:
                acc = acc + _dot(part, ones)
            halves.append(acc)
        return jnp.concatenate(halves, axis=1)

    pq = ltri_ref.shape[0]
    per = pq // CHUNK
    n_piece = ts // pq

    def pre_stages(q, upcoming=False):
        r0 = q * pq
        rows = slice(r0, r0 + pq)
        src_ref = pn_ref if upcoming else p_ref
        held = {}

        def shifted(c0, w):
            cols = slice(c0, c0 + w)
            row = lax.broadcasted_iota(jnp.int32, (pq, 1), 0)
            if streams > 1:
                first = shift0_ref[q * per:(q + 1) * per, :, cols]
                prev_row = jnp.broadcast_to(first, (per, CHUNK, w)).reshape(pq, w)
                top = row % CHUNK == 0
            else:
                if upcoming:
                    prev_row = p_ref[ts - 8:ts, cols][7:8]
                elif q == 0:
                    prev_row = shift0_ref[0][:, cols]
                else:
                    prev_row = p_ref[r0 - 8:r0, cols][7:8]
                top = row == 0
            pb = src_ref[rows, cols]
            prev = jnp.where(top, prev_row, pltpu.roll(pb, 1, 0))
            return pb + (prev - pb) * mu_ref[:, cols]

        def s_lora():
            lora = shifted(3 * HEADS_DIM, 256)
            lwla = lora[:, 0:128]
            lane = lax.broadcasted_iota(jnp.int32, (pq, 128), 1)
            held["raw"] = _dot(jnp.where(lane < LORA_W, jnp.tanh(lwla), lwla).astype(BF16),
                               wwa_ref[...])
            g_scr[rows, :] = _dot(_sigmoid(lora[:, 128:256]).astype(BF16), gup_ref[...])
            held["k"] = shifted(HEADS_DIM, HEADS_DIM)

        def s_decay():
            w_pre = w0_ref[...] + held["raw"][:, 0:HEADS_DIM]
            softplus = jnp.maximum(-w_pre, 0.0) + jnp.log(1.0 + jnp.exp(-jnp.abs(w_pre)))
            lw = -jnp.exp(-softplus - 0.5)
            ltri = ltri_ref[...]
            hi, mid, _ = _split3(lw)
            held["cum"] = _dot(ltri, hi) + _dot(ltri, mid)
            held["lw"] = lw

        def s_keys():
            k = held["k"]
            a = _sigmoid(a0_ref[...] + held["raw"][:, HEADS_DIM:])
            kk = k * kkw_ref[...]
            held["nrm"] = bdsum(kk * kk, 1)
            held.update(a=a, kk=kk, k2=k * (1.0 + (a - 1.0) * kaw_ref[...]))

        def s_bonus():
            r = shifted(0, HEADS_DIM)
            held["bon"] = bdsum(r * held["k2"] * rkw_ref[...], 1)
            rt_scr[rows, :] = r * jnp.exp(held["cum"])

        def s_norm():
            cum = held["cum"]
            kk = held["kk"] / jnp.maximum(jnp.sqrt(held["nrm"]), 1e-12)
            cum3 = cum.reshape(per, CHUNK, HEADS_DIM)
            cend = jnp.broadcast_to(cum3[:, CHUNK - 1:CHUNK, :], cum3.shape).reshape(pq, HEADS_DIM)
            diag = (lax.broadcasted_iota(jnp.int32, (pq, HEADS_DIM), 0) % CHUNK
                    == lax.broadcasted_iota(jnp.int32, (pq, HEADS_DIM), 1) % HEAD)
            held["fdec"] = bdsum(jnp.where(diag, jnp.exp(cend), 0.0), 2)
            at_scr[rows, :] = -kk * jnp.exp(cum - held["lw"])
            held.update(kk=kk, cend=cend)

        def s_inv():
            e_inv = jnp.exp(-held["cum"])
            kt_scr[rows, :] = held["k2"] * e_inv
            bt_scr[rows, :] = held["kk"] * held["a"] * e_inv

        def s_rel():
            e_rel = jnp.exp(held["cend"] - held["cum"])
            ktd_scr[rows, :] = held["k2"] * e_rel
            btd_scr[rows, :] = held["kk"] * held["a"] * e_rel
            fdec_scr[rows, :] = held["fdec"]

        def s_value():
            v = shifted(2 * HEADS_DIM, HEADS_DIM)
            v_scr[rows, :] = v
            bon_scr[rows, :] = held["bon"] * v

        return [s_lora, s_decay, s_keys, s_bonus, s_norm, s_inv, s_rel, s_value]

    def chunk_rows(c):
        return slice(c * CHUNK, (c + 1) * CHUNK)

    groups = [slice(g0, g0 + GROUP) for g0 in range(0, HEADS_DIM, GROUP)]

    named = dict(at=at_scr, rt=rt_scr, kt=kt_scr, bt=bt_scr, v=v_scr, btd=btd_scr, ktd=ktd_scr,
                 fdec=fdec_scr, ah=ah_scr, vh=vh_scr, arb=arb_scr, ark=ark_scr)
    prep_outs = (ah_scr, vh_scr, arb_scr, ark_scr)

    def loader(c, cols):
        rows = chunk_rows(c)
        return lambda name: named[name][rows, cols]

    def run(step_piece, prep_piece, extra):
        step_chunks = [] if step_piece is None else [step_piece * per + i for i in range(per)]
        prep_chunks = [] if prep_piece is None else [prep_piece * per + i for i in range(per)]
        stages, ys, finals = [], [], []
        if streams == 1:
            states = [st_scr[:, cols] for cols in groups] if step_chunks else []
            for c in step_chunks:
                out = []
                ys.append(out)
                stages += _wkv_step_stages([loader(c, cols) for cols in groups], states, out)
            finals = [(st_scr, states)] if step_chunks else []
        else:
            firsts, seconds = [], []
            for c in step_chunks:
                out = []
                ys.append(out)
                states = [state0_ref[c, :, cols] for cols in groups]
                first, second = _wkv_step_stages(
                    [loader(c, cols) for cols in groups], states, out)
                firsts.append(first)
                seconds.append(second)
                finals.append((state_out_ref.at[c], states))
            if step_chunks:
                stages += [lambda: [f() for f in firsts], lambda: [f() for f in seconds]]
        if step_chunks:
            rows = slice(step_piece * pq, (step_piece + 1) * pq)
            gn = {}

            def gn_mean():
                gn["y"] = jnp.concatenate([jnp.concatenate(out, axis=1) for out in ys], axis=0)
                gn["sum"] = bdsum(gn["y"], 1)

            def gn_var():
                gn["d"] = gn["y"] - gn["sum"] * (1.0 / HEAD)
                gn["sq"] = bdsum(gn["d"] * gn["d"], 1)

            def gn_out():
                var = gn["sq"] * (1.0 / HEAD)
                yn = (gn["d"] * lax.rsqrt(var + GN_EPS) * gng_ref[...] + gnb_ref[...]
                      + bon_scr[rows, :])
                out_ref[rows, :] = (yn * g_scr[rows, :]).astype(BF16)

            stages += [gn_mean, gn_var, gn_out]
        hooks = []
        for i in range(max(len(stages), len(extra))):
            both = stages[i:i + 1] + extra[i:i + 1]
            hooks.append(lambda both=both: [f() for f in both])
        units = [(c, cols) for c in prep_chunks for cols in groups]
        prep_out = _wkv_prep([loader(c, cols) for c, cols in units], hooks)
        for ref, states in finals:
            for cols, st in zip(groups, states):
                ref[:, cols] = st
        for (c, cols), outs in zip(units, prep_out):
            for ref, val in zip(prep_outs, outs):
                ref[chunk_rows(c), cols] = val

    look_ahead = steps > 1
    ahead_pass = max(2, n_piece - 1)
    assert ahead_pass <= n_piece or not look_ahead

    def first_piece():
        for stage in pre_stages(0):
            stage()

    if look_ahead:
        pl.when(j == 0)(first_piece)
    else:
        first_piece()
    for piece in range(n_piece + 1):
        if piece + 1 < n_piece:
            extra = pre_stages(piece + 1)
        elif piece == ahead_pass and look_ahead:
            extra = pre_stages(0, upcoming=True)
        else:
            extra = []
        run(piece - 1 if piece >= 1 else None, piece if piece < n_piece else None, extra)

    if streams == 1:
        @pl.when(j == steps - 1)
        def _():
            shift_out_ref[0] = p_ref[ts - 1:ts, :]
            state_out_ref[0] = st_scr[...]
    else:
        for s in range(streams):
            shift_out_ref[s] = p_ref[(s + 1) * CHUNK - 1:(s + 1) * CHUNK, :]


def _wkv_call(p2d, shift0, state0, prm, steps, ts):
    rows = p2d.shape[0]
    streams = state0.shape[0]
    row_map = lambda j: (j, 0)
    next_map = lambda j: (jnp.minimum(j + 1, steps - 1), 0)
    seq3 = lambda j: (0, 0, 0)
    vec = _const_spec((1, HEADS_DIM))
    big = pltpu.VMEM((ts, HEADS_DIM), F32)
    ltri = prm["ltri"][min(ts, WKV_PIECE)]
    return pl.pallas_call(
        functools.partial(_wkv_kernel, ts=ts, steps=steps),
        grid=(steps,),
        in_specs=[
            pl.BlockSpec((ts, RW_COLS), row_map),
            pl.BlockSpec((ts, RW_COLS), next_map),
            pl.BlockSpec((streams, 1, RW_COLS), seq3),
            pl.BlockSpec((streams, HEAD, HEADS_DIM), seq3),
            _const_spec((1, RW_COLS)),
            vec,
            _const_spec((128, 2 * HEADS_DIM)),
            vec,
            _const_spec((LORA_G, HEADS_DIM)),
            vec, vec, vec, vec, vec,
            _const_spec((GROUP, GROUP)),
            _const_spec(ltri.shape),
        ],
        out_specs=[
            pl.BlockSpec((ts, HEADS_DIM), row_map),
            pl.BlockSpec((streams, 1, RW_COLS), seq3),
            pl.BlockSpec((streams, HEAD, HEADS_DIM), seq3),
        ],
        out_shape=[
            jax.ShapeDtypeStruct((rows, HEADS_DIM), BF16),
            jax.ShapeDtypeStruct((streams, 1, RW_COLS), F32),
            jax.ShapeDtypeStruct((streams, HEAD, HEADS_DIM), F32),
        ],
        scratch_shapes=[
            pltpu.VMEM((HEAD, HEADS_DIM), F32),
        ] + [big] * 14,
        compiler_params=pltpu.CompilerParams(
            dimension_semantics=("arbitrary",), vmem_limit_bytes=VMEM_LIMIT),
        name="wkv",
    )(p2d, p2d, shift0, state0, prm["mu"], prm["w0"], prm["wwa"], prm["a0"], prm["gup"],
      prm["kk"], prm["ka"], prm["rk"], prm["gng"], prm["gnb"], prm["ones"], ltri)


def _mixffn_kernel(xn_ref, att_ref, rw_ref, gate_ref, mod_ref, convp_ref,
                   ln1g_ref, ln1b_ref, ln2g_ref, ln2b_ref,
                   wa_ref, wr_ref, wo_ref, wup_ref, cw_ref, cb_ref, wdn_ref,
                   y_ref, convo_ref, carry_scr, yb_scr, *, tm, steps):
    j = pl.program_id(0)
    groups = mod_ref.shape[0]
    assert groups == 1 or steps == 1

    if groups == 1:
        @pl.when(j == 0)
        def _():
            carry_scr[8 - (CONV_W - 1):8, :] = convp_ref[0]

    def mod(idx):
        return _mod_row(mod_ref, idx, tm)

    half = tm // 2
    subs = [slice(0, half), slice(half, tm)]

    def modr(idx, rs):
        m = mod(idx)
        return m if m.shape[0] == 1 else m[rs]

    ma = [_dot(att_ref[rs, :], wa_ref[...]) for rs in subs]
    mr = [_dot(rw_ref[rs, :], wr_ref[...]) for rs in subs]
    merged = [(gate_ref[rs, 0:D_MODEL] * a + gate_ref[rs, D_MODEL:] * r).astype(BF16)
              for rs, a, r in zip(subs, ma, mr)]
    mix = [_dot(m, wo_ref[...]) for m in merged]
    x1 = [_layer_norm(ALPHA * xn_ref[rs, :] + (1.0 + modr(2, rs)) * m, ln1g_ref[...], ln1b_ref[...])
          for rs, m in zip(subs, mix)]
    h2 = jnp.concatenate([(a * (1.0 + modr(4, rs)) + modr(3, rs)).astype(BF16)
                          for rs, a in zip(subs, x1)], axis=0)

    cw_blk = 256
    glen = tm // groups
    rb = FFN_ROWS if groups == 1 else tm
    row8 = lax.broadcasted_iota(jnp.int32, (8, cw_blk), 0)
    grow = lax.broadcasted_iota(jnp.int32, (tm, cw_blk), 0) % glen

    def up(blk):
        c, r0 = blk
        h = h2[r0:r0 + rb]
        return (_dot(h, wup_ref[:, c:c + cw_blk]),
                _dot(h, wup_ref[:, D_FF + c:D_FF + c + cw_blk]))

    blocks = [(c, r0) for c in range(0, D_FF, cw_blk) for r0 in range(0, tm, rb)]
    for c, r0 in blocks:
        uc, uv = up((c, r0))
        cs = slice(c, c + cw_blk)
        r1 = pltpu.roll(uc, 1, 0)
        r2 = pltpu.roll(uc, 2, 0)
        if groups == 1:
            if r0 == 0:
                c6 = carry_scr[6:7, cs]
                c7 = carry_scr[7:8, cs]
            s1 = jnp.concatenate([jnp.where(row8 == 0, c7, r1[0:8]), r1[8:]], axis=0)
            s2 = jnp.concatenate(
                [jnp.where(row8 == 0, c6, jnp.where(row8 == 1, c7, r2[0:8])), r2[8:]], axis=0)
            c6 = uc[rb - 2:rb - 1, :]
            c7 = uc[rb - 1:rb, :]
            if r0 + rb == tm:
                carry_scr[:, cs] = uc[rb - 8:rb, :]
            tail = uc[rb - (CONV_W - 1):rb, :][None]
        else:
            hist = jnp.broadcast_to(convp_ref[:, :, cs][:, :, None, :],
                                    (groups, CONV_W - 1, glen, cw_blk))
            c6 = hist[:, 0].reshape(tm, cw_blk)
            c7 = hist[:, 1].reshape(tm, cw_blk)
            s1 = jnp.where(grow == 0, c7, r1)
            s2 = jnp.where(grow == 0, c6, jnp.where(grow == 1, c7, r2))
            tail = uc.reshape(groups, glen, cw_blk)[:, glen - (CONV_W - 1):, :]
        conv = cb_ref[:, cs] + s2 * cw_ref[0:1, cs] + s1 * cw_ref[1:2, cs] + uc * cw_ref[2:3, cs]
        yb_scr[r0:r0 + rb, cs] = (conv * _sigmoid(conv) * uv).astype(BF16)

        if r0 + rb == tm:
            convo_ref[:, :, cs] = tail

    ff = [_dot(yb_scr[rs, :], wdn_ref[...]) for rs in subs]
    for rs, a, f in zip(subs, x1, ff):
        y_ref[rs, :] = _layer_norm(ALPHA * a + (1.0 + modr(5, rs)) * f, ln2g_ref[...], ln2b_ref[...])


def _mixffn_call(xn2d, att, rw, gates, mod, conv_prev, prm, steps, tm):
    rows = xn2d.shape[0]
    groups = mod.shape[0]
    row_map = lambda j: (j, 0)
    vec = _const_spec((1, D_MODEL))
    return pl.pallas_call(
        functools.partial(_mixffn_kernel, tm=tm, steps=steps),
        grid=(steps,),
        in_specs=[
            pl.BlockSpec((tm, D_MODEL), row_map),
            pl.BlockSpec((tm, HEADS_DIM), row_map),
            pl.BlockSpec((tm, HEADS_DIM), row_map),
            pl.BlockSpec((tm, GATE_COLS), row_map),
            _const_spec((groups, 6, D_MODEL)),
            _const_spec((groups, CONV_W - 1, D_FF)),
            vec, vec, vec, vec,
            _const_spec((HEADS_DIM, D_MODEL)),
            _const_spec((HEADS_DIM, D_MODEL)),
            _const_spec((D_MODEL, D_MODEL)),
            _const_spec((D_MODEL, 2 * D_FF)),
            _const_spec((CONV_W, D_FF)),
            _const_spec((1, D_FF)),
            _const_spec((D_FF, D_MODEL)),
        ],
        out_specs=[
            pl.BlockSpec((tm, D_MODEL), row_map),
            pl.BlockSpec((groups, CONV_W - 1, D_FF), lambda j: (0, 0, 0)),
        ],
        out_shape=[
            jax.ShapeDtypeStruct((rows, D_MODEL), F32),
            jax.ShapeDtypeStruct((groups, CONV_W - 1, D_FF), F32),
        ],
        scratch_shapes=[pltpu.VMEM((8, D_FF), F32), pltpu.VMEM((tm, D_FF), BF16)],
        compiler_params=pltpu.CompilerParams(
            dimension_semantics=("arbitrary",), vmem_limit_bytes=VMEM_LIMIT),
        name="mixffn",
    )(xn2d, att, rw, gates, mod, conv_prev,
      prm["ln1g"], prm["ln1b"], prm["ln2g"], prm["ln2b"],
      prm["wa"], prm["wr"], prm["wo"], prm["wup"], prm["cw"], prm["cb"], prm["wdn"])


def _pair_bias(table):
    assert CHUNK - 1 <= REL_CLIP
    top = ATT_REACH + CHUNK - 1
    n_far = top - REL_CLIP + 1
    far = jnp.broadcast_to(table[:, 2 * REL_CLIP:], (N_HEADS, n_far))
    lo_idx = top - (BAND + CHUNK - 2) + REL_CLIP
    near = table[:, lo_idx:2 * REL_CLIP][:, ::-1]
    ext = jnp.concatenate([far, near], axis=1).astype(F32) * LOG2E
    n_ext = BAND + CHUNK - 1
    period = jnp.concatenate([ext, jnp.zeros((N_HEADS, 1), F32)], axis=1)
    skew = jnp.tile(period, (1, CHUNK))[:, :CHUNK * n_ext].reshape(N_HEADS, CHUNK, n_ext)
    bias = skew[:, :, CHUNK - 1:CHUNK - 1 + BAND]
    return bias.reshape(N_HEADS // 2, 2 * CHUNK, BAND)


def _chunk_ltri(ts):
    t = jnp.arange(ts)
    return ((t[:, None] // CHUNK == t[None, :] // CHUNK) & (t[None, :] <= t[:, None])).astype(BF16)


def _trunk(x2d, mod, shift0, state0, conv_prev, caches, prm, n_seq):
    prompt = caches is None
    rows = x2d.shape[0]
    tm = min(rows, ROW_TILE)
    steps = rows // tm
    q, k, v, kv32, p, gates, xn = _inproj_call(
        x2d, mod, prm["lnig"], prm["lnib"], prm["win"], steps, tm, prompt)
    if prompt:
        att = _attn_prompt_call(q, k, v, prm["bias"])
    else:
        att = _attn_sample_call(q, k, v, *caches, prm["bias"])
    assert state0.shape[0] == n_seq
    rw, shift, state = _wkv_call(p, shift0, state0, prm, steps, tm)
    y, conv = _mixffn_call(xn, att, rw, gates, mod, conv_prev, prm, steps, tm)
    return y, kv32, state, shift, conv


def kernel(x_prompt, x_sample, cache_attn_k, cache_attn_v, state_rwkv, state_shift, state_conv,
           c_prompt, c_sample, ln_in_g, ln_in_b, w_ada, b_ada, w_in, attn_rel_bias,
           rwkv_mu, rwkv_w0, rwkv_w_up, rwkv_a0, rwkv_a_up, rwkv_g_up, rwkv_k_k, rwkv_k_a,
           rwkv_r_k, rwkv_gn_g, rwkv_gn_b, w_branch_attn, w_branch_rwkv, w_out,
           ln1_g, ln1_b, ln2_g, ln2_b, w_ffn_up, ffn_conv_w, ffn_conv_b, w_ffn_down):
    bp, sp, _ = x_prompt.shape
    bs, ss, _ = x_sample.shape
    assert bp == 1 and ss == CHUNK and w_ada.shape[0] == DEPTH
    assert sp % ROW_TILE == 0 and bs * ss <= ROW_TILE and cache_attn_k.shape[2] == ATT_REACH

    row = lambda a: a.reshape(1, -1)
    wwa = jnp.zeros((LORA_W + LORA_A, 2 * HEADS_DIM), F32)
    wwa = wwa.at[:LORA_W, :HEADS_DIM].set(rwkv_w_up[0]).at[LORA_W:, HEADS_DIM:].set(rwkv_a_up[0])
    head_id = jnp.arange(GROUP) // HEAD
    prm = dict(
        lnig=row(ln_in_g), lnib=row(ln_in_b),
        ln1g=row(ln1_g[0]), ln1b=row(ln1_b[0]), ln2g=row(ln2_g[0]), ln2b=row(ln2_b[0]),
        win=w_in[0].astype(BF16), bias=_pair_bias(attn_rel_bias[0]),
        mu=row(rwkv_mu[0]), w0=row(rwkv_w0[0]), wwa=wwa.astype(BF16), a0=row(rwkv_a0[0]),
        gup=rwkv_g_up[0].astype(BF16), kk=row(rwkv_k_k[0]), ka=row(rwkv_k_a[0]),
        rk=row(rwkv_r_k[0]), gng=row(rwkv_gn_g[0]), gnb=row(rwkv_gn_b[0]),
        ones=(head_id[:, None] == head_id[None, :]).astype(BF16),
        ltri={WKV_PIECE: _chunk_ltri(WKV_PIECE), CHUNK: _chunk_ltri(CHUNK)},
        wa=w_branch_attn[0].astype(BF16), wr=w_branch_rwkv[0].astype(BF16),
        wo=w_out[0].astype(BF16), wup=w_ffn_up[0].astype(BF16),
        cw=ffn_conv_w[0], cb=row(ffn_conv_b[0]), wdn=w_ffn_down[0].astype(BF16),
    )

    n_c = bp + bs
    c_all = jnp.concatenate([c_prompt, c_sample, jnp.zeros((16 - n_c, D_MODEL), F32)], axis=0)
    mod = _mod_call(c_all, w_ada[0], row(b_ada[0])).reshape(16, 6, D_MODEL)

    y_p, kv_p, st_p, sh_p, cv_p = _trunk(
        x_prompt.reshape(sp, D_MODEL), mod[0:bp],
        jnp.zeros((bp, 1, RW_COLS), F32), jnp.zeros((bp, HEAD, HEADS_DIM), F32),
        jnp.zeros((bp, CONV_W - 1, D_FF), F32), None, prm, n_seq=bp)

    caches = (cache_attn_k[0].reshape(bs, ATT_REACH, HEADS_DIM),
              cache_attn_v[0].reshape(bs, ATT_REACH, HEADS_DIM))
    st0 = jnp.transpose(state_rwkv[0], (0, 3, 1, 2)).reshape(bs, HEAD, HEADS_DIM)
    y_s, kv_s, st_s, sh_s, cv_s = _trunk(
        x_sample.reshape(bs * ss, D_MODEL), mod[bp:n_c],
        state_shift[0], st0, state_conv[0], caches, prm, n_seq=bs)

    def state_out(st, b):
        return jnp.transpose(st.reshape(b, HEAD, N_HEADS, HEAD), (0, 2, 3, 1))[None]

    hs = (N_HEADS, HEAD)
    return (
        y_p.reshape(bp, sp, D_MODEL),
        y_s.reshape(bs, ss, D_MODEL),
        kv_p[:, :HEADS_DIM].reshape(1, bp, ATT_REACH, *hs),
        kv_p[:, HEADS_DIM:].reshape(1, bp, ATT_REACH, *hs),
        kv_s[:, :HEADS_DIM].reshape(1, bs, ss, *hs),
        kv_s[:, HEADS_DIM:].reshape(1, bs, ss, *hs),
        state_out(st_p, bp),
        state_out(st_s, bs),
        sh_p[None],
        sh_s[None],
        cv_p[None],
        cv_s[None],
    )
```

```python
import functools

import jax
import jax.numpy as jnp
from jax import lax
from jax.experimental import pallas as pl
from jax.experimental.pallas import tpu as pltpu

F32 = jnp.float32
BF16 = jnp.bfloat16

D_MODEL = 1024
CHUNK = 64
ATT_REACH = 512
BAND = ATT_REACH + CHUNK
N_HEADS = 8
HEAD = 64
HEADS_DIM = N_HEADS * HEAD
REL_CLIP = 128
LORA_W = 64
LORA_A = 64
LORA_G = 128
ATT_COLS = 3 * HEADS_DIM
RW_COLS = 3 * HEADS_DIM + LORA_W + LORA_A + LORA_G
GATE_COLS = 2 * D_MODEL
D_FF = 2816
CONV_W = 3
LN_EPS = 1e-5
GN_EPS = 64e-5
DEPTH = 1
ALPHA = (2 * DEPTH) ** 0.25
LOG2E = 1.4426950408889634

GROUP = 256
ROW_TILE = 512
WKV_PIECE = 256
FFN_ROWS = 512
VMEM_LIMIT = 56 * 1024 * 1024


def _const_spec(shape):
    nd = len(shape)
    return pl.BlockSpec(shape, lambda *_: (0,) * nd, pipeline_mode=pl.Buffered(1))


def _layer_norm(x, g, b):
    mu = jnp.mean(x, axis=-1, keepdims=True)
    xc = x - mu
    var = jnp.mean(xc * xc, axis=-1, keepdims=True)
    return xc * lax.rsqrt(var + LN_EPS) * g + b


def _sigmoid(x):
    return 1.0 / (1.0 + jnp.exp(-x))


def _split3(x):
    hi = x.astype(BF16)
    r1 = x - hi.astype(F32)
    mid = r1.astype(BF16)
    lo = (r1 - mid.astype(F32)).astype(BF16)
    return hi, mid, lo


def _dot(a, b):
    return jnp.dot(a, b, preferred_element_type=F32)


def _dot_nt(a, b):
    return lax.dot_general(a, b, (((1,), (1,)), ((), ())), preferred_element_type=F32)


def _dot_tn(a, b):
    return lax.dot_general(a, b, (((0,), (0,)), ((), ())), preferred_element_type=F32)


def _mod_kernel(c_ref, w_ref, b_ref, o_ref):
    c = c_ref[...]
    s = (c * _sigmoid(c)).astype(BF16)
    o_ref[...] = _dot(s, w_ref[...].astype(BF16)) + b_ref[...]


def _mod_call(c_all, w_ada, b_ada):
    n = c_all.shape[0]
    nblk = 6
    return pl.pallas_call(
        _mod_kernel,
        grid=(nblk,),
        in_specs=[
            pl.BlockSpec((n, D_MODEL), lambda i: (0, 0)),
            pl.BlockSpec((D_MODEL, D_MODEL), lambda i: (0, i)),
            pl.BlockSpec((1, D_MODEL), lambda i: (0, i)),
        ],
        out_specs=pl.BlockSpec((n, D_MODEL), lambda i: (0, i)),
        out_shape=jax.ShapeDtypeStruct((n, 6 * D_MODEL), F32),
        compiler_params=pltpu.CompilerParams(dimension_semantics=("arbitrary",)),
        name="mod",
    )(c_all, w_ada, b_ada)


def _mod_row(mod_ref, idx, tm):
    groups, _, d = mod_ref.shape
    m = mod_ref[:, idx:idx + 1, :]
    if groups == 1:
        return m[0]
    return jnp.broadcast_to(m, (groups, tm // groups, d)).reshape(tm, d)


def _inproj_kernel(x_ref, mod_ref, lng_ref, lnb_ref, w_ref,
                   q_ref, k_ref, v_ref, kv_ref, p_ref, g_ref, xn_ref, *, lead):
    tm = x_ref.shape[0]
    j = pl.program_id(0)

    @pl.when(j < lead)
    def _():
        k_ref[...] = jnp.zeros_like(k_ref)
        v_ref[...] = jnp.zeros_like(v_ref)

    def half_tile(rs):
        def mrow(idx):
            m = _mod_row(mod_ref, idx, tm)
            return m if m.shape[0] == 1 else m[rs]

        xn = _layer_norm(x_ref[rs, :], lng_ref[...], lnb_ref[...])
        xn_ref[rs, :] = xn
        hb = (xn * (1.0 + mrow(1)) + mrow(0)).astype(BF16)

        def seg(a, b):
            return _dot(hb, w_ref[:, a:b])

        q_ref[rs, :] = (seg(0, HEADS_DIM) * (HEAD ** -0.5 * LOG2E)).astype(BF16)
        k = seg(HEADS_DIM, 2 * HEADS_DIM)
        k_ref[rs, :] = k.astype(BF16)
        kv_ref[rs, 0:HEADS_DIM] = k
        v = seg(2 * HEADS_DIM, 3 * HEADS_DIM)
        v_ref[rs, :] = v.astype(BF16)
        kv_ref[rs, HEADS_DIM:2 * HEADS_DIM] = v
        for c in range(0, RW_COLS, 256):
            p_ref[rs, c:c + 256] = seg(ATT_COLS + c, ATT_COLS + c + 256)
        g0 = ATT_COLS + RW_COLS
        for c in range(0, GATE_COLS, 512):
            g_ref[rs, c:c + 512] = _sigmoid(seg(g0 + c, g0 + c + 512))

    @pl.when(j >= lead)
    def _():
        half_tile(slice(0, tm // 2))
        half_tile(slice(tm // 2, tm))


def _inproj_call(x2d, mod, ln_g, ln_b, w_in_b, steps, tm, prompt):
    rows = x2d.shape[0]
    in_cols = w_in_b.shape[1]
    groups = mod.shape[0]
    if prompt:
        lead = ATT_REACH // tm
        kv_rows = ATT_REACH
        row_map = lambda j: (jnp.maximum(j - lead, 0), 0)
        kv_map = lambda j: (jnp.maximum(j - steps, 0), 0)
        ext_map = lambda j: (j, 0)
    else:
        lead = 0
        kv_rows = rows
        row_map = kv_map = ext_map = lambda j: (j, 0)
    return pl.pallas_call(
        functools.partial(_inproj_kernel, lead=lead),
        grid=(steps + lead,),
        in_specs=[
            pl.BlockSpec((tm, D_MODEL), row_map),
            _const_spec((groups, 6, D_MODEL)),
            _const_spec((1, D_MODEL)),
            _const_spec((1, D_MODEL)),
            _const_spec((D_MODEL, in_cols)),
        ],
        out_specs=[
            pl.BlockSpec((tm, HEADS_DIM), row_map),
            pl.BlockSpec((tm, HEADS_DIM), ext_map),
            pl.BlockSpec((tm, HEADS_DIM), ext_map),
            pl.BlockSpec((tm, 2 * HEADS_DIM), kv_map),
            pl.BlockSpec((tm, RW_COLS), row_map),
            pl.BlockSpec((tm, GATE_COLS), row_map),
            pl.BlockSpec((tm, D_MODEL), row_map),
        ],
        out_shape=[
            jax.ShapeDtypeStruct((rows, HEADS_DIM), BF16),
            jax.ShapeDtypeStruct((rows + lead * tm, HEADS_DIM), BF16),
            jax.ShapeDtypeStruct((rows + lead * tm, HEADS_DIM), BF16),
            jax.ShapeDtypeStruct((kv_rows, 2 * HEADS_DIM), F32),
            jax.ShapeDtypeStruct((rows, RW_COLS), F32),
            jax.ShapeDtypeStruct((rows, GATE_COLS), F32),
            jax.ShapeDtypeStruct((rows, D_MODEL), F32),
        ],
        compiler_params=pltpu.CompilerParams(
            dimension_semantics=("arbitrary",), vmem_limit_bytes=VMEM_LIMIT),
        name="inproj",
    )(x2d, mod, ln_g, ln_b, w_in_b)


def _attn_chunks(chunks, bias_ref):
    lane = lax.broadcasted_iota(jnp.int32, (CHUNK, 128), 1)
    first = lane < HEAD
    pairs = [slice(pr * 128, (pr + 1) * 128) for pr in range(N_HEADS // 2)]
    scores = []
    for qc, kb, _, _ in chunks:
        for sl in pairs:
            q2 = qc[:, sl].astype(F32)
            qs = jnp.concatenate([jnp.where(first, q2, 0.0), jnp.where(first, 0.0, q2)],
                                 axis=0).astype(BF16)
            scores.append(_dot_nt(qs, kb(sl)))
    probs, sums = [], []
    for i, s in enumerate(scores):
        thr = chunks[i // len(pairs)][3]
        s = s + bias_ref[i % len(pairs)]
        if thr is not None:
            col = lax.broadcasted_iota(jnp.int32, s.shape, 1)
            s = jnp.where(col >= thr, s, -jnp.inf)
        e = jnp.exp2(s - jnp.max(s, axis=1, keepdims=True))
        sums.append(jnp.sum(e, axis=1, keepdims=True))
        probs.append(e.astype(BF16))
    outs = []
    for ci, (_, _, vb, _) in enumerate(chunks):
        cols = []
        for pi, sl in enumerate(pairs):
            i = ci * len(pairs) + pi
            o = _dot(probs[i], vb(sl)) / sums[i]
            cols.append(jnp.where(first, o[0:CHUNK], o[CHUNK:2 * CHUNK]))
        outs.append(jnp.concatenate(cols, axis=1).astype(BF16))
    return outs


def _attn_prompt_kernel(q_ref, ka_ref, kb_ref, va_ref, vb_ref, bias_ref, o_ref, kbuf, vbuf,
                        *, chunks):
    tq = chunks * CHUNK
    kbuf[0:tq, :] = ka_ref[...]
    kbuf[tq:2 * tq, :] = kb_ref[...]
    vbuf[0:tq, :] = va_ref[...]
    vbuf[tq:2 * tq, :] = vb_ref[...]
    s = pl.program_id(0)

    per = 2

    def run(masked):
        for i in range(chunks // per):
            units, starts = [], []
            for k in range(per):
                g = i * per + k
                r0 = g * CHUNK
                thr = ATT_REACH - (s * chunks + g) * CHUNK if masked else None
                band = slice(r0, r0 + BAND)
                units.append((q_ref[r0:r0 + CHUNK, :],
                              lambda sl, band=band: kbuf[band, sl],
                              lambda sl, band=band: vbuf[band, sl], thr))
                starts.append(r0)
            for r0, o in zip(starts, _attn_chunks(units, bias_ref)):
                o_ref[r0:r0 + CHUNK, :] = o

    assert chunks * CHUNK >= ATT_REACH
    pl.when(s == 0)(functools.partial(run, True))
    pl.when(s != 0)(functools.partial(run, False))


def _attn_prompt_call(q, kext, vext, bias):
    rows = q.shape[0]
    tq = ATT_REACH
    blk = lambda off: pl.BlockSpec((tq, HEADS_DIM), lambda s: (s + off, 0))
    return pl.pallas_call(
        functools.partial(_attn_prompt_kernel, chunks=tq // CHUNK),
        grid=(rows // tq,),
        in_specs=[blk(0), blk(0), blk(1), blk(0), blk(1), _const_spec(bias.shape)],
        out_specs=blk(0),
        out_shape=jax.ShapeDtypeStruct((rows, HEADS_DIM), BF16),
        scratch_shapes=[pltpu.VMEM((2 * tq, HEADS_DIM), BF16), pltpu.VMEM((2 * tq, HEADS_DIM), BF16)],
        compiler_params=pltpu.CompilerParams(dimension_semantics=("arbitrary",)),
        name="attn_prompt",
    )(q, kext, kext, vext, vext, bias)


def _attn_sample_kernel(q_ref, k_ref, v_ref, ck_ref, cv_ref, bias_ref, o_ref):
    def band(cache_ref, new_ref, rs):
        return lambda sl: jnp.concatenate([cache_ref[:, sl], new_ref[rs, sl]], axis=0)

    units = []
    for s in range(ck_ref.shape[0]):
        rs = slice(s * CHUNK, (s + 1) * CHUNK)
        units.append((q_ref[rs, :], band(ck_ref.at[s], k_ref, rs), band(cv_ref.at[s], v_ref, rs),
                      None))
    for s, o in enumerate(_attn_chunks(units, bias_ref)):
        o_ref[s * CHUNK:(s + 1) * CHUNK, :] = o


def _attn_sample_call(q, k, v, cache_k, cache_v, bias):
    per = 2
    nb = cache_k.shape[0] // per
    new = pl.BlockSpec((per * CHUNK, HEADS_DIM), lambda b: (b, 0))
    cache = pl.BlockSpec((per, ATT_REACH, HEADS_DIM), lambda b: (b, 0, 0))
    return pl.pallas_call(
        _attn_sample_kernel,
        grid=(nb,),
        in_specs=[new, new, new, cache, cache, _const_spec(bias.shape)],
        out_specs=new,
        out_shape=jax.ShapeDtypeStruct(q.shape, BF16),
        compiler_params=pltpu.CompilerParams(dimension_semantics=("arbitrary",)),
        name="attn_sample",
    )(q, k, v, cache_k, cache_v, bias)


def _same_head():
    r = lax.broadcasted_iota(jnp.int32, (GROUP, GROUP), 0) // HEAD
    c = lax.broadcasted_iota(jnp.int32, (GROUP, GROUP), 1) // HEAD
    return r == c


def _blk(x, same_head):
    return jnp.where(same_head, jnp.concatenate([x] * 4, axis=0), 0.0).astype(BF16)


def _wkv_prep(units, hooks):
    same_head = _same_head()
    t64 = lax.broadcasted_iota(jnp.int32, (CHUNK, GROUP), 0)
    i64 = lax.broadcasted_iota(jnp.int32, (CHUNK, GROUP), 1) % HEAD
    strict = i64 < t64
    incl = i64 <= t64
    pending = list(hooks)

    def stage_done():
        if pending:
            pending.pop(0)()

    def blk(x):
        return _blk(x, same_head)

    n, a_ak, a_rb, a_rk = [], [], [], []
    for ld in units:
        lhs = jnp.concatenate([ld("at"), ld("rt")], axis=0).astype(BF16)
        rhs = jnp.concatenate([blk(ld("bt")), blk(ld("kt"))], axis=0)
        a_all = _dot_nt(lhs, rhs)
        n.append(jnp.where(strict, a_all[0:CHUNK, 0:GROUP], 0.0))
        a_ak.append(jnp.where(strict, a_all[0:CHUNK, GROUP:], 0.0))
        a_rb.append(jnp.where(incl, a_all[CHUNK:, 0:GROUP], 0.0))
        a_rk.append(jnp.where(incl, a_all[CHUNK:, GROUP:], 0.0))
    stage_done()

    x0 = [_dot(a.astype(BF16), blk(ld("v"))) for a, ld in zip(a_ak, units)]
    npow = [_dot(m.astype(BF16), blk(m)) for m in n]
    t = [jnp.where(i64 == t64, 1.0, 0.0) + m for m in n]
    stage_done()
    for _ in range(4):
        prod = [_dot(jnp.concatenate([a, b], axis=0).astype(BF16), blk(b))
                for a, b in zip(t, npow)]
        t = [a + p[0:CHUNK] for a, p in zip(t, prod)]
        npow = [p[CHUNK:] for p in prod]
        stage_done()
    t = [a + _dot(a.astype(BF16), blk(b)) for a, b in zip(t, npow)]
    stage_done()
    res = [_dot(a.astype(BF16), jnp.concatenate([blk(ld("at")), blk(x)], axis=1))
           for a, x, ld in zip(t, x0, units)]
    stage_done()
    while pending:
        stage_done()
    return [(r[:, 0:GROUP], r[:, GROUP:], b, k) for r, b, k in zip(res, a_rb, a_rk)]


def _wkv_step_stages(units, states, out):
    same_head = _same_head()
    held = {}

    def blk(x):
        return _blk(x, same_head)

    def first():
        held["sblk"] = [blk(st) for st in states]
        held["u"] = [_dot(ld("ah").astype(BF16), sb) + ld("vh")
                     for ld, sb in zip(units, held["sblk"])]

    def second():
        for i, ld in enumerate(units):
            u, v = held["u"][i], ld("v")
            y = _dot(jnp.concatenate([ld("rt"), ld("arb"), ld("ark")], axis=1).astype(BF16),
                     jnp.concatenate([held["sblk"][i], blk(u), blk(v)], axis=0))
            g = _dot_tn(jnp.concatenate([ld("btd"), ld("ktd")], axis=0).astype(BF16),
                        jnp.concatenate([u, v], axis=0).astype(BF16))
            g = jnp.where(same_head, g, 0.0)
            delta = g[0:64] + g[64:128] + g[128:192] + g[192:256]
            out.append(y)
            states[i] = states[i] * ld("fdec") + delta

    return [first, second]


def _wkv_kernel(p_ref, pn_ref, shift0_ref, state0_ref, mu_ref, w0_ref, wwa_ref, a0_ref, gup_ref,
                kkw_ref, kaw_ref, rkw_ref, gng_ref, gnb_ref, ones_ref, ltri_ref,
                out_ref, shift_out_ref, state_out_ref,
                st_scr, rt_scr, at_scr, kt_scr, bt_scr, v_scr, btd_scr, ktd_scr,
                fdec_scr, ah_scr, vh_scr, arb_scr, ark_scr, g_scr, bon_scr, *, ts, steps):
    j = pl.program_id(0)
    streams = state0_ref.shape[0]
    assert streams == 1 or (steps == 1 and ts == streams * CHUNK)

    if streams == 1:
        @pl.when(j == 0)
        def _():
            st_scr[...] = state0_ref[0]

    def bdsum(x, terms):
        ones = ones_ref[...]
        halves = []
        for c0 in range(0, HEADS_DIM, GROUP):
            parts = _split3(x[:, c0:c0 + GROUP])[:terms]
            acc = _dot(parts[0], ones)
            for part in parts[1:]:
                acc = acc + _dot(part, ones)
            halves.append(acc)
        return jnp.concatenate(halves, axis=1)

    pq = ltri_ref.shape[0]
    per = pq // CHUNK
    n_piece = ts // pq

    def pre_stages(q, upcoming=False):
        r0 = q * pq
        rows = slice(r0, r0 + pq)
        src_ref = pn_ref if upcoming else p_ref
        held = {}

        def shifted(c0, w):
            cols = slice(c0, c0 + w)
            row = lax.broadcasted_iota(jnp.int32, (pq, 1), 0)
            if streams > 1:
                first = shift0_ref[q * per:(q + 1) * per, :, cols]
                prev_row = jnp.broadcast_to(first, (per, CHUNK, w)).reshape(pq, w)
                top = row % CHUNK == 0
            else:
                if upcoming:
                    prev_row = p_ref[ts - 8:ts, cols][7:8]
                elif q == 0:
                    prev_row = shift0_ref[0][:, cols]
                else:
                    prev_row = p_ref[r0 - 8:r0, cols][7:8]
                top = row == 0
            pb = src_ref[rows, cols]
            prev = jnp.where(top, prev_row, pltpu.roll(pb, 1, 0))
            return pb + (prev - pb) * mu_ref[:, cols]

        def s_lora():
            lora = shifted(3 * HEADS_DIM, 256)
            lwla = lora[:, 0:128]
            lane = lax.broadcasted_iota(jnp.int32, (pq, 128), 1)
            held["raw"] = _dot(jnp.where(lane < LORA_W, jnp.tanh(lwla), lwla).astype(BF16),
                               wwa_ref[...])
            g_scr[rows, :] = _dot(_sigmoid(lora[:, 128:256]).astype(BF16), gup_ref[...])
            held["k"] = shifted(HEADS_DIM, HEADS_DIM)

        def s_decay():
            w_pre = w0_ref[...] + held["raw"][:, 0:HEADS_DIM]
            softplus = jnp.maximum(-w_pre, 0.0) + jnp.log(1.0 + jnp.exp(-jnp.abs(w_pre)))
            lw = -jnp.exp(-softplus - 0.5)
            ltri = ltri_ref[...]
            hi, mid, _ = _split3(lw)
            held["cum"] = _dot(ltri, hi) + _dot(ltri, mid)
            held["lw"] = lw

        def s_keys():
            k = held["k"]
            a = _sigmoid(a0_ref[...] + held["raw"][:, HEADS_DIM:])
            kk = k * kkw_ref[...]
            held["nrm"] = bdsum(kk * kk, 1)
            held.update(a=a, kk=kk, k2=k * (1.0 + (a - 1.0) * kaw_ref[...]))

        def s_bonus():
            r = shifted(0, HEADS_DIM)
            held["bon"] = bdsum(r * held["k2"] * rkw_ref[...], 1)
            rt_scr[rows, :] = r * jnp.exp(held["cum"])

        def s_norm():
            cum = held["cum"]
            kk = held["kk"] / jnp.maximum(jnp.sqrt(held["nrm"]), 1e-12)
            cum3 = cum.reshape(per, CHUNK, HEADS_DIM)
            cend = jnp.broadcast_to(cum3[:, CHUNK - 1:CHUNK, :], cum3.shape).reshape(pq, HEADS_DIM)
            diag = (lax.broadcasted_iota(jnp.int32, (pq, HEADS_DIM), 0) % CHUNK
                    == lax.broadcasted_iota(jnp.int32, (pq, HEADS_DIM), 1) % HEAD)
            held["fdec"] = bdsum(jnp.where(diag, jnp.exp(cend), 0.0), 2)
            at_scr[rows, :] = -kk * jnp.exp(cum - held["lw"])
            held.update(kk=kk, cend=cend)

        def s_inv():
            e_inv = jnp.exp(-held["cum"])
            kt_scr[rows, :] = held["k2"] * e_inv
            bt_scr[rows, :] = held["kk"] * held["a"] * e_inv

        def s_rel():
            e_rel = jnp.exp(held["cend"] - held["cum"])
            ktd_scr[rows, :] = held["k2"] * e_rel
            btd_scr[rows, :] = held["kk"] * held["a"] * e_rel
            fdec_scr[rows, :] = held["fdec"]

        def s_value():
            v = shifted(2 * HEADS_DIM, HEADS_DIM)
            v_scr[rows, :] = v
            bon_scr[rows, :] = held["bon"] * v

        return [s_lora, s_decay, s_keys, s_bonus, s_norm, s_inv, s_rel, s_value]

    def chunk_rows(c):
        return slice(c * CHUNK, (c + 1) * CHUNK)

    groups = [slice(g0, g0 + GROUP) for g0 in range(0, HEADS_DIM, GROUP)]

    named = dict(at=at_scr, rt=rt_scr, kt=kt_scr, bt=bt_scr, v=v_scr, btd=btd_scr, ktd=ktd_scr,
                 fdec=fdec_scr, ah=ah_scr, vh=vh_scr, arb=arb_scr, ark=ark_scr)
    prep_outs = (ah_scr, vh_scr, arb_scr, ark_scr)

    def loader(c, cols):
        rows = chunk_rows(c)
        return lambda name: named[name][rows, cols]

    def run(step_piece, prep_piece, extra):
        step_chunks = [] if step_piece is None else [step_piece * per + i for i in range(per)]
        prep_chunks = [] if prep_piece is None else [prep_piece * per + i for i in range(per)]
        stages, ys, finals = [], [], []
        if streams == 1:
            states = [st_scr[:, cols] for cols in groups] if step_chunks else []
            for c in step_chunks:
                out = []
                ys.append(out)
                stages += _wkv_step_stages([loader(c, cols) for cols in groups], states, out)
            finals = [(st_scr, states)] if step_chunks else []
        else:
            firsts, seconds = [], []
            for c in step_chunks:
                out = []
                ys.append(out)
                states = [state0_ref[c, :, cols] for cols in groups]
                first, second = _wkv_step_stages(
                    [loader(c, cols) for cols in groups], states, out)
                firsts.append(first)
                seconds.append(second)
                finals.append((state_out_ref.at[c], states))
            if step_chunks:
                stages += [lambda: [f() for f in firsts], lambda: [f() for f in seconds]]
        if step_chunks:
            rows = slice(step_piece * pq, (step_piece + 1) * pq)
            gn = {}

            def gn_mean():
                gn["y"] = jnp.concatenate([jnp.concatenate(out, axis=1) for out in ys], axis=0)
                gn["sum"] = bdsum(gn["y"], 1)

            def gn_var():
                gn["d"] = gn["y"] - gn["sum"] * (1.0 / HEAD)
                gn["sq"] = bdsum(gn["d"] * gn["d"], 1)

            def gn_out():
                var = gn["sq"] * (1.0 / HEAD)
                yn = (gn["d"] * lax.rsqrt(var + GN_EPS) * gng_ref[...] + gnb_ref[...]
                      + bon_scr[rows, :])
                out_ref[rows, :] = (yn * g_scr[rows, :]).astype(BF16)

            stages += [gn_mean, gn_var, gn_out]
        hooks = []
        for i in range(max(len(stages), len(extra))):
            both = stages[i:i + 1] + extra[i:i + 1]
            hooks.append(lambda both=both: [f() for f in both])
        units = [(c, cols) for c in prep_chunks for cols in groups]
        prep_out = _wkv_prep([loader(c, cols) for c, cols in units], hooks)
        for ref, states in finals:
            for cols, st in zip(groups, states):
                ref[:, cols] = st
        for (c, cols), outs in zip(units, prep_out):
            for ref, val in zip(prep_outs, outs):
                ref[chunk_rows(c), cols] = val

    look_ahead = steps > 1
    ahead_pass = max(2, n_piece - 1)
    assert ahead_pass <= n_piece or not look_ahead

    def first_piece():
        for stage in pre_stages(0):
            stage()

    if look_ahead:
        pl.when(j == 0)(first_piece)
    else:
        first_piece()
    for piece in range(n_piece + 1):
        if piece + 1 < n_piece:
            extra = pre_stages(piece + 1)
        elif piece == ahead_pass and look_ahead:
            extra = pre_stages(0, upcoming=True)
        else:
            extra = []
        run(piece - 1 if piece >= 1 else None, piece if piece < n_piece else None, extra)

    if streams == 1:
        @pl.when(j == steps - 1)
        def _():
            shift_out_ref[0] = p_ref[ts - 1:ts, :]
            state_out_ref[0] = st_scr[...]
    else:
        for s in range(streams):
            shift_out_ref[s] = p_ref[(s + 1) * CHUNK - 1:(s + 1) * CHUNK, :]


def _wkv_call(p2d, shift0, state0, prm, steps, ts):
    rows = p2d.shape[0]
    streams = state0.shape[0]
    row_map = lambda j: (j, 0)
    next_map = lambda j: (jnp.minimum(j + 1, steps - 1), 0)
    seq3 = lambda j: (0, 0, 0)
    vec = _const_spec((1, HEADS_DIM))
    big = pltpu.VMEM((ts, HEADS_DIM), F32)
    ltri = prm["ltri"][min(ts, WKV_PIECE)]
    return pl.pallas_call(
        functools.partial(_wkv_kernel, ts=ts, steps=steps),
        grid=(steps,),
        in_specs=[
            pl.BlockSpec((ts, RW_COLS), row_map),
            pl.BlockSpec((ts, RW_COLS), next_map),
            pl.BlockSpec((streams, 1, RW_COLS), seq3),
            pl.BlockSpec((streams, HEAD, HEADS_DIM), seq3),
            _const_spec((1, RW_COLS)),
            vec,
            _const_spec((128, 2 * HEADS_DIM)),
            vec,
            _const_spec((LORA_G, HEADS_DIM)),
            vec, vec, vec, vec, vec,
            _const_spec((GROUP, GROUP)),
            _const_spec(ltri.shape),
        ],
        out_specs=[
            pl.BlockSpec((ts, HEADS_DIM), row_map),
            pl.BlockSpec((streams, 1, RW_COLS), seq3),
            pl.BlockSpec((streams, HEAD, HEADS_DIM), seq3),
        ],
        out_shape=[
            jax.ShapeDtypeStruct((rows, HEADS_DIM), BF16),
            jax.ShapeDtypeStruct((streams, 1, RW_COLS), F32),
            jax.ShapeDtypeStruct((streams, HEAD, HEADS_DIM), F32),
        ],
        scratch_shapes=[
            pltpu.VMEM((HEAD, HEADS_DIM), F32),
        ] + [big] * 14,
        compiler_params=pltpu.CompilerParams(
            dimension_semantics=("arbitrary",), vmem_limit_bytes=VMEM_LIMIT),
        name="wkv",
    )(p2d, p2d, shift0, state0, prm["mu"], prm["w0"], prm["wwa"], prm["a0"], prm["gup"],
      prm["kk"], prm["ka"], prm["rk"], prm["gng"], prm["gnb"], prm["ones"], ltri)


def _mixffn_kernel(xn_ref, att_ref, rw_ref, gate_ref, mod_ref, convp_ref,
                   ln1g_ref, ln1b_ref, ln2g_ref, ln2b_ref,
                   wa_ref, wr_ref, wo_ref, wup_ref, cw_ref, cb_ref, wdn_ref,
                   y_ref, convo_ref, carry_scr, yb_scr, *, tm, steps):
    j = pl.program_id(0)
    groups = mod_ref.shape[0]
    assert groups == 1 or steps == 1

    if groups == 1:
        @pl.when(j == 0)
        def _():
            carry_scr[8 - (CONV_W - 1):8, :] = convp_ref[0]

    def mod(idx):
        return _mod_row(mod_ref, idx, tm)

    half = tm // 2
    subs = [slice(0, half), slice(half, tm)]

    def modr(idx, rs):
        m = mod(idx)
        return m if m.shape[0] == 1 else m[rs]

    ma = [_dot(att_ref[rs, :], wa_ref[...]) for rs in subs]
    mr = [_dot(rw_ref[rs, :], wr_ref[...]) for rs in subs]
    merged = [(gate_ref[rs, 0:D_MODEL] * a + gate_ref[rs, D_MODEL:] * r).astype(BF16)
              for rs, a, r in zip(subs, ma, mr)]
    mix = [_dot(m, wo_ref[...]) for m in merged]
    x1 = [_layer_norm(ALPHA * xn_ref[rs, :] + (1.0 + modr(2, rs)) * m, ln1g_ref[...], ln1b_ref[...])
          for rs, m in zip(subs, mix)]
    h2 = jnp.concatenate([(a * (1.0 + modr(4, rs)) + modr(3, rs)).astype(BF16)
                          for rs, a in zip(subs, x1)], axis=0)

    cw_blk = 256
    glen = tm // groups
    rb = FFN_ROWS if groups == 1 else tm
    row8 = lax.broadcasted_iota(jnp.int32, (8, cw_blk), 0)
    grow = lax.broadcasted_iota(jnp.int32, (tm, cw_blk), 0) % glen

    def up(blk):
        c, r0 = blk
        h = h2[r0:r0 + rb]
        return (_dot(h, wup_ref[:, c:c + cw_blk]),
                _dot(h, wup_ref[:, D_FF + c:D_FF + c + cw_blk]))

    blocks = [(c, r0) for c in range(0, D_FF, cw_blk) for r0 in range(0, tm, rb)]
    for c, r0 in blocks:
        uc, uv = up((c, r0))
        cs = slice(c, c + cw_blk)
        r1 = pltpu.roll(uc, 1, 0)
        r2 = pltpu.roll(uc, 2, 0)
        if groups == 1:
            if r0 == 0:
                c6 = carry_scr[6:7, cs]
                c7 = carry_scr[7:8, cs]
            s1 = jnp.concatenate([jnp.where(row8 == 0, c7, r1[0:8]), r1[8:]], axis=0)
            s2 = jnp.concatenate(
                [jnp.where(row8 == 0, c6, jnp.where(row8 == 1, c7, r2[0:8])), r2[8:]], axis=0)
            c6 = uc[rb - 2:rb - 1, :]
            c7 = uc[rb - 1:rb, :]
            if r0 + rb == tm:
                carry_scr[:, cs] = uc[rb - 8:rb, :]
            tail = uc[rb - (CONV_W - 1):rb, :][None]
        else:
            hist = jnp.broadcast_to(convp_ref[:, :, cs][:, :, None, :],
                                    (groups, CONV_W - 1, glen, cw_blk))
            c6 = hist[:, 0].reshape(tm, cw_blk)
            c7 = hist[:, 1].reshape(tm, cw_blk)
            s1 = jnp.where(grow == 0, c7, r1)
            s2 = jnp.where(grow == 0, c6, jnp.where(grow == 1, c7, r2))
            tail = uc.reshape(groups, glen, cw_blk)[:, glen - (CONV_W - 1):, :]
        conv = cb_ref[:, cs] + s2 * cw_ref[0:1, cs] + s1 * cw_ref[1:2, cs] + uc * cw_ref[2:3, cs]
        yb_scr[r0:r0 + rb, cs] = (conv * _sigmoid(conv) * uv).astype(BF16)

        if r0 + rb == tm:
            convo_ref[:, :, cs] = tail

    ff = [_dot(yb_scr[rs, :], wdn_ref[...]) for rs in subs]
    for rs, a, f in zip(subs, x1, ff):
        y_ref[rs, :] = _layer_norm(ALPHA * a + (1.0 + modr(5, rs)) * f, ln2g_ref[...], ln2b_ref[...])


def _mixffn_call(xn2d, att, rw, gates, mod, conv_prev, prm, steps, tm):
    rows = xn2d.shape[0]
    groups = mod.shape[0]
    row_map = lambda j: (j, 0)
    vec = _const_spec((1, D_MODEL))
    return pl.pallas_call(
        functools.partial(_mixffn_kernel, tm=tm, steps=steps),
        grid=(steps,),
        in_specs=[
            pl.BlockSpec((tm, D_MODEL), row_map),
            pl.BlockSpec((tm, HEADS_DIM), row_map),
            pl.BlockSpec((tm, HEADS_DIM), row_map),
            pl.BlockSpec((tm, GATE_COLS), row_map),
            _const_spec((groups, 6, D_MODEL)),
            _const_spec((groups, CONV_W - 1, D_FF)),
            vec, vec, vec, vec,
            _const_spec((HEADS_DIM, D_MODEL)),
            _const_spec((HEADS_DIM, D_MODEL)),
            _const_spec((D_MODEL, D_MODEL)),
            _const_spec((D_MODEL, 2 * D_FF)),
            _const_spec((CONV_W, D_FF)),
            _const_spec((1, D_FF)),
            _const_spec((D_FF, D_MODEL)),
        ],
        out_specs=[
            pl.BlockSpec((tm, D_MODEL), row_map),
            pl.BlockSpec((groups, CONV_W - 1, D_FF), lambda j: (0, 0, 0)),
        ],
        out_shape=[
            jax.ShapeDtypeStruct((rows, D_MODEL), F32),
            jax.ShapeDtypeStruct((groups, CONV_W - 1, D_FF), F32),
        ],
        scratch_shapes=[pltpu.VMEM((8, D_FF), F32), pltpu.VMEM((tm, D_FF), BF16)],
        compiler_params=pltpu.CompilerParams(
            dimension_semantics=("arbitrary",), vmem_limit_bytes=VMEM_LIMIT),
        name="mixffn",
    )(xn2d, att, rw, gates, mod, conv_prev,
      prm["ln1g"], prm["ln1b"], prm["ln2g"], prm["ln2b"],
      prm["wa"], prm["wr"], prm["wo"], prm["wup"], prm["cw"], prm["cb"], prm["wdn"])


def _pair_bias(table):
    assert CHUNK - 1 <= REL_CLIP
    top = ATT_REACH + CHUNK - 1
    n_far = top - REL_CLIP + 1
    far = jnp.broadcast_to(table[:, 2 * REL_CLIP:], (N_HEADS, n_far))
    lo_idx = top - (BAND + CHUNK - 2) + REL_CLIP
    near = table[:, lo_idx:2 * REL_CLIP][:, ::-1]
    ext = jnp.concatenate([far, near], axis=1).astype(F32) * LOG2E
    n_ext = BAND + CHUNK - 1
    period = jnp.concatenate([ext, jnp.zeros((N_HEADS, 1), F32)], axis=1)
    skew = jnp.tile(period, (1, CHUNK))[:, :CHUNK * n_ext].reshape(N_HEADS, CHUNK, n_ext)
    bias = skew[:, :, CHUNK - 1:CHUNK - 1 + BAND]
    return bias.reshape(N_HEADS // 2, 2 * CHUNK, BAND)


def _chunk_ltri(ts):
    t = jnp.arange(ts)
    return ((t[:, None] // CHUNK == t[None, :] // CHUNK) & (t[None, :] <= t[:, None])).astype(BF16)


def _trunk(x2d, mod, shift0, state0, conv_prev, caches, prm, n_seq):
    prompt = caches is None
    rows = x2d.shape[0]
    tm = min(rows, ROW_TILE)
    steps = rows // tm
    q, k, v, kv32, p, gates, xn = _inproj_call(
        x2d, mod, prm["lnig"], prm["lnib"], prm["win"], steps, tm, prompt)
    if prompt:
        att = _attn_prompt_call(q, k, v, prm["bias"])
    else:
        att = _attn_sample_call(q, k, v, *caches, prm["bias"])
    assert state0.shape[0] == n_seq
    rw, shift, state = _wkv_call(p, shift0, state0, prm, steps, tm)
    y, conv = _mixffn_call(xn, att, rw, gates, mod, conv_prev, prm, steps, tm)
    return y, kv32, state, shift, conv


def kernel(x_prompt, x_sample, cache_attn_k, cache_attn_v, state_rwkv, state_shift, state_conv,
           c_prompt, c_sample, ln_in_g, ln_in_b, w_ada, b_ada, w_in, attn_rel_bias,
           rwkv_mu, rwkv_w0, rwkv_w_up, rwkv_a0, rwkv_a_up, rwkv_g_up, rwkv_k_k, rwkv_k_a,
           rwkv_r_k, rwkv_gn_g, rwkv_gn_b, w_branch_attn, w_branch_rwkv, w_out,
           ln1_g, ln1_b, ln2_g, ln2_b, w_ffn_up, ffn_conv_w, ffn_conv_b, w_ffn_down):
    bp, sp, _ = x_prompt.shape
    bs, ss, _ = x_sample.shape
    assert bp == 1 and ss == CHUNK and w_ada.shape[0] == DEPTH
    assert sp % ROW_TILE == 0 and bs * ss <= ROW_TILE and cache_attn_k.shape[2] == ATT_REACH

    row = lambda a: a.reshape(1, -1)
    wwa = jnp.zeros((LORA_W + LORA_A, 2 * HEADS_DIM), F32)
    wwa = wwa.at[:LORA_W, :HEADS_DIM].set(rwkv_w_up[0]).at[LORA_W:, HEADS_DIM:].set(rwkv_a_up[0])
    head_id = jnp.arange(GROUP) // HEAD
    prm = dict(
        lnig=row(ln_in_g), lnib=row(ln_in_b),
        ln1g=row(ln1_g[0]), ln1b=row(ln1_b[0]), ln2g=row(ln2_g[0]), ln2b=row(ln2_b[0]),
        win=w_in[0].astype(BF16), bias=_pair_bias(attn_rel_bias[0]),
        mu=row(rwkv_mu[0]), w0=row(rwkv_w0[0]), wwa=wwa.astype(BF16), a0=row(rwkv_a0[0]),
        gup=rwkv_g_up[0].astype(BF16), kk=row(rwkv_k_k[0]), ka=row(rwkv_k_a[0]),
        rk=row(rwkv_r_k[0]), gng=row(rwkv_gn_g[0]), gnb=row(rwkv_gn_b[0]),
        ones=(head_id[:, None] == head_id[None, :]).astype(BF16),
        ltri={WKV_PIECE: _chunk_ltri(WKV_PIECE), CHUNK: _chunk_ltri(CHUNK)},
        wa=w_branch_attn[0].astype(BF16), wr=w_branch_rwkv[0].astype(BF16),
        wo=w_out[0].astype(BF16), wup=w_ffn_up[0].astype(BF16),
        cw=ffn_conv_w[0], cb=row(ffn_conv_b[0]), wdn=w_ffn_down[0].astype(BF16),
    )

    n_c = bp + bs
    c_all = jnp.concatenate([c_prompt, c_sample, jnp.zeros((16 - n_c, D_MODEL), F32)], axis=0)
    mod = _mod_call(c_all, w_ada[0], row(b_ada[0])).reshape(16, 6, D_MODEL)

    y_p, kv_p, st_p, sh_p, cv_p = _trunk(
        x_prompt.reshape(sp, D_MODEL), mod[0:bp],
        jnp.zeros((bp, 1, RW_COLS), F32), jnp.zeros((bp, HEAD, HEADS_DIM), F32),
        jnp.zeros((bp, CONV_W - 1, D_FF), F32), None, prm, n_seq=bp)

    caches = (cache_attn_k[0].astype(BF16).reshape(bs, ATT_REACH, HEADS_DIM),
              cache_attn_v[0].astype(BF16).reshape(bs, ATT_REACH, HEADS_DIM))
    st0 = jnp.transpose(state_rwkv[0], (0, 3, 1, 2)).reshape(bs, HEAD, HEADS_DIM)
    y_s, kv_s, st_s, sh_s, cv_s = _trunk(
        x_sample.reshape(bs * ss, D_MODEL), mod[bp:n_c],
        state_shift[0], st0, state_conv[0], caches, prm, n_seq=bs)

    def state_out(st, b):
        return jnp.transpose(st.reshape(b, HEAD, N_HEADS, HEAD), (0, 2, 3, 1))[None]

    hs = (N_HEADS, HEAD)
    return (
        y_p.reshape(bp, sp, D_MODEL),
        y_s.reshape(bs, ss, D_MODEL),
        kv_p[:, :HEADS_DIM].reshape(1, bp, ATT_REACH, *hs),
        kv_p[:, HEADS_DIM:].reshape(1, bp, ATT_REACH, *hs),
        kv_s[:, :HEADS_DIM].reshape(1, bs, ss, *hs),
        kv_s[:, HEADS_DIM:].reshape(1, bs, ss, *hs),
        state_out(st_p, bp),
        state_out(st_s, bs),
        sh_p[None],
        sh_s[None],
        cv_p[None],
        cv_s[None],
    )
```

```python
import functools

import jax
import jax.numpy as jnp
from jax import lax
from jax.experimental import pallas as pl
from jax.experimental.pallas import tpu as pltpu

F32 = jnp.float32
BF16 = jnp.bfloat16

D_MODEL = 1024
CHUNK = 64
ATT_REACH = 512
BAND = ATT_REACH + CHUNK
N_HEADS = 8
HEAD = 64
HEADS_DIM = N_HEADS * HEAD
REL_CLIP = 128
LORA_W = 64
LORA_A = 64
LORA_G = 128
ATT_COLS = 3 * HEADS_DIM
RW_COLS = 3 * HEADS_DIM + LORA_W + LORA_A + LORA_G
GATE_COLS = 2 * D_MODEL
D_FF = 2816
CONV_W = 3
LN_EPS = 1e-5
GN_EPS = 64e-5
DEPTH = 1
ALPHA = (2 * DEPTH) ** 0.25
LOG2E = 1.4426950408889634

GROUP = 256
ROW_TILE = 512
WKV_PIECE = 256
VMEM_LIMIT = 56 * 1024 * 1024


def _const_spec(shape):
    nd = len(shape)
    return pl.BlockSpec(shape, lambda *_: (0,) * nd, pipeline_mode=pl.Buffered(1))


def _layer_norm(x, g, b):
    mu = jnp.mean(x, axis=-1, keepdims=True)
    xc = x - mu
    var = jnp.mean(xc * xc, axis=-1, keepdims=True)
    return xc * lax.rsqrt(var + LN_EPS) * g + b


def _sigmoid(x):
    return 1.0 / (1.0 + jnp.exp(-x))


def _split3(x):
    hi = x.astype(BF16)
    r1 = x - hi.astype(F32)
    mid = r1.astype(BF16)
    lo = (r1 - mid.astype(F32)).astype(BF16)
    return hi, mid, lo


def _dot(a, b):
    return jnp.dot(a, b, preferred_element_type=F32)


def _dot_nt(a, b):
    return lax.dot_general(a, b, (((1,), (1,)), ((), ())), preferred_element_type=F32)


def _dot_tn(a, b):
    return lax.dot_general(a, b, (((0,), (0,)), ((), ())), preferred_element_type=F32)


def _mod_kernel(c_ref, w_ref, b_ref, o_ref):
    c = c_ref[...]
    s = (c * _sigmoid(c)).astype(BF16)
    o_ref[...] = _dot(s, w_ref[...].astype(BF16)) + b_ref[...]


def _mod_call(c_all, w_ada, b_ada):
    n = c_all.shape[0]
    nblk = 6
    return pl.pallas_call(
        _mod_kernel,
        grid=(nblk,),
        in_specs=[
            pl.BlockSpec((n, D_MODEL), lambda i: (0, 0)),
            pl.BlockSpec((D_MODEL, D_MODEL), lambda i: (0, i)),
            pl.BlockSpec((1, D_MODEL), lambda i: (0, i)),
        ],
        out_specs=pl.BlockSpec((n, D_MODEL), lambda i: (0, i)),
        out_shape=jax.ShapeDtypeStruct((n, 6 * D_MODEL), F32),
        compiler_params=pltpu.CompilerParams(dimension_semantics=("arbitrary",)),
        name="mod",
    )(c_all, w_ada, b_ada)


def _mod_row(mod_ref, idx, tm):
    groups, _, d = mod_ref.shape
    m = mod_ref[:, idx:idx + 1, :]
    if groups == 1:
        return m[0]
    return jnp.broadcast_to(m, (groups, tm // groups, d)).reshape(tm, d)


def _inproj_kernel(x_ref, mod_ref, lng_ref, lnb_ref, w_ref,
                   q_ref, k_ref, v_ref, k32_ref, v32_ref, p_ref, g_ref, xn_ref, *, lead):
    tm = x_ref.shape[0]
    j = pl.program_id(0)

    @pl.when(j < lead)
    def _():
        k_ref[...] = jnp.zeros_like(k_ref)
        v_ref[...] = jnp.zeros_like(v_ref)

    def half_tile(rs):
        def mrow(idx):
            m = _mod_row(mod_ref, idx, tm)
            return m if m.shape[0] == 1 else m[rs]

        xn = _layer_norm(x_ref[rs, :], lng_ref[...], lnb_ref[...])
        xn_ref[rs, :] = xn
        hb = (xn * (1.0 + mrow(1)) + mrow(0)).astype(BF16)

        def seg(a, b):
            return _dot(hb, w_ref[:, a:b])

        q_ref[rs, :] = (seg(0, HEADS_DIM) * (HEAD ** -0.5 * LOG2E)).astype(BF16)
        k = seg(HEADS_DIM, 2 * HEADS_DIM)
        k_ref[rs, :] = k.astype(BF16)
        k32_ref[rs, :] = k
        v = seg(2 * HEADS_DIM, 3 * HEADS_DIM)
        v_ref[rs, :] = v.astype(BF16)
        v32_ref[rs, :] = v
        for c in range(0, RW_COLS, 256):
            p_ref[rs, c:c + 256] = seg(ATT_COLS + c, ATT_COLS + c + 256)
        g0 = ATT_COLS + RW_COLS
        for c in range(0, GATE_COLS, 512):
            g_ref[rs, c:c + 512] = _sigmoid(seg(g0 + c, g0 + c + 512))

    @pl.when(j >= lead)
    def _():
        half_tile(slice(0, tm // 2))
        half_tile(slice(tm // 2, tm))


def _inproj_call(x2d, mod, ln_g, ln_b, w_in_b, steps, tm, prompt):
    rows = x2d.shape[0]
    in_cols = w_in_b.shape[1]
    groups = mod.shape[0]
    if prompt:
        lead = ATT_REACH // tm
        kv_rows = ATT_REACH
        row_map = lambda j: (jnp.maximum(j - lead, 0), 0)
        kv_map = lambda j: (jnp.maximum(j - steps, 0), 0)
        ext_map = lambda j: (j, 0)
    else:
        lead = 0
        kv_rows = rows
        row_map = kv_map = ext_map = lambda j: (j, 0)
    return pl.pallas_call(
        functools.partial(_inproj_kernel, lead=lead),
        grid=(steps + lead,),
        in_specs=[
            pl.BlockSpec((tm, D_MODEL), row_map),
            _const_spec((groups, 6, D_MODEL)),
            _const_spec((1, D_MODEL)),
            _const_spec((1, D_MODEL)),
            _const_spec((D_MODEL, in_cols)),
        ],
        out_specs=[
            pl.BlockSpec((tm, HEADS_DIM), row_map),
            pl.BlockSpec((tm, HEADS_DIM), ext_map),
            pl.BlockSpec((tm, HEADS_DIM), ext_map),
            pl.BlockSpec((tm, HEADS_DIM), kv_map),
            pl.BlockSpec((tm, HEADS_DIM), kv_map),
            pl.BlockSpec((tm, RW_COLS), row_map),
            pl.BlockSpec((tm, GATE_COLS), row_map),
            pl.BlockSpec((tm, D_MODEL), row_map),
        ],
        out_shape=[
            jax.ShapeDtypeStruct((rows, HEADS_DIM), BF16),
            jax.ShapeDtypeStruct((rows + lead * tm, HEADS_DIM), BF16),
            jax.ShapeDtypeStruct((rows + lead * tm, HEADS_DIM), BF16),
            jax.ShapeDtypeStruct((kv_rows, HEADS_DIM), F32),
            jax.ShapeDtypeStruct((kv_rows, HEADS_DIM), F32),
            jax.ShapeDtypeStruct((rows, RW_COLS), F32),
            jax.ShapeDtypeStruct((rows, GATE_COLS), F32),
            jax.ShapeDtypeStruct((rows, D_MODEL), F32),
        ],
        compiler_params=pltpu.CompilerParams(
            dimension_semantics=("arbitrary",), vmem_limit_bytes=VMEM_LIMIT),
        name="inproj",
    )(x2d, mod, ln_g, ln_b, w_in_b)


def _attn_chunks(chunks, bias_ref):
    lane = lax.broadcasted_iota(jnp.int32, (CHUNK, 128), 1)
    first = lane < HEAD
    pairs = [slice(pr * 128, (pr + 1) * 128) for pr in range(N_HEADS // 2)]
    scores = []
    for qc, kb, _, _ in chunks:
        for sl in pairs:
            q2 = qc[:, sl].astype(F32)
            qs = jnp.concatenate([jnp.where(first, q2, 0.0), jnp.where(first, 0.0, q2)],
                                 axis=0).astype(BF16)
            scores.append(_dot_nt(qs, kb(sl)))
    probs, sums = [], []
    for i, s in enumerate(scores):
        thr = chunks[i // len(pairs)][3]
        s = s + bias_ref[i % len(pairs)]
        if thr is not None:
            col = lax.broadcasted_iota(jnp.int32, s.shape, 1)
            s = jnp.where(col >= thr, s, -jnp.inf)
        e = jnp.exp2(s - jnp.max(s, axis=1, keepdims=True))
        sums.append(jnp.sum(e, axis=1, keepdims=True))
        probs.append(e.astype(BF16))
    outs = []
    for ci, (_, _, vb, _) in enumerate(chunks):
        cols = []
        for pi, sl in enumerate(pairs):
            i = ci * len(pairs) + pi
            o = _dot(probs[i], vb(sl)) / sums[i]
            cols.append(jnp.where(first, o[0:CHUNK], o[CHUNK:2 * CHUNK]))
        outs.append(jnp.concatenate(cols, axis=1).astype(BF16))
    return outs


def _attn_prompt_kernel(q_ref, ka_ref, kb_ref, va_ref, vb_ref, bias_ref, o_ref, kbuf, vbuf,
                        *, chunks):
    tq = chunks * CHUNK
    kbuf[0:tq, :] = ka_ref[...]
    kbuf[tq:2 * tq, :] = kb_ref[...]
    vbuf[0:tq, :] = va_ref[...]
    vbuf[tq:2 * tq, :] = vb_ref[...]
    s = pl.program_id(0)

    per = 2

    def run(masked):
        for i in range(chunks // per):
            units, starts = [], []
            for k in range(per):
                g = i * per + k
                r0 = g * CHUNK
                thr = ATT_REACH - (s * chunks + g) * CHUNK if masked else None
                band = slice(r0, r0 + BAND)
                units.append((q_ref[r0:r0 + CHUNK, :],
                              lambda sl, band=band: kbuf[band, sl],
                              lambda sl, band=band: vbuf[band, sl], thr))
                starts.append(r0)
            for r0, o in zip(starts, _attn_chunks(units, bias_ref)):
                o_ref[r0:r0 + CHUNK, :] = o

    assert chunks * CHUNK >= ATT_REACH
    pl.when(s == 0)(functools.partial(run, True))
    pl.when(s != 0)(functools.partial(run, False))


def _attn_prompt_call(q, kext, vext, bias):
    rows = q.shape[0]
    tq = ATT_REACH
    blk = lambda off: pl.BlockSpec((tq, HEADS_DIM), lambda s: (s + off, 0))
    return pl.pallas_call(
        functools.partial(_attn_prompt_kernel, chunks=tq // CHUNK),
        grid=(rows // tq,),
        in_specs=[blk(0), blk(0), blk(1), blk(0), blk(1), _const_spec(bias.shape)],
        out_specs=blk(0),
        out_shape=jax.ShapeDtypeStruct((rows, HEADS_DIM), BF16),
        scratch_shapes=[pltpu.VMEM((2 * tq, HEADS_DIM), BF16), pltpu.VMEM((2 * tq, HEADS_DIM), BF16)],
        compiler_params=pltpu.CompilerParams(dimension_semantics=("arbitrary",)),
        name="attn_prompt",
    )(q, kext, kext, vext, vext, bias)


def _attn_sample_kernel(q_ref, k_ref, v_ref, ck_ref, cv_ref, bias_ref, o_ref):
    def band(cache_ref, new_ref, rs):
        return lambda sl: jnp.concatenate([cache_ref[:, sl], new_ref[rs, sl]], axis=0)

    units = []
    for s in range(ck_ref.shape[0]):
        rs = slice(s * CHUNK, (s + 1) * CHUNK)
        units.append((q_ref[rs, :], band(ck_ref.at[s], k_ref, rs), band(cv_ref.at[s], v_ref, rs),
                      None))
    for s, o in enumerate(_attn_chunks(units, bias_ref)):
        o_ref[s * CHUNK:(s + 1) * CHUNK, :] = o


def _attn_sample_call(q, k, v, cache_k, cache_v, bias):
    per = 2
    nb = cache_k.shape[0] // per
    new = pl.BlockSpec((per * CHUNK, HEADS_DIM), lambda b: (b, 0))
    cache = pl.BlockSpec((per, ATT_REACH, HEADS_DIM), lambda b: (b, 0, 0))
    return pl.pallas_call(
        _attn_sample_kernel,
        grid=(nb,),
        in_specs=[new, new, new, cache, cache, _const_spec(bias.shape)],
        out_specs=new,
        out_shape=jax.ShapeDtypeStruct(q.shape, BF16),
        compiler_params=pltpu.CompilerParams(dimension_semantics=("arbitrary",)),
        name="attn_sample",
    )(q, k, v, cache_k, cache_v, bias)


def _same_head():
    r = lax.broadcasted_iota(jnp.int32, (GROUP, GROUP), 0) // HEAD
    c = lax.broadcasted_iota(jnp.int32, (GROUP, GROUP), 1) // HEAD
    return r == c


def _blk(x, same_head):
    return jnp.where(same_head, jnp.concatenate([x] * 4, axis=0), 0.0).astype(BF16)


def _wkv_prep(units, hooks):
    same_head = _same_head()
    t64 = lax.broadcasted_iota(jnp.int32, (CHUNK, GROUP), 0)
    i64 = lax.broadcasted_iota(jnp.int32, (CHUNK, GROUP), 1) % HEAD
    strict = i64 < t64
    incl = i64 <= t64
    pending = list(hooks)

    def stage_done():
        if pending:
            pending.pop(0)()

    def blk(x):
        return _blk(x, same_head)

    n, a_ak, a_rb, a_rk = [], [], [], []
    for ld in units:
        lhs = jnp.concatenate([ld("at"), ld("rt")], axis=0).astype(BF16)
        rhs = jnp.concatenate([blk(ld("bt")), blk(ld("kt"))], axis=0)
        a_all = _dot_nt(lhs, rhs)
        n.append(jnp.where(strict, a_all[0:CHUNK, 0:GROUP], 0.0))
        a_ak.append(jnp.where(strict, a_all[0:CHUNK, GROUP:], 0.0))
        a_rb.append(jnp.where(incl, a_all[CHUNK:, 0:GROUP], 0.0))
        a_rk.append(jnp.where(incl, a_all[CHUNK:, GROUP:], 0.0))
    stage_done()

    x0 = [_dot(a.astype(BF16), blk(ld("v"))) for a, ld in zip(a_ak, units)]
    npow = [_dot(m.astype(BF16), blk(m)) for m in n]
    t = [jnp.where(i64 == t64, 1.0, 0.0) + m for m in n]
    stage_done()
    for _ in range(4):
        prod = [_dot(jnp.concatenate([a, b], axis=0).astype(BF16), blk(b))
                for a, b in zip(t, npow)]
        t = [a + p[0:CHUNK] for a, p in zip(t, prod)]
        npow = [p[CHUNK:] for p in prod]
        stage_done()
    t = [a + _dot(a.astype(BF16), blk(b)) for a, b in zip(t, npow)]
    stage_done()
    res = [_dot(a.astype(BF16), jnp.concatenate([blk(ld("at")), blk(x)], axis=1))
           for a, x, ld in zip(t, x0, units)]
    stage_done()
    while pending:
        stage_done()
    return [(r[:, 0:GROUP], r[:, GROUP:], b, k) for r, b, k in zip(res, a_rb, a_rk)]


def _wkv_step_stages(units, states, out):
    same_head = _same_head()
    held = {}

    def blk(x):
        return _blk(x, same_head)

    def first():
        held["sblk"] = [blk(st) for st in states]
        held["u"] = [_dot(ld("ah").astype(BF16), sb) + ld("vh")
                     for ld, sb in zip(units, held["sblk"])]

    def second():
        for i, ld in enumerate(units):
            u, v = held["u"][i], ld("v")
            y = _dot(jnp.concatenate([ld("rt"), ld("arb"), ld("ark")], axis=1).astype(BF16),
                     jnp.concatenate([held["sblk"][i], blk(u), blk(v)], axis=0))
            g = _dot_tn(jnp.concatenate([ld("btd"), ld("ktd")], axis=0).astype(BF16),
                        jnp.concatenate([u, v], axis=0).astype(BF16))
            g = jnp.where(same_head, g, 0.0)
            delta = g[0:64] + g[64:128] + g[128:192] + g[192:256]
            out.append(y)
            states[i] = states[i] * ld("fdec") + delta

    return [first, second]


def _wkv_kernel(p_ref, pn_ref, shift0_ref, state0_ref, mu_ref, w0_ref, wwa_ref, a0_ref, gup_ref,
                kkw_ref, kaw_ref, rkw_ref, gng_ref, gnb_ref, ones_ref, ltri_ref,
                out_ref, shift_out_ref, state_out_ref,
                st_scr, rt_scr, at_scr, kt_scr, bt_scr, v_scr, btd_scr, ktd_scr,
                fdec_scr, ah_scr, vh_scr, arb_scr, ark_scr, g_scr, bon_scr, *, ts, steps):
    j = pl.program_id(0)
    streams = state0_ref.shape[0]
    assert streams == 1 or (steps == 1 and ts == streams * CHUNK)

    if streams == 1:
        @pl.when(j == 0)
        def _():
            st_scr[...] = state0_ref[0]

    def bdsum(x, terms):
        ones = ones_ref[...]
        halves = []
        for c0 in range(0, HEADS_DIM, GROUP):
            parts = _split3(x[:, c0:c0 + GROUP])[:terms]
            acc = _dot(parts[0], ones)
            for part in parts[1:]:
                acc = acc + _dot(part, ones)
            halves.append(acc)
        return jnp.concatenate(halves, axis=1)

    pq = ltri_ref.shape[0]
    per = pq // CHUNK
    n_piece = ts // pq

    def pre_stages(q, upcoming=False):
        r0 = q * pq
        rows = slice(r0, r0 + pq)
        src_ref = pn_ref if upcoming else p_ref
        held = {}

        def shifted(c0, w):
            cols = slice(c0, c0 + w)
            row = lax.broadcasted_iota(jnp.int32, (pq, 1), 0)
            if streams > 1:
                first = shift0_ref[q * per:(q + 1) * per, :, cols]
                prev_row = jnp.broadcast_to(first, (per, CHUNK, w)).reshape(pq, w)
                top = row % CHUNK == 0
            else:
                if upcoming:
                    prev_row = p_ref[ts - 8:ts, cols][7:8]
                elif q == 0:
                    prev_row = shift0_ref[0][:, cols]
                else:
                    prev_row = p_ref[r0 - 8:r0, cols][7:8]
                top = row == 0
            pb = src_ref[rows, cols]
            prev = jnp.where(top, prev_row, pltpu.roll(pb, 1, 0))
            return pb + (prev - pb) * mu_ref[:, cols]

        def s_lora():
            lora = shifted(3 * HEADS_DIM, 256)
            lwla = lora[:, 0:128]
            lane = lax.broadcasted_iota(jnp.int32, (pq, 128), 1)
            held["raw"] = _dot(jnp.where(lane < LORA_W, jnp.tanh(lwla), lwla).astype(BF16),
                               wwa_ref[...])
            g_scr[rows, :] = _dot(_sigmoid(lora[:, 128:256]).astype(BF16), gup_ref[...])
            held["k"] = shifted(HEADS_DIM, HEADS_DIM)

        def s_decay():
            w_pre = w0_ref[...] + held["raw"][:, 0:HEADS_DIM]
            softplus = jnp.maximum(-w_pre, 0.0) + jnp.log(1.0 + jnp.exp(-jnp.abs(w_pre)))
            lw = -jnp.exp(-softplus - 0.5)
            ltri = ltri_ref[...]
            hi, mid, _ = _split3(lw)
            held["cum"] = _dot(ltri, hi) + _dot(ltri, mid)
            held["lw"] = lw

        def s_keys():
            k = held["k"]
            a = _sigmoid(a0_ref[...] + held["raw"][:, HEADS_DIM:])
            kk = k * kkw_ref[...]
            held["nrm"] = bdsum(kk * kk, 1)
            held.update(a=a, kk=kk, k2=k * (1.0 + (a - 1.0) * kaw_ref[...]))

        def s_bonus():
            r = shifted(0, HEADS_DIM)
            held["bon"] = bdsum(r * held["k2"] * rkw_ref[...], 1)
            rt_scr[rows, :] = r * jnp.exp(held["cum"])

        def s_norm():
            cum = held["cum"]
            kk = held["kk"] / jnp.maximum(jnp.sqrt(held["nrm"]), 1e-12)
            cum3 = cum.reshape(per, CHUNK, HEADS_DIM)
            cend = jnp.broadcast_to(cum3[:, CHUNK - 1:CHUNK, :], cum3.shape).reshape(pq, HEADS_DIM)
            diag = (lax.broadcasted_iota(jnp.int32, (pq, HEADS_DIM), 0) % CHUNK
                    == lax.broadcasted_iota(jnp.int32, (pq, HEADS_DIM), 1) % HEAD)
            held["fdec"] = bdsum(jnp.where(diag, jnp.exp(cend), 0.0), 2)
            at_scr[rows, :] = -kk * jnp.exp(cum - held["lw"])
            held.update(kk=kk, cend=cend)

        def s_inv():
            e_inv = jnp.exp(-held["cum"])
            kt_scr[rows, :] = held["k2"] * e_inv
            bt_scr[rows, :] = held["kk"] * held["a"] * e_inv

        def s_rel():
            e_rel = jnp.exp(held["cend"] - held["cum"])
            ktd_scr[rows, :] = held["k2"] * e_rel
            btd_scr[rows, :] = held["kk"] * held["a"] * e_rel
            fdec_scr[rows, :] = held["fdec"]

        def s_value():
            v = shifted(2 * HEADS_DIM, HEADS_DIM)
            v_scr[rows, :] = v
            bon_scr[rows, :] = held["bon"] * v

        return [s_lora, s_decay, s_keys, s_bonus, s_norm, s_inv, s_rel, s_value]

    def chunk_rows(c):
        return slice(c * CHUNK, (c + 1) * CHUNK)

    groups = [slice(g0, g0 + GROUP) for g0 in range(0, HEADS_DIM, GROUP)]

    named = dict(at=at_scr, rt=rt_scr, kt=kt_scr, bt=bt_scr, v=v_scr, btd=btd_scr, ktd=ktd_scr,
                 fdec=fdec_scr, ah=ah_scr, vh=vh_scr, arb=arb_scr, ark=ark_scr)
    prep_outs = (ah_scr, vh_scr, arb_scr, ark_scr)

    def loader(c, cols):
        rows = chunk_rows(c)
        return lambda name: named[name][rows, cols]

    def run(step_piece, prep_piece, extra):
        step_chunks = [] if step_piece is None else [step_piece * per + i for i in range(per)]
        prep_chunks = [] if prep_piece is None else [prep_piece * per + i for i in range(per)]
        stages, ys, finals = [], [], []
        if streams == 1:
            states = [st_scr[:, cols] for cols in groups] if step_chunks else []
            for c in step_chunks:
                out = []
                ys.append(out)
                stages += _wkv_step_stages([loader(c, cols) for cols in groups], states, out)
            finals = [(st_scr, states)] if step_chunks else []
        else:
            firsts, seconds = [], []
            for c in step_chunks:
                out = []
                ys.append(out)
                states = [state0_ref[c, :, cols] for cols in groups]
                first, second = _wkv_step_stages(
                    [loader(c, cols) for cols in groups], states, out)
                firsts.append(first)
                seconds.append(second)
                finals.append((state_out_ref.at[c], states))
            if step_chunks:
                stages += [lambda: [f() for f in firsts], lambda: [f() for f in seconds]]
        if step_chunks:
            rows = slice(step_piece * pq, (step_piece + 1) * pq)
            gn = {}

            def gn_mean():
                gn["y"] = jnp.concatenate([jnp.concatenate(out, axis=1) for out in ys], axis=0)
                gn["sum"] = bdsum(gn["y"], 1)

            def gn_var():
                gn["d"] = gn["y"] - gn["sum"] * (1.0 / HEAD)
                gn["sq"] = bdsum(gn["d"] * gn["d"], 1)

            def gn_out():
                var = gn["sq"] * (1.0 / HEAD)
                yn = (gn["d"] * lax.rsqrt(var + GN_EPS) * gng_ref[...] + gnb_ref[...]
                      + bon_scr[rows, :])
                out_ref[rows, :] = (yn * g_scr[rows, :]).astype(BF16)

            stages += [gn_mean, gn_var, gn_out]
        hooks = []
        for i in range(max(len(stages), len(extra))):
            both = stages[i:i + 1] + extra[i:i + 1]
            hooks.append(lambda both=both: [f() for f in both])
        units = [(c, cols) for c in prep_chunks for cols in groups]
        prep_out = _wkv_prep([loader(c, cols) for c, cols in units], hooks)
        for ref, states in finals:
            for cols, st in zip(groups, states):
                ref[:, cols] = st
        for (c, cols), outs in zip(units, prep_out):
            for ref, val in zip(prep_outs, outs):
                ref[chunk_rows(c), cols] = val

    look_ahead = steps > 1
    ahead_pass = max(2, n_piece - 1)
    assert ahead_pass <= n_piece or not look_ahead

    def first_piece():
        for stage in pre_stages(0):
            stage()

    if look_ahead:
        pl.when(j == 0)(first_piece)
    else:
        first_piece()
    for piece in range(n_piece + 1):
        if piece + 1 < n_piece:
            extra = pre_stages(piece + 1)
        elif piece == ahead_pass and look_ahead:
            extra = pre_stages(0, upcoming=True)
        else:
            extra = []
        run(piece - 1 if piece >= 1 else None, piece if piece < n_piece else None, extra)

    if streams == 1:
        @pl.when(j == steps - 1)
        def _():
            shift_out_ref[0] = p_ref[ts - 1:ts, :]
            state_out_ref[0] = st_scr[...]
    else:
        for s in range(streams):
            shift_out_ref[s] = p_ref[(s + 1) * CHUNK - 1:(s + 1) * CHUNK, :]


def _wkv_call(p2d, shift0, state0, prm, steps, ts):
    rows = p2d.shape[0]
    streams = state0.shape[0]
    row_map = lambda j: (j, 0)
    next_map = lambda j: (jnp.minimum(j + 1, steps - 1), 0)
    seq3 = lambda j: (0, 0, 0)
    vec = _const_spec((1, HEADS_DIM))
    big = pltpu.VMEM((ts, HEADS_DIM), F32)
    ltri = prm["ltri"][min(ts, WKV_PIECE)]
    return pl.pallas_call(
        functools.partial(_wkv_kernel, ts=ts, steps=steps),
        grid=(steps,),
        in_specs=[
            pl.BlockSpec((ts, RW_COLS), row_map),
            pl.BlockSpec((ts, RW_COLS), next_map),
            pl.BlockSpec((streams, 1, RW_COLS), seq3),
            pl.BlockSpec((streams, HEAD, HEADS_DIM), seq3),
            _const_spec((1, RW_COLS)),
            vec,
            _const_spec((128, 2 * HEADS_DIM)),
            vec,
            _const_spec((LORA_G, HEADS_DIM)),
            vec, vec, vec, vec, vec,
            _const_spec((GROUP, GROUP)),
            _const_spec(ltri.shape),
        ],
        out_specs=[
            pl.BlockSpec((ts, HEADS_DIM), row_map),
            pl.BlockSpec((streams, 1, RW_COLS), seq3),
            pl.BlockSpec((streams, HEAD, HEADS_DIM), seq3),
        ],
        out_shape=[
            jax.ShapeDtypeStruct((rows, HEADS_DIM), BF16),
            jax.ShapeDtypeStruct((streams, 1, RW_COLS), F32),
            jax.ShapeDtypeStruct((streams, HEAD, HEADS_DIM), F32),
        ],
        scratch_shapes=[
            pltpu.VMEM((HEAD, HEADS_DIM), F32),
        ] + [big] * 14,
        compiler_params=pltpu.CompilerParams(
            dimension_semantics=("arbitrary",), vmem_limit_bytes=VMEM_LIMIT),
        name="wkv",
    )(p2d, p2d, shift0, state0, prm["mu"], prm["w0"], prm["wwa"], prm["a0"], prm["gup"],
      prm["kk"], prm["ka"], prm["rk"], prm["gng"], prm["gnb"], prm["ones"], ltri)


def _mixffn_kernel(xn_ref, att_ref, rw_ref, gate_ref, mod_ref, convp_ref,
                   ln1g_ref, ln1b_ref, ln2g_ref, ln2b_ref,
                   wa_ref, wr_ref, wo_ref, wup_ref, cw_ref, cb_ref, wdn_ref,
                   y_ref, convo_ref, carry_scr, yb_scr, *, tm, steps):
    j = pl.program_id(0)
    groups = mod_ref.shape[0]
    assert groups == 1 or steps == 1

    if groups == 1:
        @pl.when(j == 0)
        def _():
            carry_scr[8 - (CONV_W - 1):8, :] = convp_ref[0]

    def mod(idx):
        return _mod_row(mod_ref, idx, tm)

    half = tm // 2
    subs = [slice(0, half), slice(half, tm)]

    def modr(idx, rs):
        m = mod(idx)
        return m if m.shape[0] == 1 else m[rs]

    ma = [_dot(att_ref[rs, :], wa_ref[...]) for rs in subs]
    mr = [_dot(rw_ref[rs, :], wr_ref[...]) for rs in subs]
    merged = [(gate_ref[rs, 0:D_MODEL] * a + gate_ref[rs, D_MODEL:] * r).astype(BF16)
              for rs, a, r in zip(subs, ma, mr)]
    mix = [_dot(m, wo_ref[...]) for m in merged]
    x1 = [_layer_norm(ALPHA * xn_ref[rs, :] + (1.0 + modr(2, rs)) * m, ln1g_ref[...], ln1b_ref[...])
          for rs, m in zip(subs, mix)]
    h2 = jnp.concatenate([(a * (1.0 + modr(4, rs)) + modr(3, rs)).astype(BF16)
                          for rs, a in zip(subs, x1)], axis=0)

    cw_blk = 256
    glen = tm // groups
    row8 = lax.broadcasted_iota(jnp.int32, (8, cw_blk), 0)
    grow = lax.broadcasted_iota(jnp.int32, (tm, cw_blk), 0) % glen
    for c in range(0, D_FF, cw_blk):
        cs = slice(c, c + cw_blk)
        uc = _dot(h2, wup_ref[:, cs])
        uv = _dot(h2, wup_ref[:, D_FF + c:D_FF + c + cw_blk])
        r1 = pltpu.roll(uc, 1, 0)
        r2 = pltpu.roll(uc, 2, 0)
        if groups == 1:
            c6 = carry_scr[6:7, cs]
            c7 = carry_scr[7:8, cs]
            s1 = jnp.concatenate([jnp.where(row8 == 0, c7, r1[0:8]), r1[8:]], axis=0)
            s2 = jnp.concatenate(
                [jnp.where(row8 == 0, c6, jnp.where(row8 == 1, c7, r2[0:8])), r2[8:]], axis=0)
            carry_scr[:, cs] = uc[tm - 8:tm, :]
            tail = uc[tm - (CONV_W - 1):tm, :][None]
        else:
            hist = jnp.broadcast_to(convp_ref[:, :, cs][:, :, None, :],
                                    (groups, CONV_W - 1, glen, cw_blk))
            c6 = hist[:, 0].reshape(tm, cw_blk)
            c7 = hist[:, 1].reshape(tm, cw_blk)
            s1 = jnp.where(grow == 0, c7, r1)
            s2 = jnp.where(grow == 0, c6, jnp.where(grow == 1, c7, r2))
            tail = uc.reshape(groups, glen, cw_blk)[:, glen - (CONV_W - 1):, :]
        conv = cb_ref[:, cs] + s2 * cw_ref[0:1, cs] + s1 * cw_ref[1:2, cs] + uc * cw_ref[2:3, cs]
        yb_scr[:, cs] = (conv * _sigmoid(conv) * uv).astype(BF16)
        convo_ref[:, :, cs] = tail

    ff = [_dot(yb_scr[rs, :], wdn_ref[...]) for rs in subs]
    for rs, a, f in zip(subs, x1, ff):
        y_ref[rs, :] = _layer_norm(ALPHA * a + (1.0 + modr(5, rs)) * f, ln2g_ref[...], ln2b_ref[...])


def _mixffn_call(xn2d, att, rw, gates, mod, conv_prev, prm, steps, tm):
    rows = xn2d.shape[0]
    groups = mod.shape[0]
    row_map = lambda j: (j, 0)
    vec = _const_spec((1, D_MODEL))
    return pl.pallas_call(
        functools.partial(_mixffn_kernel, tm=tm, steps=steps),
        grid=(steps,),
        in_specs=[
            pl.BlockSpec((tm, D_MODEL), row_map),
            pl.BlockSpec((tm, HEADS_DIM), row_map),
            pl.BlockSpec((tm, HEADS_DIM), row_map),
            pl.BlockSpec((tm, GATE_COLS), row_map),
            _const_spec((groups, 6, D_MODEL)),
            _const_spec((groups, CONV_W - 1, D_FF)),
            vec, vec, vec, vec,
            _const_spec((HEADS_DIM, D_MODEL)),
            _const_spec((HEADS_DIM, D_MODEL)),
            _const_spec((D_MODEL, D_MODEL)),
            _const_spec((D_MODEL, 2 * D_FF)),
            _const_spec((CONV_W, D_FF)),
            _const_spec((1, D_FF)),
            _const_spec((D_FF, D_MODEL)),
        ],
        out_specs=[
            pl.BlockSpec((tm, D_MODEL), row_map),
            pl.BlockSpec((groups, CONV_W - 1, D_FF), lambda j: (0, 0, 0)),
        ],
        out_shape=[
            jax.ShapeDtypeStruct((rows, D_MODEL), F32),
            jax.ShapeDtypeStruct((groups, CONV_W - 1, D_FF), F32),
        ],
        scratch_shapes=[pltpu.VMEM((8, D_FF), F32), pltpu.VMEM((tm, D_FF), BF16)],
        compiler_params=pltpu.CompilerParams(
            dimension_semantics=("arbitrary",), vmem_limit_bytes=VMEM_LIMIT),
        name="mixffn",
    )(xn2d, att, rw, gates, mod, conv_prev,
      prm["ln1g"], prm["ln1b"], prm["ln2g"], prm["ln2b"],
      prm["wa"], prm["wr"], prm["wo"], prm["wup"], prm["cw"], prm["cb"], prm["wdn"])


def _pair_bias(table):
    assert CHUNK - 1 <= REL_CLIP
    top = ATT_REACH + CHUNK - 1
    n_far = top - REL_CLIP + 1
    far = jnp.broadcast_to(table[:, 2 * REL_CLIP:], (N_HEADS, n_far))
    lo_idx = top - (BAND + CHUNK - 2) + REL_CLIP
    near = table[:, lo_idx:2 * REL_CLIP][:, ::-1]
    ext = jnp.concatenate([far, near], axis=1).astype(F32) * LOG2E
    n_ext = BAND + CHUNK - 1
    period = jnp.concatenate([ext, jnp.zeros((N_HEADS, 1), F32)], axis=1)
    skew = jnp.tile(period, (1, CHUNK))[:, :CHUNK * n_ext].reshape(N_HEADS, CHUNK, n_ext)
    bias = skew[:, :, CHUNK - 1:CHUNK - 1 + BAND]
    return bias.reshape(N_HEADS // 2, 2 * CHUNK, BAND)


def _chunk_ltri(ts):
    t = jnp.arange(ts)
    return ((t[:, None] // CHUNK == t[None, :] // CHUNK) & (t[None, :] <= t[:, None])).astype(BF16)


def _trunk(x2d, mod, shift0, state0, conv_prev, caches, prm, n_seq):
    prompt = caches is None
    rows = x2d.shape[0]
    tm = min(rows, ROW_TILE)
    steps = rows // tm
    q, k, v, k32, v32, p, gates, xn = _inproj_call(
        x2d, mod, prm["lnig"], prm["lnib"], prm["win"], steps, tm, prompt)
    if prompt:
        att = _attn_prompt_call(q, k, v, prm["bias"])
    else:
        att = _attn_sample_call(q, k, v, *caches, prm["bias"])
    assert state0.shape[0] == n_seq
    rw, shift, state = _wkv_call(p, shift0, state0, prm, steps, tm)
    y, conv = _mixffn_call(xn, att, rw, gates, mod, conv_prev, prm, steps, tm)
    return y, (k32, v32), state, shift, conv


def kernel(x_prompt, x_sample, cache_attn_k, cache_attn_v, state_rwkv, state_shift, state_conv,
           c_prompt, c_sample, ln_in_g, ln_in_b, w_ada, b_ada, w_in, attn_rel_bias,
           rwkv_mu, rwkv_w0, rwkv_w_up, rwkv_a0, rwkv_a_up, rwkv_g_up, rwkv_k_k, rwkv_k_a,
           rwkv_r_k, rwkv_gn_g, rwkv_gn_b, w_branch_attn, w_branch_rwkv, w_out,
           ln1_g, ln1_b, ln2_g, ln2_b, w_ffn_up, ffn_conv_w, ffn_conv_b, w_ffn_down):
    bp, sp, _ = x_prompt.shape
    bs, ss, _ = x_sample.shape
    assert bp == 1 and ss == CHUNK and w_ada.shape[0] == DEPTH
    assert sp % ROW_TILE == 0 and bs * ss <= ROW_TILE and cache_attn_k.shape[2] == ATT_REACH

    row = lambda a: a.reshape(1, -1)
    wwa = jnp.zeros((LORA_W + LORA_A, 2 * HEADS_DIM), F32)
    wwa = wwa.at[:LORA_W, :HEADS_DIM].set(rwkv_w_up[0]).at[LORA_W:, HEADS_DIM:].set(rwkv_a_up[0])
    head_id = jnp.arange(GROUP) // HEAD
    prm = dict(
        lnig=row(ln_in_g), lnib=row(ln_in_b),
        ln1g=row(ln1_g[0]), ln1b=row(ln1_b[0]), ln2g=row(ln2_g[0]), ln2b=row(ln2_b[0]),
        win=w_in[0].astype(BF16), bias=_pair_bias(attn_rel_bias[0]),
        mu=row(rwkv_mu[0]), w0=row(rwkv_w0[0]), wwa=wwa.astype(BF16), a0=row(rwkv_a0[0]),
        gup=rwkv_g_up[0].astype(BF16), kk=row(rwkv_k_k[0]), ka=row(rwkv_k_a[0]),
        rk=row(rwkv_r_k[0]), gng=row(rwkv_gn_g[0]), gnb=row(rwkv_gn_b[0]),
        ones=(head_id[:, None] == head_id[None, :]).astype(BF16),
        ltri={WKV_PIECE: _chunk_ltri(WKV_PIECE), CHUNK: _chunk_ltri(CHUNK)},
        wa=w_branch_attn[0].astype(BF16), wr=w_branch_rwkv[0].astype(BF16),
        wo=w_out[0].astype(BF16), wup=w_ffn_up[0].astype(BF16),
        cw=ffn_conv_w[0], cb=row(ffn_conv_b[0]), wdn=w_ffn_down[0].astype(BF16),
    )

    n_c = bp + bs
    c_all = jnp.concatenate([c_prompt, c_sample, jnp.zeros((16 - n_c, D_MODEL), F32)], axis=0)
    mod = _mod_call(c_all, w_ada[0], row(b_ada[0])).reshape(16, 6, D_MODEL)

    y_p, kv_p, st_p, sh_p, cv_p = _trunk(
        x_prompt.reshape(sp, D_MODEL), mod[0:bp],
        jnp.zeros((bp, 1, RW_COLS), F32), jnp.zeros((bp, HEAD, HEADS_DIM), F32),
        jnp.zeros((bp, CONV_W - 1, D_FF), F32), None, prm, n_seq=bp)

    caches = (cache_attn_k[0].astype(BF16).reshape(bs, ATT_REACH, HEADS_DIM),
              cache_attn_v[0].astype(BF16).reshape(bs, ATT_REACH, HEADS_DIM))
    st0 = jnp.transpose(state_rwkv[0], (0, 3, 1, 2)).reshape(bs, HEAD, HEADS_DIM)
    y_s, kv_s, st_s, sh_s, cv_s = _trunk(
        x_sample.reshape(bs * ss, D_MODEL), mod[bp:n_c],
        state_shift[0], st0, state_conv[0], caches, prm, n_seq=bs)

    def state_out(st, b):
        return jnp.transpose(st.reshape(b, HEAD, N_HEADS, HEAD), (0, 2, 3, 1))[None]

    hs = (N_HEADS, HEAD)
    return (
        y_p.reshape(bp, sp, D_MODEL),
        y_s.reshape(bs, ss, D_MODEL),
        kv_p[0].reshape(1, bp, ATT_REACH, *hs),
        kv_p[1].reshape(1, bp, ATT_REACH, *hs),
        kv_s[0].reshape(1, bs, ss, *hs),
        kv_s[1].reshape(1, bs, ss, *hs),
        state_out(st_p, bp),
        state_out(st_s, bs),
        sh_p[None],
        sh_s[None],
        cv_p[None],
        cv_s[None],
    )
```

```python
import functools

import jax
import jax.numpy as jnp
from jax import lax
from jax.experimental import pallas as pl
from jax.experimental.pallas import tpu as pltpu

F32 = jnp.float32
BF16 = jnp.bfloat16

D_MODEL = 1024
CHUNK = 64
ATT_REACH = 512
BAND = ATT_REACH + CHUNK
N_HEADS = 8
HEAD = 64
HEADS_DIM = N_HEADS * HEAD
REL_CLIP = 128
LORA_W = 64
LORA_A = 64
LORA_G = 128
ATT_COLS = 3 * HEADS_DIM
RW_COLS = 3 * HEADS_DIM + LORA_W + LORA_A + LORA_G
GATE_COLS = 2 * D_MODEL
D_FF = 2816
CONV_W = 3
LN_EPS = 1e-5
GN_EPS = 64e-5
DEPTH = 1
ALPHA = (2 * DEPTH) ** 0.25
LOG2E = 1.4426950408889634

GROUP = 256
ROW_TILE = 512
WKV_PIECE = 256
VMEM_LIMIT = 56 * 1024 * 1024


def _const_spec(shape):
    nd = len(shape)
    return pl.BlockSpec(shape, lambda *_: (0,) * nd, pipeline_mode=pl.Buffered(1))


def _layer_norm(x, g, b):
    mu = jnp.mean(x, axis=-1, keepdims=True)
    xc = x - mu
    var = jnp.mean(xc * xc, axis=-1, keepdims=True)
    return xc * lax.rsqrt(var + LN_EPS) * g + b


def _sigmoid(x):
    return 1.0 / (1.0 + jnp.exp(-x))


def _split3(x):
    hi = x.astype(BF16)
    r1 = x - hi.astype(F32)
    mid = r1.astype(BF16)
    lo = (r1 - mid.astype(F32)).astype(BF16)
    return hi, mid, lo


def _dot(a, b):
    return jnp.dot(a, b, preferred_element_type=F32)


def _dot_nt(a, b):
    return lax.dot_general(a, b, (((1,), (1,)), ((), ())), preferred_element_type=F32)


def _dot_tn(a, b):
    return lax.dot_general(a, b, (((0,), (0,)), ((), ())), preferred_element_type=F32)


def _mod_kernel(c_ref, w_ref, b_ref, o_ref):
    c = c_ref[...]
    s = (c * _sigmoid(c)).astype(BF16)
    o_ref[...] = _dot(s, w_ref[...].astype(BF16)) + b_ref[...]


def _mod_call(c_all, w_ada, b_ada):
    n = c_all.shape[0]
    nblk = 6
    return pl.pallas_call(
        _mod_kernel,
        grid=(nblk,),
        in_specs=[
            pl.BlockSpec((n, D_MODEL), lambda i: (0, 0)),
            pl.BlockSpec((D_MODEL, D_MODEL), lambda i: (0, i)),
            pl.BlockSpec((1, D_MODEL), lambda i: (0, i)),
        ],
        out_specs=pl.BlockSpec((n, D_MODEL), lambda i: (0, i)),
        out_shape=jax.ShapeDtypeStruct((n, 6 * D_MODEL), F32),
        compiler_params=pltpu.CompilerParams(dimension_semantics=("arbitrary",)),
        name="mod",
    )(c_all, w_ada, b_ada)


def _mod_row(mod_ref, idx, tm):
    groups, _, d = mod_ref.shape
    m = mod_ref[:, idx:idx + 1, :]
    if groups == 1:
        return m[0]
    return jnp.broadcast_to(m, (groups, tm // groups, d)).reshape(tm, d)


def _inproj_kernel(x_ref, xnext_ref, mod_ref, lng_ref, lnb_ref, w_ref,
                   q_ref, k_ref, v_ref, k32_ref, v32_ref, p_ref, g_ref, xn_ref,
                   xna_scr, hba_scr, *, lead):
    tm = x_ref.shape[0]
    j = pl.program_id(0)

    first, second = slice(0, tm // 2), slice(tm // 2, tm)

    def front(src_ref, rs):
        def mrow(idx):
            m = _mod_row(mod_ref, idx, tm)
            return m if m.shape[0] == 1 else m[rs]

        xn = _layer_norm(src_ref[rs, :], lng_ref[...], lnb_ref[...])
        return xn, (xn * (1.0 + mrow(1)) + mrow(0)).astype(BF16)

    def half_tile(rs, prepared=False):
        if prepared:
            xn, hb = xna_scr[...], hba_scr[...]
        else:
            xn, hb = front(x_ref, rs)
        xn_ref[rs, :] = xn

        def seg(a, b):
            return _dot(hb, w_ref[:, a:b])

        g0 = ATT_COLS + RW_COLS
        for c in range(0, GATE_COLS, 512):
            g_ref[rs, c:c + 512] = _sigmoid(seg(g0 + c, g0 + c + 512))
        q_ref[rs, :] = (seg(0, HEADS_DIM) * (HEAD ** -0.5 * LOG2E)).astype(BF16)
        k = seg(HEADS_DIM, 2 * HEADS_DIM)
        k_ref[rs, :] = k.astype(BF16)
        k32_ref[rs, :] = k
        v = seg(2 * HEADS_DIM, 3 * HEADS_DIM)
        v_ref[rs, :] = v.astype(BF16)
        v32_ref[rs, :] = v
        for c in range(0, RW_COLS, 256):
            p_ref[rs, c:c + 256] = seg(ATT_COLS + c, ATT_COLS + c + 256)

    def prepare_next():
        xna_scr[...], hba_scr[...] = front(xnext_ref, first)

    @pl.when(j < lead)
    def _():
        k_ref[...] = jnp.zeros_like(k_ref)
        v_ref[...] = jnp.zeros_like(v_ref)
        prepare_next()

    @pl.when(j >= lead)
    def _():
        half_tile(first, prepared=lead > 0)
        if lead > 0:
            prepare_next()
        half_tile(second)


def _inproj_call(x2d, mod, ln_g, ln_b, w_in_b, steps, tm, prompt):
    rows = x2d.shape[0]
    in_cols = w_in_b.shape[1]
    groups = mod.shape[0]
    if prompt:
        lead = ATT_REACH // tm
        kv_rows = ATT_REACH
        row_map = lambda j: (jnp.maximum(j - lead, 0), 0)
        next_map = lambda j: (jnp.clip(j + 1 - lead, 0, steps - 1), 0)
        kv_map = lambda j: (jnp.maximum(j - steps, 0), 0)
        ext_map = lambda j: (j, 0)
    else:
        lead = 0
        kv_rows = rows
        row_map = next_map = kv_map = ext_map = lambda j: (j, 0)
    return pl.pallas_call(
        functools.partial(_inproj_kernel, lead=lead),
        grid=(steps + lead,),
        in_specs=[
            pl.BlockSpec((tm, D_MODEL), row_map),
            pl.BlockSpec((tm, D_MODEL), next_map),
            _const_spec((groups, 6, D_MODEL)),
            _const_spec((1, D_MODEL)),
            _const_spec((1, D_MODEL)),
            _const_spec((D_MODEL, in_cols)),
        ],
        scratch_shapes=[pltpu.VMEM((tm // 2, D_MODEL), F32), pltpu.VMEM((tm // 2, D_MODEL), BF16)],
        out_specs=[
            pl.BlockSpec((tm, HEADS_DIM), row_map),
            pl.BlockSpec((tm, HEADS_DIM), ext_map),
            pl.BlockSpec((tm, HEADS_DIM), ext_map),
            pl.BlockSpec((tm, HEADS_DIM), kv_map),
            pl.BlockSpec((tm, HEADS_DIM), kv_map),
            pl.BlockSpec((tm, RW_COLS), row_map),
            pl.BlockSpec((tm, GATE_COLS), row_map),
            pl.BlockSpec((tm, D_MODEL), row_map),
        ],
        out_shape=[
            jax.ShapeDtypeStruct((rows, HEADS_DIM), BF16),
            jax.ShapeDtypeStruct((rows + lead * tm, HEADS_DIM), BF16),
            jax.ShapeDtypeStruct((rows + lead * tm, HEADS_DIM), BF16),
            jax.ShapeDtypeStruct((kv_rows, HEADS_DIM), F32),
            jax.ShapeDtypeStruct((kv_rows, HEADS_DIM), F32),
            jax.ShapeDtypeStruct((rows, RW_COLS), F32),
            jax.ShapeDtypeStruct((rows, GATE_COLS), F32),
            jax.ShapeDtypeStruct((rows, D_MODEL), F32),
        ],
        compiler_params=pltpu.CompilerParams(
            dimension_semantics=("arbitrary",), vmem_limit_bytes=VMEM_LIMIT),
        name="inproj",
    )(x2d, x2d, mod, ln_g, ln_b, w_in_b)


def _attn_chunks(chunks, bias_ref):
    lane = lax.broadcasted_iota(jnp.int32, (CHUNK, 128), 1)
    first = lane < HEAD
    pairs = [slice(pr * 128, (pr + 1) * 128) for pr in range(N_HEADS // 2)]
    scores = []
    for qc, kb, _, _ in chunks:
        for sl in pairs:
            q2 = qc[:, sl].astype(F32)
            qs = jnp.concatenate([jnp.where(first, q2, 0.0), jnp.where(first, 0.0, q2)],
                                 axis=0).astype(BF16)
            scores.append(_dot_nt(qs, kb(sl)))
    probs, sums = [], []
    for i, s in enumerate(scores):
        thr = chunks[i // len(pairs)][3]
        s = s + bias_ref[i % len(pairs)]
        if thr is not None:
            col = lax.broadcasted_iota(jnp.int32, s.shape, 1)
            s = jnp.where(col >= thr, s, -jnp.inf)
        e = jnp.exp2(s - jnp.max(s, axis=1, keepdims=True))
        sums.append(jnp.sum(e, axis=1, keepdims=True))
        probs.append(e.astype(BF16))
    outs = []
    for ci, (_, _, vb, _) in enumerate(chunks):
        cols = []
        for pi, sl in enumerate(pairs):
            i = ci * len(pairs) + pi
            o = _dot(probs[i], vb(sl)) / sums[i]
            cols.append(jnp.where(first, o[0:CHUNK], o[CHUNK:2 * CHUNK]))
        outs.append(jnp.concatenate(cols, axis=1).astype(BF16))
    return outs


def _attn_prompt_kernel(q_ref, ka_ref, kb_ref, va_ref, vb_ref, bias_ref, o_ref, kbuf, vbuf,
                        *, chunks):
    tq = chunks * CHUNK
    kbuf[0:tq, :] = ka_ref[...]
    kbuf[tq:2 * tq, :] = kb_ref[...]
    vbuf[0:tq, :] = va_ref[...]
    vbuf[tq:2 * tq, :] = vb_ref[...]
    s = pl.program_id(0)

    per = 2

    def run(masked):
        for i in range(chunks // per):
            units, starts = [], []
            for k in range(per):
                g = i * per + k
                r0 = g * CHUNK
                thr = ATT_REACH - (s * chunks + g) * CHUNK if masked else None
                band = slice(r0, r0 + BAND)
                units.append((q_ref[r0:r0 + CHUNK, :],
                              lambda sl, band=band: kbuf[band, sl],
                              lambda sl, band=band: vbuf[band, sl], thr))
                starts.append(r0)
            for r0, o in zip(starts, _attn_chunks(units, bias_ref)):
                o_ref[r0:r0 + CHUNK, :] = o

    assert chunks * CHUNK >= ATT_REACH
    pl.when(s == 0)(functools.partial(run, True))
    pl.when(s != 0)(functools.partial(run, False))


def _attn_prompt_call(q, kext, vext, bias):
    rows = q.shape[0]
    tq = ATT_REACH
    blk = lambda off: pl.BlockSpec((tq, HEADS_DIM), lambda s: (s + off, 0))
    return pl.pallas_call(
        functools.partial(_attn_prompt_kernel, chunks=tq // CHUNK),
        grid=(rows // tq,),
        in_specs=[blk(0), blk(0), blk(1), blk(0), blk(1), _const_spec(bias.shape)],
        out_specs=blk(0),
        out_shape=jax.ShapeDtypeStruct((rows, HEADS_DIM), BF16),
        scratch_shapes=[pltpu.VMEM((2 * tq, HEADS_DIM), BF16), pltpu.VMEM((2 * tq, HEADS_DIM), BF16)],
        compiler_params=pltpu.CompilerParams(dimension_semantics=("arbitrary",)),
        name="attn_prompt",
    )(q, kext, kext, vext, vext, bias)


def _attn_sample_kernel(q_ref, k_ref, v_ref, ck_ref, cv_ref, bias_ref, o_ref):
    def band(cache_ref, new_ref, rs):
        return lambda sl: jnp.concatenate([cache_ref[:, sl], new_ref[rs, sl]], axis=0)

    units = []
    for s in range(ck_ref.shape[0]):
        rs = slice(s * CHUNK, (s + 1) * CHUNK)
        units.append((q_ref[rs, :], band(ck_ref.at[s], k_ref, rs), band(cv_ref.at[s], v_ref, rs),
                      None))
    for s, o in enumerate(_attn_chunks(units, bias_ref)):
        o_ref[s * CHUNK:(s + 1) * CHUNK, :] = o


def _attn_sample_call(q, k, v, cache_k, cache_v, bias):
    per = 2
    nb = cache_k.shape[0] // per
    new = pl.BlockSpec((per * CHUNK, HEADS_DIM), lambda b: (b, 0))
    cache = pl.BlockSpec((per, ATT_REACH, HEADS_DIM), lambda b: (b, 0, 0))
    return pl.pallas_call(
        _attn_sample_kernel,
        grid=(nb,),
        in_specs=[new, new, new, cache, cache, _const_spec(bias.shape)],
        out_specs=new,
        out_shape=jax.ShapeDtypeStruct(q.shape, BF16),
        compiler_params=pltpu.CompilerParams(dimension_semantics=("arbitrary",)),
        name="attn_sample",
    )(q, k, v, cache_k, cache_v, bias)


def _same_head():
    r = lax.broadcasted_iota(jnp.int32, (GROUP, GROUP), 0) // HEAD
    c = lax.broadcasted_iota(jnp.int32, (GROUP, GROUP), 1) // HEAD
    return r == c


def _blk(x, same_head):
    return jnp.where(same_head, jnp.concatenate([x] * 4, axis=0), 0.0).astype(BF16)


def _wkv_prep(units, hooks):
    same_head = _same_head()
    t64 = lax.broadcasted_iota(jnp.int32, (CHUNK, GROUP), 0)
    i64 = lax.broadcasted_iota(jnp.int32, (CHUNK, GROUP), 1) % HEAD
    strict = i64 < t64
    incl = i64 <= t64
    pending = list(hooks)

    def stage_done():
        if pending:
            pending.pop(0)()

    def blk(x):
        return _blk(x, same_head)

    n, a_ak, a_rb, a_rk = [], [], [], []
    for ld in units:
        lhs = jnp.concatenate([ld("at"), ld("rt")], axis=0).astype(BF16)
        rhs = jnp.concatenate([blk(ld("bt")), blk(ld("kt"))], axis=0)
        a_all = _dot_nt(lhs, rhs)
        n.append(jnp.where(strict, a_all[0:CHUNK, 0:GROUP], 0.0))
        a_ak.append(jnp.where(strict, a_all[0:CHUNK, GROUP:], 0.0))
        a_rb.append(jnp.where(incl, a_all[CHUNK:, 0:GROUP], 0.0))
        a_rk.append(jnp.where(incl, a_all[CHUNK:, GROUP:], 0.0))
    stage_done()

    x0 = [_dot(a.astype(BF16), blk(ld("v"))) for a, ld in zip(a_ak, units)]
    npow = [_dot(m.astype(BF16), blk(m)) for m in n]
    t = [jnp.where(i64 == t64, 1.0, 0.0) + m for m in n]
    stage_done()
    for _ in range(4):
        prod = [_dot(jnp.concatenate([a, b], axis=0).astype(BF16), blk(b))
                for a, b in zip(t, npow)]
        t = [a + p[0:CHUNK] for a, p in zip(t, prod)]
        npow = [p[CHUNK:] for p in prod]
        stage_done()
    t = [a + _dot(a.astype(BF16), blk(b)) for a, b in zip(t, npow)]
    stage_done()
    res = [_dot(a.astype(BF16), jnp.concatenate([blk(ld("at")), blk(x)], axis=1))
           for a, x, ld in zip(t, x0, units)]
    stage_done()
    while pending:
        stage_done()
    return [(r[:, 0:GROUP], r[:, GROUP:], b, k) for r, b, k in zip(res, a_rb, a_rk)]


def _wkv_step_stages(units, states, out):
    same_head = _same_head()
    held = {}

    def blk(x):
        return _blk(x, same_head)

    def first():
        held["sblk"] = [blk(st) for st in states]
        held["u"] = [_dot(ld("ah").astype(BF16), sb) + ld("vh")
                     for ld, sb in zip(units, held["sblk"])]

    def second():
        for i, ld in enumerate(units):
            u, v = held["u"][i], ld("v")
            y = _dot(jnp.concatenate([ld("rt"), ld("arb"), ld("ark")], axis=1).astype(BF16),
                     jnp.concatenate([held["sblk"][i], blk(u), blk(v)], axis=0))
            g = _dot_tn(jnp.concatenate([ld("btd"), ld("ktd")], axis=0).astype(BF16),
                        jnp.concatenate([u, v], axis=0).astype(BF16))
            g = jnp.where(same_head, g, 0.0)
            delta = g[0:64] + g[64:128] + g[128:192] + g[192:256]
            out.append(y)
            states[i] = states[i] * ld("fdec") + delta

    return [first, second]


def _wkv_kernel(p_ref, pn_ref, shift0_ref, state0_ref, mu_ref, w0_ref, wwa_ref, a0_ref, gup_ref,
                kkw_ref, kaw_ref, rkw_ref, gng_ref, gnb_ref, ones_ref, ltri_ref,
                out_ref, shift_out_ref, state_out_ref,
                st_scr, rt_scr, at_scr, kt_scr, bt_scr, v_scr, btd_scr, ktd_scr,
                fdec_scr, ah_scr, vh_scr, arb_scr, ark_scr, g_scr, bon_scr, *, ts, steps):
    j = pl.program_id(0)
    streams = state0_ref.shape[0]
    assert streams == 1 or (steps == 1 and ts == streams * CHUNK)

    if streams == 1:
        @pl.when(j == 0)
        def _():
            st_scr[...] = state0_ref[0]

    def bdsum(x, terms):
        ones = ones_ref[...]
        halves = []
        for c0 in range(0, HEADS_DIM, GROUP):
            parts = _split3(x[:, c0:c0 + GROUP])[:terms]
            acc = _dot(parts[0], ones)
            for part in parts[1:]:
                acc = acc + _dot(part, ones)
            halves.append(acc)
        return jnp.concatenate(halves, axis=1)

    pq = ltri_ref.shape[0]
    per = pq // CHUNK
    n_piece = ts // pq

    def pre_stages(q, upcoming=False):
        r0 = q * pq
        rows = slice(r0, r0 + pq)
        src_ref = pn_ref if upcoming else p_ref
        held = {}

        def shifted(c0, w):
            cols = slice(c0, c0 + w)
            row = lax.broadcasted_iota(jnp.int32, (pq, 1), 0)
            if streams > 1:
                first = shift0_ref[q * per:(q + 1) * per, :, cols]
                prev_row = jnp.broadcast_to(first, (per, CHUNK, w)).reshape(pq, w)
                top = row % CHUNK == 0
            else:
                if upcoming:
                    prev_row = p_ref[ts - 8:ts, cols][7:8]
                elif q == 0:
                    prev_row = shift0_ref[0][:, cols]
                else:
                    prev_row = p_ref[r0 - 8:r0, cols][7:8]
                top = row == 0
            pb = src_ref[rows, cols]
            prev = jnp.where(top, prev_row, pltpu.roll(pb, 1, 0))
            return pb + (prev - pb) * mu_ref[:, cols]

        def s_lora():
            lora = shifted(3 * HEADS_DIM, 256)
            lwla = lora[:, 0:128]
            lane = lax.broadcasted_iota(jnp.int32, (pq, 128), 1)
            held["raw"] = _dot(jnp.where(lane < LORA_W, jnp.tanh(lwla), lwla).astype(BF16),
                               wwa_ref[...])
            g_scr[rows, :] = _dot(_sigmoid(lora[:, 128:256]).astype(BF16), gup_ref[...])
            held["k"] = shifted(HEADS_DIM, HEADS_DIM)

        def s_decay():
            w_pre = w0_ref[...] + held["raw"][:, 0:HEADS_DIM]
            softplus = jnp.maximum(-w_pre, 0.0) + jnp.log(1.0 + jnp.exp(-jnp.abs(w_pre)))
            lw = -jnp.exp(-softplus - 0.5)
            ltri = ltri_ref[...]
            hi, mid, _ = _split3(lw)
            held["cum"] = _dot(ltri, hi) + _dot(ltri, mid)
            held["lw"] = lw

        def s_keys():
            k = held["k"]
            a = _sigmoid(a0_ref[...] + held["raw"][:, HEADS_DIM:])
            kk = k * kkw_ref[...]
            held["nrm"] = bdsum(kk * kk, 1)
            held.update(a=a, kk=kk, k2=k * (1.0 + (a - 1.0) * kaw_ref[...]))

        def s_bonus():
            r = shifted(0, HEADS_DIM)
            held["bon"] = bdsum(r * held["k2"] * rkw_ref[...], 1)
            rt_scr[rows, :] = r * jnp.exp(held["cum"])

        def s_norm():
            cum = held["cum"]
            kk = held["kk"] / jnp.maximum(jnp.sqrt(held["nrm"]), 1e-12)
            cum3 = cum.reshape(per, CHUNK, HEADS_DIM)
            cend = jnp.broadcast_to(cum3[:, CHUNK - 1:CHUNK, :], cum3.shape).reshape(pq, HEADS_DIM)
            diag = (lax.broadcasted_iota(jnp.int32, (pq, HEADS_DIM), 0) % CHUNK
                    == lax.broadcasted_iota(jnp.int32, (pq, HEADS_DIM), 1) % HEAD)
            held["fdec"] = bdsum(jnp.where(diag, jnp.exp(cend), 0.0), 2)
            at_scr[rows, :] = -kk * jnp.exp(cum - held["lw"])
            held.update(kk=kk, cend=cend)

        def s_inv():
            e_inv = jnp.exp(-held["cum"])
            kt_scr[rows, :] = held["k2"] * e_inv
            bt_scr[rows, :] = held["kk"] * held["a"] * e_inv

        def s_rel():
            e_rel = jnp.exp(held["cend"] - held["cum"])
            ktd_scr[rows, :] = held["k2"] * e_rel
            btd_scr[rows, :] = held["kk"] * held["a"] * e_rel
            fdec_scr[rows, :] = held["fdec"]

        def s_value():
            v = shifted(2 * HEADS_DIM, HEADS_DIM)
            v_scr[rows, :] = v
            bon_scr[rows, :] = held["bon"] * v

        return [s_lora, s_decay, s_keys, s_bonus, s_norm, s_inv, s_rel, s_value]

    def chunk_rows(c):
        return slice(c * CHUNK, (c + 1) * CHUNK)

    groups = [slice(g0, g0 + GROUP) for g0 in range(0, HEADS_DIM, GROUP)]

    named = dict(at=at_scr, rt=rt_scr, kt=kt_scr, bt=bt_scr, v=v_scr, btd=btd_scr, ktd=ktd_scr,
                 fdec=fdec_scr, ah=ah_scr, vh=vh_scr, arb=arb_scr, ark=ark_scr)
    prep_outs = (ah_scr, vh_scr, arb_scr, ark_scr)

    def loader(c, cols):
        rows = chunk_rows(c)
        return lambda name: named[name][rows, cols]

    def run(step_piece, prep_piece, extra):
        step_chunks = [] if step_piece is None else [step_piece * per + i for i in range(per)]
        prep_chunks = [] if prep_piece is None else [prep_piece * per + i for i in range(per)]
        stages, ys, finals = [], [], []
        if streams == 1:
            states = [st_scr[:, cols] for cols in groups] if step_chunks else []
            for c in step_chunks:
                out = []
                ys.append(out)
                stages += _wkv_step_stages([loader(c, cols) for cols in groups], states, out)
            finals = [(st_scr, states)] if step_chunks else []
        else:
            firsts, seconds = [], []
            for c in step_chunks:
                out = []
                ys.append(out)
                states = [state0_ref[c, :, cols] for cols in groups]
                first, second = _wkv_step_stages(
                    [loader(c, cols) for cols in groups], states, out)
                firsts.append(first)
                seconds.append(second)
                finals.append((state_out_ref.at[c], states))
            if step_chunks:
                stages += [lambda: [f() for f in firsts], lambda: [f() for f in seconds]]
        if step_chunks:
            rows = slice(step_piece * pq, (step_piece + 1) * pq)
            gn = {}

            def gn_mean():
                gn["y"] = jnp.concatenate([jnp.concatenate(out, axis=1) for out in ys], axis=0)
                gn["sum"] = bdsum(gn["y"], 1)

            def gn_var():
                gn["d"] = gn["y"] - gn["sum"] * (1.0 / HEAD)
                gn["sq"] = bdsum(gn["d"] * gn["d"], 1)

            def gn_out():
                var = gn["sq"] * (1.0 / HEAD)
                yn = (gn["d"] * lax.rsqrt(var + GN_EPS) * gng_ref[...] + gnb_ref[...]
                      + bon_scr[rows, :])
                out_ref[rows, :] = (yn * g_scr[rows, :]).astype(BF16)

            stages += [gn_mean, gn_var, gn_out]
        hooks = []
        for i in range(max(len(stages), len(extra))):
            both = stages[i:i + 1] + extra[i:i + 1]
            hooks.append(lambda both=both: [f() for f in both])
        units = [(c, cols) for c in prep_chunks for cols in groups]
        prep_out = _wkv_prep([loader(c, cols) for c, cols in units], hooks)
        for ref, states in finals:
            for cols, st in zip(groups, states):
                ref[:, cols] = st
        for (c, cols), outs in zip(units, prep_out):
            for ref, val in zip(prep_outs, outs):
                ref[chunk_rows(c), cols] = val

    look_ahead = steps > 1
    ahead_pass = max(2, n_piece - 1)
    assert ahead_pass <= n_piece or not look_ahead

    def first_piece():
        for stage in pre_stages(0):
            stage()

    if look_ahead:
        pl.when(j == 0)(first_piece)
    else:
        first_piece()
    for piece in range(n_piece + 1):
        if piece + 1 < n_piece:
            extra = pre_stages(piece + 1)
        elif piece == ahead_pass and look_ahead:
            extra = pre_stages(0, upcoming=True)
        else:
            extra = []
        run(piece - 1 if piece >= 1 else None, piece if piece < n_piece else None, extra)

    if streams == 1:
        @pl.when(j == steps - 1)
        def _():
            shift_out_ref[0] = p_ref[ts - 1:ts, :]
            state_out_ref[0] = st_scr[...]
    else:
        for s in range(streams):
            shift_out_ref[s] = p_ref[(s + 1) * CHUNK - 1:(s + 1) * CHUNK, :]


def _wkv_call(p2d, shift0, state0, prm, steps, ts):
    rows = p2d.shape[0]
    streams = state0.shape[0]
    row_map = lambda j: (j, 0)
    next_map = lambda j: (jnp.minimum(j + 1, steps - 1), 0)
    seq3 = lambda j: (0, 0, 0)
    vec = _const_spec((1, HEADS_DIM))
    big = pltpu.VMEM((ts, HEADS_DIM), F32)
    ltri = prm["ltri"][min(ts, WKV_PIECE)]
    return pl.pallas_call(
        functools.partial(_wkv_kernel, ts=ts, steps=steps),
        grid=(steps,),
        in_specs=[
            pl.BlockSpec((ts, RW_COLS), row_map),
            pl.BlockSpec((ts, RW_COLS), next_map),
            pl.BlockSpec((streams, 1, RW_COLS), seq3),
            pl.BlockSpec((streams, HEAD, HEADS_DIM), seq3),
            _const_spec((1, RW_COLS)),
            vec,
            _const_spec((128, 2 * HEADS_DIM)),
            vec,
            _const_spec((LORA_G, HEADS_DIM)),
            vec, vec, vec, vec, vec,
            _const_spec((GROUP, GROUP)),
            _const_spec(ltri.shape),
        ],
        out_specs=[
            pl.BlockSpec((ts, HEADS_DIM), row_map),
            pl.BlockSpec((streams, 1, RW_COLS), seq3),
            pl.BlockSpec((streams, HEAD, HEADS_DIM), seq3),
        ],
        out_shape=[
            jax.ShapeDtypeStruct((rows, HEADS_DIM), BF16),
            jax.ShapeDtypeStruct((streams, 1, RW_COLS), F32),
            jax.ShapeDtypeStruct((streams, HEAD, HEADS_DIM), F32),
        ],
        scratch_shapes=[
            pltpu.VMEM((HEAD, HEADS_DIM), F32),
        ] + [big] * 14,
        compiler_params=pltpu.CompilerParams(
            dimension_semantics=("arbitrary",), vmem_limit_bytes=VMEM_LIMIT),
        name="wkv",
    )(p2d, p2d, shift0, state0, prm["mu"], prm["w0"], prm["wwa"], prm["a0"], prm["gup"],
      prm["kk"], prm["ka"], prm["rk"], prm["gng"], prm["gnb"], prm["ones"], ltri)


def _mixffn_kernel(xn_ref, att_ref, rw_ref, gate_ref, mod_ref, convp_ref,
                   ln1g_ref, ln1b_ref, ln2g_ref, ln2b_ref,
                   wa_ref, wr_ref, wo_ref, wup_ref, cw_ref, cb_ref, wdn_ref,
                   y_ref, convo_ref, carry_scr, yb_scr, *, tm, steps):
    j = pl.program_id(0)
    groups = mod_ref.shape[0]
    assert groups == 1 or steps == 1

    if groups == 1:
        @pl.when(j == 0)
        def _():
            carry_scr[8 - (CONV_W - 1):8, :] = convp_ref[0]

    def mod(idx):
        return _mod_row(mod_ref, idx, tm)

    half = tm // 2
    subs = [slice(0, half), slice(half, tm)]

    def modr(idx, rs):
        m = mod(idx)
        return m if m.shape[0] == 1 else m[rs]

    ma = [_dot(att_ref[rs, :], wa_ref[...]) for rs in subs]
    mr = [_dot(rw_ref[rs, :], wr_ref[...]) for rs in subs]
    merged = [(gate_ref[rs, 0:D_MODEL] * a + gate_ref[rs, D_MODEL:] * r).astype(BF16)
              for rs, a, r in zip(subs, ma, mr)]
    mix = [_dot(m, wo_ref[...]) for m in merged]
    x1 = [_layer_norm(ALPHA * xn_ref[rs, :] + (1.0 + modr(2, rs)) * m, ln1g_ref[...], ln1b_ref[...])
          for rs, m in zip(subs, mix)]
    h2 = jnp.concatenate([(a * (1.0 + modr(4, rs)) + modr(3, rs)).astype(BF16)
                          for rs, a in zip(subs, x1)], axis=0)

    cw_blk = 256
    glen = tm // groups
    row8 = lax.broadcasted_iota(jnp.int32, (8, cw_blk), 0)
    grow = lax.broadcasted_iota(jnp.int32, (tm, cw_blk), 0) % glen
    for c in range(0, D_FF, cw_blk):
        cs = slice(c, c + cw_blk)
        uc = _dot(h2, wup_ref[:, cs])
        uv = _dot(h2, wup_ref[:, D_FF + c:D_FF + c + cw_blk])
        r1 = pltpu.roll(uc, 1, 0)
        r2 = pltpu.roll(uc, 2, 0)
        if groups == 1:
            c6 = carry_scr[6:7, cs]
            c7 = carry_scr[7:8, cs]
            s1 = jnp.concatenate([jnp.where(row8 == 0, c7, r1[0:8]), r1[8:]], axis=0)
            s2 = jnp.concatenate(
                [jnp.where(row8 == 0, c6, jnp.where(row8 == 1, c7, r2[0:8])), r2[8:]], axis=0)
            carry_scr[:, cs] = uc[tm - 8:tm, :]
            tail = uc[tm - (CONV_W - 1):tm, :][None]
        else:
            hist = jnp.broadcast_to(convp_ref[:, :, cs][:, :, None, :],
                                    (groups, CONV_W - 1, glen, cw_blk))
            c6 = hist[:, 0].reshape(tm, cw_blk)
            c7 = hist[:, 1].reshape(tm, cw_blk)
            s1 = jnp.where(grow == 0, c7, r1)
            s2 = jnp.where(grow == 0, c6, jnp.where(grow == 1, c7, r2))
            tail = uc.reshape(groups, glen, cw_blk)[:, glen - (CONV_W - 1):, :]
        conv = cb_ref[:, cs] + s2 * cw_ref[0:1, cs] + s1 * cw_ref[1:2, cs] + uc * cw_ref[2:3, cs]
        yb_scr[:, cs] = (conv * _sigmoid(conv) * uv).astype(BF16)
        convo_ref[:, :, cs] = tail

    ff = [_dot(yb_scr[rs, :], wdn_ref[...]) for rs in subs]
    for rs, a, f in zip(subs, x1, ff):
        y_ref[rs, :] = _layer_norm(ALPHA * a + (1.0 + modr(5, rs)) * f, ln2g_ref[...], ln2b_ref[...])


def _mixffn_call(xn2d, att, rw, gates, mod, conv_prev, prm, steps, tm):
    rows = xn2d.shape[0]
    groups = mod.shape[0]
    row_map = lambda j: (j, 0)
    vec = _const_spec((1, D_MODEL))
    return pl.pallas_call(
        functools.partial(_mixffn_kernel, tm=tm, steps=steps),
        grid=(steps,),
        in_specs=[
            pl.BlockSpec((tm, D_MODEL), row_map),
            pl.BlockSpec((tm, HEADS_DIM), row_map),
            pl.BlockSpec((tm, HEADS_DIM), row_map),
            pl.BlockSpec((tm, GATE_COLS), row_map),
            _const_spec((groups, 6, D_MODEL)),
            _const_spec((groups, CONV_W - 1, D_FF)),
            vec, vec, vec, vec,
            _const_spec((HEADS_DIM, D_MODEL)),
            _const_spec((HEADS_DIM, D_MODEL)),
            _const_spec((D_MODEL, D_MODEL)),
            _const_spec((D_MODEL, 2 * D_FF)),
            _const_spec((CONV_W, D_FF)),
            _const_spec((1, D_FF)),
            _const_spec((D_FF, D_MODEL)),
        ],
        out_specs=[
            pl.BlockSpec((tm, D_MODEL), row_map),
            pl.BlockSpec((groups, CONV_W - 1, D_FF), lambda j: (0, 0, 0)),
        ],
        out_shape=[
            jax.ShapeDtypeStruct((rows, D_MODEL), F32),
            jax.ShapeDtypeStruct((groups, CONV_W - 1, D_FF), F32),
        ],
        scratch_shapes=[pltpu.VMEM((8, D_FF), F32), pltpu.VMEM((tm, D_FF), BF16)],
        compiler_params=pltpu.CompilerParams(
            dimension_semantics=("arbitrary",), vmem_limit_bytes=VMEM_LIMIT),
        name="mixffn",
    )(xn2d, att, rw, gates, mod, conv_prev,
      prm["ln1g"], prm["ln1b"], prm["ln2g"], prm["ln2b"],
      prm["wa"], prm["wr"], prm["wo"], prm["wup"], prm["cw"], prm["cb"], prm["wdn"])


def _pair_bias(table):
    assert CHUNK - 1 <= REL_CLIP
    top = ATT_REACH + CHUNK - 1
    n_far = top - REL_CLIP + 1
    far = jnp.broadcast_to(table[:, 2 * REL_CLIP:], (N_HEADS, n_far))
    lo_idx = top - (BAND + CHUNK - 2) + REL_CLIP
    near = table[:, lo_idx:2 * REL_CLIP][:, ::-1]
    ext = jnp.concatenate([far, near], axis=1).astype(F32) * LOG2E
    n_ext = BAND + CHUNK - 1
    period = jnp.concatenate([ext, jnp.zeros((N_HEADS, 1), F32)], axis=1)
    skew = jnp.tile(period, (1, CHUNK))[:, :CHUNK * n_ext].reshape(N_HEADS, CHUNK, n_ext)
    bias = skew[:, :, CHUNK - 1:CHUNK - 1 + BAND]
    return bias.reshape(N_HEADS // 2, 2 * CHUNK, BAND)


def _chunk_ltri(ts):
    t = jnp.arange(ts)
    return ((t[:, None] // CHUNK == t[None, :] // CHUNK) & (t[None, :] <= t[:, None])).astype(BF16)


def _trunk(x2d, mod, shift0, state0, conv_prev, caches, prm, n_seq):
    prompt = caches is None
    rows = x2d.shape[0]
    tm = min(rows, ROW_TILE)
    steps = rows // tm
    q, k, v, k32, v32, p, gates, xn = _inproj_call(
        x2d, mod, prm["lnig"], prm["lnib"], prm["win"], steps, tm, prompt)
    if prompt:
        att = _attn_prompt_call(q, k, v, prm["bias"])
    else:
        att = _attn_sample_call(q, k, v, *caches, prm["bias"])
    assert state0.shape[0] == n_seq
    rw, shift, state = _wkv_call(p, shift0, state0, prm, steps, tm)
    y, conv = _mixffn_call(xn, att, rw, gates, mod, conv_prev, prm, steps, tm)
    return y, (k32, v32), state, shift, conv


def kernel(x_prompt, x_sample, cache_attn_k, cache_attn_v, state_rwkv, state_shift, state_conv,
           c_prompt, c_sample, ln_in_g, ln_in_b, w_ada, b_ada, w_in, attn_rel_bias,
           rwkv_mu, rwkv_w0, rwkv_w_up, rwkv_a0, rwkv_a_up, rwkv_g_up, rwkv_k_k, rwkv_k_a,
           rwkv_r_k, rwkv_gn_g, rwkv_gn_b, w_branch_attn, w_branch_rwkv, w_out,
           ln1_g, ln1_b, ln2_g, ln2_b, w_ffn_up, ffn_conv_w, ffn_conv_b, w_ffn_down):
    bp, sp, _ = x_prompt.shape
    bs, ss, _ = x_sample.shape
    assert bp == 1 and ss == CHUNK and w_ada.shape[0] == DEPTH
    assert sp % ROW_TILE == 0 and bs * ss <= ROW_TILE and cache_attn_k.shape[2] == ATT_REACH

    row = lambda a: a.reshape(1, -1)
    wwa = jnp.zeros((LORA_W + LORA_A, 2 * HEADS_DIM), F32)
    wwa = wwa.at[:LORA_W, :HEADS_DIM].set(rwkv_w_up[0]).at[LORA_W:, HEADS_DIM:].set(rwkv_a_up[0])
    head_id = jnp.arange(GROUP) // HEAD
    prm = dict(
        lnig=row(ln_in_g), lnib=row(ln_in_b),
        ln1g=row(ln1_g[0]), ln1b=row(ln1_b[0]), ln2g=row(ln2_g[0]), ln2b=row(ln2_b[0]),
        win=w_in[0].astype(BF16), bias=_pair_bias(attn_rel_bias[0]),
        mu=row(rwkv_mu[0]), w0=row(rwkv_w0[0]), wwa=wwa.astype(BF16), a0=row(rwkv_a0[0]),
        gup=rwkv_g_up[0].astype(BF16), kk=row(rwkv_k_k[0]), ka=row(rwkv_k_a[0]),
        rk=row(rwkv_r_k[0]), gng=row(rwkv_gn_g[0]), gnb=row(rwkv_gn_b[0]),
        ones=(head_id[:, None] == head_id[None, :]).astype(BF16),
        ltri={WKV_PIECE: _chunk_ltri(WKV_PIECE), CHUNK: _chunk_ltri(CHUNK)},
        wa=w_branch_attn[0].astype(BF16), wr=w_branch_rwkv[0].astype(BF16),
        wo=w_out[0].astype(BF16), wup=w_ffn_up[0].astype(BF16),
        cw=ffn_conv_w[0], cb=row(ffn_conv_b[0]), wdn=w_ffn_down[0].astype(BF16),
    )

    n_c = bp + bs
    c_all = jnp.concatenate([c_prompt, c_sample, jnp.zeros((16 - n_c, D_MODEL), F32)], axis=0)
    mod = _mod_call(c_all, w_ada[0], row(b_ada[0])).reshape(16, 6, D_MODEL)

    y_p, kv_p, st_p, sh_p, cv_p = _trunk(
        x_prompt.reshape(sp, D_MODEL), mod[0:bp],
        jnp.zeros((bp, 1, RW_COLS), F32), jnp.zeros((bp, HEAD, HEADS_DIM), F32),
        jnp.zeros((bp, CONV_W - 1, D_FF), F32), None, prm, n_seq=bp)

    caches = (cache_attn_k[0].astype(BF16).reshape(bs, ATT_REACH, HEADS_DIM),
              cache_attn_v[0].astype(BF16).reshape(bs, ATT_REACH, HEADS_DIM))
    st0 = jnp.transpose(state_rwkv[0], (0, 3, 1, 2)).reshape(bs, HEAD, HEADS_DIM)
    y_s, kv_s, st_s, sh_s, cv_s = _trunk(
        x_sample.reshape(bs * ss, D_MODEL), mod[bp:n_c],
        state_shift[0], st0, state_conv[0], caches, prm, n_seq=bs)

    def state_out(st, b):
        return jnp.transpose(st.reshape(b, HEAD, N_HEADS, HEAD), (0, 2, 3, 1))[None]

    hs = (N_HEADS, HEAD)
    return (
        y_p.reshape(bp, sp, D_MODEL),
        y_s.reshape(bs, ss, D_MODEL),
        kv_p[0].reshape(1, bp, ATT_REACH, *hs),
        kv_p[1].reshape(1, bp, ATT_REACH, *hs),
        kv_s[0].reshape(1, bs, ss, *hs),
        kv_s[1].reshape(1, bs, ss, *hs),
        state_out(st_p, bp),
        state_out(st_s, bs),
        sh_p[None],
        sh_s[None],
        cv_p[None],
        cv_s[None],
    )
```

```python
import functools

import jax
import jax.numpy as jnp
from jax import lax
from jax.experimental import pallas as pl
from jax.experimental.pallas import tpu as pltpu

F32 = jnp.float32
BF16 = jnp.bfloat16

D_MODEL = 1024
CHUNK = 64
ATT_REACH = 512
BAND = ATT_REACH + CHUNK
N_HEADS = 8
HEAD = 64
HEADS_DIM = N_HEADS * HEAD
REL_CLIP = 128
LORA_W = 64
LORA_A = 64
LORA_G = 128
ATT_COLS = 3 * HEADS_DIM
RW_COLS = 3 * HEADS_DIM + LORA_W + LORA_A + LORA_G
GATE_COLS = 2 * D_MODEL
D_FF = 2816
CONV_W = 3
LN_EPS = 1e-5
GN_EPS = 64e-5
DEPTH = 1
ALPHA = (2 * DEPTH) ** 0.25
LOG2E = 1.4426950408889634

GROUP = 256
ROW_TILE = 512
WKV_PIECE = 256
VMEM_LIMIT = 56 * 1024 * 1024


def _const_spec(shape):
    nd = len(shape)
    return pl.BlockSpec(shape, lambda *_: (0,) * nd, pipeline_mode=pl.Buffered(1))


def _layer_norm(x, g, b):
    mu = jnp.mean(x, axis=-1, keepdims=True)
    xc = x - mu
    var = jnp.mean(xc * xc, axis=-1, keepdims=True)
    return xc * lax.rsqrt(var + LN_EPS) * g + b


def _sigmoid(x):
    return 1.0 / (1.0 + jnp.exp(-x))


def _split3(x):
    hi = x.astype(BF16)
    r1 = x - hi.astype(F32)
    mid = r1.astype(BF16)
    lo = (r1 - mid.astype(F32)).astype(BF16)
    return hi, mid, lo


def _dot(a, b):
    return jnp.dot(a, b, preferred_element_type=F32)


def _dot_nt(a, b):
    return lax.dot_general(a, b, (((1,), (1,)), ((), ())), preferred_element_type=F32)


def _dot_tn(a, b):
    return lax.dot_general(a, b, (((0,), (0,)), ((), ())), preferred_element_type=F32)


def _mod_kernel(c_ref, w_ref, b_ref, o_ref):
    c = c_ref[...]
    s = (c * _sigmoid(c)).astype(BF16)
    o_ref[...] = _dot(s, w_ref[...].astype(BF16)) + b_ref[...]


def _mod_call(c_all, w_ada, b_ada):
    n = c_all.shape[0]
    nblk = 6
    return pl.pallas_call(
        _mod_kernel,
        grid=(nblk,),
        in_specs=[
            pl.BlockSpec((n, D_MODEL), lambda i: (0, 0)),
            pl.BlockSpec((D_MODEL, D_MODEL), lambda i: (0, i)),
            pl.BlockSpec((1, D_MODEL), lambda i: (0, i)),
        ],
        out_specs=pl.BlockSpec((n, D_MODEL), lambda i: (0, i)),
        out_shape=jax.ShapeDtypeStruct((n, 6 * D_MODEL), F32),
        compiler_params=pltpu.CompilerParams(dimension_semantics=("arbitrary",)),
        name="mod",
    )(c_all, w_ada, b_ada)


def _mod_row(mod_ref, idx, tm):
    groups, _, d = mod_ref.shape
    m = mod_ref[:, idx:idx + 1, :]
    if groups == 1:
        return m[0]
    return jnp.broadcast_to(m, (groups, tm // groups, d)).reshape(tm, d)


def _inproj_kernel(x_ref, xnext_ref, mod_ref, lng_ref, lnb_ref, w_ref,
                   q_ref, k_ref, v_ref, k32_ref, v32_ref, p_ref, g_ref, xn_ref,
                   xna_scr, hba_scr, *, lead):
    tm = x_ref.shape[0]
    j = pl.program_id(0)

    first, second = slice(0, tm // 2), slice(tm // 2, tm)

    def front(src_ref, rs):
        def mrow(idx):
            m = _mod_row(mod_ref, idx, tm)
            return m if m.shape[0] == 1 else m[rs]

        xn = _layer_norm(src_ref[rs, :], lng_ref[...], lnb_ref[...])
        return xn, (xn * (1.0 + mrow(1)) + mrow(0)).astype(BF16)

    def half_tile(rs, prepared=False):
        if prepared:
            xn, hb = xna_scr[...], hba_scr[...]
        else:
            xn, hb = front(x_ref, rs)
        xn_ref[rs, :] = xn

        def seg(a, b):
            return _dot(hb, w_ref[:, a:b])

        g0 = ATT_COLS + RW_COLS
        for c in range(0, GATE_COLS, 512):
            g_ref[rs, c:c + 512] = _sigmoid(seg(g0 + c, g0 + c + 512))
        q_ref[rs, :] = (seg(0, HEADS_DIM) * (HEAD ** -0.5 * LOG2E)).astype(BF16)
        k = seg(HEADS_DIM, 2 * HEADS_DIM)
        k_ref[rs, :] = k.astype(BF16)
        k32_ref[rs, :] = k
        v = seg(2 * HEADS_DIM, 3 * HEADS_DIM)
        v_ref[rs, :] = v.astype(BF16)
        v32_ref[rs, :] = v
        for c in range(0, RW_COLS, 256):
            p_ref[rs, c:c + 256] = seg(ATT_COLS + c, ATT_COLS + c + 256)

    def prepare_next():
        xna_scr[...], hba_scr[...] = front(xnext_ref, first)

    @pl.when(j < lead)
    def _():
        k_ref[...] = jnp.zeros_like(k_ref)
        v_ref[...] = jnp.zeros_like(v_ref)
        prepare_next()

    @pl.when(j >= lead)
    def _():
        half_tile(first, prepared=lead > 0)
        if lead > 0:
            prepare_next()
        half_tile(second)


def _inproj_call(x2d, mod, ln_g, ln_b, w_in_b, steps, tm, prompt):
    rows = x2d.shape[0]
    in_cols = w_in_b.shape[1]
    groups = mod.shape[0]
    if prompt:
        lead = ATT_REACH // tm
        kv_rows = ATT_REACH
        row_map = lambda j: (jnp.maximum(j - lead, 0), 0)
        next_map = lambda j: (jnp.clip(j + 1 - lead, 0, steps - 1), 0)
        kv_map = lambda j: (jnp.maximum(j - steps, 0), 0)
        ext_map = lambda j: (j, 0)
    else:
        lead = 0
        kv_rows = rows
        row_map = next_map = kv_map = ext_map = lambda j: (j, 0)
    return pl.pallas_call(
        functools.partial(_inproj_kernel, lead=lead),
        grid=(steps + lead,),
        in_specs=[
            pl.BlockSpec((tm, D_MODEL), row_map),
            pl.BlockSpec((tm, D_MODEL), next_map),
            _const_spec((groups, 6, D_MODEL)),
            _const_spec((1, D_MODEL)),
            _const_spec((1, D_MODEL)),
            _const_spec((D_MODEL, in_cols)),
        ],
        scratch_shapes=[pltpu.VMEM((tm // 2, D_MODEL), F32), pltpu.VMEM((tm // 2, D_MODEL), BF16)],
        out_specs=[
            pl.BlockSpec((tm, HEADS_DIM), row_map),
            pl.BlockSpec((tm, HEADS_DIM), ext_map),
            pl.BlockSpec((tm, HEADS_DIM), ext_map),
            pl.BlockSpec((tm, HEADS_DIM), kv_map),
            pl.BlockSpec((tm, HEADS_DIM), kv_map),
            pl.BlockSpec((tm, RW_COLS), row_map),
            pl.BlockSpec((tm, GATE_COLS), row_map),
            pl.BlockSpec((tm, D_MODEL), row_map),
        ],
        out_shape=[
            jax.ShapeDtypeStruct((rows, HEADS_DIM), BF16),
            jax.ShapeDtypeStruct((rows + lead * tm, HEADS_DIM), BF16),
            jax.ShapeDtypeStruct((rows + lead * tm, HEADS_DIM), BF16),
            jax.ShapeDtypeStruct((kv_rows, HEADS_DIM), F32),
            jax.ShapeDtypeStruct((kv_rows, HEADS_DIM), F32),
            jax.ShapeDtypeStruct((rows, RW_COLS), F32),
            jax.ShapeDtypeStruct((rows, GATE_COLS), F32),
            jax.ShapeDtypeStruct((rows, D_MODEL), F32),
        ],
        compiler_params=pltpu.CompilerParams(
            dimension_semantics=("arbitrary",), vmem_limit_bytes=VMEM_LIMIT),
        name="inproj",
    )(x2d, x2d, mod, ln_g, ln_b, w_in_b)


def _attn_chunks(chunks, bias_ref):
    lane = lax.broadcasted_iota(jnp.int32, (CHUNK, 128), 1)
    first = lane < HEAD
    pairs = [slice(pr * 128, (pr + 1) * 128) for pr in range(N_HEADS // 2)]
    scores = []
    for qc, kb, _, _ in chunks:
        for sl in pairs:
            q2 = qc[:, sl].astype(F32)
            qs = jnp.concatenate([jnp.where(first, q2, 0.0), jnp.where(first, 0.0, q2)],
                                 axis=0).astype(BF16)
            scores.append(_dot_nt(qs, kb(sl)))
    probs, sums = [], []
    for i, s in enumerate(scores):
        thr = chunks[i // len(pairs)][3]
        s = s + bias_ref[i % len(pairs)]
        if thr is not None:
            col = lax.broadcasted_iota(jnp.int32, s.shape, 1)
            s = jnp.where(col >= thr, s, -jnp.inf)
        e = jnp.exp2(s - jnp.max(s, axis=1, keepdims=True))
        sums.append(jnp.sum(e, axis=1, keepdims=True))
        probs.append(e.astype(BF16))
    outs = []
    for ci, (_, _, vb, _) in enumerate(chunks):
        cols = []
        for pi, sl in enumerate(pairs):
            i = ci * len(pairs) + pi
            o = _dot(probs[i], vb(sl)) / sums[i]
            cols.append(jnp.where(first, o[0:CHUNK], o[CHUNK:2 * CHUNK]))
        outs.append(jnp.concatenate(cols, axis=1).astype(BF16))
    return outs


def _attn_prompt_kernel(q_ref, ka_ref, kb_ref, va_ref, vb_ref, bias_ref, o_ref, kbuf, vbuf,
                        *, chunks):
    tq = chunks * CHUNK
    kbuf[0:tq, :] = ka_ref[...]
    kbuf[tq:2 * tq, :] = kb_ref[...]
    vbuf[0:tq, :] = va_ref[...]
    vbuf[tq:2 * tq, :] = vb_ref[...]
    s = pl.program_id(0)

    per = 2

    def run(masked):
        for i in range(chunks // per):
            units, starts = [], []
            for k in range(per):
                g = i * per + k
                r0 = g * CHUNK
                thr = ATT_REACH - (s * chunks + g) * CHUNK if masked else None
                band = slice(r0, r0 + BAND)
                units.append((q_ref[r0:r0 + CHUNK, :],
                              lambda sl, band=band: kbuf[band, sl],
                              lambda sl, band=band: vbuf[band, sl], thr))
                starts.append(r0)
            for r0, o in zip(starts, _attn_chunks(units, bias_ref)):
                o_ref[r0:r0 + CHUNK, :] = o

    assert chunks * CHUNK >= ATT_REACH
    pl.when(s == 0)(functools.partial(run, True))
    pl.when(s != 0)(functools.partial(run, False))


def _attn_prompt_call(q, kext, vext, bias):
    rows = q.shape[0]
    tq = ATT_REACH
    blk = lambda off: pl.BlockSpec((tq, HEADS_DIM), lambda s: (s + off, 0))
    return pl.pallas_call(
        functools.partial(_attn_prompt_kernel, chunks=tq // CHUNK),
        grid=(rows // tq,),
        in_specs=[blk(0), blk(0), blk(1), blk(0), blk(1), _const_spec(bias.shape)],
        out_specs=blk(0),
        out_shape=jax.ShapeDtypeStruct((rows, HEADS_DIM), BF16),
        scratch_shapes=[pltpu.VMEM((2 * tq, HEADS_DIM), BF16), pltpu.VMEM((2 * tq, HEADS_DIM), BF16)],
        compiler_params=pltpu.CompilerParams(dimension_semantics=("arbitrary",)),
        name="attn_prompt",
    )(q, kext, kext, vext, vext, bias)


def _attn_sample_kernel(q_ref, k_ref, v_ref, ck_ref, cv_ref, bias_ref, o_ref):
    def band(cache_ref, new_ref, rs):
        return lambda sl: jnp.concatenate([cache_ref[:, sl], new_ref[rs, sl]], axis=0)

    units = []
    for s in range(ck_ref.shape[0]):
        rs = slice(s * CHUNK, (s + 1) * CHUNK)
        units.append((q_ref[rs, :], band(ck_ref.at[s], k_ref, rs), band(cv_ref.at[s], v_ref, rs),
                      None))
    for s, o in enumerate(_attn_chunks(units, bias_ref)):
        o_ref[s * CHUNK:(s + 1) * CHUNK, :] = o


def _attn_sample_call(q, k, v, cache_k, cache_v, bias):
    per = 2
    nb = cache_k.shape[0] // per
    new = pl.BlockSpec((per * CHUNK, HEADS_DIM), lambda b: (b, 0))
    cache = pl.BlockSpec((per, ATT_REACH, HEADS_DIM), lambda b: (b, 0, 0))
    return pl.pallas_call(
        _attn_sample_kernel,
        grid=(nb,),
        in_specs=[new, new, new, cache, cache, _const_spec(bias.shape)],
        out_specs=new,
        out_shape=jax.ShapeDtypeStruct(q.shape, BF16),
        compiler_params=pltpu.CompilerParams(dimension_semantics=("arbitrary",)),
        name="attn_sample",
    )(q, k, v, cache_k, cache_v, bias)


def _same_head():
    r = lax.broadcasted_iota(jnp.int32, (GROUP, GROUP), 0) // HEAD
    c = lax.broadcasted_iota(jnp.int32, (GROUP, GROUP), 1) // HEAD
    return r == c


def _blk(x, same_head):
    return jnp.where(same_head, jnp.concatenate([x] * 4, axis=0), 0.0).astype(BF16)


def _wkv_prep(units, hooks):
    same_head = _same_head()
    t64 = lax.broadcasted_iota(jnp.int32, (CHUNK, GROUP), 0)
    i64 = lax.broadcasted_iota(jnp.int32, (CHUNK, GROUP), 1) % HEAD
    strict = i64 < t64
    incl = i64 <= t64
    pending = list(hooks)

    def stage_done():
        if pending:
            pending.pop(0)()

    def blk(x):
        return _blk(x, same_head)

    n, a_ak, a_rb, a_rk = [], [], [], []
    for ld in units:
        lhs = jnp.concatenate([ld("at"), ld("rt")], axis=0).astype(BF16)
        rhs = jnp.concatenate([blk(ld("bt")), blk(ld("kt"))], axis=0)
        a_all = _dot_nt(lhs, rhs)
        n.append(jnp.where(strict, a_all[0:CHUNK, 0:GROUP], 0.0))
        a_ak.append(jnp.where(strict, a_all[0:CHUNK, GROUP:], 0.0))
        a_rb.append(jnp.where(incl, a_all[CHUNK:, 0:GROUP], 0.0))
        a_rk.append(jnp.where(incl, a_all[CHUNK:, GROUP:], 0.0))
    stage_done()

    x0 = [_dot(a.astype(BF16), blk(ld("v"))) for a, ld in zip(a_ak, units)]
    npow = [_dot(m.astype(BF16), blk(m)) for m in n]
    t = [jnp.where(i64 == t64, 1.0, 0.0) + m for m in n]
    stage_done()
    for _ in range(4):
        prod = [_dot(jnp.concatenate([a, b], axis=0).astype(BF16), blk(b))
                for a, b in zip(t, npow)]
        t = [a + p[0:CHUNK] for a, p in zip(t, prod)]
        npow = [p[CHUNK:] for p in prod]
        stage_done()
    t = [a + _dot(a.astype(BF16), blk(b)) for a, b in zip(t, npow)]
    stage_done()
    res = [_dot(a.astype(BF16), jnp.concatenate([blk(ld("at")), blk(x)], axis=1))
           for a, x, ld in zip(t, x0, units)]
    stage_done()
    while pending:
        stage_done()
    return [(r[:, 0:GROUP], r[:, GROUP:], b, k) for r, b, k in zip(res, a_rb, a_rk)]


def _wkv_step_stages(units, states, out):
    same_head = _same_head()
    held = {}

    def blk(x):
        return _blk(x, same_head)

    def first():
        held["sblk"] = [blk(st) for st in states]
        held["u"] = [_dot(ld("ah").astype(BF16), sb) + ld("vh")
                     for ld, sb in zip(units, held["sblk"])]

    def second():
        for i, ld in enumerate(units):
            u, v = held["u"][i], ld("v")
            y = _dot(jnp.concatenate([ld("rt"), ld("arb"), ld("ark")], axis=1).astype(BF16),
                     jnp.concatenate([held["sblk"][i], blk(u), blk(v)], axis=0))
            g = _dot_tn(jnp.concatenate([ld("btd"), ld("ktd")], axis=0).astype(BF16),
                        jnp.concatenate([u, v], axis=0).astype(BF16))
            g = jnp.where(same_head, g, 0.0)
            delta = g[0:64] + g[64:128] + g[128:192] + g[192:256]
            out.append(y)
            states[i] = states[i] * ld("fdec") + delta

    return [first, second]


def _wkv_kernel(p_ref, pn_ref, shift0_ref, state0_ref, mu_ref, w0_ref, wwa_ref, a0_ref, gup_ref,
                kkw_ref, kaw_ref, rkw_ref, gng_ref, gnb_ref, ones_ref, ltri_ref,
                out_ref, shift_out_ref, state_out_ref,
                st_scr, rt_scr, at_scr, kt_scr, bt_scr, v_scr, btd_scr, ktd_scr,
                fdec_scr, ah_scr, vh_scr, arb_scr, ark_scr, g_scr, bon_scr, *, ts, steps):
    j = pl.program_id(0)
    streams = state0_ref.shape[0]
    assert streams == 1 or (steps == 1 and ts == streams * CHUNK)

    if streams == 1:
        @pl.when(j == 0)
        def _():
            st_scr[...] = state0_ref[0]

    def bdsum(x, terms):
        ones = ones_ref[...]
        halves = []
        for c0 in range(0, HEADS_DIM, GROUP):
            parts = _split3(x[:, c0:c0 + GROUP])[:terms]
            acc = _dot(parts[0], ones)
            for part in parts[1:]:
                acc = acc + _dot(part, ones)
            halves.append(acc)
        return jnp.concatenate(halves, axis=1)

    pq = ltri_ref.shape[0]
    per = pq // CHUNK
    n_piece = ts // pq

    def pre_stages(q, upcoming=False):
        r0 = q * pq
        rows = slice(r0, r0 + pq)
        src_ref = pn_ref if upcoming else p_ref
        held = {}

        def shifted(c0, w):
            cols = slice(c0, c0 + w)
            row = lax.broadcasted_iota(jnp.int32, (pq, 1), 0)
            if streams > 1:
                first = shift0_ref[q * per:(q + 1) * per, :, cols]
                prev_row = jnp.broadcast_to(first, (per, CHUNK, w)).reshape(pq, w)
                top = row % CHUNK == 0
            else:
                if upcoming:
                    prev_row = p_ref[ts - 8:ts, cols][7:8]
                elif q == 0:
                    prev_row = shift0_ref[0][:, cols]
                else:
                    prev_row = p_ref[r0 - 8:r0, cols][7:8]
                top = row == 0
            pb = src_ref[rows, cols]
            prev = jnp.where(top, prev_row, pltpu.roll(pb, 1, 0))
            return pb + (prev - pb) * mu_ref[:, cols]

        def s_lora():
            lora = shifted(3 * HEADS_DIM, 256)
            lwla = lora[:, 0:128]
            lane = lax.broadcasted_iota(jnp.int32, (pq, 128), 1)
            held["raw"] = _dot(jnp.where(lane < LORA_W, jnp.tanh(lwla), lwla).astype(BF16),
                               wwa_ref[...])
            g_scr[rows, :] = _dot(_sigmoid(lora[:, 128:256]).astype(BF16), gup_ref[...])
            held["k"] = shifted(HEADS_DIM, HEADS_DIM)

        def s_decay():
            w_pre = w0_ref[...] + held["raw"][:, 0:HEADS_DIM]
            softplus = jnp.maximum(-w_pre, 0.0) + jnp.log(1.0 + jnp.exp(-jnp.abs(w_pre)))
            lw = -jnp.exp(-softplus - 0.5)
            ltri = ltri_ref[...]
            hi, mid, _ = _split3(lw)
            held["cum"] = _dot(ltri, hi) + _dot(ltri, mid)
            held["lw"] = lw

        def s_keys():
            k = held["k"]
            a = _sigmoid(a0_ref[...] + held["raw"][:, HEADS_DIM:])
            kk = k * kkw_ref[...]
            held["nrm"] = bdsum(kk * kk, 1)
            held.update(a=a, kk=kk, k2=k * (1.0 + (a - 1.0) * kaw_ref[...]))

        def s_bonus():
            r = shifted(0, HEADS_DIM)
            held["bon"] = bdsum(r * held["k2"] * rkw_ref[...], 1)
            rt_scr[rows, :] = r * jnp.exp(held["cum"])

        def s_norm():
            cum = held["cum"]
            kk = held["kk"] / jnp.maximum(jnp.sqrt(held["nrm"]), 1e-12)
            cum3 = cum.reshape(per, CHUNK, HEADS_DIM)
            cend = jnp.broadcast_to(cum3[:, CHUNK - 1:CHUNK, :], cum3.shape).reshape(pq, HEADS_DIM)
            diag = (lax.broadcasted_iota(jnp.int32, (pq, HEADS_DIM), 0) % CHUNK
                    == lax.broadcasted_iota(jnp.int32, (pq, HEADS_DIM), 1) % HEAD)
            held["fdec"] = bdsum(jnp.where(diag, jnp.exp(cend), 0.0), 2)
            at_scr[rows, :] = -kk * jnp.exp(cum - held["lw"])
            held.update(kk=kk, cend=cend)

        def s_inv():
            e_inv = jnp.exp(-held["cum"])
            kt_scr[rows, :] = held["k2"] * e_inv
            bt_scr[rows, :] = held["kk"] * held["a"] * e_inv

        def s_rel():
            e_rel = jnp.exp(held["cend"] - held["cum"])
            ktd_scr[rows, :] = held["k2"] * e_rel
            btd_scr[rows, :] = held["kk"] * held["a"] * e_rel
            fdec_scr[rows, :] = held["fdec"]

        def s_value():
            v = shifted(2 * HEADS_DIM, HEADS_DIM)
            v_scr[rows, :] = v
            bon_scr[rows, :] = held["bon"] * v

        return [s_lora, s_decay, s_keys, s_bonus, s_norm, s_inv, s_rel, s_value]

    def chunk_rows(c):
        return slice(c * CHUNK, (c + 1) * CHUNK)

    groups = [slice(g0, g0 + GROUP) for g0 in range(0, HEADS_DIM, GROUP)]

    named = dict(at=at_scr, rt=rt_scr, kt=kt_scr, bt=bt_scr, v=v_scr, btd=btd_scr, ktd=ktd_scr,
                 fdec=fdec_scr, ah=ah_scr, vh=vh_scr, arb=arb_scr, ark=ark_scr)
    prep_outs = (ah_scr, vh_scr, arb_scr, ark_scr)

    def loader(c, cols):
        rows = chunk_rows(c)
        return lambda name: named[name][rows, cols]

    def run(step_piece, prep_piece, extra):
        step_chunks = [] if step_piece is None else [step_piece * per + i for i in range(per)]
        prep_chunks = [] if prep_piece is None else [prep_piece * per + i for i in range(per)]
        stages, ys, finals = [], [], []
        if streams == 1:
            states = [st_scr[:, cols] for cols in groups] if step_chunks else []
            for c in step_chunks:
                out = []
                ys.append(out)
                stages += _wkv_step_stages([loader(c, cols) for cols in groups], states, out)
            finals = [(st_scr, states)] if step_chunks else []
        else:
            firsts, seconds = [], []
            for c in step_chunks:
                out = []
                ys.append(out)
                states = [state0_ref[c, :, cols] for cols in groups]
                first, second = _wkv_step_stages(
                    [loader(c, cols) for cols in groups], states, out)
                firsts.append(first)
                seconds.append(second)
                finals.append((state_out_ref.at[c], states))
            if step_chunks:
                stages += [lambda: [f() for f in firsts], lambda: [f() for f in seconds]]
        if step_chunks:
            rows = slice(step_piece * pq, (step_piece + 1) * pq)
            gn = {}

            def gn_mean():
                gn["y"] = jnp.concatenate([jnp.concatenate(out, axis=1) for out in ys], axis=0)
                gn["sum"] = bdsum(gn["y"], 1)

            def gn_var():
                gn["d"] = gn["y"] - gn["sum"] * (1.0 / HEAD)
                gn["sq"] = bdsum(gn["d"] * gn["d"], 1)

            def gn_out():
                var = gn["sq"] * (1.0 / HEAD)
                yn = (gn["d"] * lax.rsqrt(var + GN_EPS) * gng_ref[...] + gnb_ref[...]
                      + bon_scr[rows, :])
                out_ref[rows, :] = (yn * g_scr[rows, :]).astype(BF16)

            stages += [gn_mean, gn_var, gn_out]
        hooks = []
        for i in range(max(len(stages), len(extra))):
            both = stages[i:i + 1] + extra[i:i + 1]
            hooks.append(lambda both=both: [f() for f in both])
        units = [(c, cols) for c in prep_chunks for cols in groups]
        prep_out = _wkv_prep([loader(c, cols) for c, cols in units], hooks)
        for ref, states in finals:
            for cols, st in zip(groups, states):
                ref[:, cols] = st
        for (c, cols), outs in zip(units, prep_out):
            for ref, val in zip(prep_outs, outs):
                ref[chunk_rows(c), cols] = val

    look_ahead = steps > 1
    ahead_pass = max(2, n_piece - 1)
    assert ahead_pass <= n_piece or not look_ahead

    def first_piece():
        for stage in pre_stages(0):
            stage()

    if look_ahead:
        pl.when(j == 0)(first_piece)
    else:
        first_piece()
    for piece in range(n_piece + 1):
        if piece + 1 < n_piece:
            extra = pre_stages(piece + 1)
        elif piece == ahead_pass and look_ahead:
            extra = pre_stages(0, upcoming=True)
        else:
            extra = []
        run(piece - 1 if piece >= 1 else None, piece if piece < n_piece else None, extra)

    if streams == 1:
        @pl.when(j == steps - 1)
        def _():
            shift_out_ref[0] = p_ref[ts - 1:ts, :]
            state_out_ref[0] = st_scr[...]
    else:
        for s in range(streams):
            shift_out_ref[s] = p_ref[(s + 1) * CHUNK - 1:(s + 1) * CHUNK, :]


def _wkv_call(p2d, shift0, state0, prm, steps, ts):
    rows = p2d.shape[0]
    streams = state0.shape[0]
    row_map = lambda j: (j, 0)
    next_map = lambda j: (jnp.minimum(j + 1, steps - 1), 0)
    seq3 = lambda j: (0, 0, 0)
    vec = _const_spec((1, HEADS_DIM))
    big = pltpu.VMEM((ts, HEADS_DIM), F32)
    ltri = prm["ltri"][min(ts, WKV_PIECE)]
    return pl.pallas_call(
        functools.partial(_wkv_kernel, ts=ts, steps=steps),
        grid=(steps,),
        in_specs=[
            pl.BlockSpec((ts, RW_COLS), row_map),
            pl.BlockSpec((ts, RW_COLS), next_map),
            pl.BlockSpec((streams, 1, RW_COLS), seq3),
            pl.BlockSpec((streams, HEAD, HEADS_DIM), seq3),
            _const_spec((1, RW_COLS)),
            vec,
            _const_spec((128, 2 * HEADS_DIM)),
            vec,
            _const_spec((LORA_G, HEADS_DIM)),
            vec, vec, vec, vec, vec,
            _const_spec((GROUP, GROUP)),
            _const_spec(ltri.shape),
        ],
        out_specs=[
            pl.BlockSpec((ts, HEADS_DIM), row_map),
            pl.BlockSpec((streams, 1, RW_COLS), seq3),
            pl.BlockSpec((streams, HEAD, HEADS_DIM), seq3),
        ],
        out_shape=[
            jax.ShapeDtypeStruct((rows, HEADS_DIM), BF16),
            jax.ShapeDtypeStruct((streams, 1, RW_COLS), F32),
            jax.ShapeDtypeStruct((streams, HEAD, HEADS_DIM), F32),
        ],
        scratch_shapes=[
            pltpu.VMEM((HEAD, HEADS_DIM), F32),
        ] + [big] * 14,
        compiler_params=pltpu.CompilerParams(
            dimension_semantics=("arbitrary",), vmem_limit_bytes=VMEM_LIMIT),
        name="wkv",
    )(p2d, p2d, shift0, state0, prm["mu"], prm["w0"], prm["wwa"], prm["a0"], prm["gup"],
      prm["kk"], prm["ka"], prm["rk"], prm["gng"], prm["gnb"], prm["ones"], ltri)


def _mixffn_kernel(xn_ref, att_ref, rw_ref, gate_ref, mod_ref, convp_ref,
                   ln1g_ref, ln1b_ref, ln2g_ref, ln2b_ref,
                   wa_ref, wr_ref, wo_ref, wup_ref, cw_ref, cb_ref, wdn_ref,
                   y_ref, convo_ref, carry_scr, yb_scr, x1_scr, h2_scr, *, tm, steps):
    j = pl.program_id(0)
    groups = mod_ref.shape[0]
    assert groups == 1 or steps == 1
    pipelined = steps > 1

    def mod(idx):
        return _mod_row(mod_ref, idx, tm)

    half = tm // 2
    subs = [slice(0, half), slice(half, tm)]

    def modr(idx, rs):
        m = mod(idx)
        return m if m.shape[0] == 1 else m[rs]

    def front():
        ma = [_dot(att_ref[rs, :], wa_ref[...]) for rs in subs]
        mr = [_dot(rw_ref[rs, :], wr_ref[...]) for rs in subs]
        merged = [(gate_ref[rs, 0:D_MODEL] * a + gate_ref[rs, D_MODEL:] * r).astype(BF16)
                  for rs, a, r in zip(subs, ma, mr)]
        mix = [_dot(m, wo_ref[...]) for m in merged]
        x1 = [_layer_norm(ALPHA * xn_ref[rs, :] + (1.0 + modr(2, rs)) * m,
                          ln1g_ref[...], ln1b_ref[...]) for rs, m in zip(subs, mix)]
        h2 = jnp.concatenate([(a * (1.0 + modr(4, rs)) + modr(3, rs)).astype(BF16)
                              for rs, a in zip(subs, x1)], axis=0)
        return x1, h2

    def keep(x1, h2):
        for rs, a in zip(subs, x1):
            x1_scr[rs, :] = a
        h2_scr[...] = h2

    if not pipelined:
        if groups == 1:
            carry_scr[8 - (CONV_W - 1):8, :] = convp_ref[0]
        _ffn_back(*front(), subs, modr, groups, tm, convp_ref, ln2g_ref, ln2b_ref, wup_ref, cw_ref,
                  cb_ref, wdn_ref, y_ref, convo_ref, carry_scr, yb_scr)
        return

    @pl.when(j == 0)
    def _():
        carry_scr[8 - (CONV_W - 1):8, :] = convp_ref[0]
        keep(*front())

    @pl.when(j > 0)
    def _():
        x1 = [x1_scr[rs, :] for rs in subs]
        h2 = h2_scr[...]
        keep(*front())
        _ffn_back(x1, h2, subs, modr, groups, tm, convp_ref, ln2g_ref, ln2b_ref, wup_ref, cw_ref,
                  cb_ref, wdn_ref, y_ref, convo_ref, carry_scr, yb_scr)


def _ffn_back(x1, h2, subs, modr, groups, tm, convp_ref, ln2g_ref, ln2b_ref, wup_ref, cw_ref,
              cb_ref, wdn_ref, y_ref, convo_ref, carry_scr, yb_scr):
    cw_blk = 256
    glen = tm // groups
    row8 = lax.broadcasted_iota(jnp.int32, (8, cw_blk), 0)
    grow = lax.broadcasted_iota(jnp.int32, (tm, cw_blk), 0) % glen
    for c in range(0, D_FF, cw_blk):
        cs = slice(c, c + cw_blk)
        uc = _dot(h2, wup_ref[:, cs])
        uv = _dot(h2, wup_ref[:, D_FF + c:D_FF + c + cw_blk])
        r1 = pltpu.roll(uc, 1, 0)
        r2 = pltpu.roll(uc, 2, 0)
        if groups == 1:
            c6 = carry_scr[6:7, cs]
            c7 = carry_scr[7:8, cs]
            s1 = jnp.concatenate([jnp.where(row8 == 0, c7, r1[0:8]), r1[8:]], axis=0)
            s2 = jnp.concatenate(
                [jnp.where(row8 == 0, c6, jnp.where(row8 == 1, c7, r2[0:8])), r2[8:]], axis=0)
            carry_scr[:, cs] = uc[tm - 8:tm, :]
            tail = uc[tm - (CONV_W - 1):tm, :][None]
        else:
            hist = jnp.broadcast_to(convp_ref[:, :, cs][:, :, None, :],
                                    (groups, CONV_W - 1, glen, cw_blk))
            c6 = hist[:, 0].reshape(tm, cw_blk)
            c7 = hist[:, 1].reshape(tm, cw_blk)
            s1 = jnp.where(grow == 0, c7, r1)
            s2 = jnp.where(grow == 0, c6, jnp.where(grow == 1, c7, r2))
            tail = uc.reshape(groups, glen, cw_blk)[:, glen - (CONV_W - 1):, :]
        conv = cb_ref[:, cs] + s2 * cw_ref[0:1, cs] + s1 * cw_ref[1:2, cs] + uc * cw_ref[2:3, cs]
        yb_scr[:, cs] = (conv * _sigmoid(conv) * uv).astype(BF16)
        convo_ref[:, :, cs] = tail

    ff = [_dot(yb_scr[rs, :], wdn_ref[...]) for rs in subs]
    for rs, a, f in zip(subs, x1, ff):
        y_ref[rs, :] = _layer_norm(ALPHA * a + (1.0 + modr(5, rs)) * f, ln2g_ref[...], ln2b_ref[...])


def _mixffn_call(xn2d, att, rw, gates, mod, conv_prev, prm, steps, tm):
    rows = xn2d.shape[0]
    groups = mod.shape[0]
    lag = 1 if steps > 1 else 0
    row_map = lambda j: (jnp.minimum(j, steps - 1), 0)
    out_map = lambda j: (jnp.maximum(j - lag, 0), 0)
    vec = _const_spec((1, D_MODEL))
    return pl.pallas_call(
        functools.partial(_mixffn_kernel, tm=tm, steps=steps),
        grid=(steps + lag,),
        in_specs=[
            pl.BlockSpec((tm, D_MODEL), row_map),
            pl.BlockSpec((tm, HEADS_DIM), row_map),
            pl.BlockSpec((tm, HEADS_DIM), row_map),
            pl.BlockSpec((tm, GATE_COLS), row_map),
            _const_spec((groups, 6, D_MODEL)),
            _const_spec((groups, CONV_W - 1, D_FF)),
            vec, vec, vec, vec,
            _const_spec((HEADS_DIM, D_MODEL)),
            _const_spec((HEADS_DIM, D_MODEL)),
            _const_spec((D_MODEL, D_MODEL)),
            _const_spec((D_MODEL, 2 * D_FF)),
            _const_spec((CONV_W, D_FF)),
            _const_spec((1, D_FF)),
            _const_spec((D_FF, D_MODEL)),
        ],
        out_specs=[
            pl.BlockSpec((tm, D_MODEL), out_map),
            pl.BlockSpec((groups, CONV_W - 1, D_FF), lambda j: (0, 0, 0)),
        ],
        out_shape=[
            jax.ShapeDtypeStruct((rows, D_MODEL), F32),
            jax.ShapeDtypeStruct((groups, CONV_W - 1, D_FF), F32),
        ],
        scratch_shapes=[pltpu.VMEM((8, D_FF), F32), pltpu.VMEM((tm, D_FF), BF16),
                        pltpu.VMEM((tm, D_MODEL), F32), pltpu.VMEM((tm, D_MODEL), BF16)],
        compiler_params=pltpu.CompilerParams(
            dimension_semantics=("arbitrary",), vmem_limit_bytes=VMEM_LIMIT),
        name="mixffn",
    )(xn2d, att, rw, gates, mod, conv_prev,
      prm["ln1g"], prm["ln1b"], prm["ln2g"], prm["ln2b"],
      prm["wa"], prm["wr"], prm["wo"], prm["wup"], prm["cw"], prm["cb"], prm["wdn"])


def _pair_bias(table):
    assert CHUNK - 1 <= REL_CLIP
    top = ATT_REACH + CHUNK - 1
    n_far = top - REL_CLIP + 1
    far = jnp.broadcast_to(table[:, 2 * REL_CLIP:], (N_HEADS, n_far))
    lo_idx = top - (BAND + CHUNK - 2) + REL_CLIP
    near = table[:, lo_idx:2 * REL_CLIP][:, ::-1]
    ext = jnp.concatenate([far, near], axis=1).astype(F32) * LOG2E
    n_ext = BAND + CHUNK - 1
    period = jnp.concatenate([ext, jnp.zeros((N_HEADS, 1), F32)], axis=1)
    skew = jnp.tile(period, (1, CHUNK))[:, :CHUNK * n_ext].reshape(N_HEADS, CHUNK, n_ext)
    bias = skew[:, :, CHUNK - 1:CHUNK - 1 + BAND]
    return bias.reshape(N_HEADS // 2, 2 * CHUNK, BAND)


def _chunk_ltri(ts):
    t = jnp.arange(ts)
    return ((t[:, None] // CHUNK == t[None, :] // CHUNK) & (t[None, :] <= t[:, None])).astype(BF16)


def _trunk(x2d, mod, shift0, state0, conv_prev, caches, prm, n_seq):
    prompt = caches is None
    rows = x2d.shape[0]
    tm = min(rows, ROW_TILE)
    steps = rows // tm
    q, k, v, k32, v32, p, gates, xn = _inproj_call(
        x2d, mod, prm["lnig"], prm["lnib"], prm["win"], steps, tm, prompt)
    if prompt:
        att = _attn_prompt_call(q, k, v, prm["bias"])
    else:
        att = _attn_sample_call(q, k, v, *caches, prm["bias"])
    assert state0.shape[0] == n_seq
    rw, shift, state = _wkv_call(p, shift0, state0, prm, steps, tm)
    y, conv = _mixffn_call(xn, att, rw, gates, mod, conv_prev, prm, steps, tm)
    return y, (k32, v32), state, shift, conv


def kernel(x_prompt, x_sample, cache_attn_k, cache_attn_v, state_rwkv, state_shift, state_conv,
           c_prompt, c_sample, ln_in_g, ln_in_b, w_ada, b_ada, w_in, attn_rel_bias,
           rwkv_mu, rwkv_w0, rwkv_w_up, rwkv_a0, rwkv_a_up, rwkv_g_up, rwkv_k_k, rwkv_k_a,
           rwkv_r_k, rwkv_gn_g, rwkv_gn_b, w_branch_attn, w_branch_rwkv, w_out,
           ln1_g, ln1_b, ln2_g, ln2_b, w_ffn_up, ffn_conv_w, ffn_conv_b, w_ffn_down):
    bp, sp, _ = x_prompt.shape
    bs, ss, _ = x_sample.shape
    assert bp == 1 and ss == CHUNK and w_ada.shape[0] == DEPTH
    assert sp % ROW_TILE == 0 and bs * ss <= ROW_TILE and cache_attn_k.shape[2] == ATT_REACH

    row = lambda a: a.reshape(1, -1)
    wwa = jnp.zeros((LORA_W + LORA_A, 2 * HEADS_DIM), F32)
    wwa = wwa.at[:LORA_W, :HEADS_DIM].set(rwkv_w_up[0]).at[LORA_W:, HEADS_DIM:].set(rwkv_a_up[0])
    head_id = jnp.arange(GROUP) // HEAD
    prm = dict(
        lnig=row(ln_in_g), lnib=row(ln_in_b),
        ln1g=row(ln1_g[0]), ln1b=row(ln1_b[0]), ln2g=row(ln2_g[0]), ln2b=row(ln2_b[0]),
        win=w_in[0].astype(BF16), bias=_pair_bias(attn_rel_bias[0]),
        mu=row(rwkv_mu[0]), w0=row(rwkv_w0[0]), wwa=wwa.astype(BF16), a0=row(rwkv_a0[0]),
        gup=rwkv_g_up[0].astype(BF16), kk=row(rwkv_k_k[0]), ka=row(rwkv_k_a[0]),
        rk=row(rwkv_r_k[0]), gng=row(rwkv_gn_g[0]), gnb=row(rwkv_gn_b[0]),
        ones=(head_id[:, None] == head_id[None, :]).astype(BF16),
        ltri={WKV_PIECE: _chunk_ltri(WKV_PIECE), CHUNK: _chunk_ltri(CHUNK)},
        wa=w_branch_attn[0].astype(BF16), wr=w_branch_rwkv[0].astype(BF16),
        wo=w_out[0].astype(BF16), wup=w_ffn_up[0].astype(BF16),
        cw=ffn_conv_w[0], cb=row(ffn_conv_b[0]), wdn=w_ffn_down[0].astype(BF16),
    )

    n_c = bp + bs
    c_all = jnp.concatenate([c_prompt, c_sample, jnp.zeros((16 - n_c, D_MODEL), F32)], axis=0)
    mod = _mod_call(c_all, w_ada[0], row(b_ada[0])).reshape(16, 6, D_MODEL)

    y_p, kv_p, st_p, sh_p, cv_p = _trunk(
        x_prompt.reshape(sp, D_MODEL), mod[0:bp],
        jnp.zeros((bp, 1, RW_COLS), F32), jnp.zeros((bp, HEAD, HEADS_DIM), F32),
        jnp.zeros((bp, CONV_W - 1, D_FF), F32), None, prm, n_seq=bp)

    caches = (cache_attn_k[0].astype(BF16).reshape(bs, ATT_REACH, HEADS_DIM),
              cache_attn_v[0].astype(BF16).reshape(bs, ATT_REACH, HEADS_DIM))
    st0 = jnp.transpose(state_rwkv[0], (0, 3, 1, 2)).reshape(bs, HEAD, HEADS_DIM)
    y_s, kv_s, st_s, sh_s, cv_s = _trunk(
        x_sample.reshape(bs * ss, D_MODEL), mod[bp:n_c],
        state_shift[0], st0, state_conv[0], caches, prm, n_seq=bs)

    def state_out(st, b):
        return jnp.transpose(st.reshape(b, HEAD, N_HEADS, HEAD), (0, 2, 3, 1))[None]

    hs = (N_HEADS, HEAD)
    return (
        y_p.reshape(bp, sp, D_MODEL),
        y_s.reshape(bs, ss, D_MODEL),
        kv_p[0].reshape(1, bp, ATT_REACH, *hs),
        kv_p[1].reshape(1, bp, ATT_REACH, *hs),
        kv_s[0].reshape(1, bs, ss, *hs),
        kv_s[1].reshape(1, bs, ss, *hs),
        state_out(st_p, bp),
        state_out(st_s, bs),
        sh_p[None],
        sh_s[None],
        cv_p[None],
        cv_s[None],
    )
```

```python
import functools

import jax
import jax.numpy as jnp
from jax import lax
from jax.experimental import pallas as pl
from jax.experimental.pallas import tpu as pltpu

F32 = jnp.float32
BF16 = jnp.bfloat16

D_MODEL = 1024
CHUNK = 64
ATT_REACH = 512
BAND = ATT_REACH + CHUNK
N_HEADS = 8
HEAD = 64
HEADS_DIM = N_HEADS * HEAD
REL_CLIP = 128
LORA_W = 64
LORA_A = 64
LORA_G = 128
ATT_COLS = 3 * HEADS_DIM
RW_COLS = 3 * HEADS_DIM + LORA_W + LORA_A + LORA_G
GATE_COLS = 2 * D_MODEL
D_FF = 2816
CONV_W = 3
LN_EPS = 1e-5
GN_EPS = 64e-5
DEPTH = 1
ALPHA = (2 * DEPTH) ** 0.25
LOG2E = 1.4426950408889634

GROUP = 256
ROW_TILE = 512
WKV_PIECE = 256
VMEM_LIMIT = 56 * 1024 * 1024


def _const_spec(shape):
    nd = len(shape)
    return pl.BlockSpec(shape, lambda *_: (0,) * nd, pipeline_mode=pl.Buffered(1))


def _layer_norm(x, g, b):
    mu = jnp.mean(x, axis=-1, keepdims=True)
    xc = x - mu
    var = jnp.mean(xc * xc, axis=-1, keepdims=True)
    return xc * lax.rsqrt(var + LN_EPS) * g + b


def _sigmoid(x):
    return 1.0 / (1.0 + jnp.exp(-x))


def _split3(x):
    hi = x.astype(BF16)
    r1 = x - hi.astype(F32)
    mid = r1.astype(BF16)
    lo = (r1 - mid.astype(F32)).astype(BF16)
    return hi, mid, lo


def _dot(a, b):
    return jnp.dot(a, b, preferred_element_type=F32)


def _dot_nt(a, b):
    return lax.dot_general(a, b, (((1,), (1,)), ((), ())), preferred_element_type=F32)


def _dot_tn(a, b):
    return lax.dot_general(a, b, (((0,), (0,)), ((), ())), preferred_element_type=F32)


def _mod_kernel(c_ref, w_ref, b_ref, o_ref):
    c = c_ref[...]
    s = (c * _sigmoid(c)).astype(BF16)
    o_ref[...] = _dot(s, w_ref[...].astype(BF16)) + b_ref[...]


def _mod_call(c_all, w_ada, b_ada):
    n = c_all.shape[0]
    nblk = 6
    return pl.pallas_call(
        _mod_kernel,
        grid=(nblk,),
        in_specs=[
            pl.BlockSpec((n, D_MODEL), lambda i: (0, 0)),
            pl.BlockSpec((D_MODEL, D_MODEL), lambda i: (0, i)),
            pl.BlockSpec((1, D_MODEL), lambda i: (0, i)),
        ],
        out_specs=pl.BlockSpec((n, D_MODEL), lambda i: (0, i)),
        out_shape=jax.ShapeDtypeStruct((n, 6 * D_MODEL), F32),
        compiler_params=pltpu.CompilerParams(dimension_semantics=("arbitrary",)),
        name="mod",
    )(c_all, w_ada, b_ada)


def _mod_row(mod_ref, idx, tm):
    groups, _, d = mod_ref.shape
    m = mod_ref[:, idx:idx + 1, :]
    if groups == 1:
        return m[0]
    return jnp.broadcast_to(m, (groups, tm // groups, d)).reshape(tm, d)


def _inproj_kernel(x_ref, xnext_ref, mod_ref, lng_ref, lnb_ref, w_ref,
                   q_ref, k_ref, v_ref, k32_ref, v32_ref, p_ref, g_ref, xn_ref,
                   xna_scr, hba_scr, *, lead):
    tm = x_ref.shape[0]
    j = pl.program_id(0)

    first, second = slice(0, tm // 2), slice(tm // 2, tm)

    def front(src_ref, rs):
        def mrow(idx):
            m = _mod_row(mod_ref, idx, tm)
            return m if m.shape[0] == 1 else m[rs]

        xn = _layer_norm(src_ref[rs, :], lng_ref[...], lnb_ref[...])
        return xn, (xn * (1.0 + mrow(1)) + mrow(0)).astype(BF16)

    def half_tile(rs, prepared=False):
        if prepared:
            xn, hb = xna_scr[...], hba_scr[...]
        else:
            xn, hb = front(x_ref, rs)
        xn_ref[rs, :] = xn

        def seg(a, b):
            return _dot(hb, w_ref[:, a:b])

        g0 = ATT_COLS + RW_COLS
        for c in range(0, GATE_COLS, 512):
            g_ref[rs, c:c + 512] = _sigmoid(seg(g0 + c, g0 + c + 512))
        q_ref[rs, :] = (seg(0, HEADS_DIM) * (HEAD ** -0.5 * LOG2E)).astype(BF16)
        k = seg(HEADS_DIM, 2 * HEADS_DIM)
        k_ref[rs, :] = k.astype(BF16)
        k32_ref[rs, :] = k
        v = seg(2 * HEADS_DIM, 3 * HEADS_DIM)
        v_ref[rs, :] = v.astype(BF16)
        v32_ref[rs, :] = v
        for c in range(0, RW_COLS, 256):
            p_ref[rs, c:c + 256] = seg(ATT_COLS + c, ATT_COLS + c + 256)

    def prepare_next():
        xna_scr[...], hba_scr[...] = front(xnext_ref, first)

    @pl.when(j < lead)
    def _():
        k_ref[...] = jnp.zeros_like(k_ref)
        v_ref[...] = jnp.zeros_like(v_ref)
        prepare_next()

    @pl.when(j >= lead)
    def _():
        half_tile(first, prepared=lead > 0)
        if lead > 0:
            prepare_next()
        half_tile(second)


def _inproj_call(x2d, mod, ln_g, ln_b, w_in_b, steps, tm, prompt):
    rows = x2d.shape[0]
    in_cols = w_in_b.shape[1]
    groups = mod.shape[0]
    if prompt:
        lead = ATT_REACH // tm
        kv_rows = ATT_REACH
        row_map = lambda j: (jnp.maximum(j - lead, 0), 0)
        next_map = lambda j: (jnp.clip(j + 1 - lead, 0, steps - 1), 0)
        kv_map = lambda j: (jnp.maximum(j - steps, 0), 0)
        ext_map = lambda j: (j, 0)
    else:
        lead = 0
        kv_rows = rows
        row_map = next_map = kv_map = ext_map = lambda j: (j, 0)
    return pl.pallas_call(
        functools.partial(_inproj_kernel, lead=lead),
        grid=(steps + lead,),
        in_specs=[
            pl.BlockSpec((tm, D_MODEL), row_map),
            pl.BlockSpec((tm, D_MODEL), next_map),
            _const_spec((groups, 6, D_MODEL)),
            _const_spec((1, D_MODEL)),
            _const_spec((1, D_MODEL)),
            _const_spec((D_MODEL, in_cols)),
        ],
        scratch_shapes=[pltpu.VMEM((tm // 2, D_MODEL), F32), pltpu.VMEM((tm // 2, D_MODEL), BF16)],
        out_specs=[
            pl.BlockSpec((tm, HEADS_DIM), row_map),
            pl.BlockSpec((tm, HEADS_DIM), ext_map),
            pl.BlockSpec((tm, HEADS_DIM), ext_map),
            pl.BlockSpec((tm, HEADS_DIM), kv_map),
            pl.BlockSpec((tm, HEADS_DIM), kv_map),
            pl.BlockSpec((tm, RW_COLS), row_map),
            pl.BlockSpec((tm, GATE_COLS), row_map),
            pl.BlockSpec((tm, D_MODEL), row_map),
        ],
        out_shape=[
            jax.ShapeDtypeStruct((rows, HEADS_DIM), BF16),
            jax.ShapeDtypeStruct((rows + lead * tm, HEADS_DIM), BF16),
            jax.ShapeDtypeStruct((rows + lead * tm, HEADS_DIM), BF16),
            jax.ShapeDtypeStruct((kv_rows, HEADS_DIM), F32),
            jax.ShapeDtypeStruct((kv_rows, HEADS_DIM), F32),
            jax.ShapeDtypeStruct((rows, RW_COLS), F32),
            jax.ShapeDtypeStruct((rows, GATE_COLS), F32),
            jax.ShapeDtypeStruct((rows, D_MODEL), F32),
        ],
        compiler_params=pltpu.CompilerParams(
            dimension_semantics=("arbitrary",), vmem_limit_bytes=VMEM_LIMIT),
        name="inproj",
    )(x2d, x2d, mod, ln_g, ln_b, w_in_b)


def _attn_chunks(chunks, bias_ref):
    lane = lax.broadcasted_iota(jnp.int32, (CHUNK, 128), 1)
    first = lane < HEAD
    pairs = [slice(pr * 128, (pr + 1) * 128) for pr in range(N_HEADS // 2)]
    scores = []
    for qc, kb, _, _ in chunks:
        for sl in pairs:
            q2 = qc[:, sl].astype(F32)
            qs = jnp.concatenate([jnp.where(first, q2, 0.0), jnp.where(first, 0.0, q2)],
                                 axis=0).astype(BF16)
            scores.append(_dot_nt(qs, kb(sl)))
    probs, sums = [], []
    for i, s in enumerate(scores):
        thr = chunks[i // len(pairs)][3]
        s = s + bias_ref[i % len(pairs)]
        if thr is not None:
            col = lax.broadcasted_iota(jnp.int32, s.shape, 1)
            s = jnp.where(col >= thr, s, -jnp.inf)
        e = jnp.exp2(s - jnp.max(s, axis=1, keepdims=True))
        sums.append(jnp.sum(e, axis=1, keepdims=True))
        probs.append(e.astype(BF16))
    outs = []
    for ci, (_, _, vb, _) in enumerate(chunks):
        cols = []
        for pi, sl in enumerate(pairs):
            i = ci * len(pairs) + pi
            o = _dot(probs[i], vb(sl)) / sums[i]
            cols.append(jnp.where(first, o[0:CHUNK], o[CHUNK:2 * CHUNK]))
        outs.append(jnp.concatenate(cols, axis=1).astype(BF16))
    return outs


def _attn_prompt_kernel(q_ref, kbuf, vbuf, bias_ref, o_ref, *, chunks):
    s = pl.program_id(0)

    per = 2

    def run(masked):
        for i in range(chunks // per):
            units, starts = [], []
            for k in range(per):
                g = i * per + k
                r0 = g * CHUNK
                thr = ATT_REACH - (s * chunks + g) * CHUNK if masked else None
                band = slice(r0, r0 + BAND)
                units.append((q_ref[r0:r0 + CHUNK, :],
                              lambda sl, band=band: kbuf[band, sl],
                              lambda sl, band=band: vbuf[band, sl], thr))
                starts.append(r0)
            for r0, o in zip(starts, _attn_chunks(units, bias_ref)):
                o_ref[r0:r0 + CHUNK, :] = o

    assert chunks * CHUNK >= ATT_REACH
    pl.when(s == 0)(functools.partial(run, True))
    pl.when(s != 0)(functools.partial(run, False))


def _attn_prompt_call(q, kext, vext, bias):
    rows = q.shape[0]
    tq = ATT_REACH
    blk = pl.BlockSpec((tq, HEADS_DIM), lambda s: (s, 0))
    window = pl.BlockSpec((pl.Element(tq + ATT_REACH), pl.Element(HEADS_DIM)),
                          lambda s: (s * tq, 0))
    return pl.pallas_call(
        functools.partial(_attn_prompt_kernel, chunks=tq // CHUNK),
        grid=(rows // tq,),
        in_specs=[blk, window, window, _const_spec(bias.shape)],
        out_specs=blk,
        out_shape=jax.ShapeDtypeStruct((rows, HEADS_DIM), BF16),
        compiler_params=pltpu.CompilerParams(dimension_semantics=("arbitrary",)),
        name="attn_prompt",
    )(q, kext, vext, bias)


def _attn_sample_kernel(q_ref, k_ref, v_ref, ck_ref, cv_ref, bias_ref, o_ref):
    def band(cache_ref, new_ref, rs):
        return lambda sl: jnp.concatenate([cache_ref[:, sl], new_ref[rs, sl]], axis=0)

    units = []
    for s in range(ck_ref.shape[0]):
        rs = slice(s * CHUNK, (s + 1) * CHUNK)
        units.append((q_ref[rs, :], band(ck_ref.at[s], k_ref, rs), band(cv_ref.at[s], v_ref, rs),
                      None))
    for s, o in enumerate(_attn_chunks(units, bias_ref)):
        o_ref[s * CHUNK:(s + 1) * CHUNK, :] = o


def _attn_sample_call(q, k, v, cache_k, cache_v, bias):
    per = 2
    nb = cache_k.shape[0] // per
    new = pl.BlockSpec((per * CHUNK, HEADS_DIM), lambda b: (b, 0))
    cache = pl.BlockSpec((per, ATT_REACH, HEADS_DIM), lambda b: (b, 0, 0))
    return pl.pallas_call(
        _attn_sample_kernel,
        grid=(nb,),
        in_specs=[new, new, new, cache, cache, _const_spec(bias.shape)],
        out_specs=new,
        out_shape=jax.ShapeDtypeStruct(q.shape, BF16),
        compiler_params=pltpu.CompilerParams(dimension_semantics=("arbitrary",)),
        name="attn_sample",
    )(q, k, v, cache_k, cache_v, bias)


def _same_head():
    r = lax.broadcasted_iota(jnp.int32, (GROUP, GROUP), 0) // HEAD
    c = lax.broadcasted_iota(jnp.int32, (GROUP, GROUP), 1) // HEAD
    return r == c


def _blk(x, same_head):
    return jnp.where(same_head, jnp.concatenate([x] * 4, axis=0), 0.0).astype(BF16)


def _wkv_prep(units, hooks):
    same_head = _same_head()
    t64 = lax.broadcasted_iota(jnp.int32, (CHUNK, GROUP), 0)
    i64 = lax.broadcasted_iota(jnp.int32, (CHUNK, GROUP), 1) % HEAD
    strict = i64 < t64
    incl = i64 <= t64
    pending = list(hooks)

    def stage_done():
        if pending:
            pending.pop(0)()

    def blk(x):
        return _blk(x, same_head)

    n, a_ak, a_rb, a_rk = [], [], [], []
    for ld in units:
        lhs = jnp.concatenate([ld("at"), ld("rt")], axis=0).astype(BF16)
        rhs = jnp.concatenate([blk(ld("bt")), blk(ld("kt"))], axis=0)
        a_all = _dot_nt(lhs, rhs)
        n.append(jnp.where(strict, a_all[0:CHUNK, 0:GROUP], 0.0))
        a_ak.append(jnp.where(strict, a_all[0:CHUNK, GROUP:], 0.0))
        a_rb.append(jnp.where(incl, a_all[CHUNK:, 0:GROUP], 0.0))
        a_rk.append(jnp.where(incl, a_all[CHUNK:, GROUP:], 0.0))
    stage_done()

    x0 = [_dot(a.astype(BF16), blk(ld("v"))) for a, ld in zip(a_ak, units)]
    npow = [_dot(m.astype(BF16), blk(m)) for m in n]
    t = [jnp.where(i64 == t64, 1.0, 0.0) + m for m in n]
    stage_done()
    for _ in range(4):
        prod = [_dot(jnp.concatenate([a, b], axis=0).astype(BF16), blk(b))
                for a, b in zip(t, npow)]
        t = [a + p[0:CHUNK] for a, p in zip(t, prod)]
        npow = [p[CHUNK:] for p in prod]
        stage_done()
    t = [a + _dot(a.astype(BF16), blk(b)) for a, b in zip(t, npow)]
    stage_done()
    res = [_dot(a.astype(BF16), jnp.concatenate([blk(ld("at")), blk(x)], axis=1))
           for a, x, ld in zip(t, x0, units)]
    stage_done()
    while pending:
        stage_done()
    return [(r[:, 0:GROUP], r[:, GROUP:], b, k) for r, b, k in zip(res, a_rb, a_rk)]


def _wkv_step_stages(units, states, out):
    same_head = _same_head()
    held = {}

    def blk(x):
        return _blk(x, same_head)

    def first():
        held["sblk"] = [blk(st) for st in states]
        held["u"] = [_dot(ld("ah").astype(BF16), sb) + ld("vh")
                     for ld, sb in zip(units, held["sblk"])]

    def second():
        for i, ld in enumerate(units):
            u, v = held["u"][i], ld("v")
            y = _dot(jnp.concatenate([ld("rt"), ld("arb"), ld("ark")], axis=1).astype(BF16),
                     jnp.concatenate([held["sblk"][i], blk(u), blk(v)], axis=0))
            g = _dot_tn(jnp.concatenate([ld("btd"), ld("ktd")], axis=0).astype(BF16),
                        jnp.concatenate([u, v], axis=0).astype(BF16))
            g = jnp.where(same_head, g, 0.0)
            delta = g[0:64] + g[64:128] + g[128:192] + g[192:256]
            out.append(y)
            states[i] = states[i] * ld("fdec") + delta

    return [first, second]


def _wkv_kernel(p_ref, pn_ref, shift0_ref, state0_ref, mu_ref, w0_ref, wwa_ref, a0_ref, gup_ref,
                kkw_ref, kaw_ref, rkw_ref, gng_ref, gnb_ref, ones_ref, ltri_ref,
                out_ref, shift_out_ref, state_out_ref,
                st_scr, rt_scr, at_scr, kt_scr, bt_scr, v_scr, btd_scr, ktd_scr,
                fdec_scr, ah_scr, vh_scr, arb_scr, ark_scr, g_scr, bon_scr, *, ts, steps):
    j = pl.program_id(0)
    streams = state0_ref.shape[0]
    assert streams == 1 or (steps == 1 and ts == streams * CHUNK)

    if streams == 1:
        @pl.when(j == 0)
        def _():
            st_scr[...] = state0_ref[0]

    def bdsum(x, terms):
        ones = ones_ref[...]
        halves = []
        for c0 in range(0, HEADS_DIM, GROUP):
            parts = _split3(x[:, c0:c0 + GROUP])[:terms]
            acc = _dot(parts[0], ones)
            for part in parts[1:]:
                acc = acc + _dot(part, ones)
            halves.append(acc)
        return jnp.concatenate(halves, axis=1)

    pq = ltri_ref.shape[0]
    per = pq // CHUNK
    n_piece = ts // pq

    def pre_stages(q, upcoming=False):
        r0 = q * pq
        rows = slice(r0, r0 + pq)
        src_ref = pn_ref if upcoming else p_ref
        held = {}

        def shifted(c0, w):
            cols = slice(c0, c0 + w)
            row = lax.broadcasted_iota(jnp.int32, (pq, 1), 0)
            if streams > 1:
                first = shift0_ref[q * per:(q + 1) * per, :, cols]
                prev_row = jnp.broadcast_to(first, (per, CHUNK, w)).reshape(pq, w)
                top = row % CHUNK == 0
            else:
                if upcoming:
                    prev_row = p_ref[ts - 8:ts, cols][7:8]
                elif q == 0:
                    prev_row = shift0_ref[0][:, cols]
                else:
                    prev_row = p_ref[r0 - 8:r0, cols][7:8]
                top = row == 0
            pb = src_ref[rows, cols]
            prev = jnp.where(top, prev_row, pltpu.roll(pb, 1, 0))
            return pb + (prev - pb) * mu_ref[:, cols]

        def s_lora():
            lora = shifted(3 * HEADS_DIM, 256)
            lwla = lora[:, 0:128]
            lane = lax.broadcasted_iota(jnp.int32, (pq, 128), 1)
            held["raw"] = _dot(jnp.where(lane < LORA_W, jnp.tanh(lwla), lwla).astype(BF16),
                               wwa_ref[...])
            g_scr[rows, :] = _dot(_sigmoid(lora[:, 128:256]).astype(BF16), gup_ref[...])
            held["k"] = shifted(HEADS_DIM, HEADS_DIM)

        def s_decay():
            w_pre = w0_ref[...] + held["raw"][:, 0:HEADS_DIM]
            softplus = jnp.maximum(-w_pre, 0.0) + jnp.log(1.0 + jnp.exp(-jnp.abs(w_pre)))
            lw = -jnp.exp(-softplus - 0.5)
            ltri = ltri_ref[...]
            hi, mid, _ = _split3(lw)
            held["cum"] = _dot(ltri, hi) + _dot(ltri, mid)
            held["lw"] = lw

        def s_keys():
            k = held["k"]
            a = _sigmoid(a0_ref[...] + held["raw"][:, HEADS_DIM:])
            kk = k * kkw_ref[...]
            held["nrm"] = bdsum(kk * kk, 1)
            held.update(a=a, kk=kk, k2=k * (1.0 + (a - 1.0) * kaw_ref[...]))

        def s_bonus():
            r = shifted(0, HEADS_DIM)
            held["bon"] = bdsum(r * held["k2"] * rkw_ref[...], 1)
            rt_scr[rows, :] = r * jnp.exp(held["cum"])

        def s_norm():
            cum = held["cum"]
            kk = held["kk"] / jnp.maximum(jnp.sqrt(held["nrm"]), 1e-12)
            cum3 = cum.reshape(per, CHUNK, HEADS_DIM)
            cend = jnp.broadcast_to(cum3[:, CHUNK - 1:CHUNK, :], cum3.shape).reshape(pq, HEADS_DIM)
            diag = (lax.broadcasted_iota(jnp.int32, (pq, HEADS_DIM), 0) % CHUNK
                    == lax.broadcasted_iota(jnp.int32, (pq, HEADS_DIM), 1) % HEAD)
            held["fdec"] = bdsum(jnp.where(diag, jnp.exp(cend), 0.0), 2)
            at_scr[rows, :] = -kk * jnp.exp(cum - held["lw"])
            held.update(kk=kk, cend=cend)

        def s_inv():
            e_inv = jnp.exp(-held["cum"])
            kt_scr[rows, :] = held["k2"] * e_inv
            bt_scr[rows, :] = held["kk"] * held["a"] * e_inv

        def s_rel():
            e_rel = jnp.exp(held["cend"] - held["cum"])
            ktd_scr[rows, :] = held["k2"] * e_rel
            btd_scr[rows, :] = held["kk"] * held["a"] * e_rel
            fdec_scr[rows, :] = held["fdec"]

        def s_value():
            v = shifted(2 * HEADS_DIM, HEADS_DIM)
            v_scr[rows, :] = v
            bon_scr[rows, :] = held["bon"] * v

        return [s_lora, s_decay, s_keys, s_bonus, s_norm, s_inv, s_rel, s_value]

    def chunk_rows(c):
        return slice(c * CHUNK, (c + 1) * CHUNK)

    groups = [slice(g0, g0 + GROUP) for g0 in range(0, HEADS_DIM, GROUP)]

    named = dict(at=at_scr, rt=rt_scr, kt=kt_scr, bt=bt_scr, v=v_scr, btd=btd_scr, ktd=ktd_scr,
                 fdec=fdec_scr, ah=ah_scr, vh=vh_scr, arb=arb_scr, ark=ark_scr)
    prep_outs = (ah_scr, vh_scr, arb_scr, ark_scr)

    def loader(c, cols):
        rows = chunk_rows(c)
        return lambda name: named[name][rows, cols]

    def run(step_piece, prep_piece, extra):
        step_chunks = [] if step_piece is None else [step_piece * per + i for i in range(per)]
        prep_chunks = [] if prep_piece is None else [prep_piece * per + i for i in range(per)]
        stages, ys, finals = [], [], []
        if streams == 1:
            states = [st_scr[:, cols] for cols in groups] if step_chunks else []
            for c in step_chunks:
                out = []
                ys.append(out)
                stages += _wkv_step_stages([loader(c, cols) for cols in groups], states, out)
            finals = [(st_scr, states)] if step_chunks else []
        else:
            firsts, seconds = [], []
            for c in step_chunks:
                out = []
                ys.append(out)
                states = [state0_ref[c, :, cols] for cols in groups]
                first, second = _wkv_step_stages(
                    [loader(c, cols) for cols in groups], states, out)
                firsts.append(first)
                seconds.append(second)
                finals.append((state_out_ref.at[c], states))
            if step_chunks:
                stages += [lambda: [f() for f in firsts], lambda: [f() for f in seconds]]
        if step_chunks:
            rows = slice(step_piece * pq, (step_piece + 1) * pq)
            gn = {}

            def gn_mean():
                gn["y"] = jnp.concatenate([jnp.concatenate(out, axis=1) for out in ys], axis=0)
                gn["sum"] = bdsum(gn["y"], 1)

            def gn_var():
                gn["d"] = gn["y"] - gn["sum"] * (1.0 / HEAD)
                gn["sq"] = bdsum(gn["d"] * gn["d"], 1)

            def gn_out():
                var = gn["sq"] * (1.0 / HEAD)
                yn = (gn["d"] * lax.rsqrt(var + GN_EPS) * gng_ref[...] + gnb_ref[...]
                      + bon_scr[rows, :])
                out_ref[rows, :] = (yn * g_scr[rows, :]).astype(BF16)

            stages += [gn_mean, gn_var, gn_out]
        hooks = []
        for i in range(max(len(stages), len(extra))):
            both = stages[i:i + 1] + extra[i:i + 1]
            hooks.append(lambda both=both: [f() for f in both])
        units = [(c, cols) for c in prep_chunks for cols in groups]
        prep_out = _wkv_prep([loader(c, cols) for c, cols in units], hooks)
        for ref, states in finals:
            for cols, st in zip(groups, states):
                ref[:, cols] = st
        for (c, cols), outs in zip(units, prep_out):
            for ref, val in zip(prep_outs, outs):
                ref[chunk_rows(c), cols] = val

    look_ahead = steps > 1
    ahead_pass = max(2, n_piece - 1)
    assert ahead_pass <= n_piece or not look_ahead

    def first_piece():
        for stage in pre_stages(0):
            stage()

    if look_ahead:
        pl.when(j == 0)(first_piece)
    else:
        first_piece()
    for piece in range(n_piece + 1):
        if piece + 1 < n_piece:
            extra = pre_stages(piece + 1)
        elif piece == ahead_pass and look_ahead:
            extra = pre_stages(0, upcoming=True)
        else:
            extra = []
        run(piece - 1 if piece >= 1 else None, piece if piece < n_piece else None, extra)

    if streams == 1:
        @pl.when(j == steps - 1)
        def _():
            shift_out_ref[0] = p_ref[ts - 1:ts, :]
            state_out_ref[0] = st_scr[...]
    else:
        for s in range(streams):
            shift_out_ref[s] = p_ref[(s + 1) * CHUNK - 1:(s + 1) * CHUNK, :]


def _wkv_call(p2d, shift0, state0, prm, steps, ts):
    rows = p2d.shape[0]
    streams = state0.shape[0]
    row_map = lambda j: (j, 0)
    next_map = lambda j: (jnp.minimum(j + 1, steps - 1), 0)
    seq3 = lambda j: (0, 0, 0)
    vec = _const_spec((1, HEADS_DIM))
    big = pltpu.VMEM((ts, HEADS_DIM), F32)
    ltri = prm["ltri"][min(ts, WKV_PIECE)]
    return pl.pallas_call(
        functools.partial(_wkv_kernel, ts=ts, steps=steps),
        grid=(steps,),
        in_specs=[
            pl.BlockSpec((ts, RW_COLS), row_map),
            pl.BlockSpec((ts, RW_COLS), next_map),
            pl.BlockSpec((streams, 1, RW_COLS), seq3),
            pl.BlockSpec((streams, HEAD, HEADS_DIM), seq3),
            _const_spec((1, RW_COLS)),
            vec,
            _const_spec((128, 2 * HEADS_DIM)),
            vec,
            _const_spec((LORA_G, HEADS_DIM)),
            vec, vec, vec, vec, vec,
            _const_spec((GROUP, GROUP)),
            _const_spec(ltri.shape),
        ],
        out_specs=[
            pl.BlockSpec((ts, HEADS_DIM), row_map),
            pl.BlockSpec((streams, 1, RW_COLS), seq3),
            pl.BlockSpec((streams, HEAD, HEADS_DIM), seq3),
        ],
        out_shape=[
            jax.ShapeDtypeStruct((rows, HEADS_DIM), BF16),
            jax.ShapeDtypeStruct((streams, 1, RW_COLS), F32),
            jax.ShapeDtypeStruct((streams, HEAD, HEADS_DIM), F32),
        ],
        scratch_shapes=[
            pltpu.VMEM((HEAD, HEADS_DIM), F32),
        ] + [big] * 14,
        compiler_params=pltpu.CompilerParams(
            dimension_semantics=("arbitrary",), vmem_limit_bytes=VMEM_LIMIT),
        name="wkv",
    )(p2d, p2d, shift0, state0, prm["mu"], prm["w0"], prm["wwa"], prm["a0"], prm["gup"],
      prm["kk"], prm["ka"], prm["rk"], prm["gng"], prm["gnb"], prm["ones"], ltri)


def _mixffn_kernel(xn_ref, att_ref, rw_ref, gate_ref, mod_ref, convp_ref,
                   ln1g_ref, ln1b_ref, ln2g_ref, ln2b_ref,
                   wa_ref, wr_ref, wo_ref, wup_ref, cw_ref, cb_ref, wdn_ref,
                   y_ref, convo_ref, carry_scr, yb_scr, x1_scr, h2_scr, *, tm, steps):
    j = pl.program_id(0)
    groups = mod_ref.shape[0]
    assert groups == 1 or steps == 1
    pipelined = steps > 1

    def mod(idx):
        return _mod_row(mod_ref, idx, tm)

    half = tm // 2
    subs = [slice(0, half), slice(half, tm)]

    def modr(idx, rs):
        m = mod(idx)
        return m if m.shape[0] == 1 else m[rs]

    def front():
        ma = [_dot(att_ref[rs, :], wa_ref[...]) for rs in subs]
        mr = [_dot(rw_ref[rs, :], wr_ref[...]) for rs in subs]
        merged = [(gate_ref[rs, 0:D_MODEL] * a + gate_ref[rs, D_MODEL:] * r).astype(BF16)
                  for rs, a, r in zip(subs, ma, mr)]
        mix = [_dot(m, wo_ref[...]) for m in merged]
        x1 = [_layer_norm(ALPHA * xn_ref[rs, :] + (1.0 + modr(2, rs)) * m,
                          ln1g_ref[...], ln1b_ref[...]) for rs, m in zip(subs, mix)]
        h2 = jnp.concatenate([(a * (1.0 + modr(4, rs)) + modr(3, rs)).astype(BF16)
                              for rs, a in zip(subs, x1)], axis=0)
        return x1, h2

    def keep(x1, h2):
        for rs, a in zip(subs, x1):
            x1_scr[rs, :] = a
        h2_scr[...] = h2

    if not pipelined:
        if groups == 1:
            carry_scr[8 - (CONV_W - 1):8, :] = convp_ref[0]
        _ffn_back(*front(), subs, modr, groups, tm, convp_ref, ln2g_ref, ln2b_ref, wup_ref, cw_ref,
                  cb_ref, wdn_ref, y_ref, convo_ref, carry_scr, yb_scr)
        return

    @pl.when(j == 0)
    def _():
        carry_scr[8 - (CONV_W - 1):8, :] = convp_ref[0]
        keep(*front())

    @pl.when(j > 0)
    def _():
        x1 = [x1_scr[rs, :] for rs in subs]
        h2 = h2_scr[...]
        keep(*front())
        _ffn_back(x1, h2, subs, modr, groups, tm, convp_ref, ln2g_ref, ln2b_ref, wup_ref, cw_ref,
                  cb_ref, wdn_ref, y_ref, convo_ref, carry_scr, yb_scr)


def _ffn_back(x1, h2, subs, modr, groups, tm, convp_ref, ln2g_ref, ln2b_ref, wup_ref, cw_ref,
              cb_ref, wdn_ref, y_ref, convo_ref, carry_scr, yb_scr):
    cw_blk = 256
    glen = tm // groups
    row8 = lax.broadcasted_iota(jnp.int32, (8, cw_blk), 0)
    grow = lax.broadcasted_iota(jnp.int32, (tm, cw_blk), 0) % glen
    for c in range(0, D_FF, cw_blk):
        cs = slice(c, c + cw_blk)
        uc = _dot(h2, wup_ref[:, cs])
        uv = _dot(h2, wup_ref[:, D_FF + c:D_FF + c + cw_blk])
        r1 = pltpu.roll(uc, 1, 0)
        r2 = pltpu.roll(uc, 2, 0)
        if groups == 1:
            c6 = carry_scr[6:7, cs]
            c7 = carry_scr[7:8, cs]
            s1 = jnp.concatenate([jnp.where(row8 == 0, c7, r1[0:8]), r1[8:]], axis=0)
            s2 = jnp.concatenate(
                [jnp.where(row8 == 0, c6, jnp.where(row8 == 1, c7, r2[0:8])), r2[8:]], axis=0)
            carry_scr[:, cs] = uc[tm - 8:tm, :]
            tail = uc[tm - (CONV_W - 1):tm, :][None]
        else:
            hist = jnp.broadcast_to(convp_ref[:, :, cs][:, :, None, :],
                                    (groups, CONV_W - 1, glen, cw_blk))
            c6 = hist[:, 0].reshape(tm, cw_blk)
            c7 = hist[:, 1].reshape(tm, cw_blk)
            s1 = jnp.where(grow == 0, c7, r1)
            s2 = jnp.where(grow == 0, c6, jnp.where(grow == 1, c7, r2))
            tail = uc.reshape(groups, glen, cw_blk)[:, glen - (CONV_W - 1):, :]
        conv = cb_ref[:, cs] + s2 * cw_ref[0:1, cs] + s1 * cw_ref[1:2, cs] + uc * cw_ref[2:3, cs]
        yb_scr[:, cs] = (conv * _sigmoid(conv) * uv).astype(BF16)
        convo_ref[:, :, cs] = tail

    ff = [_dot(yb_scr[rs, :], wdn_ref[...]) for rs in subs]
    for rs, a, f in zip(subs, x1, ff):
        y_ref[rs, :] = _layer_norm(ALPHA * a + (1.0 + modr(5, rs)) * f, ln2g_ref[...], ln2b_ref[...])


def _mixffn_call(xn2d, att, rw, gates, mod, conv_prev, prm, steps, tm):
    rows = xn2d.shape[0]
    groups = mod.shape[0]
    lag = 1 if steps > 1 else 0
    row_map = lambda j: (jnp.minimum(j, steps - 1), 0)
    out_map = lambda j: (jnp.maximum(j - lag, 0), 0)
    vec = _const_spec((1, D_MODEL))
    return pl.pallas_call(
        functools.partial(_mixffn_kernel, tm=tm, steps=steps),
        grid=(steps + lag,),
        in_specs=[
            pl.BlockSpec((tm, D_MODEL), row_map),
            pl.BlockSpec((tm, HEADS_DIM), row_map),
            pl.BlockSpec((tm, HEADS_DIM), row_map),
            pl.BlockSpec((tm, GATE_COLS), row_map),
            _const_spec((groups, 6, D_MODEL)),
            _const_spec((groups, CONV_W - 1, D_FF)),
            vec, vec, vec, vec,
            _const_spec((HEADS_DIM, D_MODEL)),
            _const_spec((HEADS_DIM, D_MODEL)),
            _const_spec((D_MODEL, D_MODEL)),
            _const_spec((D_MODEL, 2 * D_FF)),
            _const_spec((CONV_W, D_FF)),
            _const_spec((1, D_FF)),
            _const_spec((D_FF, D_MODEL)),
        ],
        out_specs=[
            pl.BlockSpec((tm, D_MODEL), out_map),
            pl.BlockSpec((groups, CONV_W - 1, D_FF), lambda j: (0, 0, 0)),
        ],
        out_shape=[
            jax.ShapeDtypeStruct((rows, D_MODEL), F32),
            jax.ShapeDtypeStruct((groups, CONV_W - 1, D_FF), F32),
        ],
        scratch_shapes=[pltpu.VMEM((8, D_FF), F32), pltpu.VMEM((tm, D_FF), BF16),
                        pltpu.VMEM((tm, D_MODEL), F32), pltpu.VMEM((tm, D_MODEL), BF16)],
        compiler_params=pltpu.CompilerParams(
            dimension_semantics=("arbitrary",), vmem_limit_bytes=VMEM_LIMIT),
        name="mixffn",
    )(xn2d, att, rw, gates, mod, conv_prev,
      prm["ln1g"], prm["ln1b"], prm["ln2g"], prm["ln2b"],
      prm["wa"], prm["wr"], prm["wo"], prm["wup"], prm["cw"], prm["cb"], prm["wdn"])


def _pair_bias(table):
    assert CHUNK - 1 <= REL_CLIP
    top = ATT_REACH + CHUNK - 1
    n_far = top - REL_CLIP + 1
    far = jnp.broadcast_to(table[:, 2 * REL_CLIP:], (N_HEADS, n_far))
    lo_idx = top - (BAND + CHUNK - 2) + REL_CLIP
    near = table[:, lo_idx:2 * REL_CLIP][:, ::-1]
    ext = jnp.concatenate([far, near], axis=1).astype(F32) * LOG2E
    n_ext = BAND + CHUNK - 1
    period = jnp.concatenate([ext, jnp.zeros((N_HEADS, 1), F32)], axis=1)
    skew = jnp.tile(period, (1, CHUNK))[:, :CHUNK * n_ext].reshape(N_HEADS, CHUNK, n_ext)
    bias = skew[:, :, CHUNK - 1:CHUNK - 1 + BAND]
    return bias.reshape(N_HEADS // 2, 2 * CHUNK, BAND)


def _chunk_ltri(ts):
    t = jnp.arange(ts)
    return ((t[:, None] // CHUNK == t[None, :] // CHUNK) & (t[None, :] <= t[:, None])).astype(BF16)


def _trunk(x2d, mod, shift0, state0, conv_prev, caches, prm, n_seq):
    prompt = caches is None
    rows = x2d.shape[0]
    tm = min(rows, ROW_TILE)
    steps = rows // tm
    q, k, v, k32, v32, p, gates, xn = _inproj_call(
        x2d, mod, prm["lnig"], prm["lnib"], prm["win"], steps, tm, prompt)
    if prompt:
        att = _attn_prompt_call(q, k, v, prm["bias"])
    else:
        att = _attn_sample_call(q, k, v, *caches, prm["bias"])
    assert state0.shape[0] == n_seq
    rw, shift, state = _wkv_call(p, shift0, state0, prm, steps, tm)
    y, conv = _mixffn_call(xn, att, rw, gates, mod, conv_prev, prm, steps, tm)
    return y, (k32, v32), state, shift, conv


def kernel(x_prompt, x_sample, cache_attn_k, cache_attn_v, state_rwkv, state_shift, state_conv,
           c_prompt, c_sample, ln_in_g, ln_in_b, w_ada, b_ada, w_in, attn_rel_bias,
           rwkv_mu, rwkv_w0, rwkv_w_up, rwkv_a0, rwkv_a_up, rwkv_g_up, rwkv_k_k, rwkv_k_a,
           rwkv_r_k, rwkv_gn_g, rwkv_gn_b, w_branch_attn, w_branch_rwkv, w_out,
           ln1_g, ln1_b, ln2_g, ln2_b, w_ffn_up, ffn_conv_w, ffn_conv_b, w_ffn_down):
    bp, sp, _ = x_prompt.shape
    bs, ss, _ = x_sample.shape
    assert bp == 1 and ss == CHUNK and w_ada.shape[0] == DEPTH
    assert sp % ROW_TILE == 0 and bs * ss <= ROW_TILE and cache_attn_k.shape[2] == ATT_REACH

    row = lambda a: a.reshape(1, -1)
    wwa = jnp.zeros((LORA_W + LORA_A, 2 * HEADS_DIM), F32)
    wwa = wwa.at[:LORA_W, :HEADS_DIM].set(rwkv_w_up[0]).at[LORA_W:, HEADS_DIM:].set(rwkv_a_up[0])
    head_id = jnp.arange(GROUP) // HEAD
    prm = dict(
        lnig=row(ln_in_g), lnib=row(ln_in_b),
        ln1g=row(ln1_g[0]), ln1b=row(ln1_b[0]), ln2g=row(ln2_g[0]), ln2b=row(ln2_b[0]),
        win=w_in[0].astype(BF16), bias=_pair_bias(attn_rel_bias[0]),
        mu=row(rwkv_mu[0]), w0=row(rwkv_w0[0]), wwa=wwa.astype(BF16), a0=row(rwkv_a0[0]),
        gup=rwkv_g_up[0].astype(BF16), kk=row(rwkv_k_k[0]), ka=row(rwkv_k_a[0]),
        rk=row(rwkv_r_k[0]), gng=row(rwkv_gn_g[0]), gnb=row(rwkv_gn_b[0]),
        ones=(head_id[:, None] == head_id[None, :]).astype(BF16),
        ltri={WKV_PIECE: _chunk_ltri(WKV_PIECE), CHUNK: _chunk_ltri(CHUNK)},
        wa=w_branch_attn[0].astype(BF16), wr=w_branch_rwkv[0].astype(BF16),
        wo=w_out[0].astype(BF16), wup=w_ffn_up[0].astype(BF16),
        cw=ffn_conv_w[0], cb=row(ffn_conv_b[0]), wdn=w_ffn_down[0].astype(BF16),
    )

    n_c = bp + bs
    c_all = jnp.concatenate([c_prompt, c_sample, jnp.zeros((16 - n_c, D_MODEL), F32)], axis=0)
    mod = _mod_call(c_all, w_ada[0], row(b_ada[0])).reshape(16, 6, D_MODEL)

    y_p, kv_p, st_p, sh_p, cv_p = _trunk(
        x_prompt.reshape(sp, D_MODEL), mod[0:bp],
        jnp.zeros((bp, 1, RW_COLS), F32), jnp.zeros((bp, HEAD, HEADS_DIM), F32),
        jnp.zeros((bp, CONV_W - 1, D_FF), F32), None, prm, n_seq=bp)

    caches = (cache_attn_k[0].astype(BF16).reshape(bs, ATT_REACH, HEADS_DIM),
              cache_attn_v[0].astype(BF16).reshape(bs, ATT_REACH, HEADS_DIM))
    st0 = jnp.transpose(state_rwkv[0], (0, 3, 1, 2)).reshape(bs, HEAD, HEADS_DIM)
    y_s, kv_s, st_s, sh_s, cv_s = _trunk(
        x_sample.reshape(bs * ss, D_MODEL), mod[bp:n_c],
        state_shift[0], st0, state_conv[0], caches, prm, n_seq=bs)

    def state_out(st, b):
        return jnp.transpose(st.reshape(b, HEAD, N_HEADS, HEAD), (0, 2, 3, 1))[None]

    hs = (N_HEADS, HEAD)
    return (
        y_p.reshape(bp, sp, D_MODEL),
        y_s.reshape(bs, ss, D_MODEL),
        kv_p[0].reshape(1, bp, ATT_REACH, *hs),
        kv_p[1].reshape(1, bp, ATT_REACH, *hs),
        kv_s[0].reshape(1, bs, ss, *hs),
        kv_s[1].reshape(1, bs, ss, *hs),
        state_out(st_p, bp),
        state_out(st_s, bs),
        sh_p[None],
        sh_s[None],
        cv_p[None],
        cv_s[None],
    )
```

```python
import functools

import jax
import jax.numpy as jnp
from jax import lax
from jax.experimental import pallas as pl
from jax.experimental.pallas import tpu as pltpu

F32 = jnp.float32
BF16 = jnp.bfloat16

D_MODEL = 1024
CHUNK = 64
ATT_REACH = 512
BAND = ATT_REACH + CHUNK
N_HEADS = 8
HEAD = 64
HEADS_DIM = N_HEADS * HEAD
REL_CLIP = 128
LORA_W = 64
LORA_A = 64
LORA_G = 128
ATT_COLS = 3 * HEADS_DIM
RW_COLS = 3 * HEADS_DIM + LORA_W + LORA_A + LORA_G
GATE_COLS = 2 * D_MODEL
D_FF = 2816
CONV_W = 3
LN_EPS = 1e-5
GN_EPS = 64e-5
DEPTH = 1
ALPHA = (2 * DEPTH) ** 0.25
LOG2E = 1.4426950408889634

GROUP = 256
ROW_TILE = 512
WKV_PIECE = 256
VMEM_LIMIT = 56 * 1024 * 1024


def _const_spec(shape):
    nd = len(shape)
    return pl.BlockSpec(shape, lambda *_: (0,) * nd, pipeline_mode=pl.Buffered(1))


def _layer_norm(x, g, b):
    mu = jnp.mean(x, axis=-1, keepdims=True)
    xc = x - mu
    var = jnp.mean(xc * xc, axis=-1, keepdims=True)
    return xc * lax.rsqrt(var + LN_EPS) * g + b


def _sigmoid(x):
    return 1.0 / (1.0 + jnp.exp(-x))


def _split3(x):
    hi = x.astype(BF16)
    r1 = x - hi.astype(F32)
    mid = r1.astype(BF16)
    lo = (r1 - mid.astype(F32)).astype(BF16)
    return hi, mid, lo


def _dot(a, b):
    return jnp.dot(a, b, preferred_element_type=F32)


def _dot_nt(a, b):
    return lax.dot_general(a, b, (((1,), (1,)), ((), ())), preferred_element_type=F32)


def _dot_tn(a, b):
    return lax.dot_general(a, b, (((0,), (0,)), ((), ())), preferred_element_type=F32)


def _mod_kernel(c_ref, w_ref, b_ref, o_ref):
    c = c_ref[...]
    s = (c * _sigmoid(c)).astype(BF16)
    o_ref[...] = _dot(s, w_ref[...].astype(BF16)) + b_ref[...]


def _mod_call(c_all, w_ada, b_ada):
    n = c_all.shape[0]
    nblk = 6
    return pl.pallas_call(
        _mod_kernel,
        grid=(nblk,),
        in_specs=[
            pl.BlockSpec((n, D_MODEL), lambda i: (0, 0)),
            pl.BlockSpec((D_MODEL, D_MODEL), lambda i: (0, i)),
            pl.BlockSpec((1, D_MODEL), lambda i: (0, i)),
        ],
        out_specs=pl.BlockSpec((n, D_MODEL), lambda i: (0, i)),
        out_shape=jax.ShapeDtypeStruct((n, 6 * D_MODEL), F32),
        compiler_params=pltpu.CompilerParams(dimension_semantics=("arbitrary",)),
        name="mod",
    )(c_all, w_ada, b_ada)


def _mod_row(mod_ref, idx, tm):
    groups, _, d = mod_ref.shape
    m = mod_ref[:, idx:idx + 1, :]
    if groups == 1:
        return m[0]
    return jnp.broadcast_to(m, (groups, tm // groups, d)).reshape(tm, d)


def _inproj_kernel(x_ref, xnext_ref, mod_ref, lng_ref, lnb_ref, w_ref,
                   q_ref, k_ref, v_ref, k32_ref, v32_ref, p_ref, g_ref, xn_ref,
                   xna_scr, hba_scr, *, lead):
    tm = x_ref.shape[0]
    j = pl.program_id(0)

    first, second = slice(0, tm // 2), slice(tm // 2, tm)

    def front(src_ref, rs):
        def mrow(idx):
            m = _mod_row(mod_ref, idx, tm)
            return m if m.shape[0] == 1 else m[rs]

        xn = _layer_norm(src_ref[rs, :], lng_ref[...], lnb_ref[...])
        return xn, (xn * (1.0 + mrow(1)) + mrow(0)).astype(BF16)

    def half_tile(rs, prepared=False):
        if prepared:
            xn, hb = xna_scr[...], hba_scr[...]
        else:
            xn, hb = front(x_ref, rs)
        xn_ref[rs, :] = xn

        def seg(a, b):
            return _dot(hb, w_ref[:, a:b])

        g0 = ATT_COLS + RW_COLS
        for c in range(0, GATE_COLS, 512):
            g_ref[rs, c:c + 512] = _sigmoid(seg(g0 + c, g0 + c + 512))
        q_ref[rs, :] = (seg(0, HEADS_DIM) * (HEAD ** -0.5 * LOG2E)).astype(BF16)
        k = seg(HEADS_DIM, 2 * HEADS_DIM)
        k_ref[rs, :] = k.astype(BF16)
        k32_ref[rs, :] = k
        v = seg(2 * HEADS_DIM, 3 * HEADS_DIM)
        v_ref[rs, :] = v.astype(BF16)
        v32_ref[rs, :] = v
        for c in range(0, RW_COLS, 256):
            p_ref[rs, c:c + 256] = seg(ATT_COLS + c, ATT_COLS + c + 256)

    def prepare_next():
        xna_scr[...], hba_scr[...] = front(xnext_ref, first)

    @pl.when(j < lead)
    def _():
        k_ref[...] = jnp.zeros_like(k_ref)
        v_ref[...] = jnp.zeros_like(v_ref)
        prepare_next()

    @pl.when(j >= lead)
    def _():
        half_tile(first, prepared=lead > 0)
        if lead > 0:
            prepare_next()
        half_tile(second)


def _inproj_call(x2d, mod, ln_g, ln_b, w_in_b, steps, tm, prompt):
    rows = x2d.shape[0]
    in_cols = w_in_b.shape[1]
    groups = mod.shape[0]
    if prompt:
        lead = ATT_REACH // tm
        kv_rows = ATT_REACH
        row_map = lambda j: (jnp.maximum(j - lead, 0), 0)
        next_map = lambda j: (jnp.clip(j + 1 - lead, 0, steps - 1), 0)
        kv_map = lambda j: (jnp.maximum(j - steps, 0), 0)
        ext_map = lambda j: (j, 0)
    else:
        lead = 0
        kv_rows = rows
        row_map = next_map = kv_map = ext_map = lambda j: (j, 0)
    return pl.pallas_call(
        functools.partial(_inproj_kernel, lead=lead),
        grid=(steps + lead,),
        in_specs=[
            pl.BlockSpec((tm, D_MODEL), row_map),
            pl.BlockSpec((tm, D_MODEL), next_map),
            _const_spec((groups, 6, D_MODEL)),
            _const_spec((1, D_MODEL)),
            _const_spec((1, D_MODEL)),
            _const_spec((D_MODEL, in_cols)),
        ],
        scratch_shapes=[pltpu.VMEM((tm // 2, D_MODEL), F32), pltpu.VMEM((tm // 2, D_MODEL), BF16)],
        out_specs=[
            pl.BlockSpec((tm, HEADS_DIM), row_map),
            pl.BlockSpec((tm, HEADS_DIM), ext_map),
            pl.BlockSpec((tm, HEADS_DIM), ext_map),
            pl.BlockSpec((tm, HEADS_DIM), kv_map),
            pl.BlockSpec((tm, HEADS_DIM), kv_map),
            pl.BlockSpec((tm, RW_COLS), row_map),
            pl.BlockSpec((tm, GATE_COLS), row_map),
            pl.BlockSpec((tm, D_MODEL), row_map),
        ],
        out_shape=[
            jax.ShapeDtypeStruct((rows, HEADS_DIM), BF16),
            jax.ShapeDtypeStruct((rows + lead * tm, HEADS_DIM), BF16),
            jax.ShapeDtypeStruct((rows + lead * tm, HEADS_DIM), BF16),
            jax.ShapeDtypeStruct((kv_rows, HEADS_DIM), F32),
            jax.ShapeDtypeStruct((kv_rows, HEADS_DIM), F32),
            jax.ShapeDtypeStruct((rows, RW_COLS), F32),
            jax.ShapeDtypeStruct((rows, GATE_COLS), F32),
            jax.ShapeDtypeStruct((rows, D_MODEL), F32),
        ],
        compiler_params=pltpu.CompilerParams(
            dimension_semantics=("arbitrary",), vmem_limit_bytes=VMEM_LIMIT),
        name="inproj",
    )(x2d, x2d, mod, ln_g, ln_b, w_in_b)


def _attn_chunks(chunks, bias_ref):
    lane = lax.broadcasted_iota(jnp.int32, (CHUNK, 128), 1)
    first = lane < HEAD
    pairs = [slice(pr * 128, (pr + 1) * 128) for pr in range(N_HEADS // 2)]
    scores = []
    for qc, kb, _, _ in chunks:
        for sl in pairs:
            q2 = qc[:, sl].astype(F32)
            qs = jnp.concatenate([jnp.where(first, q2, 0.0), jnp.where(first, 0.0, q2)],
                                 axis=0).astype(BF16)
            scores.append(_dot_nt(qs, kb(sl)))
    probs, sums = [], []
    for i, s in enumerate(scores):
        thr = chunks[i // len(pairs)][3]
        s = s + bias_ref[i % len(pairs)]
        if thr is not None:
            col = lax.broadcasted_iota(jnp.int32, s.shape, 1)
            s = jnp.where(col >= thr, s, -jnp.inf)
        e = jnp.exp2(s - jnp.max(s, axis=1, keepdims=True))
        sums.append(jnp.sum(e, axis=1, keepdims=True))
        probs.append(e.astype(BF16))
    outs = []
    for ci, (_, _, vb, _) in enumerate(chunks):
        cols = []
        for pi, sl in enumerate(pairs):
            i = ci * len(pairs) + pi
            o = _dot(probs[i], vb(sl)) / sums[i]
            cols.append(jnp.where(first, o[0:CHUNK], o[CHUNK:2 * CHUNK]))
        outs.append(jnp.concatenate(cols, axis=1).astype(BF16))
    return outs


def _attn_prompt_kernel(q_ref, kbuf, vbuf, bias_ref, o_ref, *, chunks):
    s = pl.program_id(0)

    per = 2

    def run(masked):
        for i in range(chunks // per):
            units, starts = [], []
            for k in range(per):
                g = i * per + k
                r0 = g * CHUNK
                thr = ATT_REACH - (s * chunks + g) * CHUNK if masked else None
                band = slice(r0, r0 + BAND)
                units.append((q_ref[r0:r0 + CHUNK, :],
                              lambda sl, band=band: kbuf[band, sl],
                              lambda sl, band=band: vbuf[band, sl], thr))
                starts.append(r0)
            for r0, o in zip(starts, _attn_chunks(units, bias_ref)):
                o_ref[r0:r0 + CHUNK, :] = o

    assert chunks * CHUNK >= ATT_REACH
    pl.when(s == 0)(functools.partial(run, True))
    pl.when(s != 0)(functools.partial(run, False))


def _attn_prompt_call(q, kext, vext, bias):
    rows = q.shape[0]
    tq = ATT_REACH
    blk = pl.BlockSpec((tq, HEADS_DIM), lambda s: (s, 0))
    window = pl.BlockSpec((pl.Element(tq + ATT_REACH), pl.Element(HEADS_DIM)),
                          lambda s: (s * tq, 0))
    return pl.pallas_call(
        functools.partial(_attn_prompt_kernel, chunks=tq // CHUNK),
        grid=(rows // tq,),
        in_specs=[blk, window, window, _const_spec(bias.shape)],
        out_specs=blk,
        out_shape=jax.ShapeDtypeStruct((rows, HEADS_DIM), BF16),
        compiler_params=pltpu.CompilerParams(dimension_semantics=("arbitrary",)),
        name="attn_prompt",
    )(q, kext, vext, bias)


def _attn_sample_kernel(q_ref, k_ref, v_ref, ck_ref, cv_ref, bias_ref, o_ref):
    def band(cache_ref, new_ref, rs):
        return lambda sl: jnp.concatenate([cache_ref[:, sl], new_ref[rs, sl]], axis=0)

    units = []
    for s in range(ck_ref.shape[0]):
        rs = slice(s * CHUNK, (s + 1) * CHUNK)
        units.append((q_ref[rs, :], band(ck_ref.at[s], k_ref, rs), band(cv_ref.at[s], v_ref, rs),
                      None))
    for s, o in enumerate(_attn_chunks(units, bias_ref)):
        o_ref[s * CHUNK:(s + 1) * CHUNK, :] = o


def _attn_sample_call(q, k, v, cache_k, cache_v, bias):
    per = 2
    nb = cache_k.shape[0] // per
    new = pl.BlockSpec((per * CHUNK, HEADS_DIM), lambda b: (b, 0))
    cache = pl.BlockSpec((per, ATT_REACH, HEADS_DIM), lambda b: (b, 0, 0))
    return pl.pallas_call(
        _attn_sample_kernel,
        grid=(nb,),
        in_specs=[new, new, new, cache, cache, _const_spec(bias.shape)],
        out_specs=new,
        out_shape=jax.ShapeDtypeStruct(q.shape, BF16),
        compiler_params=pltpu.CompilerParams(dimension_semantics=("arbitrary",)),
        name="attn_sample",
    )(q, k, v, cache_k, cache_v, bias)


def _same_head():
    r = lax.broadcasted_iota(jnp.int32, (GROUP, GROUP), 0) // HEAD
    c = lax.broadcasted_iota(jnp.int32, (GROUP, GROUP), 1) // HEAD
    return r == c


def _blk(x, same_head):
    return jnp.where(same_head, jnp.concatenate([x] * 4, axis=0), 0.0).astype(BF16)


def _wkv_prep(units, hooks):
    same_head = _same_head()
    t64 = lax.broadcasted_iota(jnp.int32, (CHUNK, GROUP), 0)
    i64 = lax.broadcasted_iota(jnp.int32, (CHUNK, GROUP), 1) % HEAD
    strict = i64 < t64
    incl = i64 <= t64
    pending = list(hooks)

    def stage_done():
        if pending:
            pending.pop(0)()

    def blk(x):
        return _blk(x, same_head)

    n, a_ak, a_rb, a_rk = [], [], [], []
    for ld in units:
        lhs = jnp.concatenate([ld("at"), ld("rt")], axis=0).astype(BF16)

        def blk_t(x):
            xt = jnp.transpose(x)
            return jnp.where(same_head, jnp.concatenate([xt] * 4, axis=1), 0.0).astype(BF16)

        rhs = jnp.concatenate([blk_t(ld("bt")), blk_t(ld("kt"))], axis=1)
        a_all = _dot(lhs, rhs)
        n.append(jnp.where(strict, a_all[0:CHUNK, 0:GROUP], 0.0))
        a_ak.append(jnp.where(strict, a_all[0:CHUNK, GROUP:], 0.0))
        a_rb.append(jnp.where(incl, a_all[CHUNK:, 0:GROUP], 0.0))
        a_rk.append(jnp.where(incl, a_all[CHUNK:, GROUP:], 0.0))
    stage_done()

    x0 = [_dot(a.astype(BF16), blk(ld("v"))) for a, ld in zip(a_ak, units)]
    npow = [_dot(m.astype(BF16), blk(m)) for m in n]
    t = [jnp.where(i64 == t64, 1.0, 0.0) + m for m in n]
    stage_done()
    for _ in range(4):
        prod = [_dot(jnp.concatenate([a, b], axis=0).astype(BF16), blk(b))
                for a, b in zip(t, npow)]
        t = [a + p[0:CHUNK] for a, p in zip(t, prod)]
        npow = [p[CHUNK:] for p in prod]
        stage_done()
    t = [a + _dot(a.astype(BF16), blk(b)) for a, b in zip(t, npow)]
    stage_done()
    res = [_dot(a.astype(BF16), jnp.concatenate([blk(ld("at")), blk(x)], axis=1))
           for a, x, ld in zip(t, x0, units)]
    stage_done()
    while pending:
        stage_done()
    return [(r[:, 0:GROUP], r[:, GROUP:], b, k) for r, b, k in zip(res, a_rb, a_rk)]


def _wkv_step_stages(units, states, out):
    same_head = _same_head()
    held = {}

    def blk(x):
        return _blk(x, same_head)

    def first():
        held["sblk"] = [blk(st) for st in states]
        held["u"] = [_dot(ld("ah").astype(BF16), sb) + ld("vh")
                     for ld, sb in zip(units, held["sblk"])]

    def second():
        for i, ld in enumerate(units):
            u, v = held["u"][i], ld("v")
            y = _dot(jnp.concatenate([ld("rt"), ld("arb"), ld("ark")], axis=1).astype(BF16),
                     jnp.concatenate([held["sblk"][i], blk(u), blk(v)], axis=0))
            g = _dot_tn(jnp.concatenate([ld("btd"), ld("ktd")], axis=0).astype(BF16),
                        jnp.concatenate([u, v], axis=0).astype(BF16))
            g = jnp.where(same_head, g, 0.0)
            delta = g[0:64] + g[64:128] + g[128:192] + g[192:256]
            out.append(y)
            states[i] = states[i] * ld("fdec") + delta

    return [first, second]


def _wkv_kernel(p_ref, pn_ref, shift0_ref, state0_ref, mu_ref, w0_ref, wwa_ref, a0_ref, gup_ref,
                kkw_ref, kaw_ref, rkw_ref, gng_ref, gnb_ref, ones_ref, ltri_ref,
                out_ref, shift_out_ref, state_out_ref,
                st_scr, rt_scr, at_scr, kt_scr, bt_scr, v_scr, btd_scr, ktd_scr,
                fdec_scr, ah_scr, vh_scr, arb_scr, ark_scr, g_scr, bon_scr, *, ts, steps):
    j = pl.program_id(0)
    streams = state0_ref.shape[0]
    assert streams == 1 or (steps == 1 and ts == streams * CHUNK)

    if streams == 1:
        @pl.when(j == 0)
        def _():
            st_scr[...] = state0_ref[0]

    def bdsum(x, terms):
        ones = ones_ref[...]
        halves = []
        for c0 in range(0, HEADS_DIM, GROUP):
            parts = _split3(x[:, c0:c0 + GROUP])[:terms]
            acc = _dot(parts[0], ones)
            for part in parts[1:]:
                acc = acc + _dot(part, ones)
            halves.append(acc)
        return jnp.concatenate(halves, axis=1)

    pq = ltri_ref.shape[0]
    per = pq // CHUNK
    n_piece = ts // pq

    def pre_stages(q, upcoming=False):
        r0 = q * pq
        rows = slice(r0, r0 + pq)
        src_ref = pn_ref if upcoming else p_ref
        held = {}

        def shifted(c0, w):
            cols = slice(c0, c0 + w)
            row = lax.broadcasted_iota(jnp.int32, (pq, 1), 0)
            if streams > 1:
                first = shift0_ref[q * per:(q + 1) * per, :, cols]
                prev_row = jnp.broadcast_to(first, (per, CHUNK, w)).reshape(pq, w)
                top = row % CHUNK == 0
            else:
                if upcoming:
                    prev_row = p_ref[ts - 8:ts, cols][7:8]
                elif q == 0:
                    prev_row = shift0_ref[0][:, cols]
                else:
                    prev_row = p_ref[r0 - 8:r0, cols][7:8]
                top = row == 0
            pb = src_ref[rows, cols]
            prev = jnp.where(top, prev_row, pltpu.roll(pb, 1, 0))
            return pb + (prev - pb) * mu_ref[:, cols]

        def s_lora():
            lora = shifted(3 * HEADS_DIM, 256)
            lwla = lora[:, 0:128]
            lane = lax.broadcasted_iota(jnp.int32, (pq, 128), 1)
            held["raw"] = _dot(jnp.where(lane < LORA_W, jnp.tanh(lwla), lwla).astype(BF16),
                               wwa_ref[...])
            g_scr[rows, :] = _dot(_sigmoid(lora[:, 128:256]).astype(BF16), gup_ref[...])
            held["k"] = shifted(HEADS_DIM, HEADS_DIM)

        def s_decay():
            w_pre = w0_ref[...] + held["raw"][:, 0:HEADS_DIM]
            softplus = jnp.maximum(-w_pre, 0.0) + jnp.log(1.0 + jnp.exp(-jnp.abs(w_pre)))
            lw = -jnp.exp(-softplus - 0.5)
            ltri = ltri_ref[...]
            hi, mid, _ = _split3(lw)
            held["cum"] = _dot(ltri, hi) + _dot(ltri, mid)
            held["lw"] = lw

        def s_keys():
            k = held["k"]
            a = _sigmoid(a0_ref[...] + held["raw"][:, HEADS_DIM:])
            kk = k * kkw_ref[...]
            held["nrm"] = bdsum(kk * kk, 1)
            held.update(a=a, kk=kk, k2=k * (1.0 + (a - 1.0) * kaw_ref[...]))

        def s_bonus():
            r = shifted(0, HEADS_DIM)
            held["bon"] = bdsum(r * held["k2"] * rkw_ref[...], 1)
            rt_scr[rows, :] = r * jnp.exp(held["cum"])

        def s_norm():
            cum = held["cum"]
            kk = held["kk"] / jnp.maximum(jnp.sqrt(held["nrm"]), 1e-12)
            cum3 = cum.reshape(per, CHUNK, HEADS_DIM)
            cend = jnp.broadcast_to(cum3[:, CHUNK - 1:CHUNK, :], cum3.shape).reshape(pq, HEADS_DIM)
            diag = (lax.broadcasted_iota(jnp.int32, (pq, HEADS_DIM), 0) % CHUNK
                    == lax.broadcasted_iota(jnp.int32, (pq, HEADS_DIM), 1) % HEAD)
            held["fdec"] = bdsum(jnp.where(diag, jnp.exp(cend), 0.0), 2)
            at_scr[rows, :] = -kk * jnp.exp(cum - held["lw"])
            held.update(kk=kk, cend=cend)

        def s_inv():
            e_inv = jnp.exp(-held["cum"])
            kt_scr[rows, :] = held["k2"] * e_inv
            bt_scr[rows, :] = held["kk"] * held["a"] * e_inv

        def s_rel():
            e_rel = jnp.exp(held["cend"] - held["cum"])
            ktd_scr[rows, :] = held["k2"] * e_rel
            btd_scr[rows, :] = held["kk"] * held["a"] * e_rel
            fdec_scr[rows, :] = held["fdec"]

        def s_value():
            v = shifted(2 * HEADS_DIM, HEADS_DIM)
            v_scr[rows, :] = v
            bon_scr[rows, :] = held["bon"] * v

        return [s_lora, s_decay, s_keys, s_bonus, s_norm, s_inv, s_rel, s_value]

    def chunk_rows(c):
        return slice(c * CHUNK, (c + 1) * CHUNK)

    groups = [slice(g0, g0 + GROUP) for g0 in range(0, HEADS_DIM, GROUP)]

    named = dict(at=at_scr, rt=rt_scr, kt=kt_scr, bt=bt_scr, v=v_scr, btd=btd_scr, ktd=ktd_scr,
                 fdec=fdec_scr, ah=ah_scr, vh=vh_scr, arb=arb_scr, ark=ark_scr)
    prep_outs = (ah_scr, vh_scr, arb_scr, ark_scr)

    def loader(c, cols):
        rows = chunk_rows(c)
        return lambda name: named[name][rows, cols]

    def run(step_piece, prep_piece, extra):
        step_chunks = [] if step_piece is None else [step_piece * per + i for i in range(per)]
        prep_chunks = [] if prep_piece is None else [prep_piece * per + i for i in range(per)]
        stages, ys, finals = [], [], []
        if streams == 1:
            states = [st_scr[:, cols] for cols in groups] if step_chunks else []
            for c in step_chunks:
                out = []
                ys.append(out)
                stages += _wkv_step_stages([loader(c, cols) for cols in groups], states, out)
            finals = [(st_scr, states)] if step_chunks else []
        else:
            firsts, seconds = [], []
            for c in step_chunks:
                out = []
                ys.append(out)
                states = [state0_ref[c, :, cols] for cols in groups]
                first, second = _wkv_step_stages(
                    [loader(c, cols) for cols in groups], states, out)
                firsts.append(first)
                seconds.append(second)
                finals.append((state_out_ref.at[c], states))
            if step_chunks:
                stages += [lambda: [f() for f in firsts], lambda: [f() for f in seconds]]
        if step_chunks:
            rows = slice(step_piece * pq, (step_piece + 1) * pq)
            gn = {}

            def gn_mean():
                gn["y"] = jnp.concatenate([jnp.concatenate(out, axis=1) for out in ys], axis=0)
                gn["sum"] = bdsum(gn["y"], 1)

            def gn_var():
                gn["d"] = gn["y"] - gn["sum"] * (1.0 / HEAD)
                gn["sq"] = bdsum(gn["d"] * gn["d"], 1)

            def gn_out():
                var = gn["sq"] * (1.0 / HEAD)
                yn = (gn["d"] * lax.rsqrt(var + GN_EPS) * gng_ref[...] + gnb_ref[...]
                      + bon_scr[rows, :])
                out_ref[rows, :] = (yn * g_scr[rows, :]).astype(BF16)

            stages += [gn_mean, gn_var, gn_out]
        hooks = []
        for i in range(max(len(stages), len(extra))):
            both = stages[i:i + 1] + extra[i:i + 1]
            hooks.append(lambda both=both: [f() for f in both])
        units = [(c, cols) for c in prep_chunks for cols in groups]
        prep_out = _wkv_prep([loader(c, cols) for c, cols in units], hooks)
        for ref, states in finals:
            for cols, st in zip(groups, states):
                ref[:, cols] = st
        for (c, cols), outs in zip(units, prep_out):
            for ref, val in zip(prep_outs, outs):
                ref[chunk_rows(c), cols] = val

    look_ahead = steps > 1
    ahead_pass = max(2, n_piece - 1)
    assert ahead_pass <= n_piece or not look_ahead

    def first_piece():
        for stage in pre_stages(0):
            stage()

    if look_ahead:
        pl.when(j == 0)(first_piece)
    else:
        first_piece()
    for piece in range(n_piece + 1):
        if piece + 1 < n_piece:
            extra = pre_stages(piece + 1)
        elif piece == ahead_pass and look_ahead:
            extra = pre_stages(0, upcoming=True)
        else:
            extra = []
        run(piece - 1 if piece >= 1 else None, piece if piece < n_piece else None, extra)

    if streams == 1:
        @pl.when(j == steps - 1)
        def _():
            shift_out_ref[0] = p_ref[ts - 1:ts, :]
            state_out_ref[0] = st_scr[...]
    else:
        for s in range(streams):
            shift_out_ref[s] = p_ref[(s + 1) * CHUNK - 1:(s + 1) * CHUNK, :]


def _wkv_call(p2d, shift0, state0, prm, steps, ts):
    rows = p2d.shape[0]
    streams = state0.shape[0]
    row_map = lambda j: (j, 0)
    next_map = lambda j: (jnp.minimum(j + 1, steps - 1), 0)
    seq3 = lambda j: (0, 0, 0)
    vec = _const_spec((1, HEADS_DIM))
    big = pltpu.VMEM((ts, HEADS_DIM), F32)
    ltri = prm["ltri"][min(ts, WKV_PIECE)]
    return pl.pallas_call(
        functools.partial(_wkv_kernel, ts=ts, steps=steps),
        grid=(steps,),
        in_specs=[
            pl.BlockSpec((ts, RW_COLS), row_map),
            pl.BlockSpec((ts, RW_COLS), next_map),
            pl.BlockSpec((streams, 1, RW_COLS), seq3),
            pl.BlockSpec((streams, HEAD, HEADS_DIM), seq3),
            _const_spec((1, RW_COLS)),
            vec,
            _const_spec((128, 2 * HEADS_DIM)),
            vec,
            _const_spec((LORA_G, HEADS_DIM)),
            vec, vec, vec, vec, vec,
            _const_spec((GROUP, GROUP)),
            _const_spec(ltri.shape),
        ],
        out_specs=[
            pl.BlockSpec((ts, HEADS_DIM), row_map),
            pl.BlockSpec((streams, 1, RW_COLS), seq3),
            pl.BlockSpec((streams, HEAD, HEADS_DIM), seq3),
        ],
        out_shape=[
            jax.ShapeDtypeStruct((rows, HEADS_DIM), BF16),
            jax.ShapeDtypeStruct((streams, 1, RW_COLS), F32),
            jax.ShapeDtypeStruct((streams, HEAD, HEADS_DIM), F32),
        ],
        scratch_shapes=[
            pltpu.VMEM((HEAD, HEADS_DIM), F32),
        ] + [big] * 14,
        compiler_params=pltpu.CompilerParams(
            dimension_semantics=("arbitrary",), vmem_limit_bytes=VMEM_LIMIT),
        name="wkv",
    )(p2d, p2d, shift0, state0, prm["mu"], prm["w0"], prm["wwa"], prm["a0"], prm["gup"],
      prm["kk"], prm["ka"], prm["rk"], prm["gng"], prm["gnb"], prm["ones"], ltri)


def _mixffn_kernel(xn_ref, att_ref, rw_ref, gate_ref, mod_ref, convp_ref,
                   ln1g_ref, ln1b_ref, ln2g_ref, ln2b_ref,
                   wa_ref, wr_ref, wo_ref, wup_ref, cw_ref, cb_ref, wdn_ref,
                   y_ref, convo_ref, carry_scr, yb_scr, x1_scr, h2_scr, *, tm, steps):
    j = pl.program_id(0)
    groups = mod_ref.shape[0]
    assert groups == 1 or steps == 1
    pipelined = steps > 1

    def mod(idx):
        return _mod_row(mod_ref, idx, tm)

    half = tm // 2
    subs = [slice(0, half), slice(half, tm)]

    def modr(idx, rs):
        m = mod(idx)
        return m if m.shape[0] == 1 else m[rs]

    def front():
        ma = [_dot(att_ref[rs, :], wa_ref[...]) for rs in subs]
        mr = [_dot(rw_ref[rs, :], wr_ref[...]) for rs in subs]
        merged = [(gate_ref[rs, 0:D_MODEL] * a + gate_ref[rs, D_MODEL:] * r).astype(BF16)
                  for rs, a, r in zip(subs, ma, mr)]
        mix = [_dot(m, wo_ref[...]) for m in merged]
        x1 = [_layer_norm(ALPHA * xn_ref[rs, :] + (1.0 + modr(2, rs)) * m,
                          ln1g_ref[...], ln1b_ref[...]) for rs, m in zip(subs, mix)]
        h2 = jnp.concatenate([(a * (1.0 + modr(4, rs)) + modr(3, rs)).astype(BF16)
                              for rs, a in zip(subs, x1)], axis=0)
        return x1, h2

    def keep(x1, h2):
        for rs, a in zip(subs, x1):
            x1_scr[rs, :] = a
        h2_scr[...] = h2

    if not pipelined:
        if groups == 1:
            carry_scr[8 - (CONV_W - 1):8, :] = convp_ref[0]
        _ffn_back(*front(), subs, modr, groups, tm, convp_ref, ln2g_ref, ln2b_ref, wup_ref, cw_ref,
                  cb_ref, wdn_ref, y_ref, convo_ref, carry_scr, yb_scr)
        return

    @pl.when(j == 0)
    def _():
        carry_scr[8 - (CONV_W - 1):8, :] = convp_ref[0]
        keep(*front())

    @pl.when(j > 0)
    def _():
        x1 = [x1_scr[rs, :] for rs in subs]
        h2 = h2_scr[...]
        keep(*front())
        _ffn_back(x1, h2, subs, modr, groups, tm, convp_ref, ln2g_ref, ln2b_ref, wup_ref, cw_ref,
                  cb_ref, wdn_ref, y_ref, convo_ref, carry_scr, yb_scr)


def _ffn_back(x1, h2, subs, modr, groups, tm, convp_ref, ln2g_ref, ln2b_ref, wup_ref, cw_ref,
              cb_ref, wdn_ref, y_ref, convo_ref, carry_scr, yb_scr):
    cw_blk = 256
    glen = tm // groups
    row8 = lax.broadcasted_iota(jnp.int32, (8, cw_blk), 0)
    grow = lax.broadcasted_iota(jnp.int32, (tm, cw_blk), 0) % glen
    for c in range(0, D_FF, cw_blk):
        cs = slice(c, c + cw_blk)
        uc = _dot(h2, wup_ref[:, cs])
        uv = _dot(h2, wup_ref[:, D_FF + c:D_FF + c + cw_blk])
        r1 = pltpu.roll(uc, 1, 0)
        r2 = pltpu.roll(uc, 2, 0)
        if groups == 1:
            c6 = carry_scr[6:7, cs]
            c7 = carry_scr[7:8, cs]
            s1 = jnp.concatenate([jnp.where(row8 == 0, c7, r1[0:8]), r1[8:]], axis=0)
            s2 = jnp.concatenate(
                [jnp.where(row8 == 0, c6, jnp.where(row8 == 1, c7, r2[0:8])), r2[8:]], axis=0)
            carry_scr[:, cs] = uc[tm - 8:tm, :]
            tail = uc[tm - (CONV_W - 1):tm, :][None]
        else:
            hist = jnp.broadcast_to(convp_ref[:, :, cs][:, :, None, :],
                                    (groups, CONV_W - 1, glen, cw_blk))
            c6 = hist[:, 0].reshape(tm, cw_blk)
            c7 = hist[:, 1].reshape(tm, cw_blk)
            s1 = jnp.where(grow == 0, c7, r1)
            s2 = jnp.where(grow == 0, c6, jnp.where(grow == 1, c7, r2))
            tail = uc.reshape(groups, glen, cw_blk)[:, glen - (CONV_W - 1):, :]
        conv = cb_ref[:, cs] + s2 * cw_ref[0:1, cs] + s1 * cw_ref[1:2, cs] + uc * cw_ref[2:3, cs]
        yb_scr[:, cs] = (conv * _sigmoid(conv) * uv).astype(BF16)
        convo_ref[:, :, cs] = tail

    ff = [_dot(yb_scr[rs, :], wdn_ref[...]) for rs in subs]
    for rs, a, f in zip(subs, x1, ff):
        y_ref[rs, :] = _layer_norm(ALPHA * a + (1.0 + modr(5, rs)) * f, ln2g_ref[...], ln2b_ref[...])


def _mixffn_call(xn2d, att, rw, gates, mod, conv_prev, prm, steps, tm):
    rows = xn2d.shape[0]
    groups = mod.shape[0]
    lag = 1 if steps > 1 else 0
    row_map = lambda j: (jnp.minimum(j, steps - 1), 0)
    out_map = lambda j: (jnp.maximum(j - lag, 0), 0)
    vec = _const_spec((1, D_MODEL))
    return pl.pallas_call(
        functools.partial(_mixffn_kernel, tm=tm, steps=steps),
        grid=(steps + lag,),
        in_specs=[
            pl.BlockSpec((tm, D_MODEL), row_map),
            pl.BlockSpec((tm, HEADS_DIM), row_map),
            pl.BlockSpec((tm, HEADS_DIM), row_map),
            pl.BlockSpec((tm, GATE_COLS), row_map),
            _const_spec((groups, 6, D_MODEL)),
            _const_spec((groups, CONV_W - 1, D_FF)),
            vec, vec, vec, vec,
            _const_spec((HEADS_DIM, D_MODEL)),
            _const_spec((HEADS_DIM, D_MODEL)),
            _const_spec((D_MODEL, D_MODEL)),
            _const_spec((D_MODEL, 2 * D_FF)),
            _const_spec((CONV_W, D_FF)),
            _const_spec((1, D_FF)),
            _const_spec((D_FF, D_MODEL)),
        ],
        out_specs=[
            pl.BlockSpec((tm, D_MODEL), out_map),
            pl.BlockSpec((groups, CONV_W - 1, D_FF), lambda j: (0, 0, 0)),
        ],
        out_shape=[
            jax.ShapeDtypeStruct((rows, D_MODEL), F32),
            jax.ShapeDtypeStruct((groups, CONV_W - 1, D_FF), F32),
        ],
        scratch_shapes=[pltpu.VMEM((8, D_FF), F32), pltpu.VMEM((tm, D_FF), BF16),
                        pltpu.VMEM((tm, D_MODEL), F32), pltpu.VMEM((tm, D_MODEL), BF16)],
        compiler_params=pltpu.CompilerParams(
            dimension_semantics=("arbitrary",), vmem_limit_bytes=VMEM_LIMIT),
        name="mixffn",
    )(xn2d, att, rw, gates, mod, conv_prev,
      prm["ln1g"], prm["ln1b"], prm["ln2g"], prm["ln2b"],
      prm["wa"], prm["wr"], prm["wo"], prm["wup"], prm["cw"], prm["cb"], prm["wdn"])


def _pair_bias(table):
    assert CHUNK - 1 <= REL_CLIP
    top = ATT_REACH + CHUNK - 1
    n_far = top - REL_CLIP + 1
    far = jnp.broadcast_to(table[:, 2 * REL_CLIP:], (N_HEADS, n_far))
    lo_idx = top - (BAND + CHUNK - 2) + REL_CLIP
    near = table[:, lo_idx:2 * REL_CLIP][:, ::-1]
    ext = jnp.concatenate([far, near], axis=1).astype(F32) * LOG2E
    n_ext = BAND + CHUNK - 1
    period = jnp.concatenate([ext, jnp.zeros((N_HEADS, 1), F32)], axis=1)
    skew = jnp.tile(period, (1, CHUNK))[:, :CHUNK * n_ext].reshape(N_HEADS, CHUNK, n_ext)
    bias = skew[:, :, CHUNK - 1:CHUNK - 1 + BAND]
    return bias.reshape(N_HEADS // 2, 2 * CHUNK, BAND)


def _chunk_ltri(ts):
    t = jnp.arange(ts)
    return ((t[:, None] // CHUNK == t[None, :] // CHUNK) & (t[None, :] <= t[:, None])).astype(BF16)


def _trunk(x2d, mod, shift0, state0, conv_prev, caches, prm, n_seq):
    prompt = caches is None
    rows = x2d.shape[0]
    tm = min(rows, ROW_TILE)
    steps = rows // tm
    q, k, v, k32, v32, p, gates, xn = _inproj_call(
        x2d, mod, prm["lnig"], prm["lnib"], prm["win"], steps, tm, prompt)
    if prompt:
        att = _attn_prompt_call(q, k, v, prm["bias"])
    else:
        att = _attn_sample_call(q, k, v, *caches, prm["bias"])
    assert state0.shape[0] == n_seq
    rw, shift, state = _wkv_call(p, shift0, state0, prm, steps, tm)
    y, conv = _mixffn_call(xn, att, rw, gates, mod, conv_prev, prm, steps, tm)
    return y, (k32, v32), state, shift, conv


def kernel(x_prompt, x_sample, cache_attn_k, cache_attn_v, state_rwkv, state_shift, state_conv,
           c_prompt, c_sample, ln_in_g, ln_in_b, w_ada, b_ada, w_in, attn_rel_bias,
           rwkv_mu, rwkv_w0, rwkv_w_up, rwkv_a0, rwkv_a_up, rwkv_g_up, rwkv_k_k, rwkv_k_a,
           rwkv_r_k, rwkv_gn_g, rwkv_gn_b, w_branch_attn, w_branch_rwkv, w_out,
           ln1_g, ln1_b, ln2_g, ln2_b, w_ffn_up, ffn_conv_w, ffn_conv_b, w_ffn_down):
    bp, sp, _ = x_prompt.shape
    bs, ss, _ = x_sample.shape
    assert bp == 1 and ss == CHUNK and w_ada.shape[0] == DEPTH
    assert sp % ROW_TILE == 0 and bs * ss <= ROW_TILE and cache_attn_k.shape[2] == ATT_REACH

    row = lambda a: a.reshape(1, -1)
    wwa = jnp.zeros((LORA_W + LORA_A, 2 * HEADS_DIM), F32)
    wwa = wwa.at[:LORA_W, :HEADS_DIM].set(rwkv_w_up[0]).at[LORA_W:, HEADS_DIM:].set(rwkv_a_up[0])
    head_id = jnp.arange(GROUP) // HEAD
    prm = dict(
        lnig=row(ln_in_g), lnib=row(ln_in_b),
        ln1g=row(ln1_g[0]), ln1b=row(ln1_b[0]), ln2g=row(ln2_g[0]), ln2b=row(ln2_b[0]),
        win=w_in[0].astype(BF16), bias=_pair_bias(attn_rel_bias[0]),
        mu=row(rwkv_mu[0]), w0=row(rwkv_w0[0]), wwa=wwa.astype(BF16), a0=row(rwkv_a0[0]),
        gup=rwkv_g_up[0].astype(BF16), kk=row(rwkv_k_k[0]), ka=row(rwkv_k_a[0]),
        rk=row(rwkv_r_k[0]), gng=row(rwkv_gn_g[0]), gnb=row(rwkv_gn_b[0]),
        ones=(head_id[:, None] == head_id[None, :]).astype(BF16),
        ltri={WKV_PIECE: _chunk_ltri(WKV_PIECE), CHUNK: _chunk_ltri(CHUNK)},
        wa=w_branch_attn[0].astype(BF16), wr=w_branch_rwkv[0].astype(BF16),
        wo=w_out[0].astype(BF16), wup=w_ffn_up[0].astype(BF16),
        cw=ffn_conv_w[0], cb=row(ffn_conv_b[0]), wdn=w_ffn_down[0].astype(BF16),
    )

    n_c = bp + bs
    c_all = jnp.concatenate([c_prompt, c_sample, jnp.zeros((16 - n_c, D_MODEL), F32)], axis=0)
    mod = _mod_call(c_all, w_ada[0], row(b_ada[0])).reshape(16, 6, D_MODEL)

    y_p, kv_p, st_p, sh_p, cv_p = _trunk(
        x_prompt.reshape(sp, D_MODEL), mod[0:bp],
        jnp.zeros((bp, 1, RW_COLS), F32), jnp.zeros((bp, HEAD, HEADS_DIM), F32),
        jnp.zeros((bp, CONV_W - 1, D_FF), F32), None, prm, n_seq=bp)

    caches = (cache_attn_k[0].astype(BF16).reshape(bs, ATT_REACH, HEADS_DIM),
              cache_attn_v[0].astype(BF16).reshape(bs, ATT_REACH, HEADS_DIM))
    st0 = jnp.transpose(state_rwkv[0], (0, 3, 1, 2)).reshape(bs, HEAD, HEADS_DIM)
    y_s, kv_s, st_s, sh_s, cv_s = _trunk(
        x_sample.reshape(bs * ss, D_MODEL), mod[bp:n_c],
        state_shift[0], st0, state_conv[0], caches, prm, n_seq=bs)

    def state_out(st, b):
        return jnp.transpose(st.reshape(b, HEAD, N_HEADS, HEAD), (0, 2, 3, 1))[None]

    hs = (N_HEADS, HEAD)
    return (
        y_p.reshape(bp, sp, D_MODEL),
        y_s.reshape(bs, ss, D_MODEL),
        kv_p[0].reshape(1, bp, ATT_REACH, *hs),
        kv_p[1].reshape(1, bp, ATT_REACH, *hs),
        kv_s[0].reshape(1, bs, ss, *hs),
        kv_s[1].reshape(1, bs, ss, *hs),
        state_out(st_p, bp),
        state_out(st_s, bs),
        sh_p[None],
        sh_s[None],
        cv_p[None],
        cv_s[None],
    )
```

```python
import functools

import jax
import jax.numpy as jnp
from jax import lax
from jax.experimental import pallas as pl
from jax.experimental.pallas import tpu as pltpu

F32 = jnp.float32
BF16 = jnp.bfloat16

D_MODEL = 1024
CHUNK = 64
ATT_REACH = 512
BAND = ATT_REACH + CHUNK
N_HEADS = 8
HEAD = 64
HEADS_DIM = N_HEADS * HEAD
REL_CLIP = 128
LORA_W = 64
LORA_A = 64
LORA_G = 128
ATT_COLS = 3 * HEADS_DIM
RW_COLS = 3 * HEADS_DIM + LORA_W + LORA_A + LORA_G
GATE_COLS = 2 * D_MODEL
D_FF = 2816
CONV_W = 3
LN_EPS = 1e-5
GN_EPS = 64e-5
DEPTH = 1
ALPHA = (2 * DEPTH) ** 0.25
LOG2E = 1.4426950408889634

GROUP = 256
ROW_TILE = 512
WKV_PIECE = 256
VMEM_LIMIT = 56 * 1024 * 1024


def _const_spec(shape):
    nd = len(shape)
    return pl.BlockSpec(shape, lambda *_: (0,) * nd, pipeline_mode=pl.Buffered(1))


def _layer_norm(x, g, b):
    mu = jnp.mean(x, axis=-1, keepdims=True)
    xc = x - mu
    var = jnp.mean(xc * xc, axis=-1, keepdims=True)
    return xc * lax.rsqrt(var + LN_EPS) * g + b


def _sigmoid(x):
    return 1.0 / (1.0 + jnp.exp(-x))


def _split3(x):
    hi = x.astype(BF16)
    r1 = x - hi.astype(F32)
    mid = r1.astype(BF16)
    lo = (r1 - mid.astype(F32)).astype(BF16)
    return hi, mid, lo


def _dot(a, b):
    return jnp.dot(a, b, preferred_element_type=F32)


def _dot_nt(a, b):
    return lax.dot_general(a, b, (((1,), (1,)), ((), ())), preferred_element_type=F32)


def _dot_tn(a, b):
    return lax.dot_general(a, b, (((0,), (0,)), ((), ())), preferred_element_type=F32)


def _mod_kernel(c_ref, w_ref, b_ref, o_ref):
    c = c_ref[...]
    s = (c * _sigmoid(c)).astype(BF16)
    o_ref[...] = _dot(s, w_ref[...].astype(BF16)) + b_ref[...]


def _mod_call(c_all, w_ada, b_ada):
    n = c_all.shape[0]
    nblk = 6
    return pl.pallas_call(
        _mod_kernel,
        grid=(nblk,),
        in_specs=[
            pl.BlockSpec((n, D_MODEL), lambda i: (0, 0)),
            pl.BlockSpec((D_MODEL, D_MODEL), lambda i: (0, i)),
            pl.BlockSpec((1, D_MODEL), lambda i: (0, i)),
        ],
        out_specs=pl.BlockSpec((n, D_MODEL), lambda i: (0, i)),
        out_shape=jax.ShapeDtypeStruct((n, 6 * D_MODEL), F32),
        compiler_params=pltpu.CompilerParams(dimension_semantics=("arbitrary",)),
        name="mod",
    )(c_all, w_ada, b_ada)


def _mod_row(mod_ref, idx, tm):
    groups, _, d = mod_ref.shape
    m = mod_ref[:, idx:idx + 1, :]
    if groups == 1:
        return m[0]
    return jnp.broadcast_to(m, (groups, tm // groups, d)).reshape(tm, d)


def _inproj_kernel(x_ref, xnext_ref, mod_ref, lng_ref, lnb_ref, w_ref,
                   q_ref, k_ref, v_ref, k32_ref, v32_ref, p_ref, g_ref, xn_ref,
                   xna_scr, hba_scr, *, lead):
    tm = x_ref.shape[0]
    j = pl.program_id(0)

    first, second = slice(0, tm // 2), slice(tm // 2, tm)

    def front(src_ref, rs):
        def mrow(idx):
            m = _mod_row(mod_ref, idx, tm)
            return m if m.shape[0] == 1 else m[rs]

        xn = _layer_norm(src_ref[rs, :], lng_ref[...], lnb_ref[...])
        return xn, (xn * (1.0 + mrow(1)) + mrow(0)).astype(BF16)

    def half_tile(rs, prepared=False):
        if prepared:
            xn, hb = xna_scr[...], hba_scr[...]
        else:
            xn, hb = front(x_ref, rs)
        xn_ref[rs, :] = xn

        def seg(a, b):
            return _dot(hb, w_ref[:, a:b])

        g0 = ATT_COLS + RW_COLS
        for c in range(0, GATE_COLS, 512):
            g_ref[rs, c:c + 512] = _sigmoid(seg(g0 + c, g0 + c + 512))
        q_ref[rs, :] = (seg(0, HEADS_DIM) * (HEAD ** -0.5 * LOG2E)).astype(BF16)
        k = seg(HEADS_DIM, 2 * HEADS_DIM)
        k_ref[rs, :] = k.astype(BF16)
        k32_ref[rs, :] = k
        v = seg(2 * HEADS_DIM, 3 * HEADS_DIM)
        v_ref[rs, :] = v.astype(BF16)
        v32_ref[rs, :] = v
        for c in range(0, RW_COLS, 256):
            p_ref[rs, c:c + 256] = seg(ATT_COLS + c, ATT_COLS + c + 256)

    def prepare_next():
        xna_scr[...], hba_scr[...] = front(xnext_ref, first)

    @pl.when(j < lead)
    def _():
        k_ref[...] = jnp.zeros_like(k_ref)
        v_ref[...] = jnp.zeros_like(v_ref)
        prepare_next()

    @pl.when(j >= lead)
    def _():
        half_tile(first, prepared=lead > 0)
        if lead > 0:
            prepare_next()
        half_tile(second)


def _inproj_call(x2d, mod, ln_g, ln_b, w_in_b, steps, tm, prompt):
    rows = x2d.shape[0]
    in_cols = w_in_b.shape[1]
    groups = mod.shape[0]
    if prompt:
        lead = ATT_REACH // tm
        kv_rows = ATT_REACH
        row_map = lambda j: (jnp.maximum(j - lead, 0), 0)
        next_map = lambda j: (jnp.clip(j + 1 - lead, 0, steps - 1), 0)
        kv_map = lambda j: (jnp.maximum(j - steps, 0), 0)
        ext_map = lambda j: (j, 0)
    else:
        lead = 0
        kv_rows = rows
        row_map = next_map = kv_map = ext_map = lambda j: (j, 0)
    return pl.pallas_call(
        functools.partial(_inproj_kernel, lead=lead),
        grid=(steps + lead,),
        in_specs=[
            pl.BlockSpec((tm, D_MODEL), row_map),
            pl.BlockSpec((tm, D_MODEL), next_map),
            _const_spec((groups, 6, D_MODEL)),
            _const_spec((1, D_MODEL)),
            _const_spec((1, D_MODEL)),
            _const_spec((D_MODEL, in_cols)),
        ],
        scratch_shapes=[pltpu.VMEM((tm // 2, D_MODEL), F32), pltpu.VMEM((tm // 2, D_MODEL), BF16)],
        out_specs=[
            pl.BlockSpec((tm, HEADS_DIM), row_map),
            pl.BlockSpec((tm, HEADS_DIM), ext_map),
            pl.BlockSpec((tm, HEADS_DIM), ext_map),
            pl.BlockSpec((tm, HEADS_DIM), kv_map),
            pl.BlockSpec((tm, HEADS_DIM), kv_map),
            pl.BlockSpec((tm, RW_COLS), row_map),
            pl.BlockSpec((tm, GATE_COLS), row_map),
            pl.BlockSpec((tm, D_MODEL), row_map),
        ],
        out_shape=[
            jax.ShapeDtypeStruct((rows, HEADS_DIM), BF16),
            jax.ShapeDtypeStruct((rows + lead * tm, HEADS_DIM), BF16),
            jax.ShapeDtypeStruct((rows + lead * tm, HEADS_DIM), BF16),
            jax.ShapeDtypeStruct((kv_rows, HEADS_DIM), F32),
            jax.ShapeDtypeStruct((kv_rows, HEADS_DIM), F32),
            jax.ShapeDtypeStruct((rows, RW_COLS), F32),
            jax.ShapeDtypeStruct((rows, GATE_COLS), F32),
            jax.ShapeDtypeStruct((rows, D_MODEL), F32),
        ],
        compiler_params=pltpu.CompilerParams(
            dimension_semantics=("arbitrary",), vmem_limit_bytes=VMEM_LIMIT),
        name="inproj",
    )(x2d, x2d, mod, ln_g, ln_b, w_in_b)


def _attn_chunks(chunks, bias_ref, keys_transposed=False):
    lane = lax.broadcasted_iota(jnp.int32, (CHUNK, 128), 1)
    first = lane < HEAD
    pairs = [slice(pr * 128, (pr + 1) * 128) for pr in range(N_HEADS // 2)]
    scores = []
    for qc, kb, _, _ in chunks:
        for sl in pairs:
            q2 = qc[:, sl].astype(F32)
            qs = jnp.concatenate([jnp.where(first, q2, 0.0), jnp.where(first, 0.0, q2)],
                                 axis=0).astype(BF16)
            scores.append(_dot(qs, kb(sl)) if keys_transposed else _dot_nt(qs, kb(sl)))
    probs, sums = [], []
    for i, s in enumerate(scores):
        thr = chunks[i // len(pairs)][3]
        s = s + bias_ref[i % len(pairs)]
        if thr is not None:
            col = lax.broadcasted_iota(jnp.int32, s.shape, 1)
            s = jnp.where(col >= thr, s, -jnp.inf)
        e = jnp.exp2(s - jnp.max(s, axis=1, keepdims=True))
        sums.append(jnp.sum(e, axis=1, keepdims=True))
        probs.append(e.astype(BF16))
    outs = []
    for ci, (_, _, vb, _) in enumerate(chunks):
        cols = []
        for pi, sl in enumerate(pairs):
            i = ci * len(pairs) + pi
            o = _dot(probs[i], vb(sl)) / sums[i]
            cols.append(jnp.where(first, o[0:CHUNK], o[CHUNK:2 * CHUNK]))
        outs.append(jnp.concatenate(cols, axis=1).astype(BF16))
    return outs


def _attn_prompt_kernel(q_ref, kbuf, vbuf, bias_ref, o_ref, kt_scr, *, chunks):
    s = pl.program_id(0)
    tq = chunks * CHUNK
    kt_scr[0] = jnp.transpose(kbuf[...])
    kt_scr[1] = jnp.transpose(jnp.concatenate(
        [kbuf[CHUNK:2 * tq, :], jnp.zeros((CHUNK, HEADS_DIM), BF16)], axis=0))

    per = 2

    def run(masked):
        for i in range(chunks // per):
            units, starts = [], []
            for k in range(per):
                g = i * per + k
                r0 = g * CHUNK
                thr = ATT_REACH - (s * chunks + g) * CHUNK if masked else None
                band = slice(r0, r0 + BAND)
                lanes = slice(r0 - (g % 2) * CHUNK, r0 - (g % 2) * CHUNK + BAND)
                units.append((q_ref[r0:r0 + CHUNK, :],
                              lambda sl, lanes=lanes, g=g: kt_scr[g % 2, sl, lanes],
                              lambda sl, band=band: vbuf[band, sl], thr))
                starts.append(r0)
            for r0, o in zip(starts, _attn_chunks(units, bias_ref, keys_transposed=True)):
                o_ref[r0:r0 + CHUNK, :] = o

    assert chunks * CHUNK >= ATT_REACH
    pl.when(s == 0)(functools.partial(run, True))
    pl.when(s != 0)(functools.partial(run, False))


def _attn_prompt_call(q, kext, vext, bias):
    rows = q.shape[0]
    tq = ATT_REACH
    blk = pl.BlockSpec((tq, HEADS_DIM), lambda s: (s, 0))
    window = pl.BlockSpec((pl.Element(tq + ATT_REACH), pl.Element(HEADS_DIM)),
                          lambda s: (s * tq, 0))
    return pl.pallas_call(
        functools.partial(_attn_prompt_kernel, chunks=tq // CHUNK),
        grid=(rows // tq,),
        in_specs=[blk, window, window, _const_spec(bias.shape)],
        out_specs=blk,
        out_shape=jax.ShapeDtypeStruct((rows, HEADS_DIM), BF16),
        scratch_shapes=[pltpu.VMEM((2, HEADS_DIM, tq + ATT_REACH), BF16)],
        compiler_params=pltpu.CompilerParams(dimension_semantics=("arbitrary",)),
        name="attn_prompt",
    )(q, kext, vext, bias)


def _attn_sample_kernel(q_ref, k_ref, v_ref, ck_ref, cv_ref, bias_ref, o_ref):
    def band(cache_ref, new_ref, rs):
        return lambda sl: jnp.concatenate([cache_ref[:, sl], new_ref[rs, sl]], axis=0)

    units = []
    for s in range(ck_ref.shape[0]):
        rs = slice(s * CHUNK, (s + 1) * CHUNK)
        units.append((q_ref[rs, :], band(ck_ref.at[s], k_ref, rs), band(cv_ref.at[s], v_ref, rs),
                      None))
    for s, o in enumerate(_attn_chunks(units, bias_ref)):
        o_ref[s * CHUNK:(s + 1) * CHUNK, :] = o


def _attn_sample_call(q, k, v, cache_k, cache_v, bias):
    per = 2
    nb = cache_k.shape[0] // per
    new = pl.BlockSpec((per * CHUNK, HEADS_DIM), lambda b: (b, 0))
    cache = pl.BlockSpec((per, ATT_REACH, HEADS_DIM), lambda b: (b, 0, 0))
    return pl.pallas_call(
        _attn_sample_kernel,
        grid=(nb,),
        in_specs=[new, new, new, cache, cache, _const_spec(bias.shape)],
        out_specs=new,
        out_shape=jax.ShapeDtypeStruct(q.shape, BF16),
        compiler_params=pltpu.CompilerParams(dimension_semantics=("arbitrary",)),
        name="attn_sample",
    )(q, k, v, cache_k, cache_v, bias)


def _same_head():
    r = lax.broadcasted_iota(jnp.int32, (GROUP, GROUP), 0) // HEAD
    c = lax.broadcasted_iota(jnp.int32, (GROUP, GROUP), 1) // HEAD
    return r == c


def _blk(x, same_head):
    return jnp.where(same_head, jnp.concatenate([x] * 4, axis=0), 0.0).astype(BF16)


def _wkv_prep(units, hooks):
    same_head = _same_head()
    t64 = lax.broadcasted_iota(jnp.int32, (CHUNK, GROUP), 0)
    i64 = lax.broadcasted_iota(jnp.int32, (CHUNK, GROUP), 1) % HEAD
    strict = i64 < t64
    incl = i64 <= t64
    pending = list(hooks)

    def stage_done():
        if pending:
            pending.pop(0)()

    def blk(x):
        return _blk(x, same_head)

    n, a_ak, a_rb, a_rk = [], [], [], []
    for ld in units:
        lhs = jnp.concatenate([ld("at"), ld("rt")], axis=0).astype(BF16)

        def blk_t(x):
            xt = jnp.transpose(x)
            return jnp.where(same_head, jnp.concatenate([xt] * 4, axis=1), 0.0).astype(BF16)

        rhs = jnp.concatenate([blk_t(ld("bt")), blk_t(ld("kt"))], axis=1)
        a_all = _dot(lhs, rhs)
        n.append(jnp.where(strict, a_all[0:CHUNK, 0:GROUP], 0.0))
        a_ak.append(jnp.where(strict, a_all[0:CHUNK, GROUP:], 0.0))
        a_rb.append(jnp.where(incl, a_all[CHUNK:, 0:GROUP], 0.0))
        a_rk.append(jnp.where(incl, a_all[CHUNK:, GROUP:], 0.0))
    stage_done()

    x0 = [_dot(a.astype(BF16), blk(ld("v"))) for a, ld in zip(a_ak, units)]
    npow = [_dot(m.astype(BF16), blk(m)) for m in n]
    t = [jnp.where(i64 == t64, 1.0, 0.0) + m for m in n]
    stage_done()
    for _ in range(4):
        prod = [_dot(jnp.concatenate([a, b], axis=0).astype(BF16), blk(b))
                for a, b in zip(t, npow)]
        t = [a + p[0:CHUNK] for a, p in zip(t, prod)]
        npow = [p[CHUNK:] for p in prod]
        stage_done()
    t = [a + _dot(a.astype(BF16), blk(b)) for a, b in zip(t, npow)]
    stage_done()
    res = [_dot(a.astype(BF16), jnp.concatenate([blk(ld("at")), blk(x)], axis=1))
           for a, x, ld in zip(t, x0, units)]
    stage_done()
    while pending:
        stage_done()
    return [(r[:, 0:GROUP], r[:, GROUP:], b, k) for r, b, k in zip(res, a_rb, a_rk)]


def _wkv_step_stages(units, states, out):
    same_head = _same_head()
    held = {}

    def blk(x):
        return _blk(x, same_head)

    def first():
        held["sblk"] = [blk(st) for st in states]
        held["u"] = [_dot(ld("ah").astype(BF16), sb) + ld("vh")
                     for ld, sb in zip(units, held["sblk"])]

    def second():
        for i, ld in enumerate(units):
            u, v = held["u"][i], ld("v")
            y = _dot(jnp.concatenate([ld("rt"), ld("arb"), ld("ark")], axis=1).astype(BF16),
                     jnp.concatenate([held["sblk"][i], blk(u), blk(v)], axis=0))
            g = _dot_tn(jnp.concatenate([ld("btd"), ld("ktd")], axis=0).astype(BF16),
                        jnp.concatenate([u, v], axis=0).astype(BF16))
            g = jnp.where(same_head, g, 0.0)
            delta = g[0:64] + g[64:128] + g[128:192] + g[192:256]
            out.append(y)
            states[i] = states[i] * ld("fdec") + delta

    return [first, second]


def _wkv_kernel(p_ref, pn_ref, shift0_ref, state0_ref, mu_ref, w0_ref, wwa_ref, a0_ref, gup_ref,
                kkw_ref, kaw_ref, rkw_ref, gng_ref, gnb_ref, ones_ref, ltri_ref,
                out_ref, shift_out_ref, state_out_ref,
                st_scr, rt_scr, at_scr, kt_scr, bt_scr, v_scr, btd_scr, ktd_scr,
                fdec_scr, ah_scr, vh_scr, arb_scr, ark_scr, g_scr, bon_scr, *, ts, steps):
    j = pl.program_id(0)
    streams = state0_ref.shape[0]
    assert streams == 1 or (steps == 1 and ts == streams * CHUNK)

    if streams == 1:
        @pl.when(j == 0)
        def _():
            st_scr[...] = state0_ref[0]

    def bdsum(x, terms):
        ones = ones_ref[...]
        halves = []
        for c0 in range(0, HEADS_DIM, GROUP):
            parts = _split3(x[:, c0:c0 + GROUP])[:terms]
            acc = _dot(parts[0], ones)
            for part in parts[1:]:
                acc = acc + _dot(part, ones)
            halves.append(acc)
        return jnp.concatenate(halves, axis=1)

    pq = ltri_ref.shape[0]
    per = pq // CHUNK
    n_piece = ts // pq

    def pre_stages(q, upcoming=False):
        r0 = q * pq
        rows = slice(r0, r0 + pq)
        src_ref = pn_ref if upcoming else p_ref
        held = {}

        def shifted(c0, w):
            cols = slice(c0, c0 + w)
            row = lax.broadcasted_iota(jnp.int32, (pq, 1), 0)
            if streams > 1:
                first = shift0_ref[q * per:(q + 1) * per, :, cols]
                prev_row = jnp.broadcast_to(first, (per, CHUNK, w)).reshape(pq, w)
                top = row % CHUNK == 0
            else:
                if upcoming:
                    prev_row = p_ref[ts - 8:ts, cols][7:8]
                elif q == 0:
                    prev_row = shift0_ref[0][:, cols]
                else:
                    prev_row = p_ref[r0 - 8:r0, cols][7:8]
                top = row == 0
            pb = src_ref[rows, cols]
            prev = jnp.where(top, prev_row, pltpu.roll(pb, 1, 0))
            return pb + (prev - pb) * mu_ref[:, cols]

        def s_lora():
            lora = shifted(3 * HEADS_DIM, 256)
            lwla = lora[:, 0:128]
            lane = lax.broadcasted_iota(jnp.int32, (pq, 128), 1)
            held["raw"] = _dot(jnp.where(lane < LORA_W, jnp.tanh(lwla), lwla).astype(BF16),
                               wwa_ref[...])
            g_scr[rows, :] = _dot(_sigmoid(lora[:, 128:256]).astype(BF16), gup_ref[...])
            held["k"] = shifted(HEADS_DIM, HEADS_DIM)

        def s_decay():
            w_pre = w0_ref[...] + held["raw"][:, 0:HEADS_DIM]
            softplus = jnp.maximum(-w_pre, 0.0) + jnp.log(1.0 + jnp.exp(-jnp.abs(w_pre)))
            lw = -jnp.exp(-softplus - 0.5)
            ltri = ltri_ref[...]
            hi, mid, _ = _split3(lw)
            held["cum"] = _dot(ltri, hi) + _dot(ltri, mid)
            held["lw"] = lw

        def s_keys():
            k = held["k"]
            a = _sigmoid(a0_ref[...] + held["raw"][:, HEADS_DIM:])
            kk = k * kkw_ref[...]
            held["nrm"] = bdsum(kk * kk, 1)
            held.update(a=a, kk=kk, k2=k * (1.0 + (a - 1.0) * kaw_ref[...]))

        def s_bonus():
            r = shifted(0, HEADS_DIM)
            held["bon"] = bdsum(r * held["k2"] * rkw_ref[...], 1)
            rt_scr[rows, :] = r * jnp.exp(held["cum"])

        def s_norm():
            cum = held["cum"]
            kk = held["kk"] / jnp.maximum(jnp.sqrt(held["nrm"]), 1e-12)
            cum3 = cum.reshape(per, CHUNK, HEADS_DIM)
            cend = jnp.broadcast_to(cum3[:, CHUNK - 1:CHUNK, :], cum3.shape).reshape(pq, HEADS_DIM)
            diag = (lax.broadcasted_iota(jnp.int32, (pq, HEADS_DIM), 0) % CHUNK
                    == lax.broadcasted_iota(jnp.int32, (pq, HEADS_DIM), 1) % HEAD)
            held["fdec"] = bdsum(jnp.where(diag, jnp.exp(cend), 0.0), 2)
            at_scr[rows, :] = -kk * jnp.exp(cum - held["lw"])
            held.update(kk=kk, cend=cend)

        def s_inv():
            e_inv = jnp.exp(-held["cum"])
            kt_scr[rows, :] = held["k2"] * e_inv
            bt_scr[rows, :] = held["kk"] * held["a"] * e_inv

        def s_rel():
            e_rel = jnp.exp(held["cend"] - held["cum"])
            ktd_scr[rows, :] = held["k2"] * e_rel
            btd_scr[rows, :] = held["kk"] * held["a"] * e_rel
            fdec_scr[rows, :] = held["fdec"]

        def s_value():
            v = shifted(2 * HEADS_DIM, HEADS_DIM)
            v_scr[rows, :] = v
            bon_scr[rows, :] = held["bon"] * v

        return [s_lora, s_decay, s_keys, s_bonus, s_norm, s_inv, s_rel, s_value]

    def chunk_rows(c):
        return slice(c * CHUNK, (c + 1) * CHUNK)

    groups = [slice(g0, g0 + GROUP) for g0 in range(0, HEADS_DIM, GROUP)]

    named = dict(at=at_scr, rt=rt_scr, kt=kt_scr, bt=bt_scr, v=v_scr, btd=btd_scr, ktd=ktd_scr,
                 fdec=fdec_scr, ah=ah_scr, vh=vh_scr, arb=arb_scr, ark=ark_scr)
    prep_outs = (ah_scr, vh_scr, arb_scr, ark_scr)

    def loader(c, cols):
        rows = chunk_rows(c)
        return lambda name: named[name][rows, cols]

    def run(step_piece, prep_piece, extra):
        step_chunks = [] if step_piece is None else [step_piece * per + i for i in range(per)]
        prep_chunks = [] if prep_piece is None else [prep_piece * per + i for i in range(per)]
        stages, ys, finals = [], [], []
        if streams == 1:
            states = [st_scr[:, cols] for cols in groups] if step_chunks else []
            for c in step_chunks:
                out = []
                ys.append(out)
                stages += _wkv_step_stages([loader(c, cols) for cols in groups], states, out)
            finals = [(st_scr, states)] if step_chunks else []
        else:
            firsts, seconds = [], []
            for c in step_chunks:
                out = []
                ys.append(out)
                states = [state0_ref[c, :, cols] for cols in groups]
                first, second = _wkv_step_stages(
                    [loader(c, cols) for cols in groups], states, out)
                firsts.append(first)
                seconds.append(second)
                finals.append((state_out_ref.at[c], states))
            if step_chunks:
                stages += [lambda: [f() for f in firsts], lambda: [f() for f in seconds]]
        if step_chunks:
            rows = slice(step_piece * pq, (step_piece + 1) * pq)
            gn = {}

            def gn_mean():
                gn["y"] = jnp.concatenate([jnp.concatenate(out, axis=1) for out in ys], axis=0)
                gn["sum"] = bdsum(gn["y"], 1)

            def gn_var():
                gn["d"] = gn["y"] - gn["sum"] * (1.0 / HEAD)
                gn["sq"] = bdsum(gn["d"] * gn["d"], 1)

            def gn_out():
                var = gn["sq"] * (1.0 / HEAD)
                yn = (gn["d"] * lax.rsqrt(var + GN_EPS) * gng_ref[...] + gnb_ref[...]
                      + bon_scr[rows, :])
                out_ref[rows, :] = (yn * g_scr[rows, :]).astype(BF16)

            stages += [gn_mean, gn_var, gn_out]
        hooks = []
        for i in range(max(len(stages), len(extra))):
            both = stages[i:i + 1] + extra[i:i + 1]
            hooks.append(lambda both=both: [f() for f in both])
        units = [(c, cols) for c in prep_chunks for cols in groups]
        prep_out = _wkv_prep([loader(c, cols) for c, cols in units], hooks)
        for ref, states in finals:
            for cols, st in zip(groups, states):
                ref[:, cols] = st
        for (c, cols), outs in zip(units, prep_out):
            for ref, val in zip(prep_outs, outs):
                ref[chunk_rows(c), cols] = val

    look_ahead = steps > 1
    ahead_pass = max(2, n_piece - 1)
    assert ahead_pass <= n_piece or not look_ahead

    def first_piece():
        for stage in pre_stages(0):
            stage()

    if look_ahead:
        pl.when(j == 0)(first_piece)
    else:
        first_piece()
    for piece in range(n_piece + 1):
        if piece + 1 < n_piece:
            extra = pre_stages(piece + 1)
        elif piece == ahead_pass and look_ahead:
            extra = pre_stages(0, upcoming=True)
        else:
            extra = []
        run(piece - 1 if piece >= 1 else None, piece if piece < n_piece else None, extra)

    if streams == 1:
        @pl.when(j == steps - 1)
        def _():
            shift_out_ref[0] = p_ref[ts - 1:ts, :]
            state_out_ref[0] = st_scr[...]
    else:
        for s in range(streams):
            shift_out_ref[s] = p_ref[(s + 1) * CHUNK - 1:(s + 1) * CHUNK, :]


def _wkv_call(p2d, shift0, state0, prm, steps, ts):
    rows = p2d.shape[0]
    streams = state0.shape[0]
    row_map = lambda j: (j, 0)
    next_map = lambda j: (jnp.minimum(j + 1, steps - 1), 0)
    seq3 = lambda j: (0, 0, 0)
    vec = _const_spec((1, HEADS_DIM))
    big = pltpu.VMEM((ts, HEADS_DIM), F32)
    ltri = prm["ltri"][min(ts, WKV_PIECE)]
    return pl.pallas_call(
        functools.partial(_wkv_kernel, ts=ts, steps=steps),
        grid=(steps,),
        in_specs=[
            pl.BlockSpec((ts, RW_COLS), row_map),
            pl.BlockSpec((ts, RW_COLS), next_map),
            pl.BlockSpec((streams, 1, RW_COLS), seq3),
            pl.BlockSpec((streams, HEAD, HEADS_DIM), seq3),
            _const_spec((1, RW_COLS)),
            vec,
            _const_spec((128, 2 * HEADS_DIM)),
            vec,
            _const_spec((LORA_G, HEADS_DIM)),
            vec, vec, vec, vec, vec,
            _const_spec((GROUP, GROUP)),
            _const_spec(ltri.shape),
        ],
        out_specs=[
            pl.BlockSpec((ts, HEADS_DIM), row_map),
            pl.BlockSpec((streams, 1, RW_COLS), seq3),
            pl.BlockSpec((streams, HEAD, HEADS_DIM), seq3),
        ],
        out_shape=[
            jax.ShapeDtypeStruct((rows, HEADS_DIM), BF16),
            jax.ShapeDtypeStruct((streams, 1, RW_COLS), F32),
            jax.ShapeDtypeStruct((streams, HEAD, HEADS_DIM), F32),
        ],
        scratch_shapes=[
            pltpu.VMEM((HEAD, HEADS_DIM), F32),
        ] + [big] * 14,
        compiler_params=pltpu.CompilerParams(
            dimension_semantics=("arbitrary",), vmem_limit_bytes=VMEM_LIMIT),
        name="wkv",
    )(p2d, p2d, shift0, state0, prm["mu"], prm["w0"], prm["wwa"], prm["a0"], prm["gup"],
      prm["kk"], prm["ka"], prm["rk"], prm["gng"], prm["gnb"], prm["ones"], ltri)


def _mixffn_kernel(xn_ref, att_ref, rw_ref, gate_ref, mod_ref, convp_ref,
                   ln1g_ref, ln1b_ref, ln2g_ref, ln2b_ref,
                   wa_ref, wr_ref, wo_ref, wup_ref, cw_ref, cb_ref, wdn_ref,
                   y_ref, convo_ref, carry_scr, yb_scr, x1_scr, h2_scr, *, tm, steps):
    j = pl.program_id(0)
    groups = mod_ref.shape[0]
    assert groups == 1 or steps == 1
    pipelined = steps > 1

    def mod(idx):
        return _mod_row(mod_ref, idx, tm)

    half = tm // 2
    subs = [slice(0, half), slice(half, tm)]

    def modr(idx, rs):
        m = mod(idx)
        return m if m.shape[0] == 1 else m[rs]

    def front():
        ma = [_dot(att_ref[rs, :], wa_ref[...]) for rs in subs]
        mr = [_dot(rw_ref[rs, :], wr_ref[...]) for rs in subs]
        merged = [(gate_ref[rs, 0:D_MODEL] * a + gate_ref[rs, D_MODEL:] * r).astype(BF16)
                  for rs, a, r in zip(subs, ma, mr)]
        mix = [_dot(m, wo_ref[...]) for m in merged]
        x1 = [_layer_norm(ALPHA * xn_ref[rs, :] + (1.0 + modr(2, rs)) * m,
                          ln1g_ref[...], ln1b_ref[...]) for rs, m in zip(subs, mix)]
        h2 = jnp.concatenate([(a * (1.0 + modr(4, rs)) + modr(3, rs)).astype(BF16)
                              for rs, a in zip(subs, x1)], axis=0)
        return x1, h2

    def keep(x1, h2):
        for rs, a in zip(subs, x1):
            x1_scr[rs, :] = a
        h2_scr[...] = h2

    if not pipelined:
        if groups == 1:
            carry_scr[8 - (CONV_W - 1):8, :] = convp_ref[0]
        _ffn_back(*front(), subs, modr, groups, tm, convp_ref, ln2g_ref, ln2b_ref, wup_ref, cw_ref,
                  cb_ref, wdn_ref, y_ref, convo_ref, carry_scr, yb_scr)
        return

    @pl.when(j == 0)
    def _():
        carry_scr[8 - (CONV_W - 1):8, :] = convp_ref[0]
        keep(*front())

    @pl.when(j > 0)
    def _():
        x1 = [x1_scr[rs, :] for rs in subs]
        h2 = h2_scr[...]
        keep(*front())
        _ffn_back(x1, h2, subs, modr, groups, tm, convp_ref, ln2g_ref, ln2b_ref, wup_ref, cw_ref,
                  cb_ref, wdn_ref, y_ref, convo_ref, carry_scr, yb_scr)


def _ffn_back(x1, h2, subs, modr, groups, tm, convp_ref, ln2g_ref, ln2b_ref, wup_ref, cw_ref,
              cb_ref, wdn_ref, y_ref, convo_ref, carry_scr, yb_scr):
    cw_blk = 256
    glen = tm // groups
    row8 = lax.broadcasted_iota(jnp.int32, (8, cw_blk), 0)
    grow = lax.broadcasted_iota(jnp.int32, (tm, cw_blk), 0) % glen
    for c in range(0, D_FF, cw_blk):
        cs = slice(c, c + cw_blk)
        uc = _dot(h2, wup_ref[:, cs])
        uv = _dot(h2, wup_ref[:, D_FF + c:D_FF + c + cw_blk])
        r1 = pltpu.roll(uc, 1, 0)
        r2 = pltpu.roll(uc, 2, 0)
        if groups == 1:
            c6 = carry_scr[6:7, cs]
            c7 = carry_scr[7:8, cs]
            s1 = jnp.concatenate([jnp.where(row8 == 0, c7, r1[0:8]), r1[8:]], axis=0)
            s2 = jnp.concatenate(
                [jnp.where(row8 == 0, c6, jnp.where(row8 == 1, c7, r2[0:8])), r2[8:]], axis=0)
            carry_scr[:, cs] = uc[tm - 8:tm, :]
            tail = uc[tm - (CONV_W - 1):tm, :][None]
        else:
            hist = jnp.broadcast_to(convp_ref[:, :, cs][:, :, None, :],
                                    (groups, CONV_W - 1, glen, cw_blk))
            c6 = hist[:, 0].reshape(tm, cw_blk)
            c7 = hist[:, 1].reshape(tm, cw_blk)
            s1 = jnp.where(grow == 0, c7, r1)
            s2 = jnp.where(grow == 0, c6, jnp.where(grow == 1, c7, r2))
            tail = uc.reshape(groups, glen, cw_blk)[:, glen - (CONV_W - 1):, :]
        conv = cb_ref[:, cs] + s2 * cw_ref[0:1, cs] + s1 * cw_ref[1:2, cs] + uc * cw_ref[2:3, cs]
        yb_scr[:, cs] = (conv * _sigmoid(conv) * uv).astype(BF16)
        convo_ref[:, :, cs] = tail

    ff = [_dot(yb_scr[rs, :], wdn_ref[...]) for rs in subs]
    for rs, a, f in zip(subs, x1, ff):
        y_ref[rs, :] = _layer_norm(ALPHA * a + (1.0 + modr(5, rs)) * f, ln2g_ref[...], ln2b_ref[...])


def _mixffn_call(xn2d, att, rw, gates, mod, conv_prev, prm, steps, tm):
    rows = xn2d.shape[0]
    groups = mod.shape[0]
    lag = 1 if steps > 1 else 0
    row_map = lambda j: (jnp.minimum(j, steps - 1), 0)
    out_map = lambda j: (jnp.maximum(j - lag, 0), 0)
    vec = _const_spec((1, D_MODEL))
    return pl.pallas_call(
        functools.partial(_mixffn_kernel, tm=tm, steps=steps),
        grid=(steps + lag,),
        in_specs=[
            pl.BlockSpec((tm, D_MODEL), row_map),
            pl.BlockSpec((tm, HEADS_DIM), row_map),
            pl.BlockSpec((tm, HEADS_DIM), row_map),
            pl.BlockSpec((tm, GATE_COLS), row_map),
            _const_spec((groups, 6, D_MODEL)),
            _const_spec((groups, CONV_W - 1, D_FF)),
            vec, vec, vec, vec,
            _const_spec((HEADS_DIM, D_MODEL)),
            _const_spec((HEADS_DIM, D_MODEL)),
            _const_spec((D_MODEL, D_MODEL)),
            _const_spec((D_MODEL, 2 * D_FF)),
            _const_spec((CONV_W, D_FF)),
            _const_spec((1, D_FF)),
            _const_spec((D_FF, D_MODEL)),
        ],
        out_specs=[
            pl.BlockSpec((tm, D_MODEL), out_map),
            pl.BlockSpec((groups, CONV_W - 1, D_FF), lambda j: (0, 0, 0)),
        ],
        out_shape=[
            jax.ShapeDtypeStruct((rows, D_MODEL), F32),
            jax.ShapeDtypeStruct((groups, CONV_W - 1, D_FF), F32),
        ],
        scratch_shapes=[pltpu.VMEM((8, D_FF), F32), pltpu.VMEM((tm, D_FF), BF16),
                        pltpu.VMEM((tm, D_MODEL), F32), pltpu.VMEM((tm, D_MODEL), BF16)],
        compiler_params=pltpu.CompilerParams(
            dimension_semantics=("arbitrary",), vmem_limit_bytes=VMEM_LIMIT),
        name="mixffn",
    )(xn2d, att, rw, gates, mod, conv_prev,
      prm["ln1g"], prm["ln1b"], prm["ln2g"], prm["ln2b"],
      prm["wa"], prm["wr"], prm["wo"], prm["wup"], prm["cw"], prm["cb"], prm["wdn"])


def _pair_bias(table):
    assert CHUNK - 1 <= REL_CLIP
    top = ATT_REACH + CHUNK - 1
    n_far = top - REL_CLIP + 1
    far = jnp.broadcast_to(table[:, 2 * REL_CLIP:], (N_HEADS, n_far))
    lo_idx = top - (BAND + CHUNK - 2) + REL_CLIP
    near = table[:, lo_idx:2 * REL_CLIP][:, ::-1]
    ext = jnp.concatenate([far, near], axis=1).astype(F32) * LOG2E
    n_ext = BAND + CHUNK - 1
    period = jnp.concatenate([ext, jnp.zeros((N_HEADS, 1), F32)], axis=1)
    skew = jnp.tile(period, (1, CHUNK))[:, :CHUNK * n_ext].reshape(N_HEADS, CHUNK, n_ext)
    bias = skew[:, :, CHUNK - 1:CHUNK - 1 + BAND]
    return bias.reshape(N_HEADS // 2, 2 * CHUNK, BAND)


def _chunk_ltri(ts):
    t = jnp.arange(ts)
    return ((t[:, None] // CHUNK == t[None, :] // CHUNK) & (t[None, :] <= t[:, None])).astype(BF16)


def _trunk(x2d, mod, shift0, state0, conv_prev, caches, prm, n_seq):
    prompt = caches is None
    rows = x2d.shape[0]
    tm = min(rows, ROW_TILE)
    steps = rows // tm
    q, k, v, k32, v32, p, gates, xn = _inproj_call(
        x2d, mod, prm["lnig"], prm["lnib"], prm["win"], steps, tm, prompt)
    if prompt:
        att = _attn_prompt_call(q, k, v, prm["bias"])
    else:
        att = _attn_sample_call(q, k, v, *caches, prm["bias"])
    assert state0.shape[0] == n_seq
    rw, shift, state = _wkv_call(p, shift0, state0, prm, steps, tm)
    y, conv = _mixffn_call(xn, att, rw, gates, mod, conv_prev, prm, steps, tm)
    return y, (k32, v32), state, shift, conv


def kernel(x_prompt, x_sample, cache_attn_k, cache_attn_v, state_rwkv, state_shift, state_conv,
           c_prompt, c_sample, ln_in_g, ln_in_b, w_ada, b_ada, w_in, attn_rel_bias,
           rwkv_mu, rwkv_w0, rwkv_w_up, rwkv_a0, rwkv_a_up, rwkv_g_up, rwkv_k_k, rwkv_k_a,
           rwkv_r_k, rwkv_gn_g, rwkv_gn_b, w_branch_attn, w_branch_rwkv, w_out,
           ln1_g, ln1_b, ln2_g, ln2_b, w_ffn_up, ffn_conv_w, ffn_conv_b, w_ffn_down):
    bp, sp, _ = x_prompt.shape
    bs, ss, _ = x_sample.shape
    assert bp == 1 and ss == CHUNK and w_ada.shape[0] == DEPTH
    assert sp % ROW_TILE == 0 and bs * ss <= ROW_TILE and cache_attn_k.shape[2] == ATT_REACH

    row = lambda a: a.reshape(1, -1)
    wwa = jnp.zeros((LORA_W + LORA_A, 2 * HEADS_DIM), F32)
    wwa = wwa.at[:LORA_W, :HEADS_DIM].set(rwkv_w_up[0]).at[LORA_W:, HEADS_DIM:].set(rwkv_a_up[0])
    head_id = jnp.arange(GROUP) // HEAD
    prm = dict(
        lnig=row(ln_in_g), lnib=row(ln_in_b),
        ln1g=row(ln1_g[0]), ln1b=row(ln1_b[0]), ln2g=row(ln2_g[0]), ln2b=row(ln2_b[0]),
        win=w_in[0].astype(BF16), bias=_pair_bias(attn_rel_bias[0]),
        mu=row(rwkv_mu[0]), w0=row(rwkv_w0[0]), wwa=wwa.astype(BF16), a0=row(rwkv_a0[0]),
        gup=rwkv_g_up[0].astype(BF16), kk=row(rwkv_k_k[0]), ka=row(rwkv_k_a[0]),
        rk=row(rwkv_r_k[0]), gng=row(rwkv_gn_g[0]), gnb=row(rwkv_gn_b[0]),
        ones=(head_id[:, None] == head_id[None, :]).astype(BF16),
        ltri={WKV_PIECE: _chunk_ltri(WKV_PIECE), CHUNK: _chunk_ltri(CHUNK)},
        wa=w_branch_attn[0].astype(BF16), wr=w_branch_rwkv[0].astype(BF16),
        wo=w_out[0].astype(BF16), wup=w_ffn_up[0].astype(BF16),
        cw=ffn_conv_w[0], cb=row(ffn_conv_b[0]), wdn=w_ffn_down[0].astype(BF16),
    )

    n_c = bp + bs
    c_all = jnp.concatenate([c_prompt, c_sample, jnp.zeros((16 - n_c, D_MODEL), F32)], axis=0)
    mod = _mod_call(c_all, w_ada[0], row(b_ada[0])).reshape(16, 6, D_MODEL)

    y_p, kv_p, st_p, sh_p, cv_p = _trunk(
        x_prompt.reshape(sp, D_MODEL), mod[0:bp],
        jnp.zeros((bp, 1, RW_COLS), F32), jnp.zeros((bp, HEAD, HEADS_DIM), F32),
        jnp.zeros((bp, CONV_W - 1, D_FF), F32), None, prm, n_seq=bp)

    caches = (cache_attn_k[0].astype(BF16).reshape(bs, ATT_REACH, HEADS_DIM),
              cache_attn_v[0].astype(BF16).reshape(bs, ATT_REACH, HEADS_DIM))
    st0 = jnp.transpose(state_rwkv[0], (0, 3, 1, 2)).reshape(bs, HEAD, HEADS_DIM)
    y_s, kv_s, st_s, sh_s, cv_s = _trunk(
        x_sample.reshape(bs * ss, D_MODEL), mod[bp:n_c],
        state_shift[0], st0, state_conv[0], caches, prm, n_seq=bs)

    def state_out(st, b):
        return jnp.transpose(st.reshape(b, HEAD, N_HEADS, HEAD), (0, 2, 3, 1))[None]

    hs = (N_HEADS, HEAD)
    return (
        y_p.reshape(bp, sp, D_MODEL),
        y_s.reshape(bs, ss, D_MODEL),
        kv_p[0].reshape(1, bp, ATT_REACH, *hs),
        kv_p[1].reshape(1, bp, ATT_REACH, *hs),
        kv_s[0].reshape(1, bs, ss, *hs),
        kv_s[1].reshape(1, bs, ss, *hs),
        state_out(st_p, bp),
        state_out(st_s, bs),
        sh_p[None],
        sh_s[None],
        cv_p[None],
        cv_s[None],
    )
```

```python
import functools

import jax
import jax.numpy as jnp
from jax import lax
from jax.experimental import pallas as pl
from jax.experimental.pallas import tpu as pltpu

F32 = jnp.float32
BF16 = jnp.bfloat16

D_MODEL = 1024
CHUNK = 64
ATT_REACH = 512
BAND = ATT_REACH + CHUNK
N_HEADS = 8
HEAD = 64
HEADS_DIM = N_HEADS * HEAD
REL_CLIP = 128
LORA_W = 64
LORA_A = 64
LORA_G = 128
ATT_COLS = 3 * HEADS_DIM
RW_COLS = 3 * HEADS_DIM + LORA_W + LORA_A + LORA_G
GATE_COLS = 2 * D_MODEL
D_FF = 2816
CONV_W = 3
LN_EPS = 1e-5
GN_EPS = 64e-5
DEPTH = 1
ALPHA = (2 * DEPTH) ** 0.25
LOG2E = 1.4426950408889634

GROUP = 256
ROW_TILE = 512
WKV_PIECE = 256
VMEM_LIMIT = 56 * 1024 * 1024


def _const_spec(shape):
    nd = len(shape)
    return pl.BlockSpec(shape, lambda *_: (0,) * nd, pipeline_mode=pl.Buffered(1))


def _layer_norm(x, g, b):
    mu = jnp.mean(x, axis=-1, keepdims=True)
    xc = x - mu
    var = jnp.mean(xc * xc, axis=-1, keepdims=True)
    return xc * lax.rsqrt(var + LN_EPS) * g + b


def _sigmoid(x):
    return 1.0 / (1.0 + jnp.exp(-x))


def _split3(x):
    hi = x.astype(BF16)
    r1 = x - hi.astype(F32)
    mid = r1.astype(BF16)
    lo = (r1 - mid.astype(F32)).astype(BF16)
    return hi, mid, lo


def _dot(a, b):
    return jnp.dot(a, b, preferred_element_type=F32)


def _dot_nt(a, b):
    return lax.dot_general(a, b, (((1,), (1,)), ((), ())), preferred_element_type=F32)


def _dot_tn(a, b):
    return lax.dot_general(a, b, (((0,), (0,)), ((), ())), preferred_element_type=F32)


def _mod_kernel(c_ref, w_ref, b_ref, o_ref):
    c = c_ref[...]
    s = (c * _sigmoid(c)).astype(BF16)
    o_ref[...] = _dot(s, w_ref[...].astype(BF16)) + b_ref[...]


def _mod_call(c_all, w_ada, b_ada):
    n = c_all.shape[0]
    nblk = 6
    return pl.pallas_call(
        _mod_kernel,
        grid=(nblk,),
        in_specs=[
            pl.BlockSpec((n, D_MODEL), lambda i: (0, 0)),
            pl.BlockSpec((D_MODEL, D_MODEL), lambda i: (0, i)),
            pl.BlockSpec((1, D_MODEL), lambda i: (0, i)),
        ],
        out_specs=pl.BlockSpec((n, D_MODEL), lambda i: (0, i)),
        out_shape=jax.ShapeDtypeStruct((n, 6 * D_MODEL), F32),
        compiler_params=pltpu.CompilerParams(dimension_semantics=("arbitrary",)),
        name="mod",
    )(c_all, w_ada, b_ada)


def _mod_row(mod_ref, idx, tm):
    groups, _, d = mod_ref.shape
    m = mod_ref[:, idx:idx + 1, :]
    if groups == 1:
        return m[0]
    return jnp.broadcast_to(m, (groups, tm // groups, d)).reshape(tm, d)


def _inproj_kernel(x_ref, xnext_ref, mod_ref, lng_ref, lnb_ref, w_ref,
                   q_ref, k_ref, v_ref, k32_ref, v32_ref, p_ref, g_ref, xn_ref,
                   xna_scr, hba_scr, *, lead):
    tm = x_ref.shape[0]
    j = pl.program_id(0)

    first, second = slice(0, tm // 2), slice(tm // 2, tm)

    def front(src_ref, rs):
        def mrow(idx):
            m = _mod_row(mod_ref, idx, tm)
            return m if m.shape[0] == 1 else m[rs]

        xn = _layer_norm(src_ref[rs, :], lng_ref[...], lnb_ref[...])
        return xn, (xn * (1.0 + mrow(1)) + mrow(0)).astype(BF16)

    def half_tile(rs, prepared=False):
        if prepared:
            xn, hb = xna_scr[...], hba_scr[...]
        else:
            xn, hb = front(x_ref, rs)
        xn_ref[rs, :] = xn

        def seg(a, b):
            return _dot(hb, w_ref[:, a:b])

        g0 = ATT_COLS + RW_COLS
        for c in range(0, GATE_COLS, 512):
            g_ref[rs, c:c + 512] = _sigmoid(seg(g0 + c, g0 + c + 512))
        q_ref[rs, :] = (seg(0, HEADS_DIM) * (HEAD ** -0.5 * LOG2E)).astype(BF16)
        k = seg(HEADS_DIM, 2 * HEADS_DIM)
        k_ref[rs, :] = k.astype(BF16)
        k32_ref[rs, :] = k
        v = seg(2 * HEADS_DIM, 3 * HEADS_DIM)
        v_ref[rs, :] = v.astype(BF16)
        v32_ref[rs, :] = v
        for c in range(0, RW_COLS, 256):
            p_ref[rs, c:c + 256] = seg(ATT_COLS + c, ATT_COLS + c + 256)

    def prepare_next():
        xna_scr[...], hba_scr[...] = front(xnext_ref, first)

    @pl.when(j < lead)
    def _():
        k_ref[...] = jnp.zeros_like(k_ref)
        v_ref[...] = jnp.zeros_like(v_ref)
        prepare_next()

    @pl.when(j >= lead)
    def _():
        half_tile(first, prepared=lead > 0)
        if lead > 0:
            prepare_next()
        half_tile(second)


def _inproj_call(x2d, mod, ln_g, ln_b, w_in_b, steps, tm, prompt):
    rows = x2d.shape[0]
    in_cols = w_in_b.shape[1]
    groups = mod.shape[0]
    if prompt:
        lead = ATT_REACH // tm
        kv_rows = ATT_REACH
        row_map = lambda j: (jnp.maximum(j - lead, 0), 0)
        next_map = lambda j: (jnp.clip(j + 1 - lead, 0, steps - 1), 0)
        kv_map = lambda j: (jnp.maximum(j - steps, 0), 0)
        ext_map = lambda j: (j, 0)
    else:
        lead = 0
        kv_rows = rows
        row_map = next_map = kv_map = ext_map = lambda j: (j, 0)
    return pl.pallas_call(
        functools.partial(_inproj_kernel, lead=lead),
        grid=(steps + lead,),
        in_specs=[
            pl.BlockSpec((tm, D_MODEL), row_map),
            pl.BlockSpec((tm, D_MODEL), next_map),
            _const_spec((groups, 6, D_MODEL)),
            _const_spec((1, D_MODEL)),
            _const_spec((1, D_MODEL)),
            _const_spec((D_MODEL, in_cols)),
        ],
        scratch_shapes=[pltpu.VMEM((tm // 2, D_MODEL), F32), pltpu.VMEM((tm // 2, D_MODEL), BF16)],
        out_specs=[
            pl.BlockSpec((tm, HEADS_DIM), row_map),
            pl.BlockSpec((tm, HEADS_DIM), ext_map),
            pl.BlockSpec((tm, HEADS_DIM), ext_map),
            pl.BlockSpec((tm, HEADS_DIM), kv_map),
            pl.BlockSpec((tm, HEADS_DIM), kv_map),
            pl.BlockSpec((tm, RW_COLS), row_map),
            pl.BlockSpec((tm, GATE_COLS), row_map),
            pl.BlockSpec((tm, D_MODEL), row_map),
        ],
        out_shape=[
            jax.ShapeDtypeStruct((rows, HEADS_DIM), BF16),
            jax.ShapeDtypeStruct((rows + lead * tm, HEADS_DIM), BF16),
            jax.ShapeDtypeStruct((rows + lead * tm, HEADS_DIM), BF16),
            jax.ShapeDtypeStruct((kv_rows, HEADS_DIM), F32),
            jax.ShapeDtypeStruct((kv_rows, HEADS_DIM), F32),
            jax.ShapeDtypeStruct((rows, RW_COLS), F32),
            jax.ShapeDtypeStruct((rows, GATE_COLS), F32),
            jax.ShapeDtypeStruct((rows, D_MODEL), F32),
        ],
        compiler_params=pltpu.CompilerParams(
            dimension_semantics=("arbitrary",), vmem_limit_bytes=VMEM_LIMIT),
        name="inproj",
    )(x2d, x2d, mod, ln_g, ln_b, w_in_b)


def _attn_chunks(chunks, bias_ref):
    lane = lax.broadcasted_iota(jnp.int32, (CHUNK, 128), 1)
    first = lane < HEAD
    pairs = [slice(pr * 128, (pr + 1) * 128) for pr in range(N_HEADS // 2)]
    scores = []
    for qc, kb, _, _ in chunks:
        for sl in pairs:
            q2 = qc[:, sl].astype(F32)
            qs = jnp.concatenate([jnp.where(first, q2, 0.0), jnp.where(first, 0.0, q2)],
                                 axis=0).astype(BF16)
            scores.append(_dot_nt(qs, kb(sl)))
    probs, sums = [], []
    for i, s in enumerate(scores):
        thr = chunks[i // len(pairs)][3]
        s = s + bias_ref[i % len(pairs)]
        if thr is not None:
            col = lax.broadcasted_iota(jnp.int32, s.shape, 1)
            s = jnp.where(col >= thr, s, -jnp.inf)
        e = jnp.exp2(s - jnp.max(s, axis=1, keepdims=True))
        sums.append(jnp.sum(e, axis=1, keepdims=True))
        probs.append(e.astype(BF16))
    outs = []
    for ci, (_, _, vb, _) in enumerate(chunks):
        cols = []
        for pi, sl in enumerate(pairs):
            i = ci * len(pairs) + pi
            o = _dot(probs[i], vb(sl)) / sums[i]
            cols.append(jnp.where(first, o[0:CHUNK], o[CHUNK:2 * CHUNK]))
        outs.append(jnp.concatenate(cols, axis=1).astype(BF16))
    return outs


def _attn_prompt_kernel(q_ref, kbuf, vbuf, bias_ref, o_ref, *, chunks):
    s = pl.program_id(0)

    per = 2

    def run(masked):
        for i in range(chunks // per):
            units, starts = [], []
            for k in range(per):
                g = i * per + k
                r0 = g * CHUNK
                thr = ATT_REACH - (s * chunks + g) * CHUNK if masked else None
                band = slice(r0, r0 + BAND)
                units.append((q_ref[r0:r0 + CHUNK, :],
                              lambda sl, band=band: kbuf[band, sl],
                              lambda sl, band=band: vbuf[band, sl], thr))
                starts.append(r0)
            for r0, o in zip(starts, _attn_chunks(units, bias_ref)):
                o_ref[r0:r0 + CHUNK, :] = o

    assert chunks * CHUNK >= ATT_REACH
    pl.when(s == 0)(functools.partial(run, True))
    pl.when(s != 0)(functools.partial(run, False))


def _attn_prompt_call(q, kext, vext, bias):
    rows = q.shape[0]
    tq = ATT_REACH
    blk = pl.BlockSpec((tq, HEADS_DIM), lambda s: (s, 0))
    window = pl.BlockSpec((pl.Element(tq + ATT_REACH), pl.Element(HEADS_DIM)),
                          lambda s: (s * tq, 0))
    return pl.pallas_call(
        functools.partial(_attn_prompt_kernel, chunks=tq // CHUNK),
        grid=(rows // tq,),
        in_specs=[blk, window, window, _const_spec(bias.shape)],
        out_specs=blk,
        out_shape=jax.ShapeDtypeStruct((rows, HEADS_DIM), BF16),
        compiler_params=pltpu.CompilerParams(dimension_semantics=("arbitrary",)),
        name="attn_prompt",
    )(q, kext, vext, bias)


def _attn_sample_kernel(q_ref, k_ref, v_ref, ck_ref, cv_ref, bias_ref, o_ref):
    def band(cache_ref, new_ref, rs):
        return lambda sl: jnp.concatenate([cache_ref[:, sl], new_ref[rs, sl]], axis=0)

    units = []
    for s in range(ck_ref.shape[0]):
        rs = slice(s * CHUNK, (s + 1) * CHUNK)
        units.append((q_ref[rs, :], band(ck_ref.at[s], k_ref, rs), band(cv_ref.at[s], v_ref, rs),
                      None))
    for s, o in enumerate(_attn_chunks(units, bias_ref)):
        o_ref[s * CHUNK:(s + 1) * CHUNK, :] = o


def _attn_sample_call(q, k, v, cache_k, cache_v, bias):
    per = 2
    nb = cache_k.shape[0] // per
    new = pl.BlockSpec((per * CHUNK, HEADS_DIM), lambda b: (b, 0))
    cache = pl.BlockSpec((per, ATT_REACH, HEADS_DIM), lambda b: (b, 0, 0))
    return pl.pallas_call(
        _attn_sample_kernel,
        grid=(nb,),
        in_specs=[new, new, new, cache, cache, _const_spec(bias.shape)],
        out_specs=new,
        out_shape=jax.ShapeDtypeStruct(q.shape, BF16),
        compiler_params=pltpu.CompilerParams(dimension_semantics=("arbitrary",)),
        name="attn_sample",
    )(q, k, v, cache_k, cache_v, bias)


def _same_head():
    r = lax.broadcasted_iota(jnp.int32, (GROUP, GROUP), 0) // HEAD
    c = lax.broadcasted_iota(jnp.int32, (GROUP, GROUP), 1) // HEAD
    return r == c


def _blk(x, same_head):
    return jnp.where(same_head, jnp.concatenate([x] * 4, axis=0), 0.0).astype(BF16)


def _wkv_prep(units, hooks):
    same_head = _same_head()
    t64 = lax.broadcasted_iota(jnp.int32, (CHUNK, GROUP), 0)
    i64 = lax.broadcasted_iota(jnp.int32, (CHUNK, GROUP), 1) % HEAD
    strict = i64 < t64
    incl = i64 <= t64
    pending = list(hooks)

    def stage_done():
        if pending:
            pending.pop(0)()

    def blk(x):
        return _blk(x, same_head)

    n, a_ak, a_rb, a_rk = [], [], [], []
    for ld in units:
        lhs = jnp.concatenate([ld("at"), ld("rt")], axis=0).astype(BF16)

        def blk_t(x):
            xt = jnp.transpose(x)
            return jnp.where(same_head, jnp.concatenate([xt] * 4, axis=1), 0.0).astype(BF16)

        rhs = jnp.concatenate([blk_t(ld("bt")), blk_t(ld("kt"))], axis=1)
        a_all = _dot(lhs, rhs)
        n.append(jnp.where(strict, a_all[0:CHUNK, 0:GROUP], 0.0))
        a_ak.append(jnp.where(strict, a_all[0:CHUNK, GROUP:], 0.0))
        a_rb.append(jnp.where(incl, a_all[CHUNK:, 0:GROUP], 0.0))
        a_rk.append(jnp.where(incl, a_all[CHUNK:, GROUP:], 0.0))
    stage_done()

    x0 = [_dot(a.astype(BF16), blk(ld("v"))) for a, ld in zip(a_ak, units)]
    npow = [_dot(m.astype(BF16), blk(m)) for m in n]
    t = [jnp.where(i64 == t64, 1.0, 0.0) + m for m in n]
    stage_done()
    for _ in range(4):
        prod = [_dot(jnp.concatenate([a, b], axis=0).astype(BF16), blk(b))
                for a, b in zip(t, npow)]
        t = [a + p[0:CHUNK] for a, p in zip(t, prod)]
        npow = [p[CHUNK:] for p in prod]
        stage_done()
    t = [a + _dot(a.astype(BF16), blk(b)) for a, b in zip(t, npow)]
    stage_done()
    res = [_dot(a.astype(BF16), jnp.concatenate([blk(ld("at")), blk(x)], axis=1))
           for a, x, ld in zip(t, x0, units)]
    stage_done()
    while pending:
        stage_done()
    return [(r[:, 0:GROUP], r[:, GROUP:], b, k) for r, b, k in zip(res, a_rb, a_rk)]


def _wkv_step_stages(units, states, out):
    same_head = _same_head()
    held = {}

    def blk(x):
        return _blk(x, same_head)

    def first():
        both = [_dot(jnp.concatenate([ld("ah"), ld("rt")], axis=0).astype(BF16), blk(st))
                for ld, st in zip(units, states)]
        held["u"] = [b[0:CHUNK] + ld("vh") for b, ld in zip(both, units)]
        held["rs"] = [b[CHUNK:] for b in both]

    def second():
        for i, ld in enumerate(units):
            u, v = held["u"][i], ld("v")
            y = held["rs"][i] + _dot(
                jnp.concatenate([ld("arb"), ld("ark")], axis=1).astype(BF16),
                jnp.concatenate([blk(u), blk(v)], axis=0))
            g = _dot_tn(jnp.concatenate([ld("btd"), ld("ktd")], axis=0).astype(BF16),
                        jnp.concatenate([u, v], axis=0).astype(BF16))
            g = jnp.where(same_head, g, 0.0)
            delta = g[0:64] + g[64:128] + g[128:192] + g[192:256]
            out.append(y)
            states[i] = states[i] * ld("fdec") + delta

    return [first, second]


def _wkv_kernel(p_ref, pn_ref, shift0_ref, state0_ref, mu_ref, w0_ref, wwa_ref, a0_ref, gup_ref,
                kkw_ref, kaw_ref, rkw_ref, gng_ref, gnb_ref, ones_ref, ltri_ref,
                out_ref, shift_out_ref, state_out_ref,
                st_scr, rt_scr, at_scr, kt_scr, bt_scr, v_scr, btd_scr, ktd_scr,
                fdec_scr, ah_scr, vh_scr, arb_scr, ark_scr, g_scr, bon_scr, *, ts, steps):
    j = pl.program_id(0)
    streams = state0_ref.shape[0]
    assert streams == 1 or (steps == 1 and ts == streams * CHUNK)

    if streams == 1:
        @pl.when(j == 0)
        def _():
            st_scr[...] = state0_ref[0]

    def bdsum(x, terms):
        ones = ones_ref[...]
        halves = []
        for c0 in range(0, HEADS_DIM, GROUP):
            parts = _split3(x[:, c0:c0 + GROUP])[:terms]
            acc = _dot(parts[0], ones)
            for part in parts[1:]:
                acc = acc + _dot(part, ones)
            halves.append(acc)
        return jnp.concatenate(halves, axis=1)

    pq = ltri_ref.shape[0]
    per = pq // CHUNK
    n_piece = ts // pq

    def pre_stages(q, upcoming=False):
        r0 = q * pq
        rows = slice(r0, r0 + pq)
        src_ref = pn_ref if upcoming else p_ref
        held = {}

        def shifted(c0, w):
            cols = slice(c0, c0 + w)
            row = lax.broadcasted_iota(jnp.int32, (pq, 1), 0)
            if streams > 1:
                first = shift0_ref[q * per:(q + 1) * per, :, cols]
                prev_row = jnp.broadcast_to(first, (per, CHUNK, w)).reshape(pq, w)
                top = row % CHUNK == 0
            else:
                if upcoming:
                    prev_row = p_ref[ts - 8:ts, cols][7:8]
                elif q == 0:
                    prev_row = shift0_ref[0][:, cols]
                else:
                    prev_row = p_ref[r0 - 8:r0, cols][7:8]
                top = row == 0
            pb = src_ref[rows, cols]
            prev = jnp.where(top, prev_row, pltpu.roll(pb, 1, 0))
            return pb + (prev - pb) * mu_ref[:, cols]

        def s_lora():
            lora = shifted(3 * HEADS_DIM, 256)
            lwla = lora[:, 0:128]
            lane = lax.broadcasted_iota(jnp.int32, (pq, 128), 1)
            held["raw"] = _dot(jnp.where(lane < LORA_W, jnp.tanh(lwla), lwla).astype(BF16),
                               wwa_ref[...])
            g_scr[rows, :] = _dot(_sigmoid(lora[:, 128:256]).astype(BF16), gup_ref[...])
            held["k"] = shifted(HEADS_DIM, HEADS_DIM)

        def s_decay():
            w_pre = w0_ref[...] + held["raw"][:, 0:HEADS_DIM]
            softplus = jnp.maximum(-w_pre, 0.0) + jnp.log(1.0 + jnp.exp(-jnp.abs(w_pre)))
            lw = -jnp.exp(-softplus - 0.5)
            ltri = ltri_ref[...]
            hi, mid, _ = _split3(lw)
            held["cum"] = _dot(ltri, hi) + _dot(ltri, mid)
            held["lw"] = lw

        def s_keys():
            k = held["k"]
            a = _sigmoid(a0_ref[...] + held["raw"][:, HEADS_DIM:])
            kk = k * kkw_ref[...]
            held["nrm"] = bdsum(kk * kk, 1)
            held.update(a=a, kk=kk, k2=k * (1.0 + (a - 1.0) * kaw_ref[...]))

        def s_bonus():
            r = shifted(0, HEADS_DIM)
            held["bon"] = bdsum(r * held["k2"] * rkw_ref[...], 1)
            rt_scr[rows, :] = r * jnp.exp(held["cum"])

        def s_norm():
            cum = held["cum"]
            kk = held["kk"] / jnp.maximum(jnp.sqrt(held["nrm"]), 1e-12)
            cum3 = cum.reshape(per, CHUNK, HEADS_DIM)
            cend = jnp.broadcast_to(cum3[:, CHUNK - 1:CHUNK, :], cum3.shape).reshape(pq, HEADS_DIM)
            diag = (lax.broadcasted_iota(jnp.int32, (pq, HEADS_DIM), 0) % CHUNK
                    == lax.broadcasted_iota(jnp.int32, (pq, HEADS_DIM), 1) % HEAD)
            held["fdec"] = bdsum(jnp.where(diag, jnp.exp(cend), 0.0), 2)
            at_scr[rows, :] = -kk * jnp.exp(cum - held["lw"])
            held.update(kk=kk, cend=cend)

        def s_inv():
            e_inv = jnp.exp(-held["cum"])
            kt_scr[rows, :] = held["k2"] * e_inv
            bt_scr[rows, :] = held["kk"] * held["a"] * e_inv

        def s_rel():
            e_rel = jnp.exp(held["cend"] - held["cum"])
            ktd_scr[rows, :] = held["k2"] * e_rel
            btd_scr[rows, :] = held["kk"] * held["a"] * e_rel
            fdec_scr[rows, :] = held["fdec"]

        def s_value():
            v = shifted(2 * HEADS_DIM, HEADS_DIM)
            v_scr[rows, :] = v
            bon_scr[rows, :] = held["bon"] * v

        return [s_lora, s_decay, s_keys, s_bonus, s_norm, s_inv, s_rel, s_value]

    def chunk_rows(c):
        return slice(c * CHUNK, (c + 1) * CHUNK)

    groups = [slice(g0, g0 + GROUP) for g0 in range(0, HEADS_DIM, GROUP)]

    named = dict(at=at_scr, rt=rt_scr, kt=kt_scr, bt=bt_scr, v=v_scr, btd=btd_scr, ktd=ktd_scr,
                 fdec=fdec_scr, ah=ah_scr, vh=vh_scr, arb=arb_scr, ark=ark_scr)
    prep_outs = (ah_scr, vh_scr, arb_scr, ark_scr)

    def loader(c, cols):
        rows = chunk_rows(c)
        return lambda name: named[name][rows, cols]

    def run(step_piece, prep_piece, extra):
        step_chunks = [] if step_piece is None else [step_piece * per + i for i in range(per)]
        prep_chunks = [] if prep_piece is None else [prep_piece * per + i for i in range(per)]
        stages, ys, finals = [], [], []
        if streams == 1:
            states = [st_scr[:, cols] for cols in groups] if step_chunks else []
            for c in step_chunks:
                out = []
                ys.append(out)
                stages += _wkv_step_stages([loader(c, cols) for cols in groups], states, out)
            finals = [(st_scr, states)] if step_chunks else []
        else:
            firsts, seconds = [], []
            for c in step_chunks:
                out = []
                ys.append(out)
                states = [state0_ref[c, :, cols] for cols in groups]
                first, second = _wkv_step_stages(
                    [loader(c, cols) for cols in groups], states, out)
                firsts.append(first)
                seconds.append(second)
                finals.append((state_out_ref.at[c], states))
            if step_chunks:
                stages += [lambda: [f() for f in firsts], lambda: [f() for f in seconds]]
        if step_chunks:
            rows = slice(step_piece * pq, (step_piece + 1) * pq)
            gn = {}

            def gn_mean():
                gn["y"] = jnp.concatenate([jnp.concatenate(out, axis=1) for out in ys], axis=0)
                gn["sum"] = bdsum(gn["y"], 1)

            def gn_var():
                gn["d"] = gn["y"] - gn["sum"] * (1.0 / HEAD)
                gn["sq"] = bdsum(gn["d"] * gn["d"], 1)

            def gn_out():
                var = gn["sq"] * (1.0 / HEAD)
                yn = (gn["d"] * lax.rsqrt(var + GN_EPS) * gng_ref[...] + gnb_ref[...]
                      + bon_scr[rows, :])
                out_ref[rows, :] = (yn * g_scr[rows, :]).astype(BF16)

            stages += [gn_mean, gn_var, gn_out]
        hooks = []
        for i in range(max(len(stages), len(extra))):
            both = stages[i:i + 1] + extra[i:i + 1]
            hooks.append(lambda both=both: [f() for f in both])
        units = [(c, cols) for c in prep_chunks for cols in groups]
        prep_out = _wkv_prep([loader(c, cols) for c, cols in units], hooks)
        for ref, states in finals:
            for cols, st in zip(groups, states):
                ref[:, cols] = st
        for (c, cols), outs in zip(units, prep_out):
            for ref, val in zip(prep_outs, outs):
                ref[chunk_rows(c), cols] = val

    look_ahead = steps > 1
    ahead_pass = max(2, n_piece - 1)
    assert ahead_pass <= n_piece or not look_ahead

    def first_piece():
        for stage in pre_stages(0):
            stage()

    if look_ahead:
        pl.when(j == 0)(first_piece)
    else:
        first_piece()
    for piece in range(n_piece + 1):
        if piece + 1 < n_piece:
            extra = pre_stages(piece + 1)
        elif piece == ahead_pass and look_ahead:
            extra = pre_stages(0, upcoming=True)
        else:
            extra = []
        run(piece - 1 if piece >= 1 else None, piece if piece < n_piece else None, extra)

    if streams == 1:
        @pl.when(j == steps - 1)
        def _():
            shift_out_ref[0] = p_ref[ts - 1:ts, :]
            state_out_ref[0] = st_scr[...]
    else:
        for s in range(streams):
            shift_out_ref[s] = p_ref[(s + 1) * CHUNK - 1:(s + 1) * CHUNK, :]


def _wkv_call(p2d, shift0, state0, prm, steps, ts):
    rows = p2d.shape[0]
    streams = state0.shape[0]
    row_map = lambda j: (j, 0)
    next_map = lambda j: (jnp.minimum(j + 1, steps - 1), 0)
    seq3 = lambda j: (0, 0, 0)
    vec = _const_spec((1, HEADS_DIM))
    big = pltpu.VMEM((ts, HEADS_DIM), F32)
    ltri = prm["ltri"][min(ts, WKV_PIECE)]
    return pl.pallas_call(
        functools.partial(_wkv_kernel, ts=ts, steps=steps),
        grid=(steps,),
        in_specs=[
            pl.BlockSpec((ts, RW_COLS), row_map),
            pl.BlockSpec((ts, RW_COLS), next_map),
            pl.BlockSpec((streams, 1, RW_COLS), seq3),
            pl.BlockSpec((streams, HEAD, HEADS_DIM), seq3),
            _const_spec((1, RW_COLS)),
            vec,
            _const_spec((128, 2 * HEADS_DIM)),
            vec,
            _const_spec((LORA_G, HEADS_DIM)),
            vec, vec, vec, vec, vec,
            _const_spec((GROUP, GROUP)),
            _const_spec(ltri.shape),
        ],
        out_specs=[
            pl.BlockSpec((ts, HEADS_DIM), row_map),
            pl.BlockSpec((streams, 1, RW_COLS), seq3),
            pl.BlockSpec((streams, HEAD, HEADS_DIM), seq3),
        ],
        out_shape=[
            jax.ShapeDtypeStruct((rows, HEADS_DIM), BF16),
            jax.ShapeDtypeStruct((streams, 1, RW_COLS), F32),
            jax.ShapeDtypeStruct((streams, HEAD, HEADS_DIM), F32),
        ],
        scratch_shapes=[
            pltpu.VMEM((HEAD, HEADS_DIM), F32),
        ] + [big] * 14,
        compiler_params=pltpu.CompilerParams(
            dimension_semantics=("arbitrary",), vmem_limit_bytes=VMEM_LIMIT),
        name="wkv",
    )(p2d, p2d, shift0, state0, prm["mu"], prm["w0"], prm["wwa"], prm["a0"], prm["gup"],
      prm["kk"], prm["ka"], prm["rk"], prm["gng"], prm["gnb"], prm["ones"], ltri)


def _mixffn_kernel(xn_ref, att_ref, rw_ref, gate_ref, mod_ref, convp_ref,
                   ln1g_ref, ln1b_ref, ln2g_ref, ln2b_ref,
                   wa_ref, wr_ref, wo_ref, wup_ref, cw_ref, cb_ref, wdn_ref,
                   y_ref, convo_ref, carry_scr, yb_scr, x1_scr, h2_scr, *, tm, steps):
    j = pl.program_id(0)
    groups = mod_ref.shape[0]
    assert groups == 1 or steps == 1
    pipelined = steps > 1

    def mod(idx):
        return _mod_row(mod_ref, idx, tm)

    half = tm // 2
    subs = [slice(0, half), slice(half, tm)]

    def modr(idx, rs):
        m = mod(idx)
        return m if m.shape[0] == 1 else m[rs]

    def front():
        ma = [_dot(att_ref[rs, :], wa_ref[...]) for rs in subs]
        mr = [_dot(rw_ref[rs, :], wr_ref[...]) for rs in subs]
        merged = [(gate_ref[rs, 0:D_MODEL] * a + gate_ref[rs, D_MODEL:] * r).astype(BF16)
                  for rs, a, r in zip(subs, ma, mr)]
        mix = [_dot(m, wo_ref[...]) for m in merged]
        x1 = [_layer_norm(ALPHA * xn_ref[rs, :] + (1.0 + modr(2, rs)) * m,
                          ln1g_ref[...], ln1b_ref[...]) for rs, m in zip(subs, mix)]
        h2 = jnp.concatenate([(a * (1.0 + modr(4, rs)) + modr(3, rs)).astype(BF16)
                              for rs, a in zip(subs, x1)], axis=0)
        return x1, h2

    def keep(x1, h2):
        for rs, a in zip(subs, x1):
            x1_scr[rs, :] = a
        h2_scr[...] = h2

    if not pipelined:
        if groups == 1:
            carry_scr[8 - (CONV_W - 1):8, :] = convp_ref[0]
        _ffn_back(*front(), subs, modr, groups, tm, convp_ref, ln2g_ref, ln2b_ref, wup_ref, cw_ref,
                  cb_ref, wdn_ref, y_ref, convo_ref, carry_scr, yb_scr)
        return

    @pl.when(j == 0)
    def _():
        carry_scr[8 - (CONV_W - 1):8, :] = convp_ref[0]
        keep(*front())

    @pl.when(j > 0)
    def _():
        x1 = [x1_scr[rs, :] for rs in subs]
        h2 = h2_scr[...]
        keep(*front())
        _ffn_back(x1, h2, subs, modr, groups, tm, convp_ref, ln2g_ref, ln2b_ref, wup_ref, cw_ref,
                  cb_ref, wdn_ref, y_ref, convo_ref, carry_scr, yb_scr)


def _ffn_back(x1, h2, subs, modr, groups, tm, convp_ref, ln2g_ref, ln2b_ref, wup_ref, cw_ref,
              cb_ref, wdn_ref, y_ref, convo_ref, carry_scr, yb_scr):
    cw_blk = 256
    glen = tm // groups
    row8 = lax.broadcasted_iota(jnp.int32, (8, cw_blk), 0)
    grow = lax.broadcasted_iota(jnp.int32, (tm, cw_blk), 0) % glen
    for c in range(0, D_FF, cw_blk):
        cs = slice(c, c + cw_blk)
        uc = _dot(h2, wup_ref[:, cs])
        uv = _dot(h2, wup_ref[:, D_FF + c:D_FF + c + cw_blk])
        r1 = pltpu.roll(uc, 1, 0)
        r2 = pltpu.roll(uc, 2, 0)
        if groups == 1:
            c6 = carry_scr[6:7, cs]
            c7 = carry_scr[7:8, cs]
            s1 = jnp.concatenate([jnp.where(row8 == 0, c7, r1[0:8]), r1[8:]], axis=0)
            s2 = jnp.concatenate(
                [jnp.where(row8 == 0, c6, jnp.where(row8 == 1, c7, r2[0:8])), r2[8:]], axis=0)
            carry_scr[:, cs] = uc[tm - 8:tm, :]
            tail = uc[tm - (CONV_W - 1):tm, :][None]
        else:
            hist = jnp.broadcast_to(convp_ref[:, :, cs][:, :, None, :],
                                    (groups, CONV_W - 1, glen, cw_blk))
            c6 = hist[:, 0].reshape(tm, cw_blk)
            c7 = hist[:, 1].reshape(tm, cw_blk)
            s1 = jnp.where(grow == 0, c7, r1)
            s2 = jnp.where(grow == 0, c6, jnp.where(grow == 1, c7, r2))
            tail = uc.reshape(groups, glen, cw_blk)[:, glen - (CONV_W - 1):, :]
        conv = cb_ref[:, cs] + s2 * cw_ref[0:1, cs] + s1 * cw_ref[1:2, cs] + uc * cw_ref[2:3, cs]
        yb_scr[:, cs] = (conv * _sigmoid(conv) * uv).astype(BF16)
        convo_ref[:, :, cs] = tail

    ff = [_dot(yb_scr[rs, :], wdn_ref[...]) for rs in subs]
    for rs, a, f in zip(subs, x1, ff):
        y_ref[rs, :] = _layer_norm(ALPHA * a + (1.0 + modr(5, rs)) * f, ln2g_ref[...], ln2b_ref[...])


def _mixffn_call(xn2d, att, rw, gates, mod, conv_prev, prm, steps, tm):
    rows = xn2d.shape[0]
    groups = mod.shape[0]
    lag = 1 if steps > 1 else 0
    row_map = lambda j: (jnp.minimum(j, steps - 1), 0)
    out_map = lambda j: (jnp.maximum(j - lag, 0), 0)
    vec = _const_spec((1, D_MODEL))
    return pl.pallas_call(
        functools.partial(_mixffn_kernel, tm=tm, steps=steps),
        grid=(steps + lag,),
        in_specs=[
            pl.BlockSpec((tm, D_MODEL), row_map),
            pl.BlockSpec((tm, HEADS_DIM), row_map),
            pl.BlockSpec((tm, HEADS_DIM), row_map),
            pl.BlockSpec((tm, GATE_COLS), row_map),
            _const_spec((groups, 6, D_MODEL)),
            _const_spec((groups, CONV_W - 1, D_FF)),
            vec, vec, vec, vec,
            _const_spec((HEADS_DIM, D_MODEL)),
            _const_spec((HEADS_DIM, D_MODEL)),
            _const_spec((D_MODEL, D_MODEL)),
            _const_spec((D_MODEL, 2 * D_FF)),
            _const_spec((CONV_W, D_FF)),
            _const_spec((1, D_FF)),
            _const_spec((D_FF, D_MODEL)),
        ],
        out_specs=[
            pl.BlockSpec((tm, D_MODEL), out_map),
            pl.BlockSpec((groups, CONV_W - 1, D_FF), lambda j: (0, 0, 0)),
        ],
        out_shape=[
            jax.ShapeDtypeStruct((rows, D_MODEL), F32),
            jax.ShapeDtypeStruct((groups, CONV_W - 1, D_FF), F32),
        ],
        scratch_shapes=[pltpu.VMEM((8, D_FF), F32), pltpu.VMEM((tm, D_FF), BF16),
                        pltpu.VMEM((tm, D_MODEL), F32), pltpu.VMEM((tm, D_MODEL), BF16)],
        compiler_params=pltpu.CompilerParams(
            dimension_semantics=("arbitrary",), vmem_limit_bytes=VMEM_LIMIT),
        name="mixffn",
    )(xn2d, att, rw, gates, mod, conv_prev,
      prm["ln1g"], prm["ln1b"], prm["ln2g"], prm["ln2b"],
      prm["wa"], prm["wr"], prm["wo"], prm["wup"], prm["cw"], prm["cb"], prm["wdn"])


def _pair_bias(table):
    assert CHUNK - 1 <= REL_CLIP
    top = ATT_REACH + CHUNK - 1
    n_far = top - REL_CLIP + 1
    far = jnp.broadcast_to(table[:, 2 * REL_CLIP:], (N_HEADS, n_far))
    lo_idx = top - (BAND + CHUNK - 2) + REL_CLIP
    near = table[:, lo_idx:2 * REL_CLIP][:, ::-1]
    ext = jnp.concatenate([far, near], axis=1).astype(F32) * LOG2E
    n_ext = BAND + CHUNK - 1
    period = jnp.concatenate([ext, jnp.zeros((N_HEADS, 1), F32)], axis=1)
    skew = jnp.tile(period, (1, CHUNK))[:, :CHUNK * n_ext].reshape(N_HEADS, CHUNK, n_ext)
    bias = skew[:, :, CHUNK - 1:CHUNK - 1 + BAND]
    return bias.reshape(N_HEADS // 2, 2 * CHUNK, BAND)


def _chunk_ltri(ts):
    t = jnp.arange(ts)
    return ((t[:, None] // CHUNK == t[None, :] // CHUNK) & (t[None, :] <= t[:, None])).astype(BF16)


def _trunk(x2d, mod, shift0, state0, conv_prev, caches, prm, n_seq):
    prompt = caches is None
    rows = x2d.shape[0]
    tm = min(rows, ROW_TILE)
    steps = rows // tm
    q, k, v, k32, v32, p, gates, xn = _inproj_call(
        x2d, mod, prm["lnig"], prm["lnib"], prm["win"], steps, tm, prompt)
    if prompt:
        att = _attn_prompt_call(q, k, v, prm["bias"])
    else:
        att = _attn_sample_call(q, k, v, *caches, prm["bias"])
    assert state0.shape[0] == n_seq
    rw, shift, state = _wkv_call(p, shift0, state0, prm, steps, tm)
    y, conv = _mixffn_call(xn, att, rw, gates, mod, conv_prev, prm, steps, tm)
    return y, (k32, v32), state, shift, conv


def kernel(x_prompt, x_sample, cache_attn_k, cache_attn_v, state_rwkv, state_shift, state_conv,
           c_prompt, c_sample, ln_in_g, ln_in_b, w_ada, b_ada, w_in, attn_rel_bias,
           rwkv_mu, rwkv_w0, rwkv_w_up, rwkv_a0, rwkv_a_up, rwkv_g_up, rwkv_k_k, rwkv_k_a,
           rwkv_r_k, rwkv_gn_g, rwkv_gn_b, w_branch_attn, w_branch_rwkv, w_out,
           ln1_g, ln1_b, ln2_g, ln2_b, w_ffn_up, ffn_conv_w, ffn_conv_b, w_ffn_down):
    bp, sp, _ = x_prompt.shape
    bs, ss, _ = x_sample.shape
    assert bp == 1 and ss == CHUNK and w_ada.shape[0] == DEPTH
    assert sp % ROW_TILE == 0 and bs * ss <= ROW_TILE and cache_attn_k.shape[2] == ATT_REACH

    row = lambda a: a.reshape(1, -1)
    wwa = jnp.zeros((LORA_W + LORA_A, 2 * HEADS_DIM), F32)
    wwa = wwa.at[:LORA_W, :HEADS_DIM].set(rwkv_w_up[0]).at[LORA_W:, HEADS_DIM:].set(rwkv_a_up[0])
    head_id = jnp.arange(GROUP) // HEAD
    prm = dict(
        lnig=row(ln_in_g), lnib=row(ln_in_b),
        ln1g=row(ln1_g[0]), ln1b=row(ln1_b[0]), ln2g=row(ln2_g[0]), ln2b=row(ln2_b[0]),
        win=w_in[0].astype(BF16), bias=_pair_bias(attn_rel_bias[0]),
        mu=row(rwkv_mu[0]), w0=row(rwkv_w0[0]), wwa=wwa.astype(BF16), a0=row(rwkv_a0[0]),
        gup=rwkv_g_up[0].astype(BF16), kk=row(rwkv_k_k[0]), ka=row(rwkv_k_a[0]),
        rk=row(rwkv_r_k[0]), gng=row(rwkv_gn_g[0]), gnb=row(rwkv_gn_b[0]),
        ones=(head_id[:, None] == head_id[None, :]).astype(BF16),
        ltri={WKV_PIECE: _chunk_ltri(WKV_PIECE), CHUNK: _chunk_ltri(CHUNK)},
        wa=w_branch_attn[0].astype(BF16), wr=w_branch_rwkv[0].astype(BF16),
        wo=w_out[0].astype(BF16), wup=w_ffn_up[0].astype(BF16),
        cw=ffn_conv_w[0], cb=row(ffn_conv_b[0]), wdn=w_ffn_down[0].astype(BF16),
    )

    n_c = bp + bs
    c_all = jnp.concatenate([c_prompt, c_sample, jnp.zeros((16 - n_c, D_MODEL), F32)], axis=0)
    mod = _mod_call(c_all, w_ada[0], row(b_ada[0])).reshape(16, 6, D_MODEL)

    y_p, kv_p, st_p, sh_p, cv_p = _trunk(
        x_prompt.reshape(sp, D_MODEL), mod[0:bp],
        jnp.zeros((bp, 1, RW_COLS), F32), jnp.zeros((bp, HEAD, HEADS_DIM), F32),
        jnp.zeros((bp, CONV_W - 1, D_FF), F32), None, prm, n_seq=bp)

    caches = (cache_attn_k[0].astype(BF16).reshape(bs, ATT_REACH, HEADS_DIM),
              cache_attn_v[0].astype(BF16).reshape(bs, ATT_REACH, HEADS_DIM))
    st0 = jnp.transpose(state_rwkv[0], (0, 3, 1, 2)).reshape(bs, HEAD, HEADS_DIM)
    y_s, kv_s, st_s, sh_s, cv_s = _trunk(
        x_sample.reshape(bs * ss, D_MODEL), mod[bp:n_c],
        state_shift[0], st0, state_conv[0], caches, prm, n_seq=bs)

    def state_out(st, b):
        return jnp.transpose(st.reshape(b, HEAD, N_HEADS, HEAD), (0, 2, 3, 1))[None]

    hs = (N_HEADS, HEAD)
    return (
        y_p.reshape(bp, sp, D_MODEL),
        y_s.reshape(bs, ss, D_MODEL),
        kv_p[0].reshape(1, bp, ATT_REACH, *hs),
        kv_p[1].reshape(1, bp, ATT_REACH, *hs),
        kv_s[0].reshape(1, bs, ss, *hs),
        kv_s[1].reshape(1, bs, ss, *hs),
        state_out(st_p, bp),
        state_out(st_s, bs),
        sh_p[None],
        sh_s[None],
        cv_p[None],
        cv_s[None],
    )
```

```python
import functools

import jax
import jax.numpy as jnp
from jax import lax
from jax.experimental import pallas as pl
from jax.experimental.pallas import tpu as pltpu

F32 = jnp.float32
BF16 = jnp.bfloat16

D_MODEL = 1024
CHUNK = 64
ATT_REACH = 512
BAND = ATT_REACH + CHUNK
N_HEADS = 8
HEAD = 64
HEADS_DIM = N_HEADS * HEAD
REL_CLIP = 128
LORA_W = 64
LORA_A = 64
LORA_G = 128
ATT_COLS = 3 * HEADS_DIM
RW_COLS = 3 * HEADS_DIM + LORA_W + LORA_A + LORA_G
GATE_COLS = 2 * D_MODEL
D_FF = 2816
CONV_W = 3
LN_EPS = 1e-5
GN_EPS = 64e-5
DEPTH = 1
ALPHA = (2 * DEPTH) ** 0.25
LOG2E = 1.4426950408889634

GROUP = 256
ROW_TILE = 512
WKV_PIECE = 256
VMEM_LIMIT = 56 * 1024 * 1024


def _const_spec(shape):
    nd = len(shape)
    return pl.BlockSpec(shape, lambda *_: (0,) * nd, pipeline_mode=pl.Buffered(1))


def _layer_norm(x, g, b):
    mu = jnp.mean(x, axis=-1, keepdims=True)
    xc = x - mu
    var = jnp.mean(xc * xc, axis=-1, keepdims=True)
    return xc * lax.rsqrt(var + LN_EPS) * g + b


def _sigmoid(x):
    return 1.0 / (1.0 + jnp.exp(-x))


def _split3(x):
    hi = x.astype(BF16)
    r1 = x - hi.astype(F32)
    mid = r1.astype(BF16)
    lo = (r1 - mid.astype(F32)).astype(BF16)
    return hi, mid, lo


def _dot(a, b):
    return jnp.dot(a, b, preferred_element_type=F32)


def _dot_nt(a, b):
    return lax.dot_general(a, b, (((1,), (1,)), ((), ())), preferred_element_type=F32)


def _dot_tn(a, b):
    return lax.dot_general(a, b, (((0,), (0,)), ((), ())), preferred_element_type=F32)


def _mod_kernel(c_ref, w_ref, b_ref, o_ref):
    c = c_ref[...]
    s = (c * _sigmoid(c)).astype(BF16)
    o_ref[...] = _dot(s, w_ref[...].astype(BF16)) + b_ref[...]


def _mod_call(c_all, w_ada, b_ada):
    n = c_all.shape[0]
    nblk = 6
    return pl.pallas_call(
        _mod_kernel,
        grid=(nblk,),
        in_specs=[
            pl.BlockSpec((n, D_MODEL), lambda i: (0, 0)),
            pl.BlockSpec((D_MODEL, D_MODEL), lambda i: (0, i)),
            pl.BlockSpec((1, D_MODEL), lambda i: (0, i)),
        ],
        out_specs=pl.BlockSpec((n, D_MODEL), lambda i: (0, i)),
        out_shape=jax.ShapeDtypeStruct((n, 6 * D_MODEL), F32),
        compiler_params=pltpu.CompilerParams(dimension_semantics=("arbitrary",)),
        name="mod",
    )(c_all, w_ada, b_ada)


def _mod_row(mod_ref, idx, tm):
    groups, _, d = mod_ref.shape
    m = mod_ref[:, idx:idx + 1, :]
    if groups == 1:
        return m[0]
    return jnp.broadcast_to(m, (groups, tm // groups, d)).reshape(tm, d)


def _inproj_kernel(x_ref, xnext_ref, mod_ref, lng_ref, lnb_ref, w_ref,
                   q_ref, k_ref, v_ref, k32_ref, v32_ref, p_ref, g_ref, xn_ref,
                   xna_scr, hba_scr, *, lead):
    tm = x_ref.shape[0]
    j = pl.program_id(0)

    first, second = slice(0, tm // 2), slice(tm // 2, tm)

    def front(src_ref, rs):
        def mrow(idx):
            m = _mod_row(mod_ref, idx, tm)
            return m if m.shape[0] == 1 else m[rs]

        xn = _layer_norm(src_ref[rs, :], lng_ref[...], lnb_ref[...])
        return xn, (xn * (1.0 + mrow(1)) + mrow(0)).astype(BF16)

    def half_tile(rs, prepared=False):
        if prepared:
            xn, hb = xna_scr[...], hba_scr[...]
        else:
            xn, hb = front(x_ref, rs)
        xn_ref[rs, :] = xn

        def seg(a, b):
            return _dot(hb, w_ref[:, a:b])

        g0 = ATT_COLS + RW_COLS
        for c in range(0, GATE_COLS, 512):
            g_ref[rs, c:c + 512] = _sigmoid(seg(g0 + c, g0 + c + 512))
        q_ref[rs, :] = (seg(0, HEADS_DIM) * (HEAD ** -0.5 * LOG2E)).astype(BF16)
        k = seg(HEADS_DIM, 2 * HEADS_DIM)
        k_ref[rs, :] = k.astype(BF16)
        k32_ref[rs, :] = k
        v = seg(2 * HEADS_DIM, 3 * HEADS_DIM)
        v_ref[rs, :] = v.astype(BF16)
        v32_ref[rs, :] = v
        for c in range(0, RW_COLS, 256):
            p_ref[rs, c:c + 256] = seg(ATT_COLS + c, ATT_COLS + c + 256)

    def prepare_next():
        xna_scr[...], hba_scr[...] = front(xnext_ref, first)

    @pl.when(j < lead)
    def _():
        k_ref[...] = jnp.zeros_like(k_ref)
        v_ref[...] = jnp.zeros_like(v_ref)
        prepare_next()

    @pl.when(j >= lead)
    def _():
        half_tile(first, prepared=lead > 0)
        if lead > 0:
            prepare_next()
        half_tile(second)


def _inproj_call(x2d, mod, ln_g, ln_b, w_in_b, steps, tm, prompt):
    rows = x2d.shape[0]
    in_cols = w_in_b.shape[1]
    groups = mod.shape[0]
    if prompt:
        lead = ATT_REACH // tm
        kv_rows = ATT_REACH
        row_map = lambda j: (jnp.maximum(j - lead, 0), 0)
        next_map = lambda j: (jnp.clip(j + 1 - lead, 0, steps - 1), 0)
        kv_map = lambda j: (jnp.maximum(j - steps, 0), 0)
        ext_map = lambda j: (j, 0)
    else:
        lead = 0
        kv_rows = rows
        row_map = next_map = kv_map = ext_map = lambda j: (j, 0)
    return pl.pallas_call(
        functools.partial(_inproj_kernel, lead=lead),
        grid=(steps + lead,),
        in_specs=[
            pl.BlockSpec((tm, D_MODEL), row_map),
            pl.BlockSpec((tm, D_MODEL), next_map),
            _const_spec((groups, 6, D_MODEL)),
            _const_spec((1, D_MODEL)),
            _const_spec((1, D_MODEL)),
            _const_spec((D_MODEL, in_cols)),
        ],
        scratch_shapes=[pltpu.VMEM((tm // 2, D_MODEL), F32), pltpu.VMEM((tm // 2, D_MODEL), BF16)],
        out_specs=[
            pl.BlockSpec((tm, HEADS_DIM), row_map),
            pl.BlockSpec((tm, HEADS_DIM), ext_map),
            pl.BlockSpec((tm, HEADS_DIM), ext_map),
            pl.BlockSpec((tm, HEADS_DIM), kv_map),
            pl.BlockSpec((tm, HEADS_DIM), kv_map),
            pl.BlockSpec((tm, RW_COLS), row_map),
            pl.BlockSpec((tm, GATE_COLS), row_map),
            pl.BlockSpec((tm, D_MODEL), row_map),
        ],
        out_shape=[
            jax.ShapeDtypeStruct((rows, HEADS_DIM), BF16),
            jax.ShapeDtypeStruct((rows + lead * tm, HEADS_DIM), BF16),
            jax.ShapeDtypeStruct((rows + lead * tm, HEADS_DIM), BF16),
            jax.ShapeDtypeStruct((kv_rows, HEADS_DIM), F32),
            jax.ShapeDtypeStruct((kv_rows, HEADS_DIM), F32),
            jax.ShapeDtypeStruct((rows, RW_COLS), F32),
            jax.ShapeDtypeStruct((rows, GATE_COLS), F32),
            jax.ShapeDtypeStruct((rows, D_MODEL), F32),
        ],
        compiler_params=pltpu.CompilerParams(
            dimension_semantics=("arbitrary",), vmem_limit_bytes=VMEM_LIMIT),
        name="inproj",
    )(x2d, x2d, mod, ln_g, ln_b, w_in_b)


def _attn_chunks(chunks, bias_ref):
    lane = lax.broadcasted_iota(jnp.int32, (CHUNK, 128), 1)
    first = lane < HEAD
    pairs = [slice(pr * 128, (pr + 1) * 128) for pr in range(N_HEADS // 2)]
    scores = []
    for qc, kb, _, _ in chunks:
        for sl in pairs:
            q2 = qc[:, sl].astype(F32)
            qs = jnp.concatenate([jnp.where(first, q2, 0.0), jnp.where(first, 0.0, q2)],
                                 axis=0).astype(BF16)
            scores.append(_dot_nt(qs, kb(sl)))
    probs, sums = [], []
    for i, s in enumerate(scores):
        thr = chunks[i // len(pairs)][3]
        s = s + bias_ref[i % len(pairs)]
        if thr is not None:
            col = lax.broadcasted_iota(jnp.int32, s.shape, 1)
            s = jnp.where(col >= thr, s, -jnp.inf)
        e = jnp.exp2(s - jnp.max(s, axis=1, keepdims=True))
        sums.append(jnp.sum(e, axis=1, keepdims=True))
        probs.append(e.astype(BF16))
    outs = []
    for ci, (_, _, vb, _) in enumerate(chunks):
        cols = []
        for pi, sl in enumerate(pairs):
            i = ci * len(pairs) + pi
            o = _dot(probs[i], vb(sl)) / sums[i]
            cols.append(jnp.where(first, o[0:CHUNK], o[CHUNK:2 * CHUNK]))
        outs.append(jnp.concatenate(cols, axis=1).astype(BF16))
    return outs


def _attn_prompt_kernel(q_ref, kbuf, vbuf, bias_ref, o_ref, *, chunks):
    s = pl.program_id(0)

    per = 2

    def run(masked):
        for i in range(chunks // per):
            units, starts = [], []
            for k in range(per):
                g = i * per + k
                r0 = g * CHUNK
                thr = ATT_REACH - (s * chunks + g) * CHUNK if masked else None
                band = slice(r0, r0 + BAND)
                units.append((q_ref[r0:r0 + CHUNK, :],
                              lambda sl, band=band: kbuf[band, sl],
                              lambda sl, band=band: vbuf[band, sl], thr))
                starts.append(r0)
            for r0, o in zip(starts, _attn_chunks(units, bias_ref)):
                o_ref[r0:r0 + CHUNK, :] = o

    assert chunks * CHUNK >= ATT_REACH
    pl.when(s == 0)(functools.partial(run, True))
    pl.when(s != 0)(functools.partial(run, False))


def _attn_prompt_call(q, kext, vext, bias):
    rows = q.shape[0]
    tq = ATT_REACH
    blk = pl.BlockSpec((tq, HEADS_DIM), lambda s: (s, 0))
    window = pl.BlockSpec((pl.Element(tq + ATT_REACH), pl.Element(HEADS_DIM)),
                          lambda s: (s * tq, 0))
    return pl.pallas_call(
        functools.partial(_attn_prompt_kernel, chunks=tq // CHUNK),
        grid=(rows // tq,),
        in_specs=[blk, window, window, _const_spec(bias.shape)],
        out_specs=blk,
        out_shape=jax.ShapeDtypeStruct((rows, HEADS_DIM), BF16),
        compiler_params=pltpu.CompilerParams(dimension_semantics=("arbitrary",)),
        name="attn_prompt",
    )(q, kext, vext, bias)


def _attn_sample_kernel(q_ref, k_ref, v_ref, ck_ref, cv_ref, bias_ref, o_ref):
    def band(cache_ref, new_ref, rs):
        return lambda sl: jnp.concatenate([cache_ref[:, sl], new_ref[rs, sl]], axis=0)

    units = []
    for s in range(ck_ref.shape[0]):
        rs = slice(s * CHUNK, (s + 1) * CHUNK)
        units.append((q_ref[rs, :], band(ck_ref.at[s], k_ref, rs), band(cv_ref.at[s], v_ref, rs),
                      None))
    for s, o in enumerate(_attn_chunks(units, bias_ref)):
        o_ref[s * CHUNK:(s + 1) * CHUNK, :] = o


def _attn_sample_call(q, k, v, cache_k, cache_v, bias):
    per = 2
    nb = cache_k.shape[0] // per
    new = pl.BlockSpec((per * CHUNK, HEADS_DIM), lambda b: (b, 0))
    cache = pl.BlockSpec((per, ATT_REACH, HEADS_DIM), lambda b: (b, 0, 0))
    return pl.pallas_call(
        _attn_sample_kernel,
        grid=(nb,),
        in_specs=[new, new, new, cache, cache, _const_spec(bias.shape)],
        out_specs=new,
        out_shape=jax.ShapeDtypeStruct(q.shape, BF16),
        compiler_params=pltpu.CompilerParams(dimension_semantics=("arbitrary",)),
        name="attn_sample",
    )(q, k, v, cache_k, cache_v, bias)


def _same_head():
    r = lax.broadcasted_iota(jnp.int32, (GROUP, GROUP), 0) // HEAD
    c = lax.broadcasted_iota(jnp.int32, (GROUP, GROUP), 1) // HEAD
    return r == c


def _blk(x, same_head):
    return jnp.where(same_head, jnp.concatenate([x] * 4, axis=0), 0.0).astype(BF16)


def _wkv_prep(units, hooks):
    same_head = _same_head()
    t64 = lax.broadcasted_iota(jnp.int32, (CHUNK, GROUP), 0)
    i64 = lax.broadcasted_iota(jnp.int32, (CHUNK, GROUP), 1) % HEAD
    strict = i64 < t64
    incl = i64 <= t64
    pending = list(hooks)

    def stage_done():
        if pending:
            pending.pop(0)()

    def blk(x):
        return _blk(x, same_head)

    n, a_ak, a_rb, a_rk = [], [], [], []
    for ld in units:
        lhs = jnp.concatenate([ld("at"), ld("rt")], axis=0).astype(BF16)

        def blk_t(x):
            xt = jnp.transpose(x)
            return jnp.where(same_head, jnp.concatenate([xt] * 4, axis=1), 0.0).astype(BF16)

        rhs = jnp.concatenate([blk_t(ld("bt")), blk_t(ld("kt"))], axis=1)
        a_all = _dot(lhs, rhs)
        n.append(jnp.where(strict, a_all[0:CHUNK, 0:GROUP], 0.0))
        a_ak.append(jnp.where(strict, a_all[0:CHUNK, GROUP:], 0.0))
        a_rb.append(jnp.where(incl, a_all[CHUNK:, 0:GROUP], 0.0))
        a_rk.append(jnp.where(incl, a_all[CHUNK:, GROUP:], 0.0))
    stage_done()

    with_v = [_dot(jnp.concatenate([a, b], axis=0).astype(BF16), blk(ld("v")))
              for a, b, ld in zip(a_ak, a_rk, units)]
    x0 = [p[0:CHUNK] for p in with_v]
    y_v = [p[CHUNK:] for p in with_v]
    npow = [_dot(m.astype(BF16), blk(m)) for m in n]
    t = [jnp.where(i64 == t64, 1.0, 0.0) + m for m in n]
    stage_done()
    for _ in range(4):
        prod = [_dot(jnp.concatenate([a, b], axis=0).astype(BF16), blk(b))
                for a, b in zip(t, npow)]
        t = [a + p[0:CHUNK] for a, p in zip(t, prod)]
        npow = [p[CHUNK:] for p in prod]
        stage_done()
    t = [a + _dot(a.astype(BF16), blk(b)) for a, b in zip(t, npow)]
    stage_done()
    res = [_dot(a.astype(BF16), jnp.concatenate([blk(ld("at")), blk(x)], axis=1))
           for a, x, ld in zip(t, x0, units)]
    stage_done()
    while pending:
        stage_done()
    return [(r[:, 0:GROUP], r[:, GROUP:], b, yv) for r, b, yv in zip(res, a_rb, y_v)]


def _wkv_step_stages(units, states, out):
    same_head = _same_head()
    held = {}

    def blk(x):
        return _blk(x, same_head)

    def first():
        both = [_dot(jnp.concatenate([ld("ah"), ld("rt")], axis=0).astype(BF16), blk(st))
                for ld, st in zip(units, states)]
        held["u"] = [b[0:CHUNK] + ld("vh") for b, ld in zip(both, units)]
        held["rs"] = [b[CHUNK:] for b in both]

    def second():
        for i, ld in enumerate(units):
            u, v = held["u"][i], ld("v")
            y = held["rs"][i] + ld("yv") + _dot(ld("arb").astype(BF16), blk(u))
            g = _dot_tn(jnp.concatenate([ld("btd"), ld("ktd")], axis=0).astype(BF16),
                        jnp.concatenate([u, v], axis=0).astype(BF16))
            g = jnp.where(same_head, g, 0.0)
            delta = g[0:64] + g[64:128] + g[128:192] + g[192:256]
            out.append(y)
            states[i] = states[i] * ld("fdec") + delta

    return [first, second]


def _wkv_kernel(p_ref, pn_ref, shift0_ref, state0_ref, mu_ref, w0_ref, wwa_ref, a0_ref, gup_ref,
                kkw_ref, kaw_ref, rkw_ref, gng_ref, gnb_ref, ones_ref, ltri_ref,
                out_ref, shift_out_ref, state_out_ref,
                st_scr, rt_scr, at_scr, kt_scr, bt_scr, v_scr, btd_scr, ktd_scr,
                fdec_scr, ah_scr, vh_scr, arb_scr, yv_scr, g_scr, bon_scr, *, ts, steps):
    j = pl.program_id(0)
    streams = state0_ref.shape[0]
    assert streams == 1 or (steps == 1 and ts == streams * CHUNK)

    if streams == 1:
        @pl.when(j == 0)
        def _():
            st_scr[...] = state0_ref[0]

    def bdsum(x, terms):
        ones = ones_ref[...]
        halves = []
        for c0 in range(0, HEADS_DIM, GROUP):
            parts = _split3(x[:, c0:c0 + GROUP])[:terms]
            acc = _dot(parts[0], ones)
            for part in parts[1:]:
                acc = acc + _dot(part, ones)
            halves.append(acc)
        return jnp.concatenate(halves, axis=1)

    pq = ltri_ref.shape[0]
    per = pq // CHUNK
    n_piece = ts // pq

    def pre_stages(q, upcoming=False):
        r0 = q * pq
        rows = slice(r0, r0 + pq)
        src_ref = pn_ref if upcoming else p_ref
        held = {}

        def shifted(c0, w):
            cols = slice(c0, c0 + w)
            row = lax.broadcasted_iota(jnp.int32, (pq, 1), 0)
            if streams > 1:
                first = shift0_ref[q * per:(q + 1) * per, :, cols]
                prev_row = jnp.broadcast_to(first, (per, CHUNK, w)).reshape(pq, w)
                top = row % CHUNK == 0
            else:
                if upcoming:
                    prev_row = p_ref[ts - 8:ts, cols][7:8]
                elif q == 0:
                    prev_row = shift0_ref[0][:, cols]
                else:
                    prev_row = p_ref[r0 - 8:r0, cols][7:8]
                top = row == 0
            pb = src_ref[rows, cols]
            prev = jnp.where(top, prev_row, pltpu.roll(pb, 1, 0))
            return pb + (prev - pb) * mu_ref[:, cols]

        def s_lora():
            lora = shifted(3 * HEADS_DIM, 256)
            lwla = lora[:, 0:128]
            lane = lax.broadcasted_iota(jnp.int32, (pq, 128), 1)
            held["raw"] = _dot(jnp.where(lane < LORA_W, jnp.tanh(lwla), lwla).astype(BF16),
                               wwa_ref[...])
            g_scr[rows, :] = _dot(_sigmoid(lora[:, 128:256]).astype(BF16), gup_ref[...])
            held["k"] = shifted(HEADS_DIM, HEADS_DIM)

        def s_decay():
            w_pre = w0_ref[...] + held["raw"][:, 0:HEADS_DIM]
            softplus = jnp.maximum(-w_pre, 0.0) + jnp.log(1.0 + jnp.exp(-jnp.abs(w_pre)))
            lw = -jnp.exp(-softplus - 0.5)
            ltri = ltri_ref[...]
            hi, mid, _ = _split3(lw)
            held["cum"] = _dot(ltri, hi) + _dot(ltri, mid)
            held["lw"] = lw

        def s_keys():
            k = held["k"]
            a = _sigmoid(a0_ref[...] + held["raw"][:, HEADS_DIM:])
            kk = k * kkw_ref[...]
            held["nrm"] = bdsum(kk * kk, 1)
            held.update(a=a, kk=kk, k2=k * (1.0 + (a - 1.0) * kaw_ref[...]))

        def s_bonus():
            r = shifted(0, HEADS_DIM)
            held["bon"] = bdsum(r * held["k2"] * rkw_ref[...], 1)
            rt_scr[rows, :] = r * jnp.exp(held["cum"])

        def s_norm():
            cum = held["cum"]
            kk = held["kk"] / jnp.maximum(jnp.sqrt(held["nrm"]), 1e-12)
            cum3 = cum.reshape(per, CHUNK, HEADS_DIM)
            cend = jnp.broadcast_to(cum3[:, CHUNK - 1:CHUNK, :], cum3.shape).reshape(pq, HEADS_DIM)
            diag = (lax.broadcasted_iota(jnp.int32, (pq, HEADS_DIM), 0) % CHUNK
                    == lax.broadcasted_iota(jnp.int32, (pq, HEADS_DIM), 1) % HEAD)
            held["fdec"] = bdsum(jnp.where(diag, jnp.exp(cend), 0.0), 2)
            at_scr[rows, :] = -kk * jnp.exp(cum - held["lw"])
            held.update(kk=kk, cend=cend)

        def s_inv():
            e_inv = jnp.exp(-held["cum"])
            kt_scr[rows, :] = held["k2"] * e_inv
            bt_scr[rows, :] = held["kk"] * held["a"] * e_inv

        def s_rel():
            e_rel = jnp.exp(held["cend"] - held["cum"])
            ktd_scr[rows, :] = held["k2"] * e_rel
            btd_scr[rows, :] = held["kk"] * held["a"] * e_rel
            fdec_scr[rows, :] = held["fdec"]

        def s_value():
            v = shifted(2 * HEADS_DIM, HEADS_DIM)
            v_scr[rows, :] = v
            bon_scr[rows, :] = held["bon"] * v

        return [s_lora, s_decay, s_keys, s_bonus, s_norm, s_inv, s_rel, s_value]

    def chunk_rows(c):
        return slice(c * CHUNK, (c + 1) * CHUNK)

    groups = [slice(g0, g0 + GROUP) for g0 in range(0, HEADS_DIM, GROUP)]

    named = dict(at=at_scr, rt=rt_scr, kt=kt_scr, bt=bt_scr, v=v_scr, btd=btd_scr, ktd=ktd_scr,
                 fdec=fdec_scr, ah=ah_scr, vh=vh_scr, arb=arb_scr, yv=yv_scr)
    prep_outs = (ah_scr, vh_scr, arb_scr, yv_scr)

    def loader(c, cols):
        rows = chunk_rows(c)
        return lambda name: named[name][rows, cols]

    def run(step_piece, prep_piece, extra):
        step_chunks = [] if step_piece is None else [step_piece * per + i for i in range(per)]
        prep_chunks = [] if prep_piece is None else [prep_piece * per + i for i in range(per)]
        stages, ys, finals = [], [], []
        if streams == 1:
            states = [st_scr[:, cols] for cols in groups] if step_chunks else []
            for c in step_chunks:
                out = []
                ys.append(out)
                stages += _wkv_step_stages([loader(c, cols) for cols in groups], states, out)
            finals = [(st_scr, states)] if step_chunks else []
        else:
            firsts, seconds = [], []
            for c in step_chunks:
                out = []
                ys.append(out)
                states = [state0_ref[c, :, cols] for cols in groups]
                first, second = _wkv_step_stages(
                    [loader(c, cols) for cols in groups], states, out)
                firsts.append(first)
                seconds.append(second)
                finals.append((state_out_ref.at[c], states))
            if step_chunks:
                stages += [lambda: [f() for f in firsts], lambda: [f() for f in seconds]]
        if step_chunks:
            rows = slice(step_piece * pq, (step_piece + 1) * pq)
            gn = {}

            def gn_mean():
                gn["y"] = jnp.concatenate([jnp.concatenate(out, axis=1) for out in ys], axis=0)
                gn["sum"] = bdsum(gn["y"], 1)

            def gn_var():
                gn["d"] = gn["y"] - gn["sum"] * (1.0 / HEAD)
                gn["sq"] = bdsum(gn["d"] * gn["d"], 1)

            def gn_out():
                var = gn["sq"] * (1.0 / HEAD)
                yn = (gn["d"] * lax.rsqrt(var + GN_EPS) * gng_ref[...] + gnb_ref[...]
                      + bon_scr[rows, :])
                out_ref[rows, :] = (yn * g_scr[rows, :]).astype(BF16)

            stages += [gn_mean, gn_var, gn_out]
        hooks = []
        for i in range(max(len(stages), len(extra))):
            both = stages[i:i + 1] + extra[i:i + 1]
            hooks.append(lambda both=both: [f() for f in both])
        units = [(c, cols) for c in prep_chunks for cols in groups]
        prep_out = _wkv_prep([loader(c, cols) for c, cols in units], hooks)
        for ref, states in finals:
            for cols, st in zip(groups, states):
                ref[:, cols] = st
        for (c, cols), outs in zip(units, prep_out):
            for ref, val in zip(prep_outs, outs):
                ref[chunk_rows(c), cols] = val

    look_ahead = steps > 1
    ahead_pass = max(2, n_piece - 1)
    assert ahead_pass <= n_piece or not look_ahead

    def first_piece():
        for stage in pre_stages(0):
            stage()

    if look_ahead:
        pl.when(j == 0)(first_piece)
    else:
        first_piece()
    for piece in range(n_piece + 1):
        if piece + 1 < n_piece:
            extra = pre_stages(piece + 1)
        elif piece == ahead_pass and look_ahead:
            extra = pre_stages(0, upcoming=True)
        else:
            extra = []
        run(piece - 1 if piece >= 1 else None, piece if piece < n_piece else None, extra)

    if streams == 1:
        @pl.when(j == steps - 1)
        def _():
            shift_out_ref[0] = p_ref[ts - 1:ts, :]
            state_out_ref[0] = st_scr[...]
    else:
        for s in range(streams):
            shift_out_ref[s] = p_ref[(s + 1) * CHUNK - 1:(s + 1) * CHUNK, :]


def _wkv_call(p2d, shift0, state0, prm, steps, ts):
    rows = p2d.shape[0]
    streams = state0.shape[0]
    row_map = lambda j: (j, 0)
    next_map = lambda j: (jnp.minimum(j + 1, steps - 1), 0)
    seq3 = lambda j: (0, 0, 0)
    vec = _const_spec((1, HEADS_DIM))
    big = pltpu.VMEM((ts, HEADS_DIM), F32)
    ltri = prm["ltri"][min(ts, WKV_PIECE)]
    return pl.pallas_call(
        functools.partial(_wkv_kernel, ts=ts, steps=steps),
        grid=(steps,),
        in_specs=[
            pl.BlockSpec((ts, RW_COLS), row_map),
            pl.BlockSpec((ts, RW_COLS), next_map),
            pl.BlockSpec((streams, 1, RW_COLS), seq3),
            pl.BlockSpec((streams, HEAD, HEADS_DIM), seq3),
            _const_spec((1, RW_COLS)),
            vec,
            _const_spec((128, 2 * HEADS_DIM)),
            vec,
            _const_spec((LORA_G, HEADS_DIM)),
            vec, vec, vec, vec, vec,
            _const_spec((GROUP, GROUP)),
            _const_spec(ltri.shape),
        ],
        out_specs=[
            pl.BlockSpec((ts, HEADS_DIM), row_map),
            pl.BlockSpec((streams, 1, RW_COLS), seq3),
            pl.BlockSpec((streams, HEAD, HEADS_DIM), seq3),
        ],
        out_shape=[
            jax.ShapeDtypeStruct((rows, HEADS_DIM), BF16),
            jax.ShapeDtypeStruct((streams, 1, RW_COLS), F32),
            jax.ShapeDtypeStruct((streams, HEAD, HEADS_DIM), F32),
        ],
        scratch_shapes=[
            pltpu.VMEM((HEAD, HEADS_DIM), F32),
        ] + [big] * 14,
        compiler_params=pltpu.CompilerParams(
            dimension_semantics=("arbitrary",), vmem_limit_bytes=VMEM_LIMIT),
        name="wkv",
    )(p2d, p2d, shift0, state0, prm["mu"], prm["w0"], prm["wwa"], prm["a0"], prm["gup"],
      prm["kk"], prm["ka"], prm["rk"], prm["gng"], prm["gnb"], prm["ones"], ltri)


def _mixffn_kernel(xn_ref, att_ref, rw_ref, gate_ref, mod_ref, convp_ref,
                   ln1g_ref, ln1b_ref, ln2g_ref, ln2b_ref,
                   wa_ref, wr_ref, wo_ref, wup_ref, cw_ref, cb_ref, wdn_ref,
                   y_ref, convo_ref, carry_scr, yb_scr, x1_scr, h2_scr, *, tm, steps):
    j = pl.program_id(0)
    groups = mod_ref.shape[0]
    assert groups == 1 or steps == 1
    pipelined = steps > 1

    def mod(idx):
        return _mod_row(mod_ref, idx, tm)

    half = tm // 2
    subs = [slice(0, half), slice(half, tm)]

    def modr(idx, rs):
        m = mod(idx)
        return m if m.shape[0] == 1 else m[rs]

    def front():
        ma = [_dot(att_ref[rs, :], wa_ref[...]) for rs in subs]
        mr = [_dot(rw_ref[rs, :], wr_ref[...]) for rs in subs]
        merged = [(gate_ref[rs, 0:D_MODEL] * a + gate_ref[rs, D_MODEL:] * r).astype(BF16)
                  for rs, a, r in zip(subs, ma, mr)]
        mix = [_dot(m, wo_ref[...]) for m in merged]
        x1 = [_layer_norm(ALPHA * xn_ref[rs, :] + (1.0 + modr(2, rs)) * m,
                          ln1g_ref[...], ln1b_ref[...]) for rs, m in zip(subs, mix)]
        h2 = jnp.concatenate([(a * (1.0 + modr(4, rs)) + modr(3, rs)).astype(BF16)
                              for rs, a in zip(subs, x1)], axis=0)
        return x1, h2

    def keep(x1, h2):
        for rs, a in zip(subs, x1):
            x1_scr[rs, :] = a
        h2_scr[...] = h2

    if not pipelined:
        if groups == 1:
            carry_scr[8 - (CONV_W - 1):8, :] = convp_ref[0]
        _ffn_back(*front(), subs, modr, groups, tm, convp_ref, ln2g_ref, ln2b_ref, wup_ref, cw_ref,
                  cb_ref, wdn_ref, y_ref, convo_ref, carry_scr, yb_scr)
        return

    @pl.when(j == 0)
    def _():
        carry_scr[8 - (CONV_W - 1):8, :] = convp_ref[0]
        keep(*front())

    @pl.when(j > 0)
    def _():
        x1 = [x1_scr[rs, :] for rs in subs]
        h2 = h2_scr[...]
        keep(*front())
        _ffn_back(x1, h2, subs, modr, groups, tm, convp_ref, ln2g_ref, ln2b_ref, wup_ref, cw_ref,
                  cb_ref, wdn_ref, y_ref, convo_ref, carry_scr, yb_scr)


def _ffn_back(x1, h2, subs, modr, groups, tm, convp_ref, ln2g_ref, ln2b_ref, wup_ref, cw_ref,
              cb_ref, wdn_ref, y_ref, convo_ref, carry_scr, yb_scr):
    cw_blk = 256
    glen = tm // groups
    row8 = lax.broadcasted_iota(jnp.int32, (8, cw_blk), 0)
    grow = lax.broadcasted_iota(jnp.int32, (tm, cw_blk), 0) % glen
    for c in range(0, D_FF, cw_blk):
        cs = slice(c, c + cw_blk)
        uc = _dot(h2, wup_ref[:, cs])
        uv = _dot(h2, wup_ref[:, D_FF + c:D_FF + c + cw_blk])
        r1 = pltpu.roll(uc, 1, 0)
        r2 = pltpu.roll(uc, 2, 0)
        if groups == 1:
            c6 = carry_scr[6:7, cs]
            c7 = carry_scr[7:8, cs]
            s1 = jnp.concatenate([jnp.where(row8 == 0, c7, r1[0:8]), r1[8:]], axis=0)
            s2 = jnp.concatenate(
                [jnp.where(row8 == 0, c6, jnp.where(row8 == 1, c7, r2[0:8])), r2[8:]], axis=0)
            carry_scr[:, cs] = uc[tm - 8:tm, :]
            tail = uc[tm - (CONV_W - 1):tm, :][None]
        else:
            hist = jnp.broadcast_to(convp_ref[:, :, cs][:, :, None, :],
                                    (groups, CONV_W - 1, glen, cw_blk))
            c6 = hist[:, 0].reshape(tm, cw_blk)
            c7 = hist[:, 1].reshape(tm, cw_blk)
            s1 = jnp.where(grow == 0, c7, r1)
            s2 = jnp.where(grow == 0, c6, jnp.where(grow == 1, c7, r2))
            tail = uc.reshape(groups, glen, cw_blk)[:, glen - (CONV_W - 1):, :]
        conv = cb_ref[:, cs] + s2 * cw_ref[0:1, cs] + s1 * cw_ref[1:2, cs] + uc * cw_ref[2:3, cs]
        yb_scr[:, cs] = (conv * _sigmoid(conv) * uv).astype(BF16)
        convo_ref[:, :, cs] = tail

    ff = [_dot(yb_scr[rs, :], wdn_ref[...]) for rs in subs]
    for rs, a, f in zip(subs, x1, ff):
        y_ref[rs, :] = _layer_norm(ALPHA * a + (1.0 + modr(5, rs)) * f, ln2g_ref[...], ln2b_ref[...])


def _mixffn_call(xn2d, att, rw, gates, mod, conv_prev, prm, steps, tm):
    rows = xn2d.shape[0]
    groups = mod.shape[0]
    lag = 1 if steps > 1 else 0
    row_map = lambda j: (jnp.minimum(j, steps - 1), 0)
    out_map = lambda j: (jnp.maximum(j - lag, 0), 0)
    vec = _const_spec((1, D_MODEL))
    return pl.pallas_call(
        functools.partial(_mixffn_kernel, tm=tm, steps=steps),
        grid=(steps + lag,),
        in_specs=[
            pl.BlockSpec((tm, D_MODEL), row_map),
            pl.BlockSpec((tm, HEADS_DIM), row_map),
            pl.BlockSpec((tm, HEADS_DIM), row_map),
            pl.BlockSpec((tm, GATE_COLS), row_map),
            _const_spec((groups, 6, D_MODEL)),
            _const_spec((groups, CONV_W - 1, D_FF)),
            vec, vec, vec, vec,
            _const_spec((HEADS_DIM, D_MODEL)),
            _const_spec((HEADS_DIM, D_MODEL)),
            _const_spec((D_MODEL, D_MODEL)),
            _const_spec((D_MODEL, 2 * D_FF)),
            _const_spec((CONV_W, D_FF)),
            _const_spec((1, D_FF)),
            _const_spec((D_FF, D_MODEL)),
        ],
        out_specs=[
            pl.BlockSpec((tm, D_MODEL), out_map),
            pl.BlockSpec((groups, CONV_W - 1, D_FF), lambda j: (0, 0, 0)),
        ],
        out_shape=[
            jax.ShapeDtypeStruct((rows, D_MODEL), F32),
            jax.ShapeDtypeStruct((groups, CONV_W - 1, D_FF), F32),
        ],
        scratch_shapes=[pltpu.VMEM((8, D_FF), F32), pltpu.VMEM((tm, D_FF), BF16),
                        pltpu.VMEM((tm, D_MODEL), F32), pltpu.VMEM((tm, D_MODEL), BF16)],
        compiler_params=pltpu.CompilerParams(
            dimension_semantics=("arbitrary",), vmem_limit_bytes=VMEM_LIMIT),
        name="mixffn",
    )(xn2d, att, rw, gates, mod, conv_prev,
      prm["ln1g"], prm["ln1b"], prm["ln2g"], prm["ln2b"],
      prm["wa"], prm["wr"], prm["wo"], prm["wup"], prm["cw"], prm["cb"], prm["wdn"])


def _pair_bias(table):
    assert CHUNK - 1 <= REL_CLIP
    top = ATT_REACH + CHUNK - 1
    n_far = top - REL_CLIP + 1
    far = jnp.broadcast_to(table[:, 2 * REL_CLIP:], (N_HEADS, n_far))
    lo_idx = top - (BAND + CHUNK - 2) + REL_CLIP
    near = table[:, lo_idx:2 * REL_CLIP][:, ::-1]
    ext = jnp.concatenate([far, near], axis=1).astype(F32) * LOG2E
    n_ext = BAND + CHUNK - 1
    period = jnp.concatenate([ext, jnp.zeros((N_HEADS, 1), F32)], axis=1)
    skew = jnp.tile(period, (1, CHUNK))[:, :CHUNK * n_ext].reshape(N_HEADS, CHUNK, n_ext)
    bias = skew[:, :, CHUNK - 1:CHUNK - 1 + BAND]
    return bias.reshape(N_HEADS // 2, 2 * CHUNK, BAND)


def _chunk_ltri(ts):
    t = jnp.arange(ts)
    return ((t[:, None] // CHUNK == t[None, :] // CHUNK) & (t[None, :] <= t[:, None])).astype(BF16)


def _trunk(x2d, mod, shift0, state0, conv_prev, caches, prm, n_seq):
    prompt = caches is None
    rows = x2d.shape[0]
    tm = min(rows, ROW_TILE)
    steps = rows // tm
    q, k, v, k32, v32, p, gates, xn = _inproj_call(
        x2d, mod, prm["lnig"], prm["lnib"], prm["win"], steps, tm, prompt)
    if prompt:
        att = _attn_prompt_call(q, k, v, prm["bias"])
    else:
        att = _attn_sample_call(q, k, v, *caches, prm["bias"])
    assert state0.shape[0] == n_seq
    rw, shift, state = _wkv_call(p, shift0, state0, prm, steps, tm)
    y, conv = _mixffn_call(xn, att, rw, gates, mod, conv_prev, prm, steps, tm)
    return y, (k32, v32), state, shift, conv


def kernel(x_prompt, x_sample, cache_attn_k, cache_attn_v, state_rwkv, state_shift, state_conv,
           c_prompt, c_sample, ln_in_g, ln_in_b, w_ada, b_ada, w_in, attn_rel_bias,
           rwkv_mu, rwkv_w0, rwkv_w_up, rwkv_a0, rwkv_a_up, rwkv_g_up, rwkv_k_k, rwkv_k_a,
           rwkv_r_k, rwkv_gn_g, rwkv_gn_b, w_branch_attn, w_branch_rwkv, w_out,
           ln1_g, ln1_b, ln2_g, ln2_b, w_ffn_up, ffn_conv_w, ffn_conv_b, w_ffn_down):
    bp, sp, _ = x_prompt.shape
    bs, ss, _ = x_sample.shape
    assert bp == 1 and ss == CHUNK and w_ada.shape[0] == DEPTH
    assert sp % ROW_TILE == 0 and bs * ss <= ROW_TILE and cache_attn_k.shape[2] == ATT_REACH

    row = lambda a: a.reshape(1, -1)
    wwa = jnp.zeros((LORA_W + LORA_A, 2 * HEADS_DIM), F32)
    wwa = wwa.at[:LORA_W, :HEADS_DIM].set(rwkv_w_up[0]).at[LORA_W:, HEADS_DIM:].set(rwkv_a_up[0])
    head_id = jnp.arange(GROUP) // HEAD
    prm = dict(
        lnig=row(ln_in_g), lnib=row(ln_in_b),
        ln1g=row(ln1_g[0]), ln1b=row(ln1_b[0]), ln2g=row(ln2_g[0]), ln2b=row(ln2_b[0]),
        win=w_in[0].astype(BF16), bias=_pair_bias(attn_rel_bias[0]),
        mu=row(rwkv_mu[0]), w0=row(rwkv_w0[0]), wwa=wwa.astype(BF16), a0=row(rwkv_a0[0]),
        gup=rwkv_g_up[0].astype(BF16), kk=row(rwkv_k_k[0]), ka=row(rwkv_k_a[0]),
        rk=row(rwkv_r_k[0]), gng=row(rwkv_gn_g[0]), gnb=row(rwkv_gn_b[0]),
        ones=(head_id[:, None] == head_id[None, :]).astype(BF16),
        ltri={WKV_PIECE: _chunk_ltri(WKV_PIECE), CHUNK: _chunk_ltri(CHUNK)},
        wa=w_branch_attn[0].astype(BF16), wr=w_branch_rwkv[0].astype(BF16),
        wo=w_out[0].astype(BF16), wup=w_ffn_up[0].astype(BF16),
        cw=ffn_conv_w[0], cb=row(ffn_conv_b[0]), wdn=w_ffn_down[0].astype(BF16),
    )

    n_c = bp + bs
    c_all = jnp.concatenate([c_prompt, c_sample, jnp.zeros((16 - n_c, D_MODEL), F32)], axis=0)
    mod = _mod_call(c_all, w_ada[0], row(b_ada[0])).reshape(16, 6, D_MODEL)

    y_p, kv_p, st_p, sh_p, cv_p = _trunk(
        x_prompt.reshape(sp, D_MODEL), mod[0:bp],
        jnp.zeros((bp, 1, RW_COLS), F32), jnp.zeros((bp, HEAD, HEADS_DIM), F32),
        jnp.zeros((bp, CONV_W - 1, D_FF), F32), None, prm, n_seq=bp)

    caches = (cache_attn_k[0].astype(BF16).reshape(bs, ATT_REACH, HEADS_DIM),
              cache_attn_v[0].astype(BF16).reshape(bs, ATT_REACH, HEADS_DIM))
    st0 = jnp.transpose(state_rwkv[0], (0, 3, 1, 2)).reshape(bs, HEAD, HEADS_DIM)
    y_s, kv_s, st_s, sh_s, cv_s = _trunk(
        x_sample.reshape(bs * ss, D_MODEL), mod[bp:n_c],
        state_shift[0], st0, state_conv[0], caches, prm, n_seq=bs)

    def state_out(st, b):
        return jnp.transpose(st.reshape(b, HEAD, N_HEADS, HEAD), (0, 2, 3, 1))[None]

    hs = (N_HEADS, HEAD)
    return (
        y_p.reshape(bp, sp, D_MODEL),
        y_s.reshape(bs, ss, D_MODEL),
        kv_p[0].reshape(1, bp, ATT_REACH, *hs),
        kv_p[1].reshape(1, bp, ATT_REACH, *hs),
        kv_s[0].reshape(1, bs, ss, *hs),
        kv_s[1].reshape(1, bs, ss, *hs),
        state_out(st_p, bp),
        state_out(st_s, bs),
        sh_p[None],
        sh_s[None],
        cv_p[None],
        cv_s[None],
    )
```

```python
import functools

import jax
import jax.numpy as jnp
from jax import lax
from jax.experimental import pallas as pl
from jax.experimental.pallas import tpu as pltpu

F32 = jnp.float32
BF16 = jnp.bfloat16

D_MODEL = 1024
CHUNK = 64
ATT_REACH = 512
BAND = ATT_REACH + CHUNK
N_HEADS = 8
HEAD = 64
HEADS_DIM = N_HEADS * HEAD
REL_CLIP = 128
LORA_W = 64
LORA_A = 64
LORA_G = 128
ATT_COLS = 3 * HEADS_DIM
RW_COLS = 3 * HEADS_DIM + LORA_W + LORA_A + LORA_G
GATE_COLS = 2 * D_MODEL
D_FF = 2816
CONV_W = 3
LN_EPS = 1e-5
GN_EPS = 64e-5
DEPTH = 1
ALPHA = (2 * DEPTH) ** 0.25
LOG2E = 1.4426950408889634

GROUP = 256
ROW_TILE = 512
WKV_PIECE = 128
VMEM_LIMIT = 56 * 1024 * 1024


def _const_spec(shape):
    nd = len(shape)
    return pl.BlockSpec(shape, lambda *_: (0,) * nd, pipeline_mode=pl.Buffered(1))


def _layer_norm(x, g, b):
    mu = jnp.mean(x, axis=-1, keepdims=True)
    xc = x - mu
    var = jnp.mean(xc * xc, axis=-1, keepdims=True)
    return xc * lax.rsqrt(var + LN_EPS) * g + b


def _sigmoid(x):
    return 1.0 / (1.0 + jnp.exp(-x))


def _split3(x):
    hi = x.astype(BF16)
    r1 = x - hi.astype(F32)
    mid = r1.astype(BF16)
    lo = (r1 - mid.astype(F32)).astype(BF16)
    return hi, mid, lo


def _dot(a, b):
    return jnp.dot(a, b, preferred_element_type=F32)


def _dot_nt(a, b):
    return lax.dot_general(a, b, (((1,), (1,)), ((), ())), preferred_element_type=F32)


def _dot_tn(a, b):
    return lax.dot_general(a, b, (((0,), (0,)), ((), ())), preferred_element_type=F32)


def _mod_kernel(c_ref, w_ref, b_ref, o_ref):
    c = c_ref[...]
    s = (c * _sigmoid(c)).astype(BF16)
    o_ref[...] = _dot(s, w_ref[...].astype(BF16)) + b_ref[...]


def _mod_call(c_all, w_ada, b_ada):
    n = c_all.shape[0]
    nblk = 6
    return pl.pallas_call(
        _mod_kernel,
        grid=(nblk,),
        in_specs=[
            pl.BlockSpec((n, D_MODEL), lambda i: (0, 0)),
            pl.BlockSpec((D_MODEL, D_MODEL), lambda i: (0, i)),
            pl.BlockSpec((1, D_MODEL), lambda i: (0, i)),
        ],
        out_specs=pl.BlockSpec((n, D_MODEL), lambda i: (0, i)),
        out_shape=jax.ShapeDtypeStruct((n, 6 * D_MODEL), F32),
        compiler_params=pltpu.CompilerParams(dimension_semantics=("arbitrary",)),
        name="mod",
    )(c_all, w_ada, b_ada)


def _mod_row(mod_ref, idx, tm):
    groups, _, d = mod_ref.shape
    m = mod_ref[:, idx:idx + 1, :]
    if groups == 1:
        return m[0]
    return jnp.broadcast_to(m, (groups, tm // groups, d)).reshape(tm, d)


def _inproj_kernel(x_ref, xnext_ref, mod_ref, lng_ref, lnb_ref, w_ref,
                   q_ref, k_ref, v_ref, k32_ref, v32_ref, p_ref, g_ref, xn_ref,
                   xna_scr, hba_scr, *, lead):
    tm = x_ref.shape[0]
    j = pl.program_id(0)

    first, second = slice(0, tm // 2), slice(tm // 2, tm)

    def front(src_ref, rs):
        def mrow(idx):
            m = _mod_row(mod_ref, idx, tm)
            return m if m.shape[0] == 1 else m[rs]

        xn = _layer_norm(src_ref[rs, :], lng_ref[...], lnb_ref[...])
        return xn, (xn * (1.0 + mrow(1)) + mrow(0)).astype(BF16)

    def half_tile(rs, prepared=False):
        if prepared:
            xn, hb = xna_scr[...], hba_scr[...]
        else:
            xn, hb = front(x_ref, rs)
        xn_ref[rs, :] = xn

        def seg(a, b):
            return _dot(hb, w_ref[:, a:b])

        g0 = ATT_COLS + RW_COLS
        for c in range(0, GATE_COLS, 512):
            g_ref[rs, c:c + 512] = _sigmoid(seg(g0 + c, g0 + c + 512))
        q_ref[rs, :] = (seg(0, HEADS_DIM) * (HEAD ** -0.5 * LOG2E)).astype(BF16)
        k = seg(HEADS_DIM, 2 * HEADS_DIM)
        k_ref[rs, :] = k.astype(BF16)
        k32_ref[rs, :] = k
        v = seg(2 * HEADS_DIM, 3 * HEADS_DIM)
        v_ref[rs, :] = v.astype(BF16)
        v32_ref[rs, :] = v
        for c in range(0, RW_COLS, 256):
            p_ref[rs, c:c + 256] = seg(ATT_COLS + c, ATT_COLS + c + 256)

    def prepare_next():
        xna_scr[...], hba_scr[...] = front(xnext_ref, first)

    @pl.when(j < lead)
    def _():
        k_ref[...] = jnp.zeros_like(k_ref)
        v_ref[...] = jnp.zeros_like(v_ref)
        prepare_next()

    @pl.when(j >= lead)
    def _():
        half_tile(first, prepared=lead > 0)
        if lead > 0:
            prepare_next()
        half_tile(second)


def _inproj_call(x2d, mod, ln_g, ln_b, w_in_b, steps, tm, prompt):
    rows = x2d.shape[0]
    in_cols = w_in_b.shape[1]
    groups = mod.shape[0]
    if prompt:
        lead = ATT_REACH // tm
        kv_rows = ATT_REACH
        row_map = lambda j: (jnp.maximum(j - lead, 0), 0)
        next_map = lambda j: (jnp.clip(j + 1 - lead, 0, steps - 1), 0)
        kv_map = lambda j: (jnp.maximum(j - steps, 0), 0)
        ext_map = lambda j: (j, 0)
    else:
        lead = 0
        kv_rows = rows
        row_map = next_map = kv_map = ext_map = lambda j: (j, 0)
    return pl.pallas_call(
        functools.partial(_inproj_kernel, lead=lead),
        grid=(steps + lead,),
        in_specs=[
            pl.BlockSpec((tm, D_MODEL), row_map),
            pl.BlockSpec((tm, D_MODEL), next_map),
            _const_spec((groups, 6, D_MODEL)),
            _const_spec((1, D_MODEL)),
            _const_spec((1, D_MODEL)),
            _const_spec((D_MODEL, in_cols)),
        ],
        scratch_shapes=[pltpu.VMEM((tm // 2, D_MODEL), F32), pltpu.VMEM((tm // 2, D_MODEL), BF16)],
        out_specs=[
            pl.BlockSpec((tm, HEADS_DIM), row_map),
            pl.BlockSpec((tm, HEADS_DIM), ext_map),
            pl.BlockSpec((tm, HEADS_DIM), ext_map),
            pl.BlockSpec((tm, HEADS_DIM), kv_map),
            pl.BlockSpec((tm, HEADS_DIM), kv_map),
            pl.BlockSpec((tm, RW_COLS), row_map),
            pl.BlockSpec((tm, GATE_COLS), row_map),
            pl.BlockSpec((tm, D_MODEL), row_map),
        ],
        out_shape=[
            jax.ShapeDtypeStruct((rows, HEADS_DIM), BF16),
            jax.ShapeDtypeStruct((rows + lead * tm, HEADS_DIM), BF16),
            jax.ShapeDtypeStruct((rows + lead * tm, HEADS_DIM), BF16),
            jax.ShapeDtypeStruct((kv_rows, HEADS_DIM), F32),
            jax.ShapeDtypeStruct((kv_rows, HEADS_DIM), F32),
            jax.ShapeDtypeStruct((rows, RW_COLS), F32),
            jax.ShapeDtypeStruct((rows, GATE_COLS), F32),
            jax.ShapeDtypeStruct((rows, D_MODEL), F32),
        ],
        compiler_params=pltpu.CompilerParams(
            dimension_semantics=("arbitrary",), vmem_limit_bytes=VMEM_LIMIT),
        name="inproj",
    )(x2d, x2d, mod, ln_g, ln_b, w_in_b)


def _attn_chunks(chunks, bias_ref):
    lane = lax.broadcasted_iota(jnp.int32, (CHUNK, 128), 1)
    first = lane < HEAD
    pairs = [slice(pr * 128, (pr + 1) * 128) for pr in range(N_HEADS // 2)]
    scores = []
    for qc, kb, _, _ in chunks:
        for sl in pairs:
            q2 = qc[:, sl].astype(F32)
            qs = jnp.concatenate([jnp.where(first, q2, 0.0), jnp.where(first, 0.0, q2)],
                                 axis=0).astype(BF16)
            scores.append(_dot_nt(qs, kb(sl)))
    probs, sums = [], []
    for i, s in enumerate(scores):
        thr = chunks[i // len(pairs)][3]
        s = s + bias_ref[i % len(pairs)]
        if thr is not None:
            col = lax.broadcasted_iota(jnp.int32, s.shape, 1)
            s = jnp.where(col >= thr, s, -jnp.inf)
        e = jnp.exp2(s - jnp.max(s, axis=1, keepdims=True))
        sums.append(jnp.sum(e, axis=1, keepdims=True))
        probs.append(e.astype(BF16))
    outs = []
    for ci, (_, _, vb, _) in enumerate(chunks):
        cols = []
        for pi, sl in enumerate(pairs):
            i = ci * len(pairs) + pi
            o = _dot(probs[i], vb(sl)) / sums[i]
            cols.append(jnp.where(first, o[0:CHUNK], o[CHUNK:2 * CHUNK]))
        outs.append(jnp.concatenate(cols, axis=1).astype(BF16))
    return outs


def _attn_prompt_kernel(q_ref, kbuf, vbuf, bias_ref, o_ref, *, chunks):
    s = pl.program_id(0)

    per = 2

    def run(masked):
        for i in range(chunks // per):
            units, starts = [], []
            for k in range(per):
                g = i * per + k
                r0 = g * CHUNK
                thr = ATT_REACH - (s * chunks + g) * CHUNK if masked else None
                band = slice(r0, r0 + BAND)
                units.append((q_ref[r0:r0 + CHUNK, :],
                              lambda sl, band=band: kbuf[band, sl],
                              lambda sl, band=band: vbuf[band, sl], thr))
                starts.append(r0)
            for r0, o in zip(starts, _attn_chunks(units, bias_ref)):
                o_ref[r0:r0 + CHUNK, :] = o

    assert chunks * CHUNK >= ATT_REACH
    pl.when(s == 0)(functools.partial(run, True))
    pl.when(s != 0)(functools.partial(run, False))


def _attn_prompt_call(q, kext, vext, bias):
    rows = q.shape[0]
    tq = ATT_REACH
    blk = pl.BlockSpec((tq, HEADS_DIM), lambda s: (s, 0))
    window = pl.BlockSpec((pl.Element(tq + ATT_REACH), pl.Element(HEADS_DIM)),
                          lambda s: (s * tq, 0))
    return pl.pallas_call(
        functools.partial(_attn_prompt_kernel, chunks=tq // CHUNK),
        grid=(rows // tq,),
        in_specs=[blk, window, window, _const_spec(bias.shape)],
        out_specs=blk,
        out_shape=jax.ShapeDtypeStruct((rows, HEADS_DIM), BF16),
        compiler_params=pltpu.CompilerParams(dimension_semantics=("arbitrary",)),
        name="attn_prompt",
    )(q, kext, vext, bias)


def _attn_sample_kernel(q_ref, k_ref, v_ref, ck_ref, cv_ref, bias_ref, o_ref):
    def band(cache_ref, new_ref, rs):
        return lambda sl: jnp.concatenate([cache_ref[:, sl], new_ref[rs, sl]], axis=0)

    units = []
    for s in range(ck_ref.shape[0]):
        rs = slice(s * CHUNK, (s + 1) * CHUNK)
        units.append((q_ref[rs, :], band(ck_ref.at[s], k_ref, rs), band(cv_ref.at[s], v_ref, rs),
                      None))
    for s, o in enumerate(_attn_chunks(units, bias_ref)):
        o_ref[s * CHUNK:(s + 1) * CHUNK, :] = o


def _attn_sample_call(q, k, v, cache_k, cache_v, bias):
    per = 2
    nb = cache_k.shape[0] // per
    new = pl.BlockSpec((per * CHUNK, HEADS_DIM), lambda b: (b, 0))
    cache = pl.BlockSpec((per, ATT_REACH, HEADS_DIM), lambda b: (b, 0, 0))
    return pl.pallas_call(
        _attn_sample_kernel,
        grid=(nb,),
        in_specs=[new, new, new, cache, cache, _const_spec(bias.shape)],
        out_specs=new,
        out_shape=jax.ShapeDtypeStruct(q.shape, BF16),
        compiler_params=pltpu.CompilerParams(dimension_semantics=("arbitrary",)),
        name="attn_sample",
    )(q, k, v, cache_k, cache_v, bias)


def _same_head():
    r = lax.broadcasted_iota(jnp.int32, (GROUP, GROUP), 0) // HEAD
    c = lax.broadcasted_iota(jnp.int32, (GROUP, GROUP), 1) // HEAD
    return r == c


def _blk(x, same_head):
    return jnp.where(same_head, jnp.concatenate([x] * 4, axis=0), 0.0).astype(BF16)


def _wkv_prep(units, hooks):
    same_head = _same_head()
    t64 = lax.broadcasted_iota(jnp.int32, (CHUNK, GROUP), 0)
    i64 = lax.broadcasted_iota(jnp.int32, (CHUNK, GROUP), 1) % HEAD
    strict = i64 < t64
    incl = i64 <= t64
    pending = list(hooks)

    def stage_done():
        if pending:
            pending.pop(0)()

    def blk(x):
        return _blk(x, same_head)

    n, a_ak, a_rb, a_rk = [], [], [], []
    for ld in units:
        lhs = jnp.concatenate([ld("at"), ld("rt")], axis=0).astype(BF16)

        def blk_t(x):
            xt = jnp.transpose(x)
            return jnp.where(same_head, jnp.concatenate([xt] * 4, axis=1), 0.0).astype(BF16)

        rhs = jnp.concatenate([blk_t(ld("bt")), blk_t(ld("kt"))], axis=1)
        a_all = _dot(lhs, rhs)
        n.append(jnp.where(strict, a_all[0:CHUNK, 0:GROUP], 0.0))
        a_ak.append(jnp.where(strict, a_all[0:CHUNK, GROUP:], 0.0))
        a_rb.append(jnp.where(incl, a_all[CHUNK:, 0:GROUP], 0.0))
        a_rk.append(jnp.where(incl, a_all[CHUNK:, GROUP:], 0.0))
    stage_done()

    with_v = [_dot(jnp.concatenate([a, b], axis=0).astype(BF16), blk(ld("v")))
              for a, b, ld in zip(a_ak, a_rk, units)]
    x0 = [p[0:CHUNK] for p in with_v]
    y_v = [p[CHUNK:] for p in with_v]
    npow = [_dot(m.astype(BF16), blk(m)) for m in n]
    t = [jnp.where(i64 == t64, 1.0, 0.0) + m for m in n]
    stage_done()
    for _ in range(4):
        prod = [_dot(jnp.concatenate([a, b], axis=0).astype(BF16), blk(b))
                for a, b in zip(t, npow)]
        t = [a + p[0:CHUNK] for a, p in zip(t, prod)]
        npow = [p[CHUNK:] for p in prod]
        stage_done()
    t = [a + _dot(a.astype(BF16), blk(b)) for a, b in zip(t, npow)]
    stage_done()
    res = [_dot(a.astype(BF16), jnp.concatenate([blk(ld("at")), blk(x)], axis=1))
           for a, x, ld in zip(t, x0, units)]
    stage_done()
    while pending:
        stage_done()
    return [(r[:, 0:GROUP], r[:, GROUP:], b, yv) for r, b, yv in zip(res, a_rb, y_v)]


def _wkv_step_stages(units, states, out):
    same_head = _same_head()
    held = {}

    def blk(x):
        return _blk(x, same_head)

    def first():
        both = [_dot(jnp.concatenate([ld("ah"), ld("rt")], axis=0).astype(BF16), blk(st))
                for ld, st in zip(units, states)]
        held["u"] = [b[0:CHUNK] + ld("vh") for b, ld in zip(both, units)]
        held["rs"] = [b[CHUNK:] for b in both]

    def second():
        for i, ld in enumerate(units):
            u, v = held["u"][i], ld("v")
            y = held["rs"][i] + ld("yv") + _dot(ld("arb").astype(BF16), blk(u))
            g = _dot_tn(jnp.concatenate([ld("btd"), ld("ktd")], axis=0).astype(BF16),
                        jnp.concatenate([u, v], axis=0).astype(BF16))
            g = jnp.where(same_head, g, 0.0)
            delta = g[0:64] + g[64:128] + g[128:192] + g[192:256]
            out.append(y)
            states[i] = states[i] * ld("fdec") + delta

    return [first, second]


def _wkv_kernel(p_ref, pn_ref, shift0_ref, state0_ref, mu_ref, w0_ref, wwa_ref, a0_ref, gup_ref,
                kkw_ref, kaw_ref, rkw_ref, gng_ref, gnb_ref, ones_ref, ltri_ref,
                out_ref, shift_out_ref, state_out_ref,
                st_scr, rt_scr, at_scr, kt_scr, bt_scr, v_scr, btd_scr, ktd_scr,
                fdec_scr, ah_scr, vh_scr, arb_scr, yv_scr, g_scr, bon_scr, *, ts, steps):
    j = pl.program_id(0)
    streams = state0_ref.shape[0]
    assert streams == 1 or (steps == 1 and ts == streams * CHUNK)

    if streams == 1:
        @pl.when(j == 0)
        def _():
            st_scr[...] = state0_ref[0]

    def bdsum(x, terms):
        ones = ones_ref[...]
        halves = []
        for c0 in range(0, HEADS_DIM, GROUP):
            parts = _split3(x[:, c0:c0 + GROUP])[:terms]
            acc = _dot(parts[0], ones)
            for part in parts[1:]:
                acc = acc + _dot(part, ones)
            halves.append(acc)
        return jnp.concatenate(halves, axis=1)

    pq = ltri_ref.shape[0]
    per = pq // CHUNK
    n_piece = ts // pq

    def pre_stages(q, upcoming=False):
        r0 = q * pq
        rows = slice(r0, r0 + pq)
        src_ref = pn_ref if upcoming else p_ref
        held = {}

        def shifted(c0, w):
            cols = slice(c0, c0 + w)
            row = lax.broadcasted_iota(jnp.int32, (pq, 1), 0)
            if streams > 1:
                first = shift0_ref[q * per:(q + 1) * per, :, cols]
                prev_row = jnp.broadcast_to(first, (per, CHUNK, w)).reshape(pq, w)
                top = row % CHUNK == 0
            else:
                if upcoming:
                    prev_row = p_ref[ts - 8:ts, cols][7:8]
                elif q == 0:
                    prev_row = shift0_ref[0][:, cols]
                else:
                    prev_row = p_ref[r0 - 8:r0, cols][7:8]
                top = row == 0
            pb = src_ref[rows, cols]
            prev = jnp.where(top, prev_row, pltpu.roll(pb, 1, 0))
            return pb + (prev - pb) * mu_ref[:, cols]

        def s_lora():
            lora = shifted(3 * HEADS_DIM, 256)
            lwla = lora[:, 0:128]
            lane = lax.broadcasted_iota(jnp.int32, (pq, 128), 1)
            held["raw"] = _dot(jnp.where(lane < LORA_W, jnp.tanh(lwla), lwla).astype(BF16),
                               wwa_ref[...])
            g_scr[rows, :] = _dot(_sigmoid(lora[:, 128:256]).astype(BF16), gup_ref[...])
            held["k"] = shifted(HEADS_DIM, HEADS_DIM)

        def s_decay():
            w_pre = w0_ref[...] + held["raw"][:, 0:HEADS_DIM]
            softplus = jnp.maximum(-w_pre, 0.0) + jnp.log(1.0 + jnp.exp(-jnp.abs(w_pre)))
            lw = -jnp.exp(-softplus - 0.5)
            ltri = ltri_ref[...]
            hi, mid, _ = _split3(lw)
            held["cum"] = _dot(ltri, hi) + _dot(ltri, mid)
            held["lw"] = lw

        def s_keys():
            k = held["k"]
            a = _sigmoid(a0_ref[...] + held["raw"][:, HEADS_DIM:])
            kk = k * kkw_ref[...]
            held["nrm"] = bdsum(kk * kk, 1)
            held.update(a=a, kk=kk, k2=k * (1.0 + (a - 1.0) * kaw_ref[...]))

        def s_bonus():
            r = shifted(0, HEADS_DIM)
            held["bon"] = bdsum(r * held["k2"] * rkw_ref[...], 1)
            rt_scr[rows, :] = r * jnp.exp(held["cum"])

        def s_norm():
            cum = held["cum"]
            kk = held["kk"] / jnp.maximum(jnp.sqrt(held["nrm"]), 1e-12)
            cum3 = cum.reshape(per, CHUNK, HEADS_DIM)
            cend = jnp.broadcast_to(cum3[:, CHUNK - 1:CHUNK, :], cum3.shape).reshape(pq, HEADS_DIM)
            diag = (lax.broadcasted_iota(jnp.int32, (pq, HEADS_DIM), 0) % CHUNK
                    == lax.broadcasted_iota(jnp.int32, (pq, HEADS_DIM), 1) % HEAD)
            held["fdec"] = bdsum(jnp.where(diag, jnp.exp(cend), 0.0), 2)
            at_scr[rows, :] = -kk * jnp.exp(cum - held["lw"])
            held.update(kk=kk, cend=cend)

        def s_inv():
            e_inv = jnp.exp(-held["cum"])
            kt_scr[rows, :] = held["k2"] * e_inv
            bt_scr[rows, :] = held["kk"] * held["a"] * e_inv

        def s_rel():
            e_rel = jnp.exp(held["cend"] - held["cum"])
            ktd_scr[rows, :] = held["k2"] * e_rel
            btd_scr[rows, :] = held["kk"] * held["a"] * e_rel
            fdec_scr[rows, :] = held["fdec"]

        def s_value():
            v = shifted(2 * HEADS_DIM, HEADS_DIM)
            v_scr[rows, :] = v
            bon_scr[rows, :] = held["bon"] * v

        return [s_lora, s_decay, s_keys, s_bonus, s_norm, s_inv, s_rel, s_value]

    def chunk_rows(c):
        return slice(c * CHUNK, (c + 1) * CHUNK)

    groups = [slice(g0, g0 + GROUP) for g0 in range(0, HEADS_DIM, GROUP)]

    named = dict(at=at_scr, rt=rt_scr, kt=kt_scr, bt=bt_scr, v=v_scr, btd=btd_scr, ktd=ktd_scr,
                 fdec=fdec_scr, ah=ah_scr, vh=vh_scr, arb=arb_scr, yv=yv_scr)
    prep_outs = (ah_scr, vh_scr, arb_scr, yv_scr)

    def loader(c, cols):
        rows = chunk_rows(c)
        return lambda name: named[name][rows, cols]

    def run(step_piece, prep_piece, extra):
        step_chunks = [] if step_piece is None else [step_piece * per + i for i in range(per)]
        prep_chunks = [] if prep_piece is None else [prep_piece * per + i for i in range(per)]
        stages, ys, finals = [], [], []
        if streams == 1:
            states = [st_scr[:, cols] for cols in groups] if step_chunks else []
            for c in step_chunks:
                out = []
                ys.append(out)
                stages += _wkv_step_stages([loader(c, cols) for cols in groups], states, out)
            finals = [(st_scr, states)] if step_chunks else []
        else:
            firsts, seconds = [], []
            for c in step_chunks:
                out = []
                ys.append(out)
                states = [state0_ref[c, :, cols] for cols in groups]
                first, second = _wkv_step_stages(
                    [loader(c, cols) for cols in groups], states, out)
                firsts.append(first)
                seconds.append(second)
                finals.append((state_out_ref.at[c], states))
            if step_chunks:
                stages += [lambda: [f() for f in firsts], lambda: [f() for f in seconds]]
        if step_chunks:
            rows = slice(step_piece * pq, (step_piece + 1) * pq)
            gn = {}

            def gn_mean():
                gn["y"] = jnp.concatenate([jnp.concatenate(out, axis=1) for out in ys], axis=0)
                gn["sum"] = bdsum(gn["y"], 1)

            def gn_var():
                gn["d"] = gn["y"] - gn["sum"] * (1.0 / HEAD)
                gn["sq"] = bdsum(gn["d"] * gn["d"], 1)

            def gn_out():
                var = gn["sq"] * (1.0 / HEAD)
                yn = (gn["d"] * lax.rsqrt(var + GN_EPS) * gng_ref[...] + gnb_ref[...]
                      + bon_scr[rows, :])
                out_ref[rows, :] = (yn * g_scr[rows, :]).astype(BF16)

            stages += [gn_mean, gn_var, gn_out]
        hooks = []
        for i in range(max(len(stages), len(extra))):
            both = stages[i:i + 1] + extra[i:i + 1]
            hooks.append(lambda both=both: [f() for f in both])
        units = [(c, cols) for c in prep_chunks for cols in groups]
        prep_out = _wkv_prep([loader(c, cols) for c, cols in units], hooks)
        for ref, states in finals:
            for cols, st in zip(groups, states):
                ref[:, cols] = st
        for (c, cols), outs in zip(units, prep_out):
            for ref, val in zip(prep_outs, outs):
                ref[chunk_rows(c), cols] = val

    look_ahead = steps > 1
    ahead_pass = max(2, n_piece - 1)
    assert ahead_pass <= n_piece or not look_ahead

    def first_piece():
        for stage in pre_stages(0):
            stage()

    if look_ahead:
        pl.when(j == 0)(first_piece)
    else:
        first_piece()
    for piece in range(n_piece + 1):
        if piece + 1 < n_piece:
            extra = pre_stages(piece + 1)
        elif piece == ahead_pass and look_ahead:
            extra = pre_stages(0, upcoming=True)
        else:
            extra = []
        run(piece - 1 if piece >= 1 else None, piece if piece < n_piece else None, extra)

    if streams == 1:
        @pl.when(j == steps - 1)
        def _():
            shift_out_ref[0] = p_ref[ts - 1:ts, :]
            state_out_ref[0] = st_scr[...]
    else:
        for s in range(streams):
            shift_out_ref[s] = p_ref[(s + 1) * CHUNK - 1:(s + 1) * CHUNK, :]


def _wkv_call(p2d, shift0, state0, prm, steps, ts):
    rows = p2d.shape[0]
    streams = state0.shape[0]
    row_map = lambda j: (j, 0)
    next_map = lambda j: (jnp.minimum(j + 1, steps - 1), 0)
    seq3 = lambda j: (0, 0, 0)
    vec = _const_spec((1, HEADS_DIM))
    big = pltpu.VMEM((ts, HEADS_DIM), F32)
    ltri = prm["ltri"][min(ts, WKV_PIECE)]
    return pl.pallas_call(
        functools.partial(_wkv_kernel, ts=ts, steps=steps),
        grid=(steps,),
        in_specs=[
            pl.BlockSpec((ts, RW_COLS), row_map),
            pl.BlockSpec((ts, RW_COLS), next_map),
            pl.BlockSpec((streams, 1, RW_COLS), seq3),
            pl.BlockSpec((streams, HEAD, HEADS_DIM), seq3),
            _const_spec((1, RW_COLS)),
            vec,
            _const_spec((128, 2 * HEADS_DIM)),
            vec,
            _const_spec((LORA_G, HEADS_DIM)),
            vec, vec, vec, vec, vec,
            _const_spec((GROUP, GROUP)),
            _const_spec(ltri.shape),
        ],
        out_specs=[
            pl.BlockSpec((ts, HEADS_DIM), row_map),
            pl.BlockSpec((streams, 1, RW_COLS), seq3),
            pl.BlockSpec((streams, HEAD, HEADS_DIM), seq3),
        ],
        out_shape=[
            jax.ShapeDtypeStruct((rows, HEADS_DIM), BF16),
            jax.ShapeDtypeStruct((streams, 1, RW_COLS), F32),
            jax.ShapeDtypeStruct((streams, HEAD, HEADS_DIM), F32),
        ],
        scratch_shapes=[
            pltpu.VMEM((HEAD, HEADS_DIM), F32),
        ] + [big] * 14,
        compiler_params=pltpu.CompilerParams(
            dimension_semantics=("arbitrary",), vmem_limit_bytes=VMEM_LIMIT),
        name="wkv",
    )(p2d, p2d, shift0, state0, prm["mu"], prm["w0"], prm["wwa"], prm["a0"], prm["gup"],
      prm["kk"], prm["ka"], prm["rk"], prm["gng"], prm["gnb"], prm["ones"], ltri)


def _mixffn_kernel(xn_ref, att_ref, rw_ref, gate_ref, mod_ref, convp_ref,
                   ln1g_ref, ln1b_ref, ln2g_ref, ln2b_ref,
                   wa_ref, wr_ref, wo_ref, wup_ref, cw_ref, cb_ref, wdn_ref,
                   y_ref, convo_ref, carry_scr, yb_scr, x1_scr, h2_scr, *, tm, steps):
    j = pl.program_id(0)
    groups = mod_ref.shape[0]
    assert groups == 1 or steps == 1
    pipelined = steps > 1

    def mod(idx):
        return _mod_row(mod_ref, idx, tm)

    half = tm // 2
    subs = [slice(0, half), slice(half, tm)]

    def modr(idx, rs):
        m = mod(idx)
        return m if m.shape[0] == 1 else m[rs]

    def front():
        ma = [_dot(att_ref[rs, :], wa_ref[...]) for rs in subs]
        mr = [_dot(rw_ref[rs, :], wr_ref[...]) for rs in subs]
        merged = [(gate_ref[rs, 0:D_MODEL] * a + gate_ref[rs, D_MODEL:] * r).astype(BF16)
                  for rs, a, r in zip(subs, ma, mr)]
        mix = [_dot(m, wo_ref[...]) for m in merged]
        x1 = [_layer_norm(ALPHA * xn_ref[rs, :] + (1.0 + modr(2, rs)) * m,
                          ln1g_ref[...], ln1b_ref[...]) for rs, m in zip(subs, mix)]
        h2 = jnp.concatenate([(a * (1.0 + modr(4, rs)) + modr(3, rs)).astype(BF16)
                              for rs, a in zip(subs, x1)], axis=0)
        return x1, h2

    def keep(x1, h2):
        for rs, a in zip(subs, x1):
            x1_scr[rs, :] = a
        h2_scr[...] = h2

    if not pipelined:
        if groups == 1:
            carry_scr[8 - (CONV_W - 1):8, :] = convp_ref[0]
        _ffn_back(*front(), subs, modr, groups, tm, convp_ref, ln2g_ref, ln2b_ref, wup_ref, cw_ref,
                  cb_ref, wdn_ref, y_ref, convo_ref, carry_scr, yb_scr)
        return

    @pl.when(j == 0)
    def _():
        carry_scr[8 - (CONV_W - 1):8, :] = convp_ref[0]
        keep(*front())

    @pl.when(j > 0)
    def _():
        x1 = [x1_scr[rs, :] for rs in subs]
        h2 = h2_scr[...]
        keep(*front())
        _ffn_back(x1, h2, subs, modr, groups, tm, convp_ref, ln2g_ref, ln2b_ref, wup_ref, cw_ref,
                  cb_ref, wdn_ref, y_ref, convo_ref, carry_scr, yb_scr)


def _ffn_back(x1, h2, subs, modr, groups, tm, convp_ref, ln2g_ref, ln2b_ref, wup_ref, cw_ref,
              cb_ref, wdn_ref, y_ref, convo_ref, carry_scr, yb_scr):
    cw_blk = 256
    glen = tm // groups
    row8 = lax.broadcasted_iota(jnp.int32, (8, cw_blk), 0)
    grow = lax.broadcasted_iota(jnp.int32, (tm, cw_blk), 0) % glen
    for c in range(0, D_FF, cw_blk):
        cs = slice(c, c + cw_blk)
        uc = _dot(h2, wup_ref[:, cs])
        uv = _dot(h2, wup_ref[:, D_FF + c:D_FF + c + cw_blk])
        r1 = pltpu.roll(uc, 1, 0)
        r2 = pltpu.roll(uc, 2, 0)
        if groups == 1:
            c6 = carry_scr[6:7, cs]
            c7 = carry_scr[7:8, cs]
            s1 = jnp.concatenate([jnp.where(row8 == 0, c7, r1[0:8]), r1[8:]], axis=0)
            s2 = jnp.concatenate(
                [jnp.where(row8 == 0, c6, jnp.where(row8 == 1, c7, r2[0:8])), r2[8:]], axis=0)
            carry_scr[:, cs] = uc[tm - 8:tm, :]
            tail = uc[tm - (CONV_W - 1):tm, :][None]
        else:
            hist = jnp.broadcast_to(convp_ref[:, :, cs][:, :, None, :],
                                    (groups, CONV_W - 1, glen, cw_blk))
            c6 = hist[:, 0].reshape(tm, cw_blk)
            c7 = hist[:, 1].reshape(tm, cw_blk)
            s1 = jnp.where(grow == 0, c7, r1)
            s2 = jnp.where(grow == 0, c6, jnp.where(grow == 1, c7, r2))
            tail = uc.reshape(groups, glen, cw_blk)[:, glen - (CONV_W - 1):, :]
        conv = cb_ref[:, cs] + s2 * cw_ref[0:1, cs] + s1 * cw_ref[1:2, cs] + uc * cw_ref[2:3, cs]
        yb_scr[:, cs] = (conv * _sigmoid(conv) * uv).astype(BF16)
        convo_ref[:, :, cs] = tail

    ff = [_dot(yb_scr[rs, :], wdn_ref[...]) for rs in subs]
    for rs, a, f in zip(subs, x1, ff):
        y_ref[rs, :] = _layer_norm(ALPHA * a + (1.0 + modr(5, rs)) * f, ln2g_ref[...], ln2b_ref[...])


def _mixffn_call(xn2d, att, rw, gates, mod, conv_prev, prm, steps, tm):
    rows = xn2d.shape[0]
    groups = mod.shape[0]
    lag = 1 if steps > 1 else 0
    row_map = lambda j: (jnp.minimum(j, steps - 1), 0)
    out_map = lambda j: (jnp.maximum(j - lag, 0), 0)
    vec = _const_spec((1, D_MODEL))
    return pl.pallas_call(
        functools.partial(_mixffn_kernel, tm=tm, steps=steps),
        grid=(steps + lag,),
        in_specs=[
            pl.BlockSpec((tm, D_MODEL), row_map),
            pl.BlockSpec((tm, HEADS_DIM), row_map),
            pl.BlockSpec((tm, HEADS_DIM), row_map),
            pl.BlockSpec((tm, GATE_COLS), row_map),
            _const_spec((groups, 6, D_MODEL)),
            _const_spec((groups, CONV_W - 1, D_FF)),
            vec, vec, vec, vec,
            _const_spec((HEADS_DIM, D_MODEL)),
            _const_spec((HEADS_DIM, D_MODEL)),
            _const_spec((D_MODEL, D_MODEL)),
            _const_spec((D_MODEL, 2 * D_FF)),
            _const_spec((CONV_W, D_FF)),
            _const_spec((1, D_FF)),
            _const_spec((D_FF, D_MODEL)),
        ],
        out_specs=[
            pl.BlockSpec((tm, D_MODEL), out_map),
            pl.BlockSpec((groups, CONV_W - 1, D_FF), lambda j: (0, 0, 0)),
        ],
        out_shape=[
            jax.ShapeDtypeStruct((rows, D_MODEL), F32),
            jax.ShapeDtypeStruct((groups, CONV_W - 1, D_FF), F32),
        ],
        scratch_shapes=[pltpu.VMEM((8, D_FF), F32), pltpu.VMEM((tm, D_FF), BF16),
                        pltpu.VMEM((tm, D_MODEL), F32), pltpu.VMEM((tm, D_MODEL), BF16)],
        compiler_params=pltpu.CompilerParams(
            dimension_semantics=("arbitrary",), vmem_limit_bytes=VMEM_LIMIT),
        name="mixffn",
    )(xn2d, att, rw, gates, mod, conv_prev,
      prm["ln1g"], prm["ln1b"], prm["ln2g"], prm["ln2b"],
      prm["wa"], prm["wr"], prm["wo"], prm["wup"], prm["cw"], prm["cb"], prm["wdn"])


def _pair_bias(table):
    assert CHUNK - 1 <= REL_CLIP
    top = ATT_REACH + CHUNK - 1
    n_far = top - REL_CLIP + 1
    far = jnp.broadcast_to(table[:, 2 * REL_CLIP:], (N_HEADS, n_far))
    lo_idx = top - (BAND + CHUNK - 2) + REL_CLIP
    near = table[:, lo_idx:2 * REL_CLIP][:, ::-1]
    ext = jnp.concatenate([far, near], axis=1).astype(F32) * LOG2E
    n_ext = BAND + CHUNK - 1
    period = jnp.concatenate([ext, jnp.zeros((N_HEADS, 1), F32)], axis=1)
    skew = jnp.tile(period, (1, CHUNK))[:, :CHUNK * n_ext].reshape(N_HEADS, CHUNK, n_ext)
    bias = skew[:, :, CHUNK - 1:CHUNK - 1 + BAND]
    return bias.reshape(N_HEADS // 2, 2 * CHUNK, BAND)


def _chunk_ltri(ts):
    t = jnp.arange(ts)
    return ((t[:, None] // CHUNK == t[None, :] // CHUNK) & (t[None, :] <= t[:, None])).astype(BF16)


def _trunk(x2d, mod, shift0, state0, conv_prev, caches, prm, n_seq):
    prompt = caches is None
    rows = x2d.shape[0]
    tm = min(rows, ROW_TILE)
    steps = rows // tm
    q, k, v, k32, v32, p, gates, xn = _inproj_call(
        x2d, mod, prm["lnig"], prm["lnib"], prm["win"], steps, tm, prompt)
    if prompt:
        att = _attn_prompt_call(q, k, v, prm["bias"])
    else:
        att = _attn_sample_call(q, k, v, *caches, prm["bias"])
    assert state0.shape[0] == n_seq
    rw, shift, state = _wkv_call(p, shift0, state0, prm, steps, tm)
    y, conv = _mixffn_call(xn, att, rw, gates, mod, conv_prev, prm, steps, tm)
    return y, (k32, v32), state, shift, conv


def kernel(x_prompt, x_sample, cache_attn_k, cache_attn_v, state_rwkv, state_shift, state_conv,
           c_prompt, c_sample, ln_in_g, ln_in_b, w_ada, b_ada, w_in, attn_rel_bias,
           rwkv_mu, rwkv_w0, rwkv_w_up, rwkv_a0, rwkv_a_up, rwkv_g_up, rwkv_k_k, rwkv_k_a,
           rwkv_r_k, rwkv_gn_g, rwkv_gn_b, w_branch_attn, w_branch_rwkv, w_out,
           ln1_g, ln1_b, ln2_g, ln2_b, w_ffn_up, ffn_conv_w, ffn_conv_b, w_ffn_down):
    bp, sp, _ = x_prompt.shape
    bs, ss, _ = x_sample.shape
    assert bp == 1 and ss == CHUNK and w_ada.shape[0] == DEPTH
    assert sp % ROW_TILE == 0 and bs * ss <= ROW_TILE and cache_attn_k.shape[2] == ATT_REACH

    row = lambda a: a.reshape(1, -1)
    wwa = jnp.zeros((LORA_W + LORA_A, 2 * HEADS_DIM), F32)
    wwa = wwa.at[:LORA_W, :HEADS_DIM].set(rwkv_w_up[0]).at[LORA_W:, HEADS_DIM:].set(rwkv_a_up[0])
    head_id = jnp.arange(GROUP) // HEAD
    prm = dict(
        lnig=row(ln_in_g), lnib=row(ln_in_b),
        ln1g=row(ln1_g[0]), ln1b=row(ln1_b[0]), ln2g=row(ln2_g[0]), ln2b=row(ln2_b[0]),
        win=w_in[0].astype(BF16), bias=_pair_bias(attn_rel_bias[0]),
        mu=row(rwkv_mu[0]), w0=row(rwkv_w0[0]), wwa=wwa.astype(BF16), a0=row(rwkv_a0[0]),
        gup=rwkv_g_up[0].astype(BF16), kk=row(rwkv_k_k[0]), ka=row(rwkv_k_a[0]),
        rk=row(rwkv_r_k[0]), gng=row(rwkv_gn_g[0]), gnb=row(rwkv_gn_b[0]),
        ones=(head_id[:, None] == head_id[None, :]).astype(BF16),
        ltri={WKV_PIECE: _chunk_ltri(WKV_PIECE), CHUNK: _chunk_ltri(CHUNK)},
        wa=w_branch_attn[0].astype(BF16), wr=w_branch_rwkv[0].astype(BF16),
        wo=w_out[0].astype(BF16), wup=w_ffn_up[0].astype(BF16),
        cw=ffn_conv_w[0], cb=row(ffn_conv_b[0]), wdn=w_ffn_down[0].astype(BF16),
    )

    n_c = bp + bs
    c_all = jnp.concatenate([c_prompt, c_sample, jnp.zeros((16 - n_c, D_MODEL), F32)], axis=0)
    mod = _mod_call(c_all, w_ada[0], row(b_ada[0])).reshape(16, 6, D_MODEL)

    y_p, kv_p, st_p, sh_p, cv_p = _trunk(
        x_prompt.reshape(sp, D_MODEL), mod[0:bp],
        jnp.zeros((bp, 1, RW_COLS), F32), jnp.zeros((bp, HEAD, HEADS_DIM), F32),
        jnp.zeros((bp, CONV_W - 1, D_FF), F32), None, prm, n_seq=bp)

    caches = (cache_attn_k[0].astype(BF16).reshape(bs, ATT_REACH, HEADS_DIM),
              cache_attn_v[0].astype(BF16).reshape(bs, ATT_REACH, HEADS_DIM))
    st0 = jnp.transpose(state_rwkv[0], (0, 3, 1, 2)).reshape(bs, HEAD, HEADS_DIM)
    y_s, kv_s, st_s, sh_s, cv_s = _trunk(
        x_sample.reshape(bs * ss, D_MODEL), mod[bp:n_c],
        state_shift[0], st0, state_conv[0], caches, prm, n_seq=bs)

    def state_out(st, b):
        return jnp.transpose(st.reshape(b, HEAD, N_HEADS, HEAD), (0, 2, 3, 1))[None]

    hs = (N_HEADS, HEAD)
    return (
        y_p.reshape(bp, sp, D_MODEL),
        y_s.reshape(bs, ss, D_MODEL),
        kv_p[0].reshape(1, bp, ATT_REACH, *hs),
        kv_p[1].reshape(1, bp, ATT_REACH, *hs),
        kv_s[0].reshape(1, bs, ss, *hs),
        kv_s[1].reshape(1, bs, ss, *hs),
        state_out(st_p, bp),
        state_out(st_s, bs),
        sh_p[None],
        sh_s[None],
        cv_p[None],
        cv_s[None],
    )
```
